```python
import jax, jax.numpy as jnp
from jax import lax
import numpy as np

D_MODEL = 1024
BATCH = 32
SEQ = 2048
DEPTH = 4
DEC_BATCH = 4
DEC_SEQ = 8192
PAST_LEN = 128

GRID_W = 64
N_BRANCH = 4
BR_WIDTH = 256
NA_HEADS = 4
NA_HEAD_DIM = 64
NA_ROWS = 8
NA_COLS = 16
Q_COLS = 16
KEY_COLS = 2 * NA_COLS
N_COL_BLOCKS = GRID_W // Q_COLS
SC_WIDTH = 256
CONV_WIDTH = 3
FN_GROUPS = 4
FN_GROUP_DIM = 64
HG_HEADS = 4
HG_DK = 64
HG_DV = 64
HG_CHUNK = 16
COL_SIZES = ([NA_HEADS * NA_HEAD_DIM] * 3 + [SC_WIDTH] * 3 + [FN_GROUPS * FN_GROUP_DIM]
             + [HG_HEADS * HG_DK] * 3 + [HG_HEADS * HG_DV] * 2)
PROJ_COLS = 3072
N_GROUPS = 4
EXPERTS_PER_GROUP = 8
N_EXPERTS = N_GROUPS * EXPERTS_PER_GROUP
TOP_K = 2
D_FF_E = 512
MOE_BLOCK = 256
DEEPNORM_ALPHA = (2 * DEPTH) ** 0.25
DEEPNORM_BETA = (8 * DEPTH) ** -0.25
LN_EPS = 1e-5
RMS_EPS = 1e-6

kernel_name = 'hybrid_bidir_encoder_natten_shortconv_fnet_hgrn2_hiermoe'


def layer_norm(x, g, b):
    xf = x.astype(jnp.float32)
    mu = jnp.mean(xf, axis=-1, keepdims=True)
    xc = xf - mu
    var = jnp.mean(xc * xc, axis=-1, keepdims=True)
    return (xc * lax.rsqrt(var + LN_EPS) * g + b).astype(x.dtype)


def split_cols(p):
    idx = np.cumsum(COL_SIZES)[:-1].tolist()
    return jnp.split(p, idx, axis=-1)


def neighbourhood_attention(q, k, v, rpb):
    b, l, h, dh = q.shape
    rows = l // GRID_W
    kr = min(NA_ROWS, rows)
    qg = q.reshape(b, rows, GRID_W, h, dh)
    kg = k.reshape(b, rows, GRID_W, h, dh)
    vg = v.reshape(b, rows, GRID_W, h, dh)
    row_start = jnp.clip(jnp.arange(rows) - kr // 2, 0, rows - kr)
    c0 = jnp.arange(N_COL_BLOCKS) * Q_COLS
    key_c0 = jnp.clip(c0 - NA_COLS // 2, 0, GRID_W - KEY_COLS)
    col_idx = key_c0[:, None] + jnp.arange(KEY_COLS)
    q_col = c0[:, None] + jnp.arange(Q_COLS)
    win_c0 = jnp.clip(q_col - NA_COLS // 2, 0, GRID_W - NA_COLS)
    kc = col_idx[:, None, :]
    valid = (kc >= win_c0[:, :, None]) & (kc < win_c0[:, :, None] + NA_COLS)
    dc = jnp.clip(kc - q_col[:, :, None], -(NA_COLS - 1), NA_COLS - 1) + NA_COLS - 1
    scale = dh ** -0.5

    def row_step(r):
        r0 = row_start[r]
        k_rows = lax.dynamic_slice_in_dim(kg, r0, kr, axis=1)
        v_rows = lax.dynamic_slice_in_dim(vg, r0, kr, axis=1)
        k_blk = jnp.take(k_rows, col_idx, axis=2)
        v_blk = jnp.take(v_rows, col_idx, axis=2)
        q_row = lax.dynamic_index_in_dim(qg, r, axis=1, keepdims=False).reshape(b, N_COL_BLOCKS, Q_COLS, h, dh)
        s = jnp.einsum('bnqhd,binjhd->bhnqij', q_row, k_blk,
                       preferred_element_type=jnp.float32) * scale
        dr = r0 + jnp.arange(kr) - r + NA_ROWS - 1
        bias = rpb[:, dr[None, None, :, None], dc[:, :, None, :]]
        s = jnp.where(valid[:, :, None, :], s + bias.astype(jnp.float32), -1e30)
        p = jax.nn.softmax(s.reshape(s.shape[:4] + (kr * KEY_COLS,)), axis=-1).reshape(s.shape)
        o = jnp.einsum('bhnqij,binjhd->bnqhd', p.astype(v.dtype), v_blk)
        return o.reshape(b, GRID_W, h, dh)

    out = lax.map(row_step, jnp.arange(rows))
    return jnp.moveaxis(out, 0, 1).reshape(b, l, h * dh)


def short_conv(xb, gate_b, gate_c, conv_w):
    u = gate_c * xb
    up = jnp.pad(u, ((0, 0), (1, 1), (0, 0)))
    conv = conv_w[0] * up[:, :-2] + conv_w[1] * up[:, 1:-1] + conv_w[2] * up[:, 2:]
    return gate_b * conv


def fourier_mix(xf):
    b, l, _ = xf.shape
    z = xf.reshape(b, l, FN_GROUPS, FN_GROUP_DIM).astype(jnp.float32)
    y = jnp.fft.fft2(z, axes=(1, 3), norm='ortho').real
    return y.reshape(b, l, FN_GROUPS * FN_GROUP_DIM).astype(xf.dtype)


def gla_chunked(q, k, v, log_f):
    b, l, h, dk = q.shape
    dv = v.shape[-1]
    nc = l // HG_CHUNK

    def to_chunks(t):
        return jnp.moveaxis(t.reshape(b, nc, HG_CHUNK, h, t.shape[-1]), 1, 0)

    tri = jnp.tril(jnp.ones((HG_CHUNK, HG_CHUNK), bool))[None, :, :, None, None]

    def step(state, inp):
        qc, kc, vc, gc = inp
        a = jnp.cumsum(gc, axis=1)
        diff = jnp.where(tri, a[:, :, None] - a[:, None, :], -jnp.inf)
        scores = jnp.einsum('bthk,bshk,btshk->bths', qc, kc, jnp.exp(diff))
        o = (jnp.einsum('bths,bshv->bthv', scores, vc)
             + jnp.einsum('bthk,bhkv->bthv', qc * jnp.exp(a), state))
        a_end = a[:, -1]
        state = (jnp.exp(a_end)[..., None] * state
                 + jnp.einsum('bshk,bshv->bhkv', kc * jnp.exp(a_end[:, None] - a), vc))
        return state, o

    s0 = jnp.zeros((b, h, dk, dv), jnp.float32)
    _, o = lax.scan(step, s0, (to_chunks(q), to_chunks(k), to_chunks(v), to_chunks(log_f)))
    return jnp.moveaxis(o, 0, 1).reshape(b, l, h, dv)


def hgrn2_bidirectional(qd, f_fwd, f_bwd, vd, gd, lb_f, lb_b, norm_g, first_layer):
    b, l, _ = qd.shape

    def heads(t):
        return t.reshape(b, l, HG_HEADS, -1).astype(jnp.float32)

    q = jax.nn.silu(heads(qd))
    v = heads(vd)

    def direction(f_logit, lb, flip):
        if first_layer:
            log_f = jax.nn.log_sigmoid(f_logit)
            k = jax.nn.sigmoid(-f_logit)
        else:
            f = lb + (1.0 - lb) * jax.nn.sigmoid(f_logit)
            log_f = jnp.log(f)
            k = 1.0 - f
        if flip:
            rev = lambda t: jnp.flip(t, axis=1)
            return rev(gla_chunked(rev(q), rev(k), rev(v), rev(log_f)))
        return gla_chunked(q, k, v, log_f)

    lbf = lb_f.reshape(HG_HEADS, HG_DK)
    lbb = lb_b.reshape(HG_HEADS, HG_DK)
    o = direction(heads(f_fwd), lbf, False) + direction(heads(f_bwd), lbb, True)
    o = o * lax.rsqrt(jnp.mean(o * o, axis=-1, keepdims=True) + RMS_EPS) * norm_g
    return (o * jax.nn.silu(heads(gd))).reshape(b, l, HG_HEADS * HG_DV).astype(qd.dtype)


def token_mixing(x, w_in, na_rpb, conv_w, lb_f, lb_b, hg_norm_g, first_layer, w_gate, b_gate, w_br, w_out):
    b, l, _ = x.shape
    p = x @ w_in
    qa, ka, va, xb, gb, gc, xf, qd, ffd, fbd, vd, gd = split_cols(p)
    ah = lambda t: t.reshape(b, l, NA_HEADS, NA_HEAD_DIM)
    y_a = neighbourhood_attention(ah(qa), ah(ka), ah(va), na_rpb)
    y_b = short_conv(xb, gb, gc, conv_w)
    y_c = fourier_mix(xf)
    y_d = hgrn2_bidirectional(qd, ffd, fbd, vd, gd, lb_f, lb_b, hg_norm_g, first_layer)
    merged = sum(jax.nn.sigmoid(x @ w_gate[i] + b_gate[i]) * (y @ w_br[i])
                 for i, y in enumerate((y_a, y_b, y_c, y_d)))
    return merged @ w_out


def hierarchical_moe(x, wg, bg, we, be, w13, w2):
    b, l, d = x.shape
    t = b * l
    xt = x.reshape(t, d)
    lg = (xt @ wg + bg).astype(jnp.float32)
    g_sel = jnp.argmax(lg, axis=-1)
    p_g = jnp.take_along_axis(jax.nn.softmax(lg, axis=-1), g_sel[:, None], axis=1)
    le = (xt @ we + be).astype(jnp.float32).reshape(t, N_GROUPS, EXPERTS_PER_GROUP)
    le = jnp.take_along_axis(le, g_sel[:, None, None], axis=1)[:, 0]
    top_v, top_i = lax.top_k(le, TOP_K)
    w_k = jax.nn.softmax(top_v, axis=-1) * p_g
    e_k = (g_sel[:, None] * EXPERTS_PER_GROUP + top_i).astype(jnp.int32)
    tk = t * TOP_K
    flat_e = e_k.reshape(tk)
    counts = jnp.zeros((N_EXPERTS,), jnp.int32).at[flat_e].add(1)
    padded = (counts + MOE_BLOCK - 1) // MOE_BLOCK * MOE_BLOCK
    pend = jnp.cumsum(padded)
    pstart = pend - padded
    start = jnp.cumsum(counts) - counts
    order = jnp.argsort(flat_e)
    sorted_e = flat_e[order]
    dest_sorted = pstart[sorted_e] + jnp.arange(tk, dtype=jnp.int32) - start[sorted_e]
    dest = jnp.zeros((tk,), jnp.int32).at[order].set(dest_sorted)
    n_rows = tk + N_EXPERTS * MOE_BLOCK
    n_blocks = n_rows // MOE_BLOCK
    tok = jnp.arange(tk) // TOP_K
    buf = jnp.zeros((n_rows, d), x.dtype).at[dest].set(xt[tok])
    block_e = jnp.minimum(jnp.searchsorted(pend, jnp.arange(n_blocks, dtype=jnp.int32) * MOE_BLOCK,
                                           side='right'), N_EXPERTS - 1)

    def expert_block(args):
        xb, e = args
        hgt, hup = jnp.split(xb @ w13[e], 2, axis=-1)
        return (jax.nn.silu(hgt) * hup) @ w2[e]

    out = lax.map(expert_block, (buf.reshape(n_blocks, MOE_BLOCK, d), block_e))
    y = out.reshape(n_rows, d)[dest].reshape(t, TOP_K, d)
    return jnp.einsum('tk,tkd->td', w_k.astype(x.dtype), y).reshape(b, l, d)


def trunk(x, ln_in_g, ln_in_b, w_in, na_rpb, conv_w, hgrn_lb, hgrn_norm_g, w_gate, b_gate, w_br, w_out,
          ln1_g, ln1_b, router_g_w, router_g_b, router_e_w, router_e_b, w13, w2, ln2_g, ln2_b):
    x = layer_norm(x, ln_in_g, ln_in_b)
    lb = jnp.cumsum(jax.nn.softmax(hgrn_lb.astype(jnp.float32), axis=1), axis=1)
    lb = lb - lb[:, :1]
    for i in range(DEPTH):
        mix = token_mixing(x, w_in[i], na_rpb[i], conv_w[i], lb[0, i], lb[1, i], hgrn_norm_g[i], i == 0,
                           w_gate[i], b_gate[i], w_br[i], w_out[i])
        x = layer_norm(DEEPNORM_ALPHA * x + mix, ln1_g[i], ln1_b[i])
        ffn = hierarchical_moe(x, router_g_w[i], router_g_b[i], router_e_w[i], router_e_b[i], w13[i], w2[i])
        x = layer_norm(DEEPNORM_ALPHA * x + ffn, ln2_g[i], ln2_b[i])
    return x


def setup_inputs(seed: int = 0) -> dict:
    key = jax.random.key(seed)
    ks = jax.random.split(key, 24)

    def nrm(k, shape, scale):
        return jax.random.normal(k, shape, jnp.float32) * scale

    D = D_MODEL
    return {
        'x_prompt': nrm(ks[0], (BATCH, SEQ, D), 1.0),
        'x_sample': nrm(ks[1], (DEC_BATCH, DEC_SEQ, D), 1.0),
        'ln_in_g': 1.0 + nrm(ks[2], (D,), 0.02),
        'ln_in_b': nrm(ks[3], (D,), 0.02),
        'w_in': nrm(ks[4], (DEPTH, D, PROJ_COLS), D ** -0.5),
        'na_rpb': nrm(ks[5], (DEPTH, NA_HEADS, 2 * NA_ROWS - 1, 2 * NA_COLS - 1), 0.02),
        'conv_w': nrm(ks[6], (DEPTH, CONV_WIDTH, SC_WIDTH), CONV_WIDTH ** -0.5),
        'hgrn_lb': nrm(ks[7], (2, DEPTH, HG_HEADS * HG_DK), 0.1),
        'hgrn_norm_g': 1.0 + nrm(ks[8], (DEPTH, HG_DV), 0.02),
        'w_gate': nrm(ks[9], (DEPTH, N_BRANCH, D, D), D ** -0.5),
        'b_gate': nrm(ks[10], (DEPTH, N_BRANCH, D), 0.02),
        'w_br': nrm(ks[11], (DEPTH, N_BRANCH, BR_WIDTH, D), BR_WIDTH ** -0.5 * DEEPNORM_BETA),
        'w_out': nrm(ks[12], (DEPTH, D, D), D ** -0.5 * DEEPNORM_BETA),
        'ln1_g': 1.0 + nrm(ks[13], (DEPTH, D), 0.02),
        'ln1_b': nrm(ks[14], (DEPTH, D), 0.02),
        'router_g_w': nrm(ks[15], (DEPTH, D, N_GROUPS), D ** -0.5),
        'router_g_b': nrm(ks[16], (DEPTH, N_GROUPS), 0.01),
        'router_e_w': nrm(ks[17], (DEPTH, D, N_EXPERTS), D ** -0.5),
        'router_e_b': nrm(ks[18], (DEPTH, N_EXPERTS), 0.01),
        'w13': nrm(ks[19], (DEPTH, N_EXPERTS, D, 2 * D_FF_E), D ** -0.5),
        'w2': nrm(ks[20], (DEPTH, N_EXPERTS, D_FF_E, D), D_FF_E ** -0.5 * DEEPNORM_BETA),
        'ln2_g': 1.0 + nrm(ks[21], (DEPTH, D), 0.02),
        'ln2_b': nrm(ks[22], (DEPTH, D), 0.02),
    }


def reference(x_prompt, x_sample, ln_in_g, ln_in_b, w_in, na_rpb, conv_w, hgrn_lb, hgrn_norm_g, w_gate, b_gate,
              w_br, w_out, ln1_g, ln1_b, router_g_w, router_g_b, router_e_w, router_e_b, w13, w2, ln2_g, ln2_b):
    y_prompt = trunk(x_prompt, ln_in_g, ln_in_b, w_in, na_rpb, conv_w, hgrn_lb, hgrn_norm_g, w_gate, b_gate,
                     w_br, w_out, ln1_g, ln1_b, router_g_w, router_g_b, router_e_w, router_e_b, w13, w2,
                     ln2_g, ln2_b)
    y_sample = trunk(x_sample, ln_in_g, ln_in_b, w_in, na_rpb, conv_w, hgrn_lb, hgrn_norm_g, w_gate, b_gate,
                     w_br, w_out, ln1_g, ln1_b, router_g_w, router_g_b, router_e_w, router_e_b, w13, w2,
                     ln2_g, ln2_b)
    return (y_prompt, y_sample)
```

```python
import functools

import numpy as np
import jax
import jax.numpy as jnp
from jax import lax
from jax.experimental import pallas as pl
from jax.experimental.pallas import tpu as pltpu

F32 = jnp.float32
BF16 = jnp.bfloat16
HIGHEST = lax.Precision.HIGHEST

D_MODEL = 1024
DEPTH = 4
GRID_W = 64
BR = 256
N_HEADS = 4
HEAD_DIM = 64
NA_ROWS = 8
NA_COLS = 16
FN_GROUP_DIM = 64
N_GROUPS = 4
EXPERTS_PER_GROUP = 8
N_EXPERTS = N_GROUPS * EXPERTS_PER_GROUP
TOP_K = 2
D_FF_E = 512
DEEPNORM_ALPHA = (2 * DEPTH) ** 0.25
LN_EPS = 1e-5
RMS_EPS = 1e-6
NEG_INF = -1e30

LANES = 128
VMEM_LIMIT_BYTES = 56 * 1024 * 1024
MOE_ROWS = 256
HG_CHUNK = 128
FFT_L1 = 64

COL_QA, COL_KA, COL_VA, COL_XB, COL_GB, COL_GC, COL_XF, COL_QD, COL_FF, COL_FB, COL_VD, COL_GD = range(12)


def _cparams(*sem):
    return pltpu.CompilerParams(dimension_semantics=sem, vmem_limit_bytes=VMEM_LIMIT_BYTES)


def _sigmoid(x):
    return 1.0 / (1.0 + jnp.exp(-x))


def _ln_rows(x, g, b):
    mu = jnp.mean(x, axis=-1, keepdims=True)
    xc = x - mu
    var = jnp.mean(xc * xc, axis=-1, keepdims=True)
    return xc * lax.rsqrt(var + LN_EPS) * g + b


def _head_of_lane(shape):
    return lax.broadcasted_iota(jnp.int32, shape, len(shape) - 1) // HEAD_DIM


def _head_stack(x):
    head = _head_of_lane(x.shape)
    return jnp.concatenate([jnp.where(head == h, x, jnp.zeros_like(x)) for h in range(N_HEADS)], axis=0)


def _same_head_matrix():
    h = np.arange(BR) // HEAD_DIM
    return (h[:, None] == h[None, :]).astype(np.float32)


def _ln_kernel(x_ref, g_ref, b_ref, o_ref):
    o_ref[...] = _ln_rows(x_ref[...], g_ref[...], b_ref[...])


def _layer_norm(x, g, b, tm=1024):
    t, d = x.shape
    row = pl.BlockSpec((tm, d), lambda i: (i, 0))
    vec = pl.BlockSpec((1, d), lambda i: (0, 0))
    return pl.pallas_call(
        _ln_kernel, grid=(t // tm,), in_specs=[row, vec, vec], out_specs=row,
        out_shape=jax.ShapeDtypeStruct((t, d), F32), compiler_params=_cparams("parallel"),
    )(x, g.reshape(1, d), b.reshape(1, d))


def _proj_kernel(x_ref, w_ref, pa_ref, pb_ref, pc_ref, pd_ref):
    xb = x_ref[...].astype(BF16)

    def cols(lo, hi):
        return jnp.dot(xb, w_ref[:, lo * BR:hi * BR], preferred_element_type=F32)

    pa_ref[...] = cols(COL_QA, COL_XB).astype(pa_ref.dtype)
    pb_ref[...] = cols(COL_XB, COL_XF)
    pc_ref[...] = cols(COL_XF, COL_QD)
    pd_ref[...] = cols(COL_QD, COL_GD + 1)


def _input_projection(x, w_in, tm=512):
    t, d = x.shape
    n = w_in.shape[1]
    widths = (3 * BR, 3 * BR, BR, 5 * BR)
    dtypes = (BF16, F32, F32, F32)
    return pl.pallas_call(
        _proj_kernel, grid=(t // tm,),
        in_specs=[pl.BlockSpec((tm, d), lambda i: (i, 0)), pl.BlockSpec((d, n), lambda i: (0, 0))],
        out_specs=[pl.BlockSpec((tm, w), lambda i: (i, 0)) for w in widths],
        out_shape=[jax.ShapeDtypeStruct((t, w), dt) for w, dt in zip(widths, dtypes)],
        compiler_params=_cparams("parallel"),
    )(x, w_in)


def _attn_bias_table(rpb):
    c = np.arange(GRID_W)
    win_c0 = np.clip(c - NA_COLS // 2, 0, GRID_W - NA_COLS)
    valid = (c[None, :] >= win_c0[:, None]) & (c[None, :] < win_c0[:, None] + NA_COLS)
    dc = np.clip(c[None, :] - c[:, None], -(NA_COLS - 1), NA_COLS - 1) + NA_COLS - 1
    dr = np.arange(NA_ROWS)[None, :] - np.arange(NA_ROWS)[:, None] + NA_ROWS - 1
    t = rpb.astype(F32)[:, dr][:, :, :, dc]
    t = jnp.where(valid[None, None, None], t, NEG_INF)
    return t.transpose(1, 0, 3, 2, 4).reshape(NA_ROWS, N_HEADS * GRID_W, NA_ROWS * GRID_W)


def _attn_kernel(q_ref, k_ref, v_ref, bias_ref, o_ref, *, n_rows, rows_per_tile):
    i = pl.program_id(1)
    head = _head_of_lane((GRID_W, BR))
    scale = HEAD_DIM ** -0.5
    span = NA_ROWS * GRID_W
    for j in range(rows_per_tile):
        r = i * rows_per_tile + j
        r0 = jnp.clip(r - NA_ROWS // 2, 0, n_rows - NA_ROWS)
        q = q_ref[0, j * GRID_W:(j + 1) * GRID_W, :]
        qs = _head_stack(q)
        start = pl.multiple_of(r0 * GRID_W, GRID_W)
        ks = k_ref[0, pl.ds(start, span), :]
        vs = v_ref[0, pl.ds(start, span), :]
        s = lax.dot_general(qs, ks, (((1,), (1,)), ((), ())), preferred_element_type=F32)
        s = s * scale + bias_ref[r - r0]
        m = jnp.max(s, axis=-1, keepdims=True)
        e = jnp.exp(s - m)
        p = e / jnp.sum(e, axis=-1, keepdims=True)
        o4 = jnp.dot(p.astype(BF16), vs, preferred_element_type=F32)
        o = jnp.zeros((GRID_W, BR), F32)
        for h in range(N_HEADS):
            o = o + jnp.where(head == h, o4[h * GRID_W:(h + 1) * GRID_W, :], 0.0)
        o_ref[0, j * GRID_W:(j + 1) * GRID_W, :] = o.astype(o_ref.dtype)


def _neighbourhood_attention(pa, bias, rows_per_tile=8):
    b, l, _ = pa.shape
    n_rows = l // GRID_W
    tq = rows_per_tile * GRID_W
    kern = functools.partial(_attn_kernel, n_rows=n_rows, rows_per_tile=rows_per_tile)
    return pl.pallas_call(
        kern, grid=(b, l // tq),
        in_specs=[pl.BlockSpec((1, tq, BR), lambda bi, i: (bi, i, 0)),
                  pl.BlockSpec((1, l, BR), lambda bi, i: (bi, 0, 1)),
                  pl.BlockSpec((1, l, BR), lambda bi, i: (bi, 0, 2)),
                  pl.BlockSpec(bias.shape, lambda bi, i: (0, 0, 0))],
        out_specs=pl.BlockSpec((1, tq, BR), lambda bi, i: (bi, i, 0)),
        out_shape=jax.ShapeDtypeStruct((b, l, BR), BF16),
        compiler_params=_cparams("parallel", "arbitrary"),
    )(pa, pa, pa, bias)


def _conv_kernel(xb_ref, gb_ref, gc_ref, xbp_ref, gcp_ref, xbn_ref, gcn_ref, w_ref, o_ref, *, n_tiles):
    i = pl.program_id(1)
    tl = xb_ref.shape[1]
    u = gc_ref[0] * xb_ref[0]
    u_before = jnp.where(i > 0, gcp_ref[0, 7:8, :] * xbp_ref[0, 7:8, :], 0.0)
    u_after = jnp.where(i < n_tiles - 1, gcn_ref[0, 0:1, :] * xbn_ref[0, 0:1, :], 0.0)
    row = lax.broadcasted_iota(jnp.int32, (tl, 1), 0)
    u_prev = jnp.where(row == 0, u_before, pltpu.roll(u, 1, 0))
    u_next = jnp.where(row == tl - 1, u_after, pltpu.roll(u, tl - 1, 0))
    conv = w_ref[0:1, :] * u_prev + w_ref[1:2, :] * u + w_ref[2:3, :] * u_next
    o_ref[0] = (gb_ref[0] * conv).astype(o_ref.dtype)


def _short_conv(pb, conv_w, tl=1024):
    b, l, _ = pb.shape
    n_tiles = l // tl
    per8 = tl // 8
    last8 = l // 8 - 1

    def main(col):
        return pl.BlockSpec((1, tl, BR), lambda bi, i: (bi, i, col))

    def before(col):
        return pl.BlockSpec((1, 8, BR), lambda bi, i: (bi, jnp.maximum(i * per8 - 1, 0), col))

    def after(col):
        return pl.BlockSpec((1, 8, BR), lambda bi, i: (bi, jnp.minimum((i + 1) * per8, last8), col))

    return pl.pallas_call(
        functools.partial(_conv_kernel, n_tiles=n_tiles), grid=(b, n_tiles),
        in_specs=[main(0), main(1), main(2), before(0), before(2), after(0), after(2),
                  pl.BlockSpec(conv_w.shape, lambda bi, i: (0, 0))],
        out_specs=pl.BlockSpec((1, tl, BR), lambda bi, i: (bi, i, 0)),
        out_shape=jax.ShapeDtypeStruct((b, l, BR), BF16),
        compiler_params=_cparams("parallel", "parallel"),
    )(pb, pb, pb, pb, pb, pb, pb, conv_w)


def _dft_cos_sin(n):
    ang = 2.0 * np.pi * ((np.arange(n)[:, None] * np.arange(n)[None, :]) % n) / n
    return np.cos(ang).astype(np.float32), np.sin(ang).astype(np.float32)


def _fft_a_kernel(z_ref, c1_ref, s1_ref, tc_ref, ts_ref, br_ref, bi_ref):
    z = z_ref[0]
    ar = jnp.dot(c1_ref[...], z, precision=HIGHEST, preferred_element_type=F32)
    ai = -jnp.dot(s1_ref[...], z, precision=HIGHEST, preferred_element_type=F32)
    tc = tc_ref[...]
    ts = ts_ref[...]
    br_ref[0] = ar * tc + ai * ts
    bi_ref[0] = ai * tc - ar * ts


def _fft_b_kernel(br_ref, bi_ref, c2_ref, s2_ref, cc_ref, sc_ref, o_ref, *, k1_tile, scale):
    dot = functools.partial(jnp.dot, precision=HIGHEST, preferred_element_type=F32)
    c2 = c2_ref[...]
    s2 = s2_ref[...]
    for j in range(k1_tile):
        br = br_ref[0, j]
        bi = bi_ref[0, j]
        hr = dot(c2, br) + dot(s2, bi)
        hi = dot(c2, bi) - dot(s2, br)
        y = (dot(hr, cc_ref[...]) + dot(hi, sc_ref[...])) * scale
        o_ref[0, :, j * BR:(j + 1) * BR] = y.astype(o_ref.dtype)


def _fourier_mix(pc, k1_tile=8):
    b, l, _ = pc.shape
    l1 = FFT_L1
    l2 = l // l1
    c1, s1 = _dft_cos_sin(l1)
    c2, s2 = _dft_cos_sin(l2)
    cg, sg = _dft_cos_sin(FN_GROUP_DIM)
    eye = np.eye(BR // FN_GROUP_DIM, dtype=np.float32)
    cc, sc = np.kron(eye, cg), np.kron(eye, sg)
    idx = (jnp.arange(l1, dtype=jnp.int32)[:, None] * jnp.arange(l2, dtype=jnp.int32)[None, :]) % l
    ang = idx.astype(F32) * (2.0 * np.pi / l)
    tc = jnp.broadcast_to(jnp.cos(ang)[:, :, None], (l1, l2, BR)).reshape(l1, l2 * BR)
    ts = jnp.broadcast_to(jnp.sin(ang)[:, :, None], (l1, l2, BR)).reshape(l1, l2 * BR)

    width = min(l2 * BR, 4096)
    zspec = pl.BlockSpec((1, l1, width), lambda bi, i: (bi, 0, i))
    tspec = pl.BlockSpec((l1, width), lambda bi, i: (0, i))
    mspec = pl.BlockSpec((l1, l1), lambda bi, i: (0, 0))
    br, bi_ = pl.pallas_call(
        _fft_a_kernel, grid=(b, l2 * BR // width),
        in_specs=[zspec, mspec, mspec, tspec, tspec], out_specs=[zspec, zspec],
        out_shape=[jax.ShapeDtypeStruct((b, l1, l2 * BR), F32)] * 2,
        compiler_params=_cparams("parallel", "parallel"),
    )(pc.reshape(b, l1, l2 * BR), c1, s1, tc, ts)

    bspec = pl.BlockSpec((1, k1_tile, l2, BR), lambda bi, i: (bi, i, 0, 0))
    m2spec = pl.BlockSpec((l2, l2), lambda bi, i: (0, 0))
    gspec = pl.BlockSpec((BR, BR), lambda bi, i: (0, 0))
    scale = 1.0 / float(np.sqrt(l * FN_GROUP_DIM))
    y = pl.pallas_call(
        functools.partial(_fft_b_kernel, k1_tile=k1_tile, scale=scale), grid=(b, l1 // k1_tile),
        in_specs=[bspec, bspec, m2spec, m2spec, gspec, gspec],
        out_specs=pl.BlockSpec((1, l2, k1_tile * BR), lambda bi, i: (bi, 0, i)),
        out_shape=jax.ShapeDtypeStruct((b, l2, l1 * BR), BF16),
        compiler_params=_cparams("parallel", "parallel"),
    )(br.reshape(b, l1, l2, BR), bi_.reshape(b, l1, l2, BR), c2, s2, cc, sc)
    return y.reshape(b, l, BR)


def _hgrn_level_masks():
    t = np.arange(HG_CHUNK)
    out = []
    half = 8
    while 2 * half < HG_CHUNK:
        same = (t[:, None] // (2 * half)) == (t[None, :] // (2 * half))
        out.append(np.tile(same, (1, N_HEADS)).astype(np.float32))
        half *= 2
    return np.stack(out)


def _hgrn_kernel(qd_ref, fl_ref, vd_ref, lb_ref, mask_ref, bd_ref, o_ref, st_ref, *, first_layer, chunks_per_tile):
    c = HG_CHUNK

    @pl.when(pl.program_id(1) == 0)
    def _():
        st_ref[...] = jnp.zeros_like(st_ref)

    bd = bd_ref[...]
    bd16 = bd.astype(BF16)
    lb = lb_ref[...]
    row = lax.broadcasted_iota(jnp.int32, (c, 1), 0)
    sub = row % 8
    nt = (((1,), (1,)), ((), ()))

    for ci in range(chunks_per_tile):
        sl = slice(ci * c, (ci + 1) * c)
        qd = qd_ref[0, sl, :]
        fl = fl_ref[0, sl, :]
        v = vd_ref[0, sl, :]
        q = qd * _sigmoid(qd)
        if first_layer:
            f = _sigmoid(fl)
            k = _sigmoid(-fl)
        else:
            f = lb + (1.0 - lb) * _sigmoid(fl)
            k = (1.0 - lb) * _sigmoid(-fl)

        o = jnp.dot((q * k).astype(BF16), bd16, preferred_element_type=F32) * v
        g = f
        for d in range(1, 8):
            if d > 1:
                g = g * pltpu.roll(f, d - 1, 0)
            p = jnp.where(sub >= d, q * g * pltpu.roll(k, d, 0), 0.0)
            o = o + jnp.dot(p.astype(BF16), bd16, preferred_element_type=F32) * pltpu.roll(v, d, 0)

        sp = f
        for j in (1, 2, 4):
            sp = sp * jnp.where(sub >= j, pltpu.roll(sp, j, 0), 1.0)
        ss = jnp.where(sub <= 6, pltpu.roll(f, c - 1, 0), 1.0)
        for j in (1, 2, 4):
            ss = ss * jnp.where(sub + j <= 7, pltpu.roll(ss, c - j, 0), 1.0)

        scores = None
        half = 8
        level = 0
        while half < c:
            right = ((row // half) % 2) == 1
            qb = jnp.where(right, q * sp, 0.0).astype(BF16)
            kb = jnp.where(right, 0.0, k * ss).astype(BF16)
            x = lax.dot_general(qb, _head_stack(kb), nt, preferred_element_type=F32)
            if 2 * half < c:
                x = x * mask_ref[level]
            scores = x if scores is None else scores + x
            blocks = c // (2 * half)
            sp3 = sp.reshape(blocks, 2 * half, BR)
            tot_left = jnp.broadcast_to(sp3[:, half - 1:half, :], sp3.shape).reshape(c, BR)
            tot_right = jnp.broadcast_to(sp3[:, 2 * half - 1:2 * half, :], sp3.shape).reshape(c, BR)
            sp = sp * jnp.where(right, tot_left, 1.0)
            ss = ss * jnp.where(right, 1.0, tot_right)
            half *= 2
            level += 1

        o = o + jnp.dot(scores.astype(BF16), _head_stack(v.astype(BF16)), preferred_element_type=F32)

        st = st_ref[...]
        o = o + lax.dot_general((q * sp).astype(BF16), st.astype(BF16), nt, preferred_element_type=F32)
        o_ref[0, sl, :] = o
        upd = jnp.dot(v.T.astype(BF16), (k * ss).astype(BF16), preferred_element_type=F32)
        st_ref[...] = st * sp[c - 1:c, :] + upd * bd


def _hgrn_scan(qd, fl, vd, cols, lb, first_layer, chunks_per_tile=2):
    b, l, _ = qd.shape
    tile = HG_CHUNK * chunks_per_tile
    masks = _hgrn_level_masks()
    bd = _same_head_matrix()

    def seq(col):
        return pl.BlockSpec((1, tile, BR), lambda bi, i: (bi, i, col))

    return pl.pallas_call(
        functools.partial(_hgrn_kernel, first_layer=first_layer, chunks_per_tile=chunks_per_tile),
        grid=(b, l // tile),
        in_specs=[seq(cols[0]), seq(cols[1]), seq(cols[2]),
                  pl.BlockSpec((1, BR), lambda bi, i: (0, 0)),
                  pl.BlockSpec(masks.shape, lambda bi, i: (0, 0, 0)),
                  pl.BlockSpec((BR, BR), lambda bi, i: (0, 0))],
        out_specs=pl.BlockSpec((1, tile, BR), lambda bi, i: (bi, i, 0)),
        out_shape=jax.ShapeDtypeStruct((b, l, BR), F32),
        scratch_shapes=[pltpu.VMEM((BR, BR), F32)],
        compiler_params=_cparams("parallel", "arbitrary"),
    )(qd, fl, vd, lb.reshape(1, BR), masks, bd)


def _hgrn_out_kernel(of_ref, ob_ref, gd_ref, g_ref, bd_ref, o_ref):
    o = of_ref[...] + ob_ref[...]
    ms = jnp.dot(o * o, bd_ref[...], precision=HIGHEST, preferred_element_type=F32) * (1.0 / HEAD_DIM)
    gd = gd_ref[...]
    o_ref[...] = (o * lax.rsqrt(ms + RMS_EPS) * g_ref[...] * (gd * _sigmoid(gd))).astype(o_ref.dtype)


def _hgrn_output(o_f, o_b, pd, norm_g, tm=1024):
    t = o_f.shape[0]
    row = pl.BlockSpec((tm, BR), lambda i: (i, 0))
    return pl.pallas_call(
        _hgrn_out_kernel, grid=(t // tm,),
        in_specs=[row, row, pl.BlockSpec((tm, BR), lambda i: (i, 4)),
                  pl.BlockSpec((1, BR), lambda i: (0, 0)), pl.BlockSpec((BR, BR), lambda i: (0, 0))],
        out_specs=row, out_shape=jax.ShapeDtypeStruct((t, BR), BF16),
        compiler_params=_cparams("parallel"),
    )(o_f, o_b, pd, jnp.tile(norm_g, N_HEADS).reshape(1, BR), _same_head_matrix())


def _hgrn2_bidirectional(pd, lb_f, lb_b, norm_g, first_layer):
    b, l, _ = pd.shape
    o_f = _hgrn_scan(pd, pd, pd, (0, 1, 3), lb_f, first_layer)
    rev = jnp.flip(pd[:, :, :4 * BR], axis=1)
    o_b = jnp.flip(_hgrn_scan(rev, rev, rev, (0, 2, 3), lb_b, first_layer), axis=1)
    y = _hgrn_output(o_f.reshape(b * l, BR), o_b.reshape(b * l, BR), pd.reshape(b * l, 5 * BR), norm_g)
    return y.reshape(b, l, BR)


def _merge_kernel(x_ref, ya_ref, yb_ref, yc_ref, yd_ref, wg_ref, bg_ref, wbr_ref, wo_ref, g_ref, b_ref, o_ref):
    x = x_ref[...]
    xb = x.astype(BF16)
    merged = None
    for i, y_ref in enumerate((ya_ref, yb_ref, yc_ref, yd_ref)):
        gate = _sigmoid(jnp.dot(xb, wg_ref[i], preferred_element_type=F32) + bg_ref[i])
        term = gate * jnp.dot(y_ref[...], wbr_ref[i], preferred_element_type=F32)
        merged = term if merged is None else merged + term
    mix = jnp.dot(merged.astype(BF16), wo_ref[...], preferred_element_type=F32)
    o_ref[...] = _ln_rows(DEEPNORM_ALPHA * x + mix, g_ref[...], b_ref[...])


def _merge(x, ys, w_gate, b_gate, w_br, w_out, g, b, tm=512):
    t, d = x.shape
    row = pl.BlockSpec((tm, d), lambda i: (i, 0))
    yrow = pl.BlockSpec((tm, BR), lambda i: (i, 0))
    vec = pl.BlockSpec((1, d), lambda i: (0, 0))

    def whole(a):
        return pl.BlockSpec(a.shape, lambda i: (0,) * a.ndim)

    bg = b_gate.reshape(b_gate.shape[0], 1, d)
    return pl.pallas_call(
        _merge_kernel, grid=(t // tm,),
        in_specs=[row, yrow, yrow, yrow, yrow, whole(w_gate), whole(bg), whole(w_br), whole(w_out), vec, vec],
        out_specs=row, out_shape=jax.ShapeDtypeStruct((t, d), F32),
        compiler_params=_cparams("parallel"),
    )(x, *ys, w_gate, bg, w_br, w_out, g.reshape(1, d), b.reshape(1, d))


def _router_kernel(x_ref, w_ref, b_ref, e_ref, p_ref):
    logits = jnp.dot(x_ref[...], w_ref[...], precision=HIGHEST, preferred_element_type=F32) + b_ref[...]
    lane = lax.broadcasted_iota(jnp.int32, logits.shape, 1)
    big = jnp.int32(LANES)
    is_group = (lane >= N_EXPERTS) & (lane < N_EXPERTS + N_GROUPS)
    lg = jnp.where(is_group, logits, -jnp.inf)
    mg = jnp.max(lg, axis=-1, keepdims=True)
    g_sel = jnp.min(jnp.where(lg == mg, lane, big), axis=-1, keepdims=True) - N_EXPERTS
    p_g = 1.0 / jnp.sum(jnp.exp(lg - mg), axis=-1, keepdims=True)
    in_group = (lane < N_EXPERTS) & ((lane // EXPERTS_PER_GROUP) == g_sel)
    le = jnp.where(in_group, logits, -jnp.inf)
    v1 = jnp.max(le, axis=-1, keepdims=True)
    i1 = jnp.min(jnp.where(le == v1, lane, big), axis=-1, keepdims=True)
    le2 = jnp.where(lane == i1, -jnp.inf, le)
    v2 = jnp.max(le2, axis=-1, keepdims=True)
    i2 = jnp.min(jnp.where(le2 == v2, lane, big), axis=-1, keepdims=True)
    e2 = jnp.exp(v2 - v1)
    w1 = p_g / (1.0 + e2)
    w2 = p_g * e2 / (1.0 + e2)
    e_ref[...] = jnp.where(lane == 0, i1, jnp.where(lane == 1, i2, 0))
    p_ref[...] = jnp.where(lane == 0, w1, jnp.where(lane == 1, w2, 0.0))


def _router(x, wg, bg, we, be, tm=1024):
    t, d = x.shape
    w = jnp.zeros((d, LANES), F32).at[:, :N_EXPERTS].set(we).at[:, N_EXPERTS:N_EXPERTS + N_GROUPS].set(wg)
    bias = jnp.zeros((1, LANES), F32).at[0, :N_EXPERTS].set(be).at[0, N_EXPERTS:N_EXPERTS + N_GROUPS].set(bg)
    out = pl.BlockSpec((tm, LANES), lambda i: (i, 0))
    return pl.pallas_call(
        _router_kernel, grid=(t // tm,),
        in_specs=[pl.BlockSpec((tm, d), lambda i: (i, 0)), pl.BlockSpec((d, LANES), lambda i: (0, 0)),
                  pl.BlockSpec((1, LANES), lambda i: (0, 0))],
        out_specs=[out, out],
        out_shape=[jax.ShapeDtypeStruct((t, LANES), jnp.int32), jax.ShapeDtypeStruct((t, LANES), F32)],
        compiler_params=_cparams("parallel"),
    )(x, w, bias)


def _expert_kernel(be_ref, nb_ref, x_ref, w13_ref, w2_ref, o_ref):
    @pl.when(pl.program_id(0) < nb_ref[0])
    def _():
        h = jnp.dot(x_ref[...], w13_ref[0], preferred_element_type=F32)
        hg = h[:, :D_FF_E]
        act = hg * _sigmoid(hg) * h[:, D_FF_E:]
        o_ref[...] = jnp.dot(act.astype(BF16), w2_ref[0], preferred_element_type=F32).astype(o_ref.dtype)

    @pl.when(pl.program_id(0) >= nb_ref[0])
    def _():
        o_ref[...] = jnp.zeros_like(o_ref)


def _expert_blocks(xs, block_e, n_used, w13, w2):
    n_rows, d = xs.shape
    grid_spec = pltpu.PrefetchScalarGridSpec(
        num_scalar_prefetch=2, grid=(n_rows // MOE_ROWS,),
        in_specs=[pl.BlockSpec((MOE_ROWS, d), lambda i, be, nb: (i, 0)),
                  pl.BlockSpec((1, d, 2 * D_FF_E), lambda i, be, nb: (be[i], 0, 0)),
                  pl.BlockSpec((1, D_FF_E, d), lambda i, be, nb: (be[i], 0, 0))],
        out_specs=pl.BlockSpec((MOE_ROWS, d), lambda i, be, nb: (i, 0)))
    return pl.pallas_call(
        _expert_kernel, grid_spec=grid_spec, out_shape=jax.ShapeDtypeStruct((n_rows, d), BF16),
        compiler_params=_cparams("arbitrary"),
    )(block_e, n_used, xs, w13, w2)


def _combine_kernel(x_ref, y_ref, p_ref, g_ref, b_ref, o_ref):
    d = x_ref.shape[1]
    p = p_ref[...]
    ffn = p[:, 0:1] * y_ref[:, :d].astype(F32) + p[:, 1:2] * y_ref[:, d:].astype(F32)
    o_ref[...] = _ln_rows(DEEPNORM_ALPHA * x_ref[...] + ffn, g_ref[...], b_ref[...])


def _combine(x, y_pairs, p, g, b, tm=1024):
    t, d = x.shape
    row = pl.BlockSpec((tm, d), lambda i: (i, 0))
    vec = pl.BlockSpec((1, d), lambda i: (0, 0))
    return pl.pallas_call(
        _combine_kernel, grid=(t // tm,),
        in_specs=[row, pl.BlockSpec((tm, TOP_K * d), lambda i: (i, 0)),
                  pl.BlockSpec((tm, LANES), lambda i: (i, 0)), vec, vec],
        out_specs=row, out_shape=jax.ShapeDtypeStruct((t, d), F32),
        compiler_params=_cparams("parallel"),
    )(x, y_pairs, p, g.reshape(1, d), b.reshape(1, d))


def _hierarchical_moe(x, wg, bg, we, be, w13, w2, g, b):
    t, d = x.shape
    e_lanes, p_lanes = _router(x, wg, bg, we, be)
    tk = t * TOP_K
    flat_e = e_lanes[:, :TOP_K].reshape(tk)
    counts = jnp.zeros((N_EXPERTS,), jnp.int32).at[flat_e].add(1)
    padded = (counts + MOE_ROWS - 1) // MOE_ROWS * MOE_ROWS
    pend = jnp.cumsum(padded)
    pstart = pend - padded
    start = jnp.cumsum(counts) - counts
    order = jnp.argsort(flat_e).astype(jnp.int32)
    sorted_e = flat_e[order]
    dest_sorted = pstart[sorted_e] + jnp.arange(tk, dtype=jnp.int32) - start[sorted_e]
    n_rows = tk + N_EXPERTS * MOE_ROWS
    n_blocks = n_rows // MOE_ROWS
    row_tok = jnp.zeros((n_rows,), jnp.int32).at[dest_sorted].set(order // TOP_K)
    dest = jnp.zeros((tk,), jnp.int32).at[order].set(dest_sorted)
    block_e = jnp.minimum(jnp.searchsorted(pend, jnp.arange(n_blocks, dtype=jnp.int32) * MOE_ROWS, side='right'),
                          N_EXPERTS - 1).astype(jnp.int32)
    n_used = (pend[-1:] // MOE_ROWS).astype(jnp.int32)
    xs = jnp.take(x.astype(BF16), row_tok, axis=0)
    ys = _expert_blocks(xs, block_e, n_used, w13, w2)
    y_pairs = jnp.take(ys, dest, axis=0).reshape(t, TOP_K * d)
    return _combine(x, y_pairs, p_lanes, g, b)


def _trunk(x, prm):
    b, l, d = x.shape
    t = b * l
    x = _layer_norm(x.reshape(t, d), prm['ln_in_g'], prm['ln_in_b'])
    for i in range(DEPTH):
        pa, pb, pc, pd = _input_projection(x, prm['w_in'][i])
        y_a = _neighbourhood_attention(pa.reshape(b, l, -1), prm['attn_bias'][i])
        y_b = _short_conv(pb.reshape(b, l, -1), prm['conv_w'][i])
        y_c = _fourier_mix(pc.reshape(b, l, -1))
        y_d = _hgrn2_bidirectional(pd.reshape(b, l, -1), prm['lb'][0, i], prm['lb'][1, i],
                                   prm['hgrn_norm_g'][i], i == 0)
        ys = [y.reshape(t, BR) for y in (y_a, y_b, y_c, y_d)]
        x = _merge(x, ys, prm['w_gate'][i], prm['b_gate'][i], prm['w_br'][i], prm['w_out'][i],
                   prm['ln1_g'][i], prm['ln1_b'][i])
        x = _hierarchical_moe(x, prm['router_g_w'][i], prm['router_g_b'][i], prm['router_e_w'][i],
                              prm['router_e_b'][i], prm['w13'][i], prm['w2'][i], prm['ln2_g'][i], prm['ln2_b'][i])
    return x.reshape(b, l, d)


def kernel(x_prompt, x_sample, ln_in_g, ln_in_b, w_in, na_rpb, conv_w, hgrn_lb, hgrn_norm_g, w_gate, b_gate, w_br,
           w_out, ln1_g, ln1_b, router_g_w, router_g_b, router_e_w, router_e_b, w13, w2, ln2_g, ln2_b):
    lb = jnp.cumsum(jax.nn.softmax(hgrn_lb.astype(F32), axis=1), axis=1)
    lb = lb - lb[:, :1]
    prm = dict(
        ln_in_g=ln_in_g, ln_in_b=ln_in_b, w_in=w_in.astype(BF16),
        attn_bias=jnp.stack([_attn_bias_table(na_rpb[i]) for i in range(DEPTH)]),
        conv_w=conv_w, lb=lb, hgrn_norm_g=hgrn_norm_g,
        w_gate=w_gate.astype(BF16), b_gate=b_gate, w_br=w_br.astype(BF16), w_out=w_out.astype(BF16),
        ln1_g=ln1_g, ln1_b=ln1_b, router_g_w=router_g_w, router_g_b=router_g_b,
        router_e_w=router_e_w, router_e_b=router_e_b, w13=w13.astype(BF16), w2=w2.astype(BF16),
        ln2_g=ln2_g, ln2_b=ln2_b)
    return _trunk(x_prompt, prm), _trunk(x_sample, prm)
```

```python
import functools

import numpy as np
import jax
import jax.numpy as jnp
from jax import lax
from jax.experimental import pallas as pl
from jax.experimental.pallas import tpu as pltpu

F32 = jnp.float32
BF16 = jnp.bfloat16
HIGHEST = lax.Precision.HIGHEST

D_MODEL = 1024
DEPTH = 4
GRID_W = 64
BR = 256
N_HEADS = 4
HEAD_DIM = 64
NA_ROWS = 8
NA_COLS = 16
FN_GROUP_DIM = 64
N_GROUPS = 4
EXPERTS_PER_GROUP = 8
N_EXPERTS = N_GROUPS * EXPERTS_PER_GROUP
TOP_K = 2
D_FF_E = 512
DEEPNORM_ALPHA = (2 * DEPTH) ** 0.25
LN_EPS = 1e-5
RMS_EPS = 1e-6
NEG_INF = -1e30

LANES = 128
VMEM_LIMIT_BYTES = 56 * 1024 * 1024
MOE_ROWS = 256
HG_CHUNK = 128
FFT_L1 = 64

COL_QA, COL_KA, COL_VA, COL_XB, COL_GB, COL_GC, COL_XF, COL_QD, COL_FF, COL_FB, COL_VD, COL_GD = range(12)


def _cparams(*sem):
    return pltpu.CompilerParams(dimension_semantics=sem, vmem_limit_bytes=VMEM_LIMIT_BYTES)


def _sigmoid(x):
    return 1.0 / (1.0 + jnp.exp(-x))


def _ln_rows(x, g, b):
    mu = jnp.mean(x, axis=-1, keepdims=True)
    xc = x - mu
    var = jnp.mean(xc * xc, axis=-1, keepdims=True)
    return xc * lax.rsqrt(var + LN_EPS) * g + b


def _head_of_lane(shape):
    return lax.broadcasted_iota(jnp.int32, shape, len(shape) - 1) // HEAD_DIM


def _head_stack(x):
    head = _head_of_lane(x.shape)
    return jnp.concatenate([jnp.where(head == h, x, jnp.zeros_like(x)) for h in range(N_HEADS)], axis=0)


def _same_head_matrix():
    h = np.arange(BR) // HEAD_DIM
    return (h[:, None] == h[None, :]).astype(np.float32)


def _ln_kernel(x_ref, g_ref, b_ref, o_ref):
    o_ref[...] = _ln_rows(x_ref[...], g_ref[...], b_ref[...])


def _layer_norm(x, g, b, tm=1024):
    t, d = x.shape
    row = pl.BlockSpec((tm, d), lambda i: (i, 0))
    vec = pl.BlockSpec((1, d), lambda i: (0, 0))
    return pl.pallas_call(
        _ln_kernel, grid=(t // tm,), in_specs=[row, vec, vec], out_specs=row,
        out_shape=jax.ShapeDtypeStruct((t, d), F32), compiler_params=_cparams("parallel"), name="ln_in",
    )(x, g.reshape(1, d), b.reshape(1, d))


def _proj_kernel(x_ref, w_ref, pa_ref, pb_ref, pc_ref, pd_ref):
    xb = x_ref[...].astype(BF16)

    def cols(lo, hi):
        return jnp.dot(xb, w_ref[:, lo * BR:hi * BR], preferred_element_type=F32)

    pa_ref[...] = cols(COL_QA, COL_XB).astype(pa_ref.dtype)
    pb_ref[...] = cols(COL_XB, COL_XF)
    pc_ref[...] = cols(COL_XF, COL_QD)
    pd_ref[...] = cols(COL_QD, COL_GD + 1)


def _input_projection(x, w_in, tm=512):
    t, d = x.shape
    n = w_in.shape[1]
    widths = (3 * BR, 3 * BR, BR, 5 * BR)
    dtypes = (BF16, F32, F32, F32)
    return pl.pallas_call(
        _proj_kernel, grid=(t // tm,),
        in_specs=[pl.BlockSpec((tm, d), lambda i: (i, 0)), pl.BlockSpec((d, n), lambda i: (0, 0))],
        out_specs=[pl.BlockSpec((tm, w), lambda i: (i, 0)) for w in widths],
        out_shape=[jax.ShapeDtypeStruct((t, w), dt) for w, dt in zip(widths, dtypes)],
        compiler_params=_cparams("parallel"), name="in_proj",
    )(x, w_in)


def _attn_bias_table(rpb):
    c = np.arange(GRID_W)
    win_c0 = np.clip(c - NA_COLS // 2, 0, GRID_W - NA_COLS)
    valid = (c[None, :] >= win_c0[:, None]) & (c[None, :] < win_c0[:, None] + NA_COLS)
    dc = np.clip(c[None, :] - c[:, None], -(NA_COLS - 1), NA_COLS - 1) + NA_COLS - 1
    dr = np.arange(NA_ROWS)[None, :] - np.arange(NA_ROWS)[:, None] + NA_ROWS - 1
    t = rpb.astype(F32)[:, dr][:, :, :, dc]
    t = jnp.where(valid[None, None, None], t, NEG_INF)
    return t.transpose(1, 0, 3, 2, 4).reshape(NA_ROWS, N_HEADS * GRID_W, NA_ROWS * GRID_W)


def _attn_kernel(q_ref, k_ref, v_ref, bias_ref, o_ref, *, n_rows, rows_per_tile):
    i = pl.program_id(1)
    head = _head_of_lane((GRID_W, BR))
    scale = HEAD_DIM ** -0.5
    span = NA_ROWS * GRID_W
    for j in range(rows_per_tile):
        r = i * rows_per_tile + j
        r0 = jnp.clip(r - NA_ROWS // 2, 0, n_rows - NA_ROWS)
        q = q_ref[0, j * GRID_W:(j + 1) * GRID_W, :]
        qs = _head_stack(q)
        start = pl.multiple_of(r0 * GRID_W, GRID_W)
        ks = k_ref[0, pl.ds(start, span), :]
        vs = v_ref[0, pl.ds(start, span), :]
        s = lax.dot_general(qs, ks, (((1,), (1,)), ((), ())), preferred_element_type=F32)
        s = s * scale + bias_ref[r - r0]
        m = jnp.max(s, axis=-1, keepdims=True)
        e = jnp.exp(s - m)
        p = e / jnp.sum(e, axis=-1, keepdims=True)
        o4 = jnp.dot(p.astype(BF16), vs, preferred_element_type=F32)
        o = jnp.zeros((GRID_W, BR), F32)
        for h in range(N_HEADS):
            o = o + jnp.where(head == h, o4[h * GRID_W:(h + 1) * GRID_W, :], 0.0)
        o_ref[0, j * GRID_W:(j + 1) * GRID_W, :] = o.astype(o_ref.dtype)


def _neighbourhood_attention(pa, bias, rows_per_tile=8):
    b, l, _ = pa.shape
    n_rows = l // GRID_W
    tq = rows_per_tile * GRID_W
    kern = functools.partial(_attn_kernel, n_rows=n_rows, rows_per_tile=rows_per_tile)
    return pl.pallas_call(
        kern, grid=(b, l // tq),
        in_specs=[pl.BlockSpec((1, tq, BR), lambda bi, i: (bi, i, 0)),
                  pl.BlockSpec((1, l, BR), lambda bi, i: (bi, 0, 1)),
                  pl.BlockSpec((1, l, BR), lambda bi, i: (bi, 0, 2)),
                  pl.BlockSpec(bias.shape, lambda bi, i: (0, 0, 0))],
        out_specs=pl.BlockSpec((1, tq, BR), lambda bi, i: (bi, i, 0)),
        out_shape=jax.ShapeDtypeStruct((b, l, BR), BF16),
        compiler_params=_cparams("parallel", "arbitrary"), name="nbr_attn",
    )(pa, pa, pa, bias)


def _conv_kernel(xb_ref, gb_ref, gc_ref, xbp_ref, gcp_ref, xbn_ref, gcn_ref, w_ref, o_ref, *, n_tiles):
    i = pl.program_id(1)
    tl = xb_ref.shape[1]
    u = gc_ref[0] * xb_ref[0]
    u_before = jnp.where(i > 0, gcp_ref[0, 7:8, :] * xbp_ref[0, 7:8, :], 0.0)
    u_after = jnp.where(i < n_tiles - 1, gcn_ref[0, 0:1, :] * xbn_ref[0, 0:1, :], 0.0)
    row = lax.broadcasted_iota(jnp.int32, (tl, 1), 0)
    u_prev = jnp.where(row == 0, u_before, pltpu.roll(u, 1, 0))
    u_next = jnp.where(row == tl - 1, u_after, pltpu.roll(u, tl - 1, 0))
    conv = w_ref[0:1, :] * u_prev + w_ref[1:2, :] * u + w_ref[2:3, :] * u_next
    o_ref[0] = (gb_ref[0] * conv).astype(o_ref.dtype)


def _short_conv(pb, conv_w, tl=1024):
    b, l, _ = pb.shape
    n_tiles = l // tl
    per8 = tl // 8
    last8 = l // 8 - 1

    def main(col):
        return pl.BlockSpec((1, tl, BR), lambda bi, i: (bi, i, col))

    def before(col):
        return pl.BlockSpec((1, 8, BR), lambda bi, i: (bi, jnp.maximum(i * per8 - 1, 0), col))

    def after(col):
        return pl.BlockSpec((1, 8, BR), lambda bi, i: (bi, jnp.minimum((i + 1) * per8, last8), col))

    return pl.pallas_call(
        functools.partial(_conv_kernel, n_tiles=n_tiles), grid=(b, n_tiles),
        in_specs=[main(0), main(1), main(2), before(0), before(2), after(0), after(2),
                  pl.BlockSpec(conv_w.shape, lambda bi, i: (0, 0))],
        out_specs=pl.BlockSpec((1, tl, BR), lambda bi, i: (bi, i, 0)),
        out_shape=jax.ShapeDtypeStruct((b, l, BR), BF16),
        compiler_params=_cparams("parallel", "parallel"), name="short_conv",
    )(pb, pb, pb, pb, pb, pb, pb, conv_w)


def _dft_cos_sin(n):
    ang = 2.0 * np.pi * ((np.arange(n)[:, None] * np.arange(n)[None, :]) % n) / n
    return np.cos(ang).astype(np.float32), np.sin(ang).astype(np.float32)


def _fft_a_kernel(z_ref, c1_ref, s1_ref, tc_ref, ts_ref, br_ref, bi_ref):
    z = z_ref[0]
    ar = jnp.dot(c1_ref[...], z, precision=HIGHEST, preferred_element_type=F32)
    ai = -jnp.dot(s1_ref[...], z, precision=HIGHEST, preferred_element_type=F32)
    tc = tc_ref[...]
    ts = ts_ref[...]
    br_ref[0] = ar * tc + ai * ts
    bi_ref[0] = ai * tc - ar * ts


def _fft_b_kernel(br_ref, bi_ref, c2_ref, s2_ref, cc_ref, sc_ref, o_ref, *, k1_tile, scale):
    dot = functools.partial(jnp.dot, precision=HIGHEST, preferred_element_type=F32)
    c2 = c2_ref[...]
    s2 = s2_ref[...]
    for j in range(k1_tile):
        br = br_ref[0, j]
        bi = bi_ref[0, j]
        hr = dot(c2, br) + dot(s2, bi)
        hi = dot(c2, bi) - dot(s2, br)
        y = (dot(hr, cc_ref[...]) + dot(hi, sc_ref[...])) * scale
        o_ref[0, :, j * BR:(j + 1) * BR] = y.astype(o_ref.dtype)


def _fft_twiddles(l):
    l1 = FFT_L1
    l2 = l // l1
    idx = (jnp.arange(l1, dtype=jnp.int32)[:, None] * jnp.arange(l2, dtype=jnp.int32)[None, :]) % l
    ang = idx.astype(F32) * (2.0 * np.pi / l)
    tc = jnp.broadcast_to(jnp.cos(ang)[:, :, None], (l1, l2, BR)).reshape(l1, l2 * BR)
    ts = jnp.broadcast_to(jnp.sin(ang)[:, :, None], (l1, l2, BR)).reshape(l1, l2 * BR)
    return tc, ts


def _fourier_mix(pc, tc, ts, k1_tile=8):
    b, l, _ = pc.shape
    l1 = FFT_L1
    l2 = l // l1
    c1, s1 = _dft_cos_sin(l1)
    c2, s2 = _dft_cos_sin(l2)
    cg, sg = _dft_cos_sin(FN_GROUP_DIM)
    eye = np.eye(BR // FN_GROUP_DIM, dtype=np.float32)
    cc, sc = np.kron(eye, cg), np.kron(eye, sg)

    width = min(l2 * BR, 4096)
    zspec = pl.BlockSpec((1, l1, width), lambda bi, i: (bi, 0, i))
    tspec = pl.BlockSpec((l1, width), lambda bi, i: (0, i))
    mspec = pl.BlockSpec((l1, l1), lambda bi, i: (0, 0))
    br, bi_ = pl.pallas_call(
        _fft_a_kernel, grid=(b, l2 * BR // width),
        in_specs=[zspec, mspec, mspec, tspec, tspec], out_specs=[zspec, zspec],
        out_shape=[jax.ShapeDtypeStruct((b, l1, l2 * BR), F32)] * 2,
        compiler_params=_cparams("parallel", "parallel"), name="fft_major",
    )(pc.reshape(b, l1, l2 * BR), c1, s1, tc, ts)

    bspec = pl.BlockSpec((1, k1_tile, l2, BR), lambda bi, i: (bi, i, 0, 0))
    m2spec = pl.BlockSpec((l2, l2), lambda bi, i: (0, 0))
    gspec = pl.BlockSpec((BR, BR), lambda bi, i: (0, 0))
    scale = 1.0 / float(np.sqrt(l * FN_GROUP_DIM))
    y = pl.pallas_call(
        functools.partial(_fft_b_kernel, k1_tile=k1_tile, scale=scale), grid=(b, l1 // k1_tile),
        in_specs=[bspec, bspec, m2spec, m2spec, gspec, gspec],
        out_specs=pl.BlockSpec((1, l2, k1_tile * BR), lambda bi, i: (bi, 0, i)),
        out_shape=jax.ShapeDtypeStruct((b, l2, l1 * BR), BF16),
        compiler_params=_cparams("parallel", "parallel"), name="fft_minor",
    )(br.reshape(b, l1, l2, BR), bi_.reshape(b, l1, l2, BR), c2, s2, cc, sc)
    return y.reshape(b, l, BR)


def _hgrn_level_masks():
    t = np.arange(HG_CHUNK)
    out = []
    half = 8
    while 2 * half < HG_CHUNK:
        same = (t[:, None] // (2 * half)) == (t[None, :] // (2 * half))
        out.append(np.tile(same, (1, N_HEADS)).astype(np.float32))
        half *= 2
    return np.stack(out)


def _hgrn_kernel(qd_ref, fl_ref, vd_ref, lb_ref, mask_ref, bd_ref, o_ref, st_ref, *, first_layer, chunks_per_tile,
                 rev):
    c = HG_CHUNK

    @pl.when(pl.program_id(1) == 0)
    def _():
        st_ref[...] = jnp.zeros_like(st_ref)

    bd = bd_ref[...]
    bd16 = bd.astype(BF16)
    lb = lb_ref[...]
    row = lax.broadcasted_iota(jnp.int32, (c, 1), 0)
    sub = row % 8
    nt = (((1,), (1,)), ((), ()))

    def earlier(x, j):
        return pltpu.roll(x, (c - j) if rev else j, 0)

    def later(x, j):
        return pltpu.roll(x, j if rev else (c - j), 0)

    def has_earlier(j):
        return (sub + j <= 7) if rev else (sub >= j)

    def has_later(j):
        return (sub >= j) if rev else (sub + j <= 7)

    for ci in (reversed(range(chunks_per_tile)) if rev else range(chunks_per_tile)):
        sl = slice(ci * c, (ci + 1) * c)
        qd = qd_ref[0, sl, :]
        fl = fl_ref[0, sl, :]
        v = vd_ref[0, sl, :]
        q = qd * _sigmoid(qd)
        if first_layer:
            f = _sigmoid(fl)
            k = _sigmoid(-fl)
        else:
            f = lb + (1.0 - lb) * _sigmoid(fl)
            k = (1.0 - lb) * _sigmoid(-fl)

        o = jnp.dot((q * k).astype(BF16), bd16, preferred_element_type=F32) * v
        g = f
        for d in range(1, 8):
            if d > 1:
                g = g * earlier(f, d - 1)
            p = jnp.where(has_earlier(d), q * g * earlier(k, d), 0.0)
            o = o + jnp.dot(p.astype(BF16), bd16, preferred_element_type=F32) * earlier(v, d)

        qf = f
        for j in (1, 2, 4):
            qf = qf * jnp.where(has_earlier(j), earlier(qf, j), 1.0)
        kf = jnp.where(has_later(1), later(f, 1), 1.0)
        for j in (1, 2, 4):
            kf = kf * jnp.where(has_later(j), later(kf, j), 1.0)

        scores = None
        half = 8
        level = 0
        while half < c:
            late = ((row // half) % 2) == (0 if rev else 1)
            qb = jnp.where(late, q * qf, 0.0).astype(BF16)
            kb = jnp.where(late, 0.0, k * kf).astype(BF16)
            x = lax.dot_general(qb, _head_stack(kb), nt, preferred_element_type=F32)
            if 2 * half < c:
                x = x * mask_ref[level]
            scores = x if scores is None else scores + x
            blocks = c // (2 * half)
            qf3 = qf.reshape(blocks, 2 * half, BR)
            end_early = half if rev else half - 1
            end_late = 0 if rev else 2 * half - 1
            tot_early = jnp.broadcast_to(qf3[:, end_early:end_early + 1, :], qf3.shape).reshape(c, BR)
            tot_late = jnp.broadcast_to(qf3[:, end_late:end_late + 1, :], qf3.shape).reshape(c, BR)
            qf = qf * jnp.where(late, tot_early, 1.0)
            kf = kf * jnp.where(late, 1.0, tot_late)
            half *= 2
            level += 1

        o = o + jnp.dot(scores.astype(BF16), _head_stack(v.astype(BF16)), preferred_element_type=F32)

        st = st_ref[...]
        o = o + lax.dot_general((q * qf).astype(BF16), st.astype(BF16), nt, preferred_element_type=F32)
        o_ref[0, sl, :] = o
        upd = jnp.dot(v.T.astype(BF16), (k * kf).astype(BF16), preferred_element_type=F32)
        last = 0 if rev else c - 1
        st_ref[...] = st * qf[last:last + 1, :] + upd * bd


def _hgrn_scan(qd, fl, vd, cols, lb, first_layer, rev, chunks_per_tile=2):
    b, l, _ = qd.shape
    tile = HG_CHUNK * chunks_per_tile
    n_tiles = l // tile
    masks = _hgrn_level_masks()
    bd = _same_head_matrix()

    def tile_index(i):
        return n_tiles - 1 - i if rev else i

    def seq(col):
        return pl.BlockSpec((1, tile, BR), lambda bi, i: (bi, tile_index(i), col))

    return pl.pallas_call(
        functools.partial(_hgrn_kernel, first_layer=first_layer, chunks_per_tile=chunks_per_tile, rev=rev),
        grid=(b, n_tiles),
        in_specs=[seq(cols[0]), seq(cols[1]), seq(cols[2]),
                  pl.BlockSpec((1, BR), lambda bi, i: (0, 0)),
                  pl.BlockSpec(masks.shape, lambda bi, i: (0, 0, 0)),
                  pl.BlockSpec((BR, BR), lambda bi, i: (0, 0))],
        out_specs=pl.BlockSpec((1, tile, BR), lambda bi, i: (bi, tile_index(i), 0)),
        out_shape=jax.ShapeDtypeStruct((b, l, BR), F32),
        scratch_shapes=[pltpu.VMEM((BR, BR), F32)],
        compiler_params=_cparams("parallel", "arbitrary"),
        name="hgrn_bwd" if rev else "hgrn_fwd",
    )(qd, fl, vd, lb.reshape(1, BR), masks, bd)


def _hgrn_out_kernel(of_ref, ob_ref, gd_ref, g_ref, bd_ref, o_ref):
    o = of_ref[...] + ob_ref[...]
    ms = jnp.dot(o * o, bd_ref[...], precision=HIGHEST, preferred_element_type=F32) * (1.0 / HEAD_DIM)
    gd = gd_ref[...]
    o_ref[...] = (o * lax.rsqrt(ms + RMS_EPS) * g_ref[...] * (gd * _sigmoid(gd))).astype(o_ref.dtype)


def _hgrn_output(o_f, o_b, pd, norm_g, tm=1024):
    t = o_f.shape[0]
    row = pl.BlockSpec((tm, BR), lambda i: (i, 0))
    return pl.pallas_call(
        _hgrn_out_kernel, grid=(t // tm,),
        in_specs=[row, row, pl.BlockSpec((tm, BR), lambda i: (i, 4)),
                  pl.BlockSpec((1, BR), lambda i: (0, 0)), pl.BlockSpec((BR, BR), lambda i: (0, 0))],
        out_specs=row, out_shape=jax.ShapeDtypeStruct((t, BR), BF16),
        compiler_params=_cparams("parallel"), name="hgrn_out",
    )(o_f, o_b, pd, jnp.tile(norm_g, N_HEADS).reshape(1, BR), _same_head_matrix())


def _hgrn2_bidirectional(pd, lb_f, lb_b, norm_g, first_layer):
    b, l, _ = pd.shape
    o_f = _hgrn_scan(pd, pd, pd, (0, 1, 3), lb_f, first_layer, rev=False)
    o_b = _hgrn_scan(pd, pd, pd, (0, 2, 3), lb_b, first_layer, rev=True)
    y = _hgrn_output(o_f.reshape(b * l, BR), o_b.reshape(b * l, BR), pd.reshape(b * l, 5 * BR), norm_g)
    return y.reshape(b, l, BR)


def _merge_kernel(x_ref, ya_ref, yb_ref, yc_ref, yd_ref, wg_ref, bg_ref, wbr_ref, wo_ref, g_ref, b_ref, o_ref):
    x = x_ref[...]
    xb = x.astype(BF16)
    merged = None
    for i, y_ref in enumerate((ya_ref, yb_ref, yc_ref, yd_ref)):
        gate = _sigmoid(jnp.dot(xb, wg_ref[i], preferred_element_type=F32) + bg_ref[i])
        term = gate * jnp.dot(y_ref[...], wbr_ref[i], preferred_element_type=F32)
        merged = term if merged is None else merged + term
    mix = jnp.dot(merged.astype(BF16), wo_ref[...], preferred_element_type=F32)
    o_ref[...] = _ln_rows(DEEPNORM_ALPHA * x + mix, g_ref[...], b_ref[...])


def _merge(x, ys, w_gate, b_gate, w_br, w_out, g, b, tm=512):
    t, d = x.shape
    row = pl.BlockSpec((tm, d), lambda i: (i, 0))
    yrow = pl.BlockSpec((tm, BR), lambda i: (i, 0))
    vec = pl.BlockSpec((1, d), lambda i: (0, 0))

    def whole(a):
        return pl.BlockSpec(a.shape, lambda i: (0,) * a.ndim)

    bg = b_gate.reshape(b_gate.shape[0], 1, d)
    return pl.pallas_call(
        _merge_kernel, grid=(t // tm,),
        in_specs=[row, yrow, yrow, yrow, yrow, whole(w_gate), whole(bg), whole(w_br), whole(w_out), vec, vec],
        out_specs=row, out_shape=jax.ShapeDtypeStruct((t, d), F32),
        compiler_params=_cparams("parallel"), name="merge_ln1",
    )(x, *ys, w_gate, bg, w_br, w_out, g.reshape(1, d), b.reshape(1, d))


def _router_kernel(x_ref, w_ref, b_ref, tri_ref, e_ref, p_ref, cnt_ref, run_ref):
    @pl.when(pl.program_id(0) == 0)
    def _():
        run_ref[...] = jnp.zeros_like(run_ref)

    logits = jnp.dot(x_ref[...], w_ref[...], precision=HIGHEST, preferred_element_type=F32) + b_ref[...]
    lane = lax.broadcasted_iota(jnp.int32, logits.shape, 1)
    big = jnp.int32(LANES)
    is_group = (lane >= N_EXPERTS) & (lane < N_EXPERTS + N_GROUPS)
    lg = jnp.where(is_group, logits, -jnp.inf)
    mg = jnp.max(lg, axis=-1, keepdims=True)
    g_sel = jnp.min(jnp.where(lg == mg, lane, big), axis=-1, keepdims=True) - N_EXPERTS
    p_g = 1.0 / jnp.sum(jnp.exp(lg - mg), axis=-1, keepdims=True)
    in_group = (lane < N_EXPERTS) & ((lane // EXPERTS_PER_GROUP) == g_sel)
    le = jnp.where(in_group, logits, -jnp.inf)
    v1 = jnp.max(le, axis=-1, keepdims=True)
    i1 = jnp.min(jnp.where(le == v1, lane, big), axis=-1, keepdims=True)
    le2 = jnp.where(lane == i1, -jnp.inf, le)
    v2 = jnp.max(le2, axis=-1, keepdims=True)
    i2 = jnp.min(jnp.where(le2 == v2, lane, big), axis=-1, keepdims=True)
    e2 = jnp.exp(v2 - v1)
    w1 = p_g / (1.0 + e2)
    w2 = p_g * e2 / (1.0 + e2)
    hot1 = lane == i1
    hot2 = lane == i2
    hot = (hot1 | hot2).astype(BF16)
    before = jnp.dot(tri_ref[...], hot, preferred_element_type=F32) + run_ref[...]
    rank1 = jnp.sum(jnp.where(hot1, before, 0.0), axis=-1, keepdims=True).astype(jnp.int32)
    rank2 = jnp.sum(jnp.where(hot2, before, 0.0), axis=-1, keepdims=True).astype(jnp.int32)
    run = run_ref[...] + jnp.sum(hot.astype(F32), axis=0, keepdims=True)
    run_ref[...] = run
    cnt_ref[...] = run.astype(jnp.int32)
    e_ref[...] = jnp.where(lane == 0, i1, jnp.where(lane == 1, i2, jnp.where(lane == 2, rank1,
                                                                               jnp.where(lane == 3, rank2, 0))))
    p_ref[...] = jnp.where(lane == 0, w1, jnp.where(lane == 1, w2, 0.0))


def _router(x, wg, bg, we, be, tm=512):
    t, d = x.shape
    w = jnp.zeros((d, LANES), F32).at[:, :N_EXPERTS].set(we).at[:, N_EXPERTS:N_EXPERTS + N_GROUPS].set(wg)
    bias = jnp.zeros((1, LANES), F32).at[0, :N_EXPERTS].set(be).at[0, N_EXPERTS:N_EXPERTS + N_GROUPS].set(bg)
    tri = np.tril(np.ones((tm, tm), np.float32), -1).astype(BF16)
    out = pl.BlockSpec((tm, LANES), lambda i: (i, 0))
    vec = pl.BlockSpec((1, LANES), lambda i: (0, 0))
    return pl.pallas_call(
        _router_kernel, grid=(t // tm,),
        in_specs=[pl.BlockSpec((tm, d), lambda i: (i, 0)), pl.BlockSpec((d, LANES), lambda i: (0, 0)), vec,
                  pl.BlockSpec((tm, tm), lambda i: (0, 0))],
        out_specs=[out, out, vec],
        out_shape=[jax.ShapeDtypeStruct((t, LANES), jnp.int32), jax.ShapeDtypeStruct((t, LANES), F32),
                   jax.ShapeDtypeStruct((1, LANES), jnp.int32)],
        scratch_shapes=[pltpu.VMEM((1, LANES), F32)],
        compiler_params=_cparams("arbitrary"), name="moe_router",
    )(x, w, bias, tri)


def _expert_kernel(be_ref, nb_ref, x_ref, w13_ref, w2_ref, o_ref):
    @pl.when(pl.program_id(0) < nb_ref[0])
    def _():
        h = jnp.dot(x_ref[...], w13_ref[0], preferred_element_type=F32)
        hg = h[:, :D_FF_E]
        act = hg * _sigmoid(hg) * h[:, D_FF_E:]
        o_ref[...] = jnp.dot(act.astype(BF16), w2_ref[0], preferred_element_type=F32).astype(o_ref.dtype)

    @pl.when(pl.program_id(0) >= nb_ref[0])
    def _():
        o_ref[...] = jnp.zeros_like(o_ref)


def _expert_blocks(xs, block_e, n_used, w13, w2):
    n_rows, d = xs.shape
    grid_spec = pltpu.PrefetchScalarGridSpec(
        num_scalar_prefetch=2, grid=(n_rows // MOE_ROWS,),
        in_specs=[pl.BlockSpec((MOE_ROWS, d), lambda i, be, nb: (i, 0)),
                  pl.BlockSpec((1, d, 2 * D_FF_E), lambda i, be, nb: (be[i], 0, 0)),
                  pl.BlockSpec((1, D_FF_E, d), lambda i, be, nb: (be[i], 0, 0))],
        out_specs=pl.BlockSpec((MOE_ROWS, d), lambda i, be, nb: (i, 0)))
    return pl.pallas_call(
        _expert_kernel, grid_spec=grid_spec, out_shape=jax.ShapeDtypeStruct((n_rows, d), BF16),
        compiler_params=_cparams("arbitrary"), name="moe_experts",
    )(block_e, n_used, xs, w13, w2)


def _combine_kernel(x_ref, y_ref, p_ref, g_ref, b_ref, o_ref):
    d = x_ref.shape[1]
    p = p_ref[...]
    ffn = p[:, 0:1] * y_ref[:, :d].astype(F32) + p[:, 1:2] * y_ref[:, d:].astype(F32)
    o_ref[...] = _ln_rows(DEEPNORM_ALPHA * x_ref[...] + ffn, g_ref[...], b_ref[...])


def _combine(x, y_pairs, p, g, b, tm=1024):
    t, d = x.shape
    row = pl.BlockSpec((tm, d), lambda i: (i, 0))
    vec = pl.BlockSpec((1, d), lambda i: (0, 0))
    return pl.pallas_call(
        _combine_kernel, grid=(t // tm,),
        in_specs=[row, pl.BlockSpec((tm, TOP_K * d), lambda i: (i, 0)),
                  pl.BlockSpec((tm, LANES), lambda i: (i, 0)), vec, vec],
        out_specs=row, out_shape=jax.ShapeDtypeStruct((t, d), F32),
        compiler_params=_cparams("parallel"), name="moe_combine_ln2",
    )(x, y_pairs, p, g.reshape(1, d), b.reshape(1, d))


def _hierarchical_moe(x, wg, bg, we, be, w13, w2, g, b):
    t, d = x.shape
    e_lanes, p_lanes, cnt = _router(x, wg, bg, we, be)
    tk = t * TOP_K
    e_k = e_lanes[:, :TOP_K]
    counts = cnt[0, :N_EXPERTS]
    padded = (counts + MOE_ROWS - 1) // MOE_ROWS * MOE_ROWS
    pend = jnp.cumsum(padded)
    pstart = pend - padded
    start = jnp.cumsum(counts) - counts
    dest = (pstart[e_k] + e_lanes[:, TOP_K:2 * TOP_K]).reshape(tk)
    n_rows = tk + N_EXPERTS * MOE_ROWS
    n_blocks = n_rows // MOE_ROWS
    block_e = jnp.minimum(jnp.searchsorted(pend, jnp.arange(n_blocks, dtype=jnp.int32) * MOE_ROWS, side='right'),
                          N_EXPERTS - 1).astype(jnp.int32)
    n_used = (pend[-1:] // MOE_ROWS).astype(jnp.int32)
    order = jnp.argsort(e_k.reshape(tk)).astype(jnp.int32)
    row = jnp.arange(n_rows, dtype=jnp.int32)
    row_e = jnp.repeat(block_e, MOE_ROWS)
    within = row - pstart[row_e]
    row_tok = jnp.where(within < counts[row_e], order[jnp.minimum(within + start[row_e], tk - 1)] // TOP_K, 0)
    xs = jnp.take(x.astype(BF16), row_tok, axis=0)
    ys = _expert_blocks(xs, block_e, n_used, w13, w2)
    y_pairs = jnp.take(ys, dest, axis=0).reshape(t, TOP_K * d)
    return _combine(x, y_pairs, p_lanes, g, b)


def _trunk(x, prm):
    b, l, d = x.shape
    t = b * l
    x = _layer_norm(x.reshape(t, d), prm['ln_in_g'], prm['ln_in_b'])
    tc, ts = _fft_twiddles(l)
    for i in range(DEPTH):
        pa, pb, pc, pd = _input_projection(x, prm['w_in'][i])
        y_a = _neighbourhood_attention(pa.reshape(b, l, -1), prm['attn_bias'][i])
        y_b = _short_conv(pb.reshape(b, l, -1), prm['conv_w'][i])
        y_c = _fourier_mix(pc.reshape(b, l, -1), tc, ts)
        y_d = _hgrn2_bidirectional(pd.reshape(b, l, -1), prm['lb'][0, i], prm['lb'][1, i],
                                   prm['hgrn_norm_g'][i], i == 0)
        ys = [y.reshape(t, BR) for y in (y_a, y_b, y_c, y_d)]
        x = _merge(x, ys, prm['w_gate'][i], prm['b_gate'][i], prm['w_br'][i], prm['w_out'][i],
                   prm['ln1_g'][i], prm['ln1_b'][i])
        x = _hierarchical_moe(x, prm['router_g_w'][i], prm['router_g_b'][i], prm['router_e_w'][i],
                              prm['router_e_b'][i], prm['w13'][i], prm['w2'][i], prm['ln2_g'][i], prm['ln2_b'][i])
    return x.reshape(b, l, d)


def kernel(x_prompt, x_sample, ln_in_g, ln_in_b, w_in, na_rpb, conv_w, hgrn_lb, hgrn_norm_g, w_gate, b_gate, w_br,
           w_out, ln1_g, ln1_b, router_g_w, router_g_b, router_e_w, router_e_b, w13, w2, ln2_g, ln2_b):
    lb = jnp.cumsum(jax.nn.softmax(hgrn_lb.astype(F32), axis=1), axis=1)
    lb = lb - lb[:, :1]
    prm = dict(
        ln_in_g=ln_in_g, ln_in_b=ln_in_b, w_in=w_in.astype(BF16),
        attn_bias=jnp.stack([_attn_bias_table(na_rpb[i]) for i in range(DEPTH)]),
        conv_w=conv_w, lb=lb, hgrn_norm_g=hgrn_norm_g,
        w_gate=w_gate.astype(BF16), b_gate=b_gate, w_br=w_br.astype(BF16), w_out=w_out.astype(BF16),
        ln1_g=ln1_g, ln1_b=ln1_b, router_g_w=router_g_w, router_g_b=router_g_b,
        router_e_w=router_e_w, router_e_b=router_e_b, w13=w13.astype(BF16), w2=w2.astype(BF16),
        ln2_g=ln2_g, ln2_b=ln2_b)
    return _trunk(x_prompt, prm), _trunk(x_sample, prm)
```

```python
import functools

import numpy as np
import jax
import jax.numpy as jnp
from jax import lax
from jax.experimental import pallas as pl
from jax.experimental.pallas import tpu as pltpu

F32 = jnp.float32
BF16 = jnp.bfloat16
HIGHEST = lax.Precision.HIGHEST

D_MODEL = 1024
DEPTH = 4
GRID_W = 64
BR = 256
N_HEADS = 4
HEAD_DIM = 64
NA_ROWS = 8
NA_COLS = 16
FN_GROUP_DIM = 64
N_GROUPS = 4
EXPERTS_PER_GROUP = 8
N_EXPERTS = N_GROUPS * EXPERTS_PER_GROUP
TOP_K = 2
D_FF_E = 512
DEEPNORM_ALPHA = (2 * DEPTH) ** 0.25
LN_EPS = 1e-5
RMS_EPS = 1e-6
NEG_INF = -1e30

LANES = 128
VMEM_LIMIT_BYTES = 56 * 1024 * 1024
MOE_ROWS = 256
HG_CHUNK = 128
FFT_L1 = 64

COL_QA, COL_KA, COL_VA, COL_XB, COL_GB, COL_GC, COL_XF, COL_QD, COL_FF, COL_FB, COL_VD, COL_GD = range(12)
PROJ_GROUPS = ((0, 3), (3, 6), (6, 8), (8, 13))


def _cparams(*sem):
    return pltpu.CompilerParams(dimension_semantics=sem, vmem_limit_bytes=VMEM_LIMIT_BYTES)


def _sigmoid(x):
    return 1.0 / (1.0 + jnp.exp(-x))


def _ln_rows(x, g, b):
    mu = jnp.mean(x, axis=-1, keepdims=True)
    xc = x - mu
    var = jnp.mean(xc * xc, axis=-1, keepdims=True)
    return xc * lax.rsqrt(var + LN_EPS) * g + b


def _head_of_lane(shape):
    return lax.broadcasted_iota(jnp.int32, shape, len(shape) - 1) // HEAD_DIM


def _head_stack(x):
    head = _head_of_lane(x.shape)
    return jnp.concatenate([jnp.where(head == h, x, jnp.zeros_like(x)) for h in range(N_HEADS)], axis=0)


def _same_head_matrix():
    h = np.arange(BR) // HEAD_DIM
    return (h[:, None] == h[None, :]).astype(np.float32)


def _ln_kernel(x_ref, g_ref, b_ref, o_ref):
    o_ref[...] = _ln_rows(x_ref[...], g_ref[...], b_ref[...])


def _layer_norm(x, g, b, tm=1024):
    t, d = x.shape
    row = pl.BlockSpec((tm, d), lambda i: (i, 0))
    vec = pl.BlockSpec((1, d), lambda i: (0, 0))
    return pl.pallas_call(
        _ln_kernel, grid=(t // tm,), in_specs=[row, vec, vec], out_specs=row,
        out_shape=jax.ShapeDtypeStruct((t, d), F32), compiler_params=_cparams("parallel"), name="ln_in",
    )(x, g.reshape(1, d), b.reshape(1, d))


def _proj_kernel(x_ref, w_ref, pa_ref, pb_ref, pc_ref, pd_ref):
    xb = x_ref[...].astype(BF16)

    def cols(lo, hi):
        return jnp.dot(xb, w_ref[:, lo * BR:hi * BR], preferred_element_type=F32)

    for ref, (lo, hi) in zip((pa_ref, pb_ref, pc_ref, pd_ref), PROJ_GROUPS):
        ref[...] = cols(lo, hi).astype(ref.dtype)


def _fold_channel_dft(w_in):
    cg, sg = _dft_cos_sin(FN_GROUP_DIM)
    eye = np.eye(BR // FN_GROUP_DIM, dtype=np.float32)
    w_xf = w_in[:, COL_XF * BR:(COL_XF + 1) * BR]
    w_re = jnp.dot(w_xf, np.kron(eye, cg), precision=HIGHEST)
    w_im = -jnp.dot(w_xf, np.kron(eye, sg), precision=HIGHEST)
    return jnp.concatenate([w_in[:, :COL_XF * BR], w_re, w_im, w_in[:, (COL_XF + 1) * BR:]], axis=1)


def _input_projection(x, w_in, tm=512):
    t, d = x.shape
    n = w_in.shape[1]
    widths = tuple((hi - lo) * BR for lo, hi in PROJ_GROUPS)
    dtypes = (BF16, F32, F32, F32)
    return pl.pallas_call(
        _proj_kernel, grid=(t // tm,),
        in_specs=[pl.BlockSpec((tm, d), lambda i: (i, 0)), pl.BlockSpec((d, n), lambda i: (0, 0))],
        out_specs=[pl.BlockSpec((tm, w), lambda i: (i, 0)) for w in widths],
        out_shape=[jax.ShapeDtypeStruct((t, w), dt) for w, dt in zip(widths, dtypes)],
        compiler_params=_cparams("parallel"), name="in_proj",
    )(x, w_in)


def _attn_bias_table(rpb):
    c = np.arange(GRID_W)
    win_c0 = np.clip(c - NA_COLS // 2, 0, GRID_W - NA_COLS)
    valid = (c[None, :] >= win_c0[:, None]) & (c[None, :] < win_c0[:, None] + NA_COLS)
    dc = np.clip(c[None, :] - c[:, None], -(NA_COLS - 1), NA_COLS - 1) + NA_COLS - 1
    dr = np.arange(NA_ROWS)[None, :] - np.arange(NA_ROWS)[:, None] + NA_ROWS - 1
    row_hot = (dr[:, :, None] == np.arange(2 * NA_ROWS - 1)).astype(np.float32)
    col_hot = (dc[:, :, None] == np.arange(2 * NA_COLS - 1)).astype(np.float32)
    t = jnp.einsum('oia,hab,qkb->ohqik', row_hot, rpb.astype(F32), col_hot, precision=HIGHEST)
    t = jnp.where(valid[None, None, :, None, :], t, NEG_INF)
    return t.reshape(NA_ROWS, N_HEADS * GRID_W, NA_ROWS * GRID_W)


def _attn_kernel(q_ref, k_ref, v_ref, bias_ref, o_ref, *, n_rows, rows_per_tile):
    i = pl.program_id(1)
    head = _head_of_lane((GRID_W, BR))
    scale = HEAD_DIM ** -0.5
    span = NA_ROWS * GRID_W
    for j in range(rows_per_tile):
        r = i * rows_per_tile + j
        r0 = jnp.clip(r - NA_ROWS // 2, 0, n_rows - NA_ROWS)
        q = q_ref[0, j * GRID_W:(j + 1) * GRID_W, :]
        qs = _head_stack(q)
        start = pl.multiple_of(r0 * GRID_W, GRID_W)
        ks = k_ref[0, pl.ds(start, span), :]
        vs = v_ref[0, pl.ds(start, span), :]
        s = lax.dot_general(qs, ks, (((1,), (1,)), ((), ())), preferred_element_type=F32)
        s = s * scale + bias_ref[r - r0]
        m = jnp.max(s, axis=-1, keepdims=True)
        e = jnp.exp(s - m)
        p = e / jnp.sum(e, axis=-1, keepdims=True)
        o4 = jnp.dot(p.astype(BF16), vs, preferred_element_type=F32)
        o = jnp.zeros((GRID_W, BR), F32)
        for h in range(N_HEADS):
            o = o + jnp.where(head == h, o4[h * GRID_W:(h + 1) * GRID_W, :], 0.0)
        o_ref[0, j * GRID_W:(j + 1) * GRID_W, :] = o.astype(o_ref.dtype)


def _neighbourhood_attention(pa, bias, rows_per_tile=8):
    b, l, _ = pa.shape
    n_rows = l // GRID_W
    tq = rows_per_tile * GRID_W
    kern = functools.partial(_attn_kernel, n_rows=n_rows, rows_per_tile=rows_per_tile)
    return pl.pallas_call(
        kern, grid=(b, l // tq),
        in_specs=[pl.BlockSpec((1, tq, BR), lambda bi, i: (bi, i, 0)),
                  pl.BlockSpec((1, l, BR), lambda bi, i: (bi, 0, 1)),
                  pl.BlockSpec((1, l, BR), lambda bi, i: (bi, 0, 2)),
                  pl.BlockSpec(bias.shape, lambda bi, i: (0, 0, 0))],
        out_specs=pl.BlockSpec((1, tq, BR), lambda bi, i: (bi, i, 0)),
        out_shape=jax.ShapeDtypeStruct((b, l, BR), BF16),
        compiler_params=_cparams("parallel", "arbitrary"), name="nbr_attn",
    )(pa, pa, pa, bias)


def _conv_kernel(xb_ref, gb_ref, gc_ref, xbp_ref, gcp_ref, xbn_ref, gcn_ref, w_ref, o_ref, *, n_tiles):
    i = pl.program_id(1)
    tl = xb_ref.shape[1]
    u = gc_ref[0] * xb_ref[0]
    u_before = jnp.where(i > 0, gcp_ref[0, 7:8, :] * xbp_ref[0, 7:8, :], 0.0)
    u_after = jnp.where(i < n_tiles - 1, gcn_ref[0, 0:1, :] * xbn_ref[0, 0:1, :], 0.0)
    row = lax.broadcasted_iota(jnp.int32, (tl, 1), 0)
    u_prev = jnp.where(row == 0, u_before, pltpu.roll(u, 1, 0))
    u_next = jnp.where(row == tl - 1, u_after, pltpu.roll(u, tl - 1, 0))
    conv = w_ref[0:1, :] * u_prev + w_ref[1:2, :] * u + w_ref[2:3, :] * u_next
    o_ref[0] = (gb_ref[0] * conv).astype(o_ref.dtype)


def _short_conv(pb, conv_w, tl=1024):
    b, l, _ = pb.shape
    n_tiles = l // tl
    per8 = tl // 8
    last8 = l // 8 - 1

    def main(col):
        return pl.BlockSpec((1, tl, BR), lambda bi, i: (bi, i, col))

    def before(col):
        return pl.BlockSpec((1, 8, BR), lambda bi, i: (bi, jnp.maximum(i * per8 - 1, 0), col))

    def after(col):
        return pl.BlockSpec((1, 8, BR), lambda bi, i: (bi, jnp.minimum((i + 1) * per8, last8), col))

    return pl.pallas_call(
        functools.partial(_conv_kernel, n_tiles=n_tiles), grid=(b, n_tiles),
        in_specs=[main(0), main(1), main(2), before(0), before(2), after(0), after(2),
                  pl.BlockSpec(conv_w.shape, lambda bi, i: (0, 0))],
        out_specs=pl.BlockSpec((1, tl, BR), lambda bi, i: (bi, i, 0)),
        out_shape=jax.ShapeDtypeStruct((b, l, BR), BF16),
        compiler_params=_cparams("parallel", "parallel"), name="short_conv",
    )(pb, pb, pb, pb, pb, pb, pb, conv_w)


def _dft_cos_sin(n):
    ang = 2.0 * np.pi * ((np.arange(n)[:, None] * np.arange(n)[None, :]) % n) / n
    return np.cos(ang).astype(np.float32), np.sin(ang).astype(np.float32)


def _fft_major_kernel(g_ref, c1_ref, s1_ref, tc_ref, ts_ref, br_ref, bi_ref):
    dot = functools.partial(jnp.dot, precision=HIGHEST, preferred_element_type=F32)
    for s in range(8):
        g = g_ref[0, :, s, :]
        swapped = jnp.concatenate([g[:, BR:], -g[:, :BR]], axis=1)
        a = dot(c1_ref[...], g) + dot(s1_ref[...], swapped)
        ar, ai = a[:, :BR], a[:, BR:]
        tc = tc_ref[:, s, :]
        ts = ts_ref[:, s, :]
        br_ref[0, :, s, :] = ar * tc + ai * ts
        bi_ref[0, :, s, :] = ai * tc - ar * ts


def _fft_minor_kernel(br_ref, bi_ref, c2_ref, s2_ref, o_ref, *, scale):
    dot = functools.partial(jnp.dot, precision=HIGHEST, preferred_element_type=F32)
    for j in range(8):
        y = dot(c2_ref[...], br_ref[0, j]) + dot(s2_ref[...], bi_ref[0, j])
        o_ref[0, :, j, :] = (y * scale).astype(o_ref.dtype)


def _fft_twiddles(l):
    l1 = FFT_L1
    l2 = l // l1
    idx = (jnp.arange(l1, dtype=jnp.int32)[:, None] * jnp.arange(l2, dtype=jnp.int32)[None, :]) % l
    ang = idx.astype(F32) * (2.0 * np.pi / l)
    tc = jnp.broadcast_to(jnp.cos(ang)[:, :, None], (l1, l2, BR))
    ts = jnp.broadcast_to(jnp.sin(ang)[:, :, None], (l1, l2, BR))
    return tc, ts


def _fourier_mix(pc, tc, ts):
    b, l, _ = pc.shape
    l1 = FFT_L1
    l2 = l // l1
    c1, s1 = _dft_cos_sin(l1)
    c2, s2 = _dft_cos_sin(l2)

    def strip(width):
        return pl.BlockSpec((1, l1, 8, width), lambda bi, i: (bi, 0, i, 0))

    tspec = pl.BlockSpec((l1, 8, BR), lambda bi, i: (0, i, 0))
    mspec = pl.BlockSpec((l1, l1), lambda bi, i: (0, 0))
    br, bi_ = pl.pallas_call(
        _fft_major_kernel, grid=(b, l2 // 8),
        in_specs=[strip(2 * BR), mspec, mspec, tspec, tspec], out_specs=[strip(BR), strip(BR)],
        out_shape=[jax.ShapeDtypeStruct((b, l1, l2, BR), F32)] * 2,
        compiler_params=_cparams("parallel", "parallel"), name="fft_major",
    )(pc.reshape(b, l1, l2, 2 * BR), c1, s1, tc, ts)

    bspec = pl.BlockSpec((1, 8, l2, BR), lambda bi, i: (bi, i, 0, 0))
    m2spec = pl.BlockSpec((l2, l2), lambda bi, i: (0, 0))
    scale = 1.0 / float(np.sqrt(l * FN_GROUP_DIM))
    y = pl.pallas_call(
        functools.partial(_fft_minor_kernel, scale=scale), grid=(b, l1 // 8),
        in_specs=[bspec, bspec, m2spec, m2spec],
        out_specs=pl.BlockSpec((1, l2, 8, BR), lambda bi, i: (bi, 0, i, 0)),
        out_shape=jax.ShapeDtypeStruct((b, l2, l1, BR), F32),
        compiler_params=_cparams("parallel", "parallel"), name="fft_minor",
    )(br, bi_, c2, s2)
    return y.reshape(b, l, BR)


def _hgrn_level_masks():
    t = np.arange(HG_CHUNK)
    out = []
    half = 8
    while 2 * half < HG_CHUNK:
        same = (t[:, None] // (2 * half)) == (t[None, :] // (2 * half))
        out.append(np.tile(same, (1, N_HEADS)).astype(np.float32))
        half *= 2
    return np.stack(out)


def _hgrn_kernel(qd_ref, fl_ref, vd_ref, lb_ref, mask_ref, bd_ref, o_ref, st_ref, *, first_layer, chunks_per_tile,
                 rev):
    c = HG_CHUNK

    @pl.when(pl.program_id(1) == 0)
    def _():
        st_ref[...] = jnp.zeros_like(st_ref)

    bd = bd_ref[...]
    bd16 = bd.astype(BF16)
    lb = lb_ref[...]
    row = lax.broadcasted_iota(jnp.int32, (c, 1), 0)
    sub = row % 8
    nt = (((1,), (1,)), ((), ()))

    def earlier(x, j):
        return pltpu.roll(x, (c - j) if rev else j, 0)

    def later(x, j):
        return pltpu.roll(x, j if rev else (c - j), 0)

    def has_earlier(j):
        return (sub + j <= 7) if rev else (sub >= j)

    def has_later(j):
        return (sub >= j) if rev else (sub + j <= 7)

    for ci in (reversed(range(chunks_per_tile)) if rev else range(chunks_per_tile)):
        sl = slice(ci * c, (ci + 1) * c)
        qd = qd_ref[0, sl, :]
        fl = fl_ref[0, sl, :]
        v = vd_ref[0, sl, :]
        q = qd * _sigmoid(qd)
        if first_layer:
            f = _sigmoid(fl)
            k = _sigmoid(-fl)
        else:
            f = lb + (1.0 - lb) * _sigmoid(fl)
            k = (1.0 - lb) * _sigmoid(-fl)

        o = jnp.dot((q * k).astype(BF16), bd16, preferred_element_type=F32) * v
        g = f
        for d in range(1, 8):
            if d > 1:
                g = g * earlier(f, d - 1)
            p = jnp.where(has_earlier(d), q * g * earlier(k, d), 0.0)
            o = o + jnp.dot(p.astype(BF16), bd16, preferred_element_type=F32) * earlier(v, d)

        qf = f
        for j in (1, 2, 4):
            qf = qf * jnp.where(has_earlier(j), earlier(qf, j), 1.0)
        kf = jnp.where(has_later(1), later(f, 1), 1.0)
        for j in (1, 2, 4):
            kf = kf * jnp.where(has_later(j), later(kf, j), 1.0)

        scores = None
        half = 8
        level = 0
        while half < c:
            late = ((row // half) % 2) == (0 if rev else 1)
            qb = jnp.where(late, q * qf, 0.0).astype(BF16)
            kb = jnp.where(late, 0.0, k * kf).astype(BF16)
            x = lax.dot_general(qb, _head_stack(kb), nt, preferred_element_type=F32)
            if 2 * half < c:
                x = x * mask_ref[level]
            scores = x if scores is None else scores + x
            blocks = c // (2 * half)
            qf3 = qf.reshape(blocks, 2 * half, BR)
            end_early = half if rev else half - 1
            end_late = 0 if rev else 2 * half - 1
            tot_early = jnp.broadcast_to(qf3[:, end_early:end_early + 1, :], qf3.shape).reshape(c, BR)
            tot_late = jnp.broadcast_to(qf3[:, end_late:end_late + 1, :], qf3.shape).reshape(c, BR)
            qf = qf * jnp.where(late, tot_early, 1.0)
            kf = kf * jnp.where(late, 1.0, tot_late)
            half *= 2
            level += 1

        o = o + jnp.dot(scores.astype(BF16), _head_stack(v.astype(BF16)), preferred_element_type=F32)

        st = st_ref[...]
        o = o + lax.dot_general((q * qf).astype(BF16), st.astype(BF16), nt, preferred_element_type=F32)
        o_ref[0, sl, :] = o
        upd = jnp.dot(v.T.astype(BF16), (k * kf).astype(BF16), preferred_element_type=F32)
        last = 0 if rev else c - 1
        st_ref[...] = st * qf[last:last + 1, :] + upd * bd


def _hgrn_scan(qd, fl, vd, cols, lb, first_layer, rev, chunks_per_tile=2):
    b, l, _ = qd.shape
    tile = HG_CHUNK * chunks_per_tile
    n_tiles = l // tile
    masks = _hgrn_level_masks()
    bd = _same_head_matrix()

    def tile_index(i):
        return n_tiles - 1 - i if rev else i

    def seq(col):
        return pl.BlockSpec((1, tile, BR), lambda bi, i: (bi, tile_index(i), col))

    return pl.pallas_call(
        functools.partial(_hgrn_kernel, first_layer=first_layer, chunks_per_tile=chunks_per_tile, rev=rev),
        grid=(b, n_tiles),
        in_specs=[seq(cols[0]), seq(cols[1]), seq(cols[2]),
                  pl.BlockSpec((1, BR), lambda bi, i: (0, 0)),
                  pl.BlockSpec(masks.shape, lambda bi, i: (0, 0, 0)),
                  pl.BlockSpec((BR, BR), lambda bi, i: (0, 0))],
        out_specs=pl.BlockSpec((1, tile, BR), lambda bi, i: (bi, tile_index(i), 0)),
        out_shape=jax.ShapeDtypeStruct((b, l, BR), F32),
        scratch_shapes=[pltpu.VMEM((BR, BR), F32)],
        compiler_params=_cparams("parallel", "arbitrary"),
        name="hgrn_bwd" if rev else "hgrn_fwd",
    )(qd, fl, vd, lb.reshape(1, BR), masks, bd)


def _hgrn_out_kernel(of_ref, ob_ref, gd_ref, g_ref, bd_ref, o_ref):
    o = of_ref[...] + ob_ref[...]
    ms = jnp.dot(o * o, bd_ref[...], precision=HIGHEST, preferred_element_type=F32) * (1.0 / HEAD_DIM)
    gd = gd_ref[...]
    o_ref[...] = (o * lax.rsqrt(ms + RMS_EPS) * g_ref[...] * (gd * _sigmoid(gd))).astype(o_ref.dtype)


def _hgrn_output(o_f, o_b, pd, norm_g, tm=1024):
    t = o_f.shape[0]
    row = pl.BlockSpec((tm, BR), lambda i: (i, 0))
    return pl.pallas_call(
        _hgrn_out_kernel, grid=(t // tm,),
        in_specs=[row, row, pl.BlockSpec((tm, BR), lambda i: (i, 4)),
                  pl.BlockSpec((1, BR), lambda i: (0, 0)), pl.BlockSpec((BR, BR), lambda i: (0, 0))],
        out_specs=row, out_shape=jax.ShapeDtypeStruct((t, BR), BF16),
        compiler_params=_cparams("parallel"), name="hgrn_out",
    )(o_f, o_b, pd, jnp.tile(norm_g, N_HEADS).reshape(1, BR), _same_head_matrix())


def _hgrn2_bidirectional(pd, lb_f, lb_b, norm_g, first_layer):
    b, l, _ = pd.shape
    o_f = _hgrn_scan(pd, pd, pd, (0, 1, 3), lb_f, first_layer, rev=False)
    o_b = _hgrn_scan(pd, pd, pd, (0, 2, 3), lb_b, first_layer, rev=True)
    y = _hgrn_output(o_f.reshape(b * l, BR), o_b.reshape(b * l, BR), pd.reshape(b * l, 5 * BR), norm_g)
    return y.reshape(b, l, BR)


def _merge_kernel(x_ref, ya_ref, yb_ref, yc_ref, yd_ref, wg_ref, bg_ref, wbr_ref, wo_ref, g_ref, b_ref, o_ref):
    x = x_ref[...]
    xb = x.astype(BF16)
    merged = None
    for i, y_ref in enumerate((ya_ref, yb_ref, yc_ref, yd_ref)):
        gate = _sigmoid(jnp.dot(xb, wg_ref[i], preferred_element_type=F32) + bg_ref[i])
        term = gate * jnp.dot(y_ref[...].astype(BF16), wbr_ref[i], preferred_element_type=F32)
        merged = term if merged is None else merged + term
    mix = jnp.dot(merged.astype(BF16), wo_ref[...], preferred_element_type=F32)
    o_ref[...] = _ln_rows(DEEPNORM_ALPHA * x + mix, g_ref[...], b_ref[...])


def _merge(x, ys, w_gate, b_gate, w_br, w_out, g, b, tm=512):
    t, d = x.shape
    row = pl.BlockSpec((tm, d), lambda i: (i, 0))
    yrow = pl.BlockSpec((tm, BR), lambda i: (i, 0))
    vec = pl.BlockSpec((1, d), lambda i: (0, 0))

    def whole(a):
        return pl.BlockSpec(a.shape, lambda i: (0,) * a.ndim)

    bg = b_gate.reshape(b_gate.shape[0], 1, d)
    return pl.pallas_call(
        _merge_kernel, grid=(t // tm,),
        in_specs=[row, yrow, yrow, yrow, yrow, whole(w_gate), whole(bg), whole(w_br), whole(w_out), vec, vec],
        out_specs=row, out_shape=jax.ShapeDtypeStruct((t, d), F32),
        compiler_params=_cparams("parallel"), name="merge_ln1",
    )(x, *ys, w_gate, bg, w_br, w_out, g.reshape(1, d), b.reshape(1, d))


def _router_kernel(x_ref, w_ref, b_ref, tri_ref, e_ref, p_ref, cnt_ref, run_ref):
    @pl.when(pl.program_id(0) == 0)
    def _():
        run_ref[...] = jnp.zeros_like(run_ref)

    logits = jnp.dot(x_ref[...], w_ref[...], precision=HIGHEST, preferred_element_type=F32) + b_ref[...]
    lane = lax.broadcasted_iota(jnp.int32, logits.shape, 1)
    big = jnp.int32(LANES)
    is_group = (lane >= N_EXPERTS) & (lane < N_EXPERTS + N_GROUPS)
    lg = jnp.where(is_group, logits, -jnp.inf)
    mg = jnp.max(lg, axis=-1, keepdims=True)
    g_sel = jnp.min(jnp.where(lg == mg, lane, big), axis=-1, keepdims=True) - N_EXPERTS
    p_g = 1.0 / jnp.sum(jnp.exp(lg - mg), axis=-1, keepdims=True)
    in_group = (lane < N_EXPERTS) & ((lane // EXPERTS_PER_GROUP) == g_sel)
    le = jnp.where(in_group, logits, -jnp.inf)
    v1 = jnp.max(le, axis=-1, keepdims=True)
    i1 = jnp.min(jnp.where(le == v1, lane, big), axis=-1, keepdims=True)
    le2 = jnp.where(lane == i1, -jnp.inf, le)
    v2 = jnp.max(le2, axis=-1, keepdims=True)
    i2 = jnp.min(jnp.where(le2 == v2, lane, big), axis=-1, keepdims=True)
    e2 = jnp.exp(v2 - v1)
    w1 = p_g / (1.0 + e2)
    w2 = p_g * e2 / (1.0 + e2)
    hot1 = lane == i1
    hot2 = lane == i2
    hot = (hot1 | hot2).astype(BF16)
    before = jnp.dot(tri_ref[...], hot, preferred_element_type=F32) + run_ref[...]
    rank1 = jnp.sum(jnp.where(hot1, before, 0.0), axis=-1, keepdims=True).astype(jnp.int32)
    rank2 = jnp.sum(jnp.where(hot2, before, 0.0), axis=-1, keepdims=True).astype(jnp.int32)
    run = run_ref[...] + jnp.sum(hot.astype(F32), axis=0, keepdims=True)
    run_ref[...] = run
    cnt_ref[...] = run.astype(jnp.int32)
    e_ref[...] = jnp.where(lane == 0, i1, jnp.where(lane == 1, i2, jnp.where(lane == 2, rank1,
                                                                               jnp.where(lane == 3, rank2, 0))))
    p_ref[...] = jnp.where(lane == 0, w1, jnp.where(lane == 1, w2, 0.0))


def _router(x, wg, bg, we, be, tm=512):
    t, d = x.shape
    w = jnp.zeros((d, LANES), F32).at[:, :N_EXPERTS].set(we).at[:, N_EXPERTS:N_EXPERTS + N_GROUPS].set(wg)
    bias = jnp.zeros((1, LANES), F32).at[0, :N_EXPERTS].set(be).at[0, N_EXPERTS:N_EXPERTS + N_GROUPS].set(bg)
    tri = np.tril(np.ones((tm, tm), np.float32), -1).astype(BF16)
    out = pl.BlockSpec((tm, LANES), lambda i: (i, 0))
    vec = pl.BlockSpec((1, LANES), lambda i: (0, 0))
    return pl.pallas_call(
        _router_kernel, grid=(t // tm,),
        in_specs=[pl.BlockSpec((tm, d), lambda i: (i, 0)), pl.BlockSpec((d, LANES), lambda i: (0, 0)), vec,
                  pl.BlockSpec((tm, tm), lambda i: (0, 0))],
        out_specs=[out, out, vec],
        out_shape=[jax.ShapeDtypeStruct((t, LANES), jnp.int32), jax.ShapeDtypeStruct((t, LANES), F32),
                   jax.ShapeDtypeStruct((1, LANES), jnp.int32)],
        scratch_shapes=[pltpu.VMEM((1, LANES), F32)],
        compiler_params=_cparams("arbitrary"), name="moe_router",
    )(x, w, bias, tri)


def _expert_work_items(counts, n_blocks):
    end = jnp.cumsum(counts)
    start = end - counts
    first_blk = start // MOE_ROWS
    n_e = jnp.where(counts > 0, (end - 1) // MOE_ROWS - first_blk + 1, 0)
    item_end = jnp.cumsum(n_e)
    item_start = item_end - n_e
    w = jnp.arange(n_blocks + N_EXPERTS, dtype=jnp.int32)
    live = w < item_end[-1]
    wc = jnp.minimum(w, item_end[-1] - 1)
    e_w = jnp.sum((item_end[None, :] <= wc[:, None]).astype(jnp.int32), axis=1)
    blk = first_blk[e_w] + wc - item_start[e_w]
    lo = jnp.where(live, jnp.maximum(start[e_w], blk * MOE_ROWS) - blk * MOE_ROWS, 0)
    hi = jnp.where(live, jnp.minimum(end[e_w], (blk + 1) * MOE_ROWS) - blk * MOE_ROWS, 0)
    first = jnp.concatenate([jnp.ones((1,), jnp.int32), (blk[1:] != blk[:-1]).astype(jnp.int32)])
    return jnp.stack([e_w, blk, lo, hi, first]).astype(jnp.int32)


def _expert_kernel(items_ref, x_ref, w13_ref, w2_ref, o_ref):
    w = pl.program_id(0)
    lo = items_ref[2, w]
    hi = items_ref[3, w]

    @pl.when(hi > lo)
    def _():
        h = jnp.dot(x_ref[...], w13_ref[0], preferred_element_type=F32)
        hg = h[:, :D_FF_E]
        act = hg * _sigmoid(hg) * h[:, D_FF_E:]
        y = jnp.dot(act.astype(BF16), w2_ref[0], preferred_element_type=F32).astype(o_ref.dtype)
        row = lax.broadcasted_iota(jnp.int32, (MOE_ROWS, 1), 0)
        mine = (row >= lo) & (row < hi)

        @pl.when(items_ref[4, w] == 1)
        def _():
            o_ref[...] = jnp.where(mine, y, jnp.zeros_like(y))

        @pl.when(items_ref[4, w] == 0)
        def _():
            o_ref[...] = jnp.where(mine, y, o_ref[...])


def _expert_blocks(xs, items, w13, w2):
    n_rows, d = xs.shape
    grid_spec = pltpu.PrefetchScalarGridSpec(
        num_scalar_prefetch=1, grid=(items.shape[1],),
        in_specs=[pl.BlockSpec((MOE_ROWS, d), lambda w, it: (it[1, w], 0)),
                  pl.BlockSpec((1, d, 2 * D_FF_E), lambda w, it: (it[0, w], 0, 0)),
                  pl.BlockSpec((1, D_FF_E, d), lambda w, it: (it[0, w], 0, 0))],
        out_specs=pl.BlockSpec((MOE_ROWS, d), lambda w, it: (it[1, w], 0)))
    return pl.pallas_call(
        _expert_kernel, grid_spec=grid_spec, out_shape=jax.ShapeDtypeStruct((n_rows, d), BF16),
        compiler_params=_cparams("arbitrary"), name="moe_experts",
    )(items, xs, w13, w2)


def _combine_kernel(x_ref, y_ref, p_ref, g_ref, b_ref, o_ref):
    d = x_ref.shape[1]
    p = p_ref[...]
    ffn = p[:, 0:1] * y_ref[:, :d].astype(F32) + p[:, 1:2] * y_ref[:, d:].astype(F32)
    o_ref[...] = _ln_rows(DEEPNORM_ALPHA * x_ref[...] + ffn, g_ref[...], b_ref[...])


def _combine(x, y_pairs, p, g, b, tm=1024):
    t, d = x.shape
    row = pl.BlockSpec((tm, d), lambda i: (i, 0))
    vec = pl.BlockSpec((1, d), lambda i: (0, 0))
    return pl.pallas_call(
        _combine_kernel, grid=(t // tm,),
        in_specs=[row, pl.BlockSpec((tm, TOP_K * d), lambda i: (i, 0)),
                  pl.BlockSpec((tm, LANES), lambda i: (i, 0)), vec, vec],
        out_specs=row, out_shape=jax.ShapeDtypeStruct((t, d), F32),
        compiler_params=_cparams("parallel"), name="moe_combine_ln2",
    )(x, y_pairs, p, g.reshape(1, d), b.reshape(1, d))


def _hierarchical_moe(x, wg, bg, we, be, w13, w2, g, b):
    t, d = x.shape
    e_lanes, p_lanes, cnt = _router(x, wg, bg, we, be)
    tk = t * TOP_K
    e_k = e_lanes[:, :TOP_K]
    counts = cnt[0, :N_EXPERTS]
    start = jnp.cumsum(counts) - counts
    dest = (start[e_k] + e_lanes[:, TOP_K:2 * TOP_K]).reshape(tk)
    order = jnp.argsort(e_k.reshape(tk)).astype(jnp.int32)
    xs = jnp.take(x.astype(BF16), order // TOP_K, axis=0)
    ys = _expert_blocks(xs, _expert_work_items(counts, tk // MOE_ROWS), w13, w2)
    y_pairs = jnp.take(ys, dest, axis=0).reshape(t, TOP_K * d)
    return _combine(x, y_pairs, p_lanes, g, b)


def _trunk(x, prm):
    b, l, d = x.shape
    t = b * l
    x = _layer_norm(x.reshape(t, d), prm['ln_in_g'], prm['ln_in_b'])
    tc, ts = _fft_twiddles(l)
    for i in range(DEPTH):
        pa, pb, pc, pd = _input_projection(x, prm['w_in'][i])
        y_a = _neighbourhood_attention(pa.reshape(b, l, -1), prm['attn_bias'][i])
        y_b = _short_conv(pb.reshape(b, l, -1), prm['conv_w'][i])
        y_c = _fourier_mix(pc.reshape(b, l, -1), tc, ts)
        y_d = _hgrn2_bidirectional(pd.reshape(b, l, -1), prm['lb'][0, i], prm['lb'][1, i],
                                   prm['hgrn_norm_g'][i], i == 0)
        ys = [y.reshape(t, BR) for y in (y_a, y_b, y_c, y_d)]
        x = _merge(x, ys, prm['w_gate'][i], prm['b_gate'][i], prm['w_br'][i], prm['w_out'][i],
                   prm['ln1_g'][i], prm['ln1_b'][i])
        x = _hierarchical_moe(x, prm['router_g_w'][i], prm['router_g_b'][i], prm['router_e_w'][i],
                              prm['router_e_b'][i], prm['w13'][i], prm['w2'][i], prm['ln2_g'][i], prm['ln2_b'][i])
    return x.reshape(b, l, d)


def kernel(x_prompt, x_sample, ln_in_g, ln_in_b, w_in, na_rpb, conv_w, hgrn_lb, hgrn_norm_g, w_gate, b_gate, w_br,
           w_out, ln1_g, ln1_b, router_g_w, router_g_b, router_e_w, router_e_b, w13, w2, ln2_g, ln2_b):
    lb = jnp.cumsum(jax.nn.softmax(hgrn_lb.astype(F32), axis=1), axis=1)
    lb = lb - lb[:, :1]
    prm = dict(
        ln_in_g=ln_in_g, ln_in_b=ln_in_b,
        w_in=jnp.stack([_fold_channel_dft(w_in[i]) for i in range(DEPTH)]).astype(BF16),
        attn_bias=jnp.stack([_attn_bias_table(na_rpb[i]) for i in range(DEPTH)]),
        conv_w=conv_w, lb=lb, hgrn_norm_g=hgrn_norm_g,
        w_gate=w_gate.astype(BF16), b_gate=b_gate, w_br=w_br.astype(BF16), w_out=w_out.astype(BF16),
        ln1_g=ln1_g, ln1_b=ln1_b, router_g_w=router_g_w, router_g_b=router_g_b,
        router_e_w=router_e_w, router_e_b=router_e_b, w13=w13.astype(BF16), w2=w2.astype(BF16),
        ln2_g=ln2_g, ln2_b=ln2_b)
    return _trunk(x_prompt, prm), _trunk(x_sample, prm)
```

```python
import functools

import numpy as np
import jax
import jax.numpy as jnp
from jax import lax
from jax.experimental import pallas as pl
from jax.experimental.pallas import tpu as pltpu

F32 = jnp.float32
BF16 = jnp.bfloat16
HIGHEST = lax.Precision.HIGHEST

D_MODEL = 1024
DEPTH = 4
GRID_W = 64
BR = 256
N_HEADS = 4
HEAD_DIM = 64
NA_ROWS = 8
NA_COLS = 16
FN_GROUP_DIM = 64
N_GROUPS = 4
EXPERTS_PER_GROUP = 8
N_EXPERTS = N_GROUPS * EXPERTS_PER_GROUP
TOP_K = 2
D_FF_E = 512
DEEPNORM_ALPHA = (2 * DEPTH) ** 0.25
LN_EPS = 1e-5
RMS_EPS = 1e-6
NEG_INF = -1e30

LANES = 128
VMEM_LIMIT_BYTES = 56 * 1024 * 1024
MOE_ROWS = 512
HG_CHUNK = 128
FFT_L1 = 64

COL_QA, COL_KA, COL_VA, COL_XB, COL_GB, COL_GC, COL_XF, COL_QD, COL_FF, COL_FB, COL_VD, COL_GD = range(12)
PROJ_GROUPS = ((0, 3), (3, 6), (6, 8), (8, 13))


def _cparams(*sem):
    return pltpu.CompilerParams(dimension_semantics=sem, vmem_limit_bytes=VMEM_LIMIT_BYTES)


def _sigmoid(x):
    return 1.0 / (1.0 + jnp.exp(-x))


def _split_bf16(x):
    hi = x.astype(BF16)
    return hi, (x - hi.astype(F32)).astype(BF16)


def _dot_split(a_hi, a_lo, b_hi, b_lo):
    dot = functools.partial(jnp.dot, preferred_element_type=F32)
    return dot(a_hi, b_hi) + (dot(a_lo, b_hi) + dot(a_hi, b_lo))


def _ln_rows(x, g, b):
    mu = jnp.mean(x, axis=-1, keepdims=True)
    xc = x - mu
    var = jnp.mean(xc * xc, axis=-1, keepdims=True)
    return xc * lax.rsqrt(var + LN_EPS) * g + b


def _head_of_lane(shape):
    return lax.broadcasted_iota(jnp.int32, shape, len(shape) - 1) // HEAD_DIM


def _head_stack(x):
    head = _head_of_lane(x.shape)
    return jnp.concatenate([jnp.where(head == h, x, jnp.zeros_like(x)) for h in range(N_HEADS)], axis=0)


def _same_head_matrix():
    h = np.arange(BR) // HEAD_DIM
    return (h[:, None] == h[None, :]).astype(np.float32)


def _ln_kernel(x_ref, g_ref, b_ref, o_ref):
    o_ref[...] = _ln_rows(x_ref[...], g_ref[...], b_ref[...])


def _layer_norm(x, g, b, tm=1024):
    t, d = x.shape
    row = pl.BlockSpec((tm, d), lambda i: (i, 0))
    vec = pl.BlockSpec((1, d), lambda i: (0, 0))
    return pl.pallas_call(
        _ln_kernel, grid=(t // tm,), in_specs=[row, vec, vec], out_specs=row,
        out_shape=jax.ShapeDtypeStruct((t, d), F32), compiler_params=_cparams("parallel"), name="ln_in",
    )(x, g.reshape(1, d), b.reshape(1, d))


def _proj_kernel(x_ref, w_ref, pa_ref, pb_ref, pc_ref, pd_ref):
    xb = x_ref[...].astype(BF16)

    def cols(lo, hi):
        return jnp.dot(xb, w_ref[:, lo * BR:hi * BR], preferred_element_type=F32)

    for ref, (lo, hi) in zip((pa_ref, pb_ref, pc_ref, pd_ref), PROJ_GROUPS):
        ref[...] = cols(lo, hi).astype(ref.dtype)


def _fold_channel_dft(w_in):
    cg, sg = _dft_cos_sin(FN_GROUP_DIM)
    eye = np.eye(BR // FN_GROUP_DIM, dtype=np.float32)
    w_xf = w_in[:, COL_XF * BR:(COL_XF + 1) * BR]
    w_re = jnp.dot(w_xf, np.kron(eye, cg), precision=HIGHEST)
    w_im = -jnp.dot(w_xf, np.kron(eye, sg), precision=HIGHEST)
    return jnp.concatenate([w_in[:, :COL_XF * BR], w_re, w_im, w_in[:, (COL_XF + 1) * BR:]], axis=1)


def _input_projection(x, w_in, tm=512):
    t, d = x.shape
    n = w_in.shape[1]
    widths = tuple((hi - lo) * BR for lo, hi in PROJ_GROUPS)
    dtypes = (BF16, F32, F32, F32)
    return pl.pallas_call(
        _proj_kernel, grid=(t // tm,),
        in_specs=[pl.BlockSpec((tm, d), lambda i: (i, 0)), pl.BlockSpec((d, n), lambda i: (0, 0))],
        out_specs=[pl.BlockSpec((tm, w), lambda i: (i, 0)) for w in widths],
        out_shape=[jax.ShapeDtypeStruct((t, w), dt) for w, dt in zip(widths, dtypes)],
        compiler_params=_cparams("parallel"), name="in_proj",
    )(x, w_in)


def _attn_bias_table(rpb):
    c = np.arange(GRID_W)
    win_c0 = np.clip(c - NA_COLS // 2, 0, GRID_W - NA_COLS)
    valid = (c[None, :] >= win_c0[:, None]) & (c[None, :] < win_c0[:, None] + NA_COLS)
    dc = np.clip(c[None, :] - c[:, None], -(NA_COLS - 1), NA_COLS - 1) + NA_COLS - 1
    dr = np.arange(NA_ROWS)[None, :] - np.arange(NA_ROWS)[:, None] + NA_ROWS - 1
    row_hot = (dr[:, :, None] == np.arange(2 * NA_ROWS - 1)).astype(np.float32)
    col_hot = (dc[:, :, None] == np.arange(2 * NA_COLS - 1)).astype(np.float32)
    t = jnp.einsum('oia,hab,qkb->ohqik', row_hot, rpb.astype(F32), col_hot, precision=HIGHEST)
    t = jnp.where(valid[None, None, :, None, :], t, NEG_INF)
    return t.reshape(NA_ROWS, N_HEADS * GRID_W, NA_ROWS * GRID_W)


def _attn_kernel(q_ref, k_ref, v_ref, bias_ref, o_ref, *, n_rows, rows_per_tile):
    i = pl.program_id(1)
    head = _head_of_lane((GRID_W, BR))
    scale = HEAD_DIM ** -0.5
    span = NA_ROWS * GRID_W
    for j in range(rows_per_tile):
        r = i * rows_per_tile + j
        r0 = jnp.clip(r - NA_ROWS // 2, 0, n_rows - NA_ROWS)
        q = q_ref[0, j * GRID_W:(j + 1) * GRID_W, :]
        qs = _head_stack(q)
        start = pl.multiple_of(r0 * GRID_W, GRID_W)
        ks = k_ref[0, pl.ds(start, span), :]
        vs = v_ref[0, pl.ds(start, span), :]
        s = lax.dot_general(qs, ks, (((1,), (1,)), ((), ())), preferred_element_type=F32)
        s = s * scale + bias_ref[r - r0]
        m = jnp.max(s, axis=-1, keepdims=True)
        e = jnp.exp(s - m)
        p = e / jnp.sum(e, axis=-1, keepdims=True)
        o4 = jnp.dot(p.astype(BF16), vs, preferred_element_type=F32)
        o = jnp.zeros((GRID_W, BR), F32)
        for h in range(N_HEADS):
            o = o + jnp.where(head == h, o4[h * GRID_W:(h + 1) * GRID_W, :], 0.0)
        o_ref[0, j * GRID_W:(j + 1) * GRID_W, :] = o.astype(o_ref.dtype)


def _neighbourhood_attention(pa, bias, rows_per_tile=8):
    b, l, _ = pa.shape
    n_rows = l // GRID_W
    tq = rows_per_tile * GRID_W
    kern = functools.partial(_attn_kernel, n_rows=n_rows, rows_per_tile=rows_per_tile)
    return pl.pallas_call(
        kern, grid=(b, l // tq),
        in_specs=[pl.BlockSpec((1, tq, BR), lambda bi, i: (bi, i, 0)),
                  pl.BlockSpec((1, l, BR), lambda bi, i: (bi, 0, 1)),
                  pl.BlockSpec((1, l, BR), lambda bi, i: (bi, 0, 2)),
                  pl.BlockSpec(bias.shape, lambda bi, i: (0, 0, 0))],
        out_specs=pl.BlockSpec((1, tq, BR), lambda bi, i: (bi, i, 0)),
        out_shape=jax.ShapeDtypeStruct((b, l, BR), BF16),
        compiler_params=_cparams("parallel", "arbitrary"), name="nbr_attn",
    )(pa, pa, pa, bias)


def _conv_kernel(xb_ref, gb_ref, gc_ref, xbp_ref, gcp_ref, xbn_ref, gcn_ref, w_ref, o_ref, *, n_tiles):
    i = pl.program_id(1)
    tl = xb_ref.shape[1]
    u = gc_ref[0] * xb_ref[0]
    u_before = jnp.where(i > 0, gcp_ref[0, 7:8, :] * xbp_ref[0, 7:8, :], 0.0)
    u_after = jnp.where(i < n_tiles - 1, gcn_ref[0, 0:1, :] * xbn_ref[0, 0:1, :], 0.0)
    row = lax.broadcasted_iota(jnp.int32, (tl, 1), 0)
    u_prev = jnp.where(row == 0, u_before, pltpu.roll(u, 1, 0))
    u_next = jnp.where(row == tl - 1, u_after, pltpu.roll(u, tl - 1, 0))
    conv = w_ref[0:1, :] * u_prev + w_ref[1:2, :] * u + w_ref[2:3, :] * u_next
    o_ref[0] = (gb_ref[0] * conv).astype(o_ref.dtype)


def _short_conv(pb, conv_w, tl=1024):
    b, l, _ = pb.shape
    n_tiles = l // tl
    per8 = tl // 8
    last8 = l // 8 - 1

    def main(col):
        return pl.BlockSpec((1, tl, BR), lambda bi, i: (bi, i, col))

    def before(col):
        return pl.BlockSpec((1, 8, BR), lambda bi, i: (bi, jnp.maximum(i * per8 - 1, 0), col))

    def after(col):
        return pl.BlockSpec((1, 8, BR), lambda bi, i: (bi, jnp.minimum((i + 1) * per8, last8), col))

    return pl.pallas_call(
        functools.partial(_conv_kernel, n_tiles=n_tiles), grid=(b, n_tiles),
        in_specs=[main(0), main(1), main(2), before(0), before(2), after(0), after(2),
                  pl.BlockSpec(conv_w.shape, lambda bi, i: (0, 0))],
        out_specs=pl.BlockSpec((1, tl, BR), lambda bi, i: (bi, i, 0)),
        out_shape=jax.ShapeDtypeStruct((b, l, BR), BF16),
        compiler_params=_cparams("parallel", "parallel"), name="short_conv",
    )(pb, pb, pb, pb, pb, pb, pb, conv_w)


def _dft_cos_sin(n):
    ang = 2.0 * np.pi * ((np.arange(n)[:, None] * np.arange(n)[None, :]) % n) / n
    return np.cos(ang).astype(np.float32), np.sin(ang).astype(np.float32)


def _fft_major_kernel(g_ref, m_ref, tc_ref, ts_ref, br_ref, bi_ref):
    c_hi, c_lo, s_hi, s_lo = m_ref[0], m_ref[1], m_ref[2], m_ref[3]
    for s in range(8):
        g = g_ref[0, :, s, :]
        g_hi, g_lo = _split_bf16(g)
        w_hi, w_lo = _split_bf16(jnp.concatenate([g[:, BR:], -g[:, :BR]], axis=1))
        a = _dot_split(c_hi, c_lo, g_hi, g_lo) + _dot_split(s_hi, s_lo, w_hi, w_lo)
        ar, ai = a[:, :BR], a[:, BR:]
        tc = tc_ref[:, s, :]
        ts = ts_ref[:, s, :]
        br_ref[0, :, s, :] = ar * tc + ai * ts
        bi_ref[0, :, s, :] = ai * tc - ar * ts


def _fft_minor_kernel(br_ref, bi_ref, m_ref, o_ref, *, scale):
    c_hi, c_lo, s_hi, s_lo = m_ref[0], m_ref[1], m_ref[2], m_ref[3]
    for j in range(8):
        y = (_dot_split(c_hi, c_lo, *_split_bf16(br_ref[0, j]))
             + _dot_split(s_hi, s_lo, *_split_bf16(bi_ref[0, j])))
        o_ref[0, :, j, :] = (y * scale).astype(o_ref.dtype)


def _fft_twiddles(l):
    l1 = FFT_L1
    l2 = l // l1
    idx = (jnp.arange(l1, dtype=jnp.int32)[:, None] * jnp.arange(l2, dtype=jnp.int32)[None, :]) % l
    ang = idx.astype(F32) * (2.0 * np.pi / l)
    tc = jnp.broadcast_to(jnp.cos(ang)[:, :, None], (l1, l2, BR))
    ts = jnp.broadcast_to(jnp.sin(ang)[:, :, None], (l1, l2, BR))
    return tc, ts


def _fourier_mix(pc, tc, ts):
    b, l, _ = pc.shape
    l1 = FFT_L1
    l2 = l // l1

    def split_parts(n):
        parts = []
        for m in _dft_cos_sin(n):
            hi = jnp.asarray(m).astype(BF16)
            parts += [hi, (jnp.asarray(m) - hi.astype(F32)).astype(BF16)]
        return jnp.stack(parts)

    def strip(width):
        return pl.BlockSpec((1, l1, 8, width), lambda bi, i: (bi, 0, i, 0))

    tspec = pl.BlockSpec((l1, 8, BR), lambda bi, i: (0, i, 0))
    mspec = pl.BlockSpec((4, l1, l1), lambda bi, i: (0, 0, 0))
    br, bi_ = pl.pallas_call(
        _fft_major_kernel, grid=(b, l2 // 8),
        in_specs=[strip(2 * BR), mspec, tspec, tspec], out_specs=[strip(BR), strip(BR)],
        out_shape=[jax.ShapeDtypeStruct((b, l1, l2, BR), F32)] * 2,
        compiler_params=_cparams("parallel", "parallel"), name="fft_major",
    )(pc.reshape(b, l1, l2, 2 * BR), split_parts(l1), tc, ts)

    bspec = pl.BlockSpec((1, 8, l2, BR), lambda bi, i: (bi, i, 0, 0))
    m2spec = pl.BlockSpec((4, l2, l2), lambda bi, i: (0, 0, 0))
    scale = 1.0 / float(np.sqrt(l * FN_GROUP_DIM))
    y = pl.pallas_call(
        functools.partial(_fft_minor_kernel, scale=scale), grid=(b, l1 // 8),
        in_specs=[bspec, bspec, m2spec],
        out_specs=pl.BlockSpec((1, l2, 8, BR), lambda bi, i: (bi, 0, i, 0)),
        out_shape=jax.ShapeDtypeStruct((b, l2, l1, BR), F32),
        compiler_params=_cparams("parallel", "parallel"), name="fft_minor",
    )(br, bi_, split_parts(l2))
    return y.reshape(b, l, BR)


def _hgrn_level_masks():
    t = np.arange(HG_CHUNK)
    out = []
    half = 8
    while 2 * half < HG_CHUNK:
        same = (t[:, None] // (2 * half)) == (t[None, :] // (2 * half))
        out.append(np.tile(same, (1, N_HEADS)).astype(np.float32))
        half *= 2
    return np.stack(out)


def _hgrn_kernel(qd_ref, fl_ref, vd_ref, lb_ref, mask_ref, bd_ref, o_ref, st_ref, *, first_layer, chunks_per_tile,
                 rev):
    c = HG_CHUNK

    @pl.when(pl.program_id(1) == 0)
    def _():
        st_ref[...] = jnp.zeros_like(st_ref)

    bd = bd_ref[...]
    bd16 = bd.astype(BF16)
    lb = lb_ref[...]
    row = lax.broadcasted_iota(jnp.int32, (c, 1), 0)
    sub = row % 8
    nt = (((1,), (1,)), ((), ()))

    def roll8(x, j):
        return pltpu.roll(x.reshape(c // 8, 8, BR), j % 8, 1).reshape(c, BR)

    def earlier(x, j):
        return roll8(x, -j if rev else j)

    def later(x, j):
        return roll8(x, j if rev else -j)

    def has_earlier(j):
        return (sub + j <= 7) if rev else (sub >= j)

    def has_later(j):
        return (sub >= j) if rev else (sub + j <= 7)

    for ci in (reversed(range(chunks_per_tile)) if rev else range(chunks_per_tile)):
        sl = slice(ci * c, (ci + 1) * c)
        qd = qd_ref[0, sl, :]
        fl = fl_ref[0, sl, :]
        v = vd_ref[0, sl, :]
        q = qd * _sigmoid(qd)
        if first_layer:
            f = _sigmoid(fl)
            k = _sigmoid(-fl)
        else:
            f = lb + (1.0 - lb) * _sigmoid(fl)
            k = (1.0 - lb) * _sigmoid(-fl)

        o = jnp.dot((q * k).astype(BF16), bd16, preferred_element_type=F32) * v
        g = f
        for d in range(1, 8):
            if d > 1:
                g = g * earlier(f, d - 1)
            p = jnp.where(has_earlier(d), q * g * earlier(k, d), 0.0)
            o = o + jnp.dot(p.astype(BF16), bd16, preferred_element_type=F32) * earlier(v, d)

        qf = f
        for j in (1, 2, 4):
            qf = qf * jnp.where(has_earlier(j), earlier(qf, j), 1.0)
        kf = jnp.where(has_later(1), later(f, 1), 1.0)
        for j in (1, 2, 4):
            kf = kf * jnp.where(has_later(j), later(kf, j), 1.0)

        scores = None
        half = 8
        level = 0
        while half < c:
            late = ((row // half) % 2) == (0 if rev else 1)
            qb = jnp.where(late, q * qf, 0.0).astype(BF16)
            kb = jnp.where(late, 0.0, k * kf).astype(BF16)
            x = lax.dot_general(qb, _head_stack(kb), nt, preferred_element_type=F32)
            if 2 * half < c:
                x = x * mask_ref[level]
            scores = x if scores is None else scores + x
            blocks = c // (2 * half)
            qf3 = qf.reshape(blocks, 2 * half, BR)
            end_early = half if rev else half - 1
            end_late = 0 if rev else 2 * half - 1
            tot_early = jnp.broadcast_to(qf3[:, end_early:end_early + 1, :], qf3.shape).reshape(c, BR)
            tot_late = jnp.broadcast_to(qf3[:, end_late:end_late + 1, :], qf3.shape).reshape(c, BR)
            qf = qf * jnp.where(late, tot_early, 1.0)
            kf = kf * jnp.where(late, 1.0, tot_late)
            half *= 2
            level += 1

        o = o + jnp.dot(scores.astype(BF16), _head_stack(v.astype(BF16)), preferred_element_type=F32)

        st = st_ref[...]
        o = o + lax.dot_general((q * qf).astype(BF16), st.astype(BF16), nt, preferred_element_type=F32)
        o_ref[0, sl, :] = o
        upd = jnp.dot(v.T.astype(BF16), (k * kf).astype(BF16), preferred_element_type=F32)
        last = 0 if rev else c - 1
        st_ref[...] = st * qf[last:last + 1, :] + upd * bd


def _hgrn_scan(qd, fl, vd, cols, lb, first_layer, rev, chunks_per_tile=2):
    b, l, _ = qd.shape
    tile = HG_CHUNK * chunks_per_tile
    n_tiles = l // tile
    masks = _hgrn_level_masks()
    bd = _same_head_matrix()

    def tile_index(i):
        return n_tiles - 1 - i if rev else i

    def seq(col):
        return pl.BlockSpec((1, tile, BR), lambda bi, i: (bi, tile_index(i), col))

    return pl.pallas_call(
        functools.partial(_hgrn_kernel, first_layer=first_layer, chunks_per_tile=chunks_per_tile, rev=rev),
        grid=(b, n_tiles),
        in_specs=[seq(cols[0]), seq(cols[1]), seq(cols[2]),
                  pl.BlockSpec((1, BR), lambda bi, i: (0, 0)),
                  pl.BlockSpec(masks.shape, lambda bi, i: (0, 0, 0)),
                  pl.BlockSpec((BR, BR), lambda bi, i: (0, 0))],
        out_specs=pl.BlockSpec((1, tile, BR), lambda bi, i: (bi, tile_index(i), 0)),
        out_shape=jax.ShapeDtypeStruct((b, l, BR), F32),
        scratch_shapes=[pltpu.VMEM((BR, BR), F32)],
        compiler_params=_cparams("parallel", "arbitrary"),
        name="hgrn_bwd" if rev else "hgrn_fwd",
    )(qd, fl, vd, lb.reshape(1, BR), masks, bd)


def _hgrn_out_kernel(of_ref, ob_ref, gd_ref, g_ref, bd_ref, o_ref):
    o = of_ref[...] + ob_ref[...]
    ms = jnp.dot(o * o, bd_ref[...], precision=HIGHEST, preferred_element_type=F32) * (1.0 / HEAD_DIM)
    gd = gd_ref[...]
    o_ref[...] = (o * lax.rsqrt(ms + RMS_EPS) * g_ref[...] * (gd * _sigmoid(gd))).astype(o_ref.dtype)


def _hgrn_output(o_f, o_b, pd, norm_g, tm=1024):
    t = o_f.shape[0]
    row = pl.BlockSpec((tm, BR), lambda i: (i, 0))
    return pl.pallas_call(
        _hgrn_out_kernel, grid=(t // tm,),
        in_specs=[row, row, pl.BlockSpec((tm, BR), lambda i: (i, 4)),
                  pl.BlockSpec((1, BR), lambda i: (0, 0)), pl.BlockSpec((BR, BR), lambda i: (0, 0))],
        out_specs=row, out_shape=jax.ShapeDtypeStruct((t, BR), BF16),
        compiler_params=_cparams("parallel"), name="hgrn_out",
    )(o_f, o_b, pd, jnp.tile(norm_g, N_HEADS).reshape(1, BR), _same_head_matrix())


def _hgrn2_bidirectional(pd, lb_f, lb_b, norm_g, first_layer):
    b, l, _ = pd.shape
    o_f = _hgrn_scan(pd, pd, pd, (0, 1, 3), lb_f, first_layer, rev=False)
    o_b = _hgrn_scan(pd, pd, pd, (0, 2, 3), lb_b, first_layer, rev=True)
    y = _hgrn_output(o_f.reshape(b * l, BR), o_b.reshape(b * l, BR), pd.reshape(b * l, 5 * BR), norm_g)
    return y.reshape(b, l, BR)


def _merge_kernel(x_ref, ya_ref, yb_ref, yc_ref, yd_ref, wg_ref, bg_ref, wbr_ref, wo_ref, g_ref, b_ref, o_ref,
                  o16_ref):
    x = x_ref[...]
    xb = x.astype(BF16)
    merged = None
    for i, y_ref in enumerate((ya_ref, yb_ref, yc_ref, yd_ref)):
        gate = _sigmoid(jnp.dot(xb, wg_ref[i], preferred_element_type=F32) + bg_ref[i])
        term = gate * jnp.dot(y_ref[...].astype(BF16), wbr_ref[i], preferred_element_type=F32)
        merged = term if merged is None else merged + term
    mix = jnp.dot(merged.astype(BF16), wo_ref[...], preferred_element_type=F32)
    out = _ln_rows(DEEPNORM_ALPHA * x + mix, g_ref[...], b_ref[...])
    o_ref[...] = out
    o16_ref[...] = out.astype(BF16)


def _merge(x, ys, w_gate, b_gate, w_br, w_out, g, b, tm=512):
    t, d = x.shape
    row = pl.BlockSpec((tm, d), lambda i: (i, 0))
    yrow = pl.BlockSpec((tm, BR), lambda i: (i, 0))
    vec = pl.BlockSpec((1, d), lambda i: (0, 0))

    def whole(a):
        return pl.BlockSpec(a.shape, lambda i: (0,) * a.ndim)

    bg = b_gate.reshape(b_gate.shape[0], 1, d)
    return pl.pallas_call(
        _merge_kernel, grid=(t // tm,),
        in_specs=[row, yrow, yrow, yrow, yrow, whole(w_gate), whole(bg), whole(w_br), whole(w_out), vec, vec],
        out_specs=[row, row], out_shape=[jax.ShapeDtypeStruct((t, d), F32), jax.ShapeDtypeStruct((t, d), BF16)],
        compiler_params=_cparams("parallel"), name="merge_ln1",
    )(x, *ys, w_gate, bg, w_br, w_out, g.reshape(1, d), b.reshape(1, d))


def _router_kernel(x_ref, whi_ref, wlo_ref, b_ref, tri_ref, e_ref, p_ref, cnt_ref, run_ref):
    @pl.when(pl.program_id(0) == 0)
    def _():
        run_ref[...] = jnp.zeros_like(run_ref)

    x_hi, x_lo = _split_bf16(x_ref[...])
    logits = _dot_split(x_hi, x_lo, whi_ref[...], wlo_ref[...]) + b_ref[...]
    lane = lax.broadcasted_iota(jnp.int32, logits.shape, 1)
    big = jnp.int32(LANES)
    is_group = (lane >= N_EXPERTS) & (lane < N_EXPERTS + N_GROUPS)
    lg = jnp.where(is_group, logits, -jnp.inf)
    mg = jnp.max(lg, axis=-1, keepdims=True)
    g_sel = jnp.min(jnp.where(lg == mg, lane, big), axis=-1, keepdims=True) - N_EXPERTS
    p_g = 1.0 / jnp.sum(jnp.exp(lg - mg), axis=-1, keepdims=True)
    in_group = (lane < N_EXPERTS) & ((lane // EXPERTS_PER_GROUP) == g_sel)
    le = jnp.where(in_group, logits, -jnp.inf)
    v1 = jnp.max(le, axis=-1, keepdims=True)
    i1 = jnp.min(jnp.where(le == v1, lane, big), axis=-1, keepdims=True)
    le2 = jnp.where(lane == i1, -jnp.inf, le)
    v2 = jnp.max(le2, axis=-1, keepdims=True)
    i2 = jnp.min(jnp.where(le2 == v2, lane, big), axis=-1, keepdims=True)
    e2 = jnp.exp(v2 - v1)
    w1 = p_g / (1.0 + e2)
    w2 = p_g * e2 / (1.0 + e2)
    hot1 = lane == i1
    hot2 = lane == i2
    hot = (hot1 | hot2).astype(BF16)
    before = jnp.dot(tri_ref[...], hot, preferred_element_type=F32) + run_ref[...]
    rank1 = jnp.sum(jnp.where(hot1, before, 0.0), axis=-1, keepdims=True).astype(jnp.int32)
    rank2 = jnp.sum(jnp.where(hot2, before, 0.0), axis=-1, keepdims=True).astype(jnp.int32)
    run = run_ref[...] + jnp.sum(hot.astype(F32), axis=0, keepdims=True)
    run_ref[...] = run
    cnt_ref[...] = run.astype(jnp.int32)
    e_ref[...] = jnp.where(lane == 0, i1, jnp.where(lane == 1, i2, jnp.where(lane == 2, rank1,
                                                                               jnp.where(lane == 3, rank2, 0))))
    p_ref[...] = jnp.where(lane == 0, w1, jnp.where(lane == 1, w2, 0.0))


def _router(x, wg, bg, we, be, tm=512):
    t, d = x.shape
    w = jnp.zeros((d, LANES), F32).at[:, :N_EXPERTS].set(we).at[:, N_EXPERTS:N_EXPERTS + N_GROUPS].set(wg)
    bias = jnp.zeros((1, LANES), F32).at[0, :N_EXPERTS].set(be).at[0, N_EXPERTS:N_EXPERTS + N_GROUPS].set(bg)
    w_hi, w_lo = _split_bf16(w)
    tri = np.tril(np.ones((tm, tm), np.float32), -1).astype(BF16)
    out = pl.BlockSpec((tm, LANES), lambda i: (i, 0))
    vec = pl.BlockSpec((1, LANES), lambda i: (0, 0))
    wspec = pl.BlockSpec((d, LANES), lambda i: (0, 0))
    return pl.pallas_call(
        _router_kernel, grid=(t // tm,),
        in_specs=[pl.BlockSpec((tm, d), lambda i: (i, 0)), wspec, wspec, vec,
                  pl.BlockSpec((tm, tm), lambda i: (0, 0))],
        out_specs=[out, out, vec],
        out_shape=[jax.ShapeDtypeStruct((t, LANES), jnp.int32), jax.ShapeDtypeStruct((t, LANES), F32),
                   jax.ShapeDtypeStruct((1, LANES), jnp.int32)],
        scratch_shapes=[pltpu.VMEM((1, LANES), F32)],
        compiler_params=_cparams("arbitrary"), name="moe_router",
    )(x, w_hi, w_lo, bias, tri)


def _expert_work_items(counts, n_blocks):
    end = jnp.cumsum(counts)
    start = end - counts
    first_blk = start // MOE_ROWS
    n_e = jnp.where(counts > 0, (end - 1) // MOE_ROWS - first_blk + 1, 0)
    item_end = jnp.cumsum(n_e)
    item_start = item_end - n_e
    w = jnp.arange(n_blocks + N_EXPERTS, dtype=jnp.int32)
    live = w < item_end[-1]
    wc = jnp.minimum(w, item_end[-1] - 1)
    e_w = jnp.sum((item_end[None, :] <= wc[:, None]).astype(jnp.int32), axis=1)
    blk = first_blk[e_w] + wc - item_start[e_w]
    lo = jnp.where(live, jnp.maximum(start[e_w], blk * MOE_ROWS) - blk * MOE_ROWS, 0)
    hi = jnp.where(live, jnp.minimum(end[e_w], (blk + 1) * MOE_ROWS) - blk * MOE_ROWS, 0)
    first = jnp.concatenate([jnp.ones((1,), jnp.int32), (blk[1:] != blk[:-1]).astype(jnp.int32)])
    return jnp.stack([e_w, blk, lo, hi, first]).astype(jnp.int32)


def _expert_kernel(items_ref, x_ref, w13_ref, w2_ref, o_ref):
    w = pl.program_id(0)
    lo = items_ref[2, w]
    hi = items_ref[3, w]

    @pl.when(hi > lo)
    def _():
        h = jnp.dot(x_ref[...], w13_ref[0], preferred_element_type=F32)
        hg = h[:, :D_FF_E]
        act = hg * _sigmoid(hg) * h[:, D_FF_E:]
        y = jnp.dot(act.astype(BF16), w2_ref[0], preferred_element_type=F32).astype(o_ref.dtype)
        row = lax.broadcasted_iota(jnp.int32, (MOE_ROWS, 1), 0)
        mine = (row >= lo) & (row < hi)

        @pl.when(items_ref[4, w] == 1)
        def _():
            o_ref[...] = jnp.where(mine, y, jnp.zeros_like(y))

        @pl.when(items_ref[4, w] == 0)
        def _():
            o_ref[...] = jnp.where(mine, y, o_ref[...])


def _expert_blocks(xs, items, w13, w2):
    n_rows, d = xs.shape
    grid_spec = pltpu.PrefetchScalarGridSpec(
        num_scalar_prefetch=1, grid=(items.shape[1],),
        in_specs=[pl.BlockSpec((MOE_ROWS, d), lambda w, it: (it[1, w], 0)),
                  pl.BlockSpec((1, d, 2 * D_FF_E), lambda w, it: (it[0, w], 0, 0)),
                  pl.BlockSpec((1, D_FF_E, d), lambda w, it: (it[0, w], 0, 0))],
        out_specs=pl.BlockSpec((MOE_ROWS, d), lambda w, it: (it[1, w], 0)))
    return pl.pallas_call(
        _expert_kernel, grid_spec=grid_spec, out_shape=jax.ShapeDtypeStruct((n_rows, d), BF16),
        compiler_params=_cparams("arbitrary"), name="moe_experts",
    )(items, xs, w13, w2)


def _combine_kernel(x_ref, y0_ref, y1_ref, p_ref, g_ref, b_ref, o_ref):
    p = p_ref[...]
    ffn = p[:, 0:1] * y0_ref[...].astype(F32) + p[:, 1:2] * y1_ref[...].astype(F32)
    o_ref[...] = _ln_rows(DEEPNORM_ALPHA * x_ref[...] + ffn, g_ref[...], b_ref[...])


def _combine(x, y0, y1, p, g, b, tm=1024):
    t, d = x.shape
    row = pl.BlockSpec((tm, d), lambda i: (i, 0))
    vec = pl.BlockSpec((1, d), lambda i: (0, 0))
    return pl.pallas_call(
        _combine_kernel, grid=(t // tm,),
        in_specs=[row, row, row, pl.BlockSpec((tm, LANES), lambda i: (i, 0)), vec, vec],
        out_specs=row, out_shape=jax.ShapeDtypeStruct((t, d), F32),
        compiler_params=_cparams("parallel"), name="moe_combine_ln2",
    )(x, y0, y1, p, g.reshape(1, d), b.reshape(1, d))


def _hierarchical_moe(x, x16, wg, bg, we, be, w13, w2, g, b):
    t, d = x.shape
    e_lanes, p_lanes, cnt = _router(x, wg, bg, we, be)
    tk = t * TOP_K
    e_k = e_lanes[:, :TOP_K]
    counts = cnt[0, :N_EXPERTS]
    start = jnp.cumsum(counts) - counts
    dest = start[e_k] + e_lanes[:, TOP_K:2 * TOP_K]
    order = jnp.argsort(e_k.reshape(tk)).astype(jnp.int32)
    xs = jnp.take(x16, order // TOP_K, axis=0, mode='clip')
    ys = _expert_blocks(xs, _expert_work_items(counts, tk // MOE_ROWS), w13, w2)
    y0 = jnp.take(ys, dest[:, 0], axis=0, mode='clip')
    y1 = jnp.take(ys, dest[:, 1], axis=0, mode='clip')
    return _combine(x, y0, y1, p_lanes, g, b)


def _trunk(x, prm):
    b, l, d = x.shape
    t = b * l
    x = _layer_norm(x.reshape(t, d), prm['ln_in_g'], prm['ln_in_b'])
    tc, ts = _fft_twiddles(l)
    for i in range(DEPTH):
        pa, pb, pc, pd = _input_projection(x, prm['w_in'][i])
        y_a = _neighbourhood_attention(pa.reshape(b, l, -1), prm['attn_bias'][i])
        y_b = _short_conv(pb.reshape(b, l, -1), prm['conv_w'][i])
        y_c = _fourier_mix(pc.reshape(b, l, -1), tc, ts)
        y_d = _hgrn2_bidirectional(pd.reshape(b, l, -1), prm['lb'][0, i], prm['lb'][1, i],
                                   prm['hgrn_norm_g'][i], i == 0)
        ys = [y.reshape(t, BR) for y in (y_a, y_b, y_c, y_d)]
        x, x16 = _merge(x, ys, prm['w_gate'][i], prm['b_gate'][i], prm['w_br'][i], prm['w_out'][i],
                        prm['ln1_g'][i], prm['ln1_b'][i])
        x = _hierarchical_moe(x, x16, prm['router_g_w'][i], prm['router_g_b'][i], prm['router_e_w'][i],
                              prm['router_e_b'][i], prm['w13'][i], prm['w2'][i], prm['ln2_g'][i], prm['ln2_b'][i])
    return x.reshape(b, l, d)


def kernel(x_prompt, x_sample, ln_in_g, ln_in_b, w_in, na_rpb, conv_w, hgrn_lb, hgrn_norm_g, w_gate, b_gate, w_br,
           w_out, ln1_g, ln1_b, router_g_w, router_g_b, router_e_w, router_e_b, w13, w2, ln2_g, ln2_b):
    lb = jnp.cumsum(jax.nn.softmax(hgrn_lb.astype(F32), axis=1), axis=1)
    lb = lb - lb[:, :1]
    prm = dict(
        ln_in_g=ln_in_g, ln_in_b=ln_in_b,
        w_in=jnp.stack([_fold_channel_dft(w_in[i]) for i in range(DEPTH)]).astype(BF16),
        attn_bias=jnp.stack([_attn_bias_table(na_rpb[i]) for i in range(DEPTH)]),
        conv_w=conv_w, lb=lb, hgrn_norm_g=hgrn_norm_g,
        w_gate=w_gate.astype(BF16), b_gate=b_gate, w_br=w_br.astype(BF16), w_out=w_out.astype(BF16),
        ln1_g=ln1_g, ln1_b=ln1_b, router_g_w=router_g_w, router_g_b=router_g_b,
        router_e_w=router_e_w, router_e_b=router_e_b, w13=w13.astype(BF16), w2=w2.astype(BF16),
        ln2_g=ln2_g, ln2_b=ln2_b)
    return _trunk(x_prompt, prm), _trunk(x_sample, prm)
```

```python
import functools

import numpy as np
import jax
import jax.numpy as jnp
from jax import lax
from jax.experimental import pallas as pl
from jax.experimental.pallas import tpu as pltpu

F32 = jnp.float32
BF16 = jnp.bfloat16
HIGHEST = lax.Precision.HIGHEST

D_MODEL = 1024
DEPTH = 4
GRID_W = 64
BR = 256
N_HEADS = 4
HEAD_DIM = 64
NA_ROWS = 8
NA_COLS = 16
FN_GROUP_DIM = 64
N_GROUPS = 4
EXPERTS_PER_GROUP = 8
N_EXPERTS = N_GROUPS * EXPERTS_PER_GROUP
TOP_K = 2
D_FF_E = 512
DEEPNORM_ALPHA = (2 * DEPTH) ** 0.25
LN_EPS = 1e-5
RMS_EPS = 1e-6
NEG_INF = -1e30

LANES = 128
VMEM_LIMIT_BYTES = 56 * 1024 * 1024
MOE_ROWS = 512
HG_CHUNK = 128
FFT_L1 = 64

COL_QA, COL_KA, COL_VA, COL_XB, COL_GB, COL_GC, COL_XF, COL_QD, COL_FF, COL_FB, COL_VD, COL_GD = range(12)
PROJ_GROUPS = ((0, 3), (3, 6), (6, 8), (8, 13))


def _cparams(*sem):
    return pltpu.CompilerParams(dimension_semantics=sem, vmem_limit_bytes=VMEM_LIMIT_BYTES)


def _sigmoid(x):
    return 1.0 / (1.0 + jnp.exp(-x))


def _split_bf16(x):
    hi = x.astype(BF16)
    return hi, (x - hi.astype(F32)).astype(BF16)


def _dot_split(a_hi, a_lo, b_hi, b_lo):
    dot = functools.partial(jnp.dot, preferred_element_type=F32)
    return dot(a_hi, b_hi) + (dot(a_lo, b_hi) + dot(a_hi, b_lo))


def _ln_rows(x, g, b):
    mu = jnp.mean(x, axis=-1, keepdims=True)
    xc = x - mu
    var = jnp.mean(xc * xc, axis=-1, keepdims=True)
    return xc * lax.rsqrt(var + LN_EPS) * g + b


def _head_of_lane(shape):
    return lax.broadcasted_iota(jnp.int32, shape, len(shape) - 1) // HEAD_DIM


def _head_stack(x):
    head = _head_of_lane(x.shape)
    return jnp.concatenate([jnp.where(head == h, x, jnp.zeros_like(x)) for h in range(N_HEADS)], axis=0)


def _same_head_matrix():
    h = np.arange(BR) // HEAD_DIM
    return (h[:, None] == h[None, :]).astype(np.float32)


def _ln_kernel(x_ref, g_ref, b_ref, o_ref):
    o_ref[...] = _ln_rows(x_ref[...], g_ref[...], b_ref[...])


def _layer_norm(x, g, b, tm=1024):
    t, d = x.shape
    row = pl.BlockSpec((tm, d), lambda i: (i, 0))
    vec = pl.BlockSpec((1, d), lambda i: (0, 0))
    return pl.pallas_call(
        _ln_kernel, grid=(t // tm,), in_specs=[row, vec, vec], out_specs=row,
        out_shape=jax.ShapeDtypeStruct((t, d), F32), compiler_params=_cparams("parallel"), name="ln_in",
    )(x, g.reshape(1, d), b.reshape(1, d))


def _proj_kernel(x_ref, w_ref, pa_ref, pb_ref, pc_ref, pd_ref):
    xb = x_ref[...].astype(BF16)

    def cols(lo, hi):
        return jnp.dot(xb, w_ref[:, lo * BR:hi * BR], preferred_element_type=F32)

    for ref, (lo, hi) in zip((pa_ref, pb_ref, pc_ref, pd_ref), PROJ_GROUPS):
        ref[...] = cols(lo, hi).astype(ref.dtype)


def _fold_channel_dft(w_in):
    cg, sg = _dft_cos_sin(FN_GROUP_DIM)
    eye = np.eye(BR // FN_GROUP_DIM, dtype=np.float32)
    w_xf = w_in[:, COL_XF * BR:(COL_XF + 1) * BR]
    w_re = jnp.dot(w_xf, np.kron(eye, cg), precision=HIGHEST)
    w_im = -jnp.dot(w_xf, np.kron(eye, sg), precision=HIGHEST)
    return jnp.concatenate([w_in[:, :COL_XF * BR], w_re, w_im, w_in[:, (COL_XF + 1) * BR:]], axis=1)


def _input_projection(x, w_in, tm=1024):
    t, d = x.shape
    n = w_in.shape[1]
    widths = tuple((hi - lo) * BR for lo, hi in PROJ_GROUPS)
    dtypes = (BF16, F32, F32, F32)
    return pl.pallas_call(
        _proj_kernel, grid=(t // tm,),
        in_specs=[pl.BlockSpec((tm, d), lambda i: (i, 0)), pl.BlockSpec((d, n), lambda i: (0, 0))],
        out_specs=[pl.BlockSpec((tm, w), lambda i: (i, 0)) for w in widths],
        out_shape=[jax.ShapeDtypeStruct((t, w), dt) for w, dt in zip(widths, dtypes)],
        compiler_params=_cparams("parallel"), name="in_proj",
    )(x, w_in)


def _attn_bias_table(rpb):
    c = np.arange(GRID_W)
    win_c0 = np.clip(c - NA_COLS // 2, 0, GRID_W - NA_COLS)
    valid = (c[None, :] >= win_c0[:, None]) & (c[None, :] < win_c0[:, None] + NA_COLS)
    dc = np.clip(c[None, :] - c[:, None], -(NA_COLS - 1), NA_COLS - 1) + NA_COLS - 1
    dr = np.arange(NA_ROWS)[None, :] - np.arange(NA_ROWS)[:, None] + NA_ROWS - 1
    row_hot = (dr[:, :, None] == np.arange(2 * NA_ROWS - 1)).astype(np.float32)
    col_hot = (dc[:, :, None] == np.arange(2 * NA_COLS - 1)).astype(np.float32)
    t = jnp.einsum('oia,hab,qkb->ohqik', row_hot, rpb.astype(F32), col_hot, precision=HIGHEST)
    t = jnp.where(valid[None, None, :, None, :], t, NEG_INF)
    return t.reshape(NA_ROWS, N_HEADS * GRID_W, NA_ROWS * GRID_W)


def _attn_kernel(q_ref, k_ref, v_ref, bias_ref, o_ref, *, n_rows, rows_per_tile):
    i = pl.program_id(1)
    head = _head_of_lane((GRID_W, BR))
    scale = HEAD_DIM ** -0.5
    span = NA_ROWS * GRID_W
    for j in range(rows_per_tile):
        r = i * rows_per_tile + j
        r0 = jnp.clip(r - NA_ROWS // 2, 0, n_rows - NA_ROWS)
        q = q_ref[0, j * GRID_W:(j + 1) * GRID_W, :]
        qs = _head_stack(q)
        start = pl.multiple_of(r0 * GRID_W, GRID_W)
        ks = k_ref[0, pl.ds(start, span), :]
        vs = v_ref[0, pl.ds(start, span), :]
        s = lax.dot_general(qs, ks, (((1,), (1,)), ((), ())), preferred_element_type=F32)
        s = s * scale + bias_ref[r - r0]
        m = jnp.max(s, axis=-1, keepdims=True)
        e = jnp.exp(s - m)
        p = e / jnp.sum(e, axis=-1, keepdims=True)
        o4 = jnp.dot(p.astype(BF16), vs, preferred_element_type=F32)
        o = jnp.zeros((GRID_W, BR), F32)
        for h in range(N_HEADS):
            o = o + jnp.where(head == h, o4[h * GRID_W:(h + 1) * GRID_W, :], 0.0)
        o_ref[0, j * GRID_W:(j + 1) * GRID_W, :] = o.astype(o_ref.dtype)


def _neighbourhood_attention(pa, bias, rows_per_tile=8):
    b, l, _ = pa.shape
    n_rows = l // GRID_W
    tq = rows_per_tile * GRID_W
    kern = functools.partial(_attn_kernel, n_rows=n_rows, rows_per_tile=rows_per_tile)
    return pl.pallas_call(
        kern, grid=(b, l // tq),
        in_specs=[pl.BlockSpec((1, tq, BR), lambda bi, i: (bi, i, 0)),
                  pl.BlockSpec((1, l, BR), lambda bi, i: (bi, 0, 1)),
                  pl.BlockSpec((1, l, BR), lambda bi, i: (bi, 0, 2)),
                  pl.BlockSpec(bias.shape, lambda bi, i: (0, 0, 0))],
        out_specs=pl.BlockSpec((1, tq, BR), lambda bi, i: (bi, i, 0)),
        out_shape=jax.ShapeDtypeStruct((b, l, BR), BF16),
        compiler_params=_cparams("parallel", "arbitrary"), name="nbr_attn",
    )(pa, pa, pa, bias)


def _conv_kernel(xb_ref, gb_ref, gc_ref, xbp_ref, gcp_ref, xbn_ref, gcn_ref, w_ref, o_ref, *, n_tiles):
    i = pl.program_id(1)
    tl = xb_ref.shape[1]
    u = gc_ref[0] * xb_ref[0]
    u_before = jnp.where(i > 0, gcp_ref[0, 7:8, :] * xbp_ref[0, 7:8, :], 0.0)
    u_after = jnp.where(i < n_tiles - 1, gcn_ref[0, 0:1, :] * xbn_ref[0, 0:1, :], 0.0)
    row = lax.broadcasted_iota(jnp.int32, (tl, 1), 0)
    u_prev = jnp.where(row == 0, u_before, pltpu.roll(u, 1, 0))
    u_next = jnp.where(row == tl - 1, u_after, pltpu.roll(u, tl - 1, 0))
    conv = w_ref[0:1, :] * u_prev + w_ref[1:2, :] * u + w_ref[2:3, :] * u_next
    o_ref[0] = (gb_ref[0] * conv).astype(o_ref.dtype)


def _short_conv(pb, conv_w, tl=1024):
    b, l, _ = pb.shape
    n_tiles = l // tl
    per8 = tl // 8
    last8 = l // 8 - 1

    def main(col):
        return pl.BlockSpec((1, tl, BR), lambda bi, i: (bi, i, col))

    def before(col):
        return pl.BlockSpec((1, 8, BR), lambda bi, i: (bi, jnp.maximum(i * per8 - 1, 0), col))

    def after(col):
        return pl.BlockSpec((1, 8, BR), lambda bi, i: (bi, jnp.minimum((i + 1) * per8, last8), col))

    return pl.pallas_call(
        functools.partial(_conv_kernel, n_tiles=n_tiles), grid=(b, n_tiles),
        in_specs=[main(0), main(1), main(2), before(0), before(2), after(0), after(2),
                  pl.BlockSpec(conv_w.shape, lambda bi, i: (0, 0))],
        out_specs=pl.BlockSpec((1, tl, BR), lambda bi, i: (bi, i, 0)),
        out_shape=jax.ShapeDtypeStruct((b, l, BR), BF16),
        compiler_params=_cparams("parallel", "parallel"), name="short_conv",
    )(pb, pb, pb, pb, pb, pb, pb, conv_w)


def _dft_cos_sin(n):
    ang = 2.0 * np.pi * ((np.arange(n)[:, None] * np.arange(n)[None, :]) % n) / n
    return np.cos(ang).astype(np.float32), np.sin(ang).astype(np.float32)


def _fft_major_kernel(g_ref, m_ref, tc_ref, ts_ref, br_ref, bi_ref):
    dot = functools.partial(jnp.dot, precision=HIGHEST, preferred_element_type=F32)
    for s in range(8):
        g = g_ref[0, :, s, :]
        swapped = jnp.concatenate([g[:, BR:], -g[:, :BR]], axis=1)
        a = dot(m_ref[0], g) + dot(m_ref[1], swapped)
        ar, ai = a[:, :BR], a[:, BR:]
        tc = tc_ref[:, s, :]
        ts = ts_ref[:, s, :]
        br_ref[0, :, s, :] = ar * tc + ai * ts
        bi_ref[0, :, s, :] = ai * tc - ar * ts


def _fft_minor_kernel(br_ref, bi_ref, m_ref, o_ref, *, scale):
    c_hi, c_lo, s_hi, s_lo = m_ref[0], m_ref[1], m_ref[2], m_ref[3]
    for j in range(8):
        y = (_dot_split(c_hi, c_lo, *_split_bf16(br_ref[0, j]))
             + _dot_split(s_hi, s_lo, *_split_bf16(bi_ref[0, j])))
        o_ref[0, :, j, :] = (y * scale).astype(o_ref.dtype)


def _fft_twiddles(l):
    l1 = FFT_L1
    l2 = l // l1
    idx = (jnp.arange(l1, dtype=jnp.int32)[:, None] * jnp.arange(l2, dtype=jnp.int32)[None, :]) % l
    ang = idx.astype(F32) * (2.0 * np.pi / l)
    tc = jnp.broadcast_to(jnp.cos(ang)[:, :, None], (l1, l2, BR))
    ts = jnp.broadcast_to(jnp.sin(ang)[:, :, None], (l1, l2, BR))
    return tc, ts


def _fourier_mix(pc, tc, ts):
    b, l, _ = pc.shape
    l1 = FFT_L1
    l2 = l // l1

    def split_parts(n):
        parts = []
        for m in _dft_cos_sin(n):
            hi = jnp.asarray(m).astype(BF16)
            parts += [hi, (jnp.asarray(m) - hi.astype(F32)).astype(BF16)]
        return jnp.stack(parts)

    def strip(width):
        return pl.BlockSpec((1, l1, 8, width), lambda bi, i: (bi, 0, i, 0))

    tspec = pl.BlockSpec((l1, 8, BR), lambda bi, i: (0, i, 0))
    mspec = pl.BlockSpec((2, l1, l1), lambda bi, i: (0, 0, 0))
    br, bi_ = pl.pallas_call(
        _fft_major_kernel, grid=(b, l2 // 8),
        in_specs=[strip(2 * BR), mspec, tspec, tspec], out_specs=[strip(BR), strip(BR)],
        out_shape=[jax.ShapeDtypeStruct((b, l1, l2, BR), F32)] * 2,
        compiler_params=_cparams("parallel", "parallel"), name="fft_major",
    )(pc.reshape(b, l1, l2, 2 * BR), np.stack(_dft_cos_sin(l1)), tc, ts)

    bspec = pl.BlockSpec((1, 8, l2, BR), lambda bi, i: (bi, i, 0, 0))
    m2spec = pl.BlockSpec((4, l2, l2), lambda bi, i: (0, 0, 0))
    scale = 1.0 / float(np.sqrt(l * FN_GROUP_DIM))
    y = pl.pallas_call(
        functools.partial(_fft_minor_kernel, scale=scale), grid=(b, l1 // 8),
        in_specs=[bspec, bspec, m2spec],
        out_specs=pl.BlockSpec((1, l2, 8, BR), lambda bi, i: (bi, 0, i, 0)),
        out_shape=jax.ShapeDtypeStruct((b, l2, l1, BR), F32),
        compiler_params=_cparams("parallel", "parallel"), name="fft_minor",
    )(br, bi_, split_parts(l2))
    return y.reshape(b, l, BR)


def _hgrn_level_masks():
    t = np.arange(HG_CHUNK)
    out = []
    half = 8
    while 2 * half < HG_CHUNK:
        same = (t[:, None] // (2 * half)) == (t[None, :] // (2 * half))
        out.append(np.tile(same, (1, N_HEADS)).astype(np.float32))
        half *= 2
    return np.stack(out)


def _hgrn_kernel(qd_ref, fl_ref, vd_ref, lb_ref, mask_ref, bd_ref, o_ref, st_ref, *, first_layer, chunks_per_tile,
                 rev):
    c = HG_CHUNK

    @pl.when(pl.program_id(1) == 0)
    def _():
        st_ref[...] = jnp.zeros_like(st_ref)

    bd = bd_ref[...]
    bd16 = bd.astype(BF16)
    lb = lb_ref[...]
    row = lax.broadcasted_iota(jnp.int32, (c, 1), 0)
    sub = row % 8
    nt = (((1,), (1,)), ((), ()))

    def roll8(x, j):
        return pltpu.roll(x.reshape(c // 8, 8, BR), j % 8, 1).reshape(c, BR)

    def earlier(x, j):
        return roll8(x, -j if rev else j)

    def later(x, j):
        return roll8(x, j if rev else -j)

    def has_earlier(j):
        return (sub + j <= 7) if rev else (sub >= j)

    def has_later(j):
        return (sub >= j) if rev else (sub + j <= 7)

    for ci in (reversed(range(chunks_per_tile)) if rev else range(chunks_per_tile)):
        sl = slice(ci * c, (ci + 1) * c)
        qd = qd_ref[0, sl, :]
        fl = fl_ref[0, sl, :]
        v = vd_ref[0, sl, :]
        q = qd * _sigmoid(qd)
        if first_layer:
            f = _sigmoid(fl)
            k = _sigmoid(-fl)
        else:
            f = lb + (1.0 - lb) * _sigmoid(fl)
            k = (1.0 - lb) * _sigmoid(-fl)

        o = jnp.dot((q * k).astype(BF16), bd16, preferred_element_type=F32) * v
        g = f
        for d in range(1, 8):
            if d > 1:
                g = g * earlier(f, d - 1)
            p = jnp.where(has_earlier(d), q * g * earlier(k, d), 0.0)
            o = o + jnp.dot(p.astype(BF16), bd16, preferred_element_type=F32) * earlier(v, d)

        qf = f
        for j in (1, 2, 4):
            qf = qf * jnp.where(has_earlier(j), earlier(qf, j), 1.0)
        kf = jnp.where(has_later(1), later(f, 1), 1.0)
        for j in (1, 2, 4):
            kf = kf * jnp.where(has_later(j), later(kf, j), 1.0)

        scores = None
        half = 8
        level = 0
        while half < c:
            late = ((row // half) % 2) == (0 if rev else 1)
            qb = jnp.where(late, q * qf, 0.0).astype(BF16)
            kb = jnp.where(late, 0.0, k * kf).astype(BF16)
            x = lax.dot_general(qb, _head_stack(kb), nt, preferred_element_type=F32)
            if 2 * half < c:
                x = x * mask_ref[level]
            scores = x if scores is None else scores + x
            blocks = c // (2 * half)
            qf3 = qf.reshape(blocks, 2 * half, BR)
            end_early = half if rev else half - 1
            end_late = 0 if rev else 2 * half - 1
            tot_early = jnp.broadcast_to(qf3[:, end_early:end_early + 1, :], qf3.shape).reshape(c, BR)
            tot_late = jnp.broadcast_to(qf3[:, end_late:end_late + 1, :], qf3.shape).reshape(c, BR)
            qf = qf * jnp.where(late, tot_early, 1.0)
            kf = kf * jnp.where(late, 1.0, tot_late)
            half *= 2
            level += 1

        o = o + jnp.dot(scores.astype(BF16), _head_stack(v.astype(BF16)), preferred_element_type=F32)

        st = st_ref[...]
        o = o + lax.dot_general((q * qf).astype(BF16), st.astype(BF16), nt, preferred_element_type=F32)
        o_ref[0, sl, :] = o
        upd = jnp.dot(v.T.astype(BF16), (k * kf).astype(BF16), preferred_element_type=F32)
        last = 0 if rev else c - 1
        st_ref[...] = st * qf[last:last + 1, :] + upd * bd


def _hgrn_scan(qd, fl, vd, cols, lb, first_layer, rev, chunks_per_tile=4):
    b, l, _ = qd.shape
    tile = HG_CHUNK * chunks_per_tile
    n_tiles = l // tile
    masks = _hgrn_level_masks()
    bd = _same_head_matrix()

    def tile_index(i):
        return n_tiles - 1 - i if rev else i

    def seq(col):
        return pl.BlockSpec((1, tile, BR), lambda bi, i: (bi, tile_index(i), col))

    return pl.pallas_call(
        functools.partial(_hgrn_kernel, first_layer=first_layer, chunks_per_tile=chunks_per_tile, rev=rev),
        grid=(b, n_tiles),
        in_specs=[seq(cols[0]), seq(cols[1]), seq(cols[2]),
                  pl.BlockSpec((1, BR), lambda bi, i: (0, 0)),
                  pl.BlockSpec(masks.shape, lambda bi, i: (0, 0, 0)),
                  pl.BlockSpec((BR, BR), lambda bi, i: (0, 0))],
        out_specs=pl.BlockSpec((1, tile, BR), lambda bi, i: (bi, tile_index(i), 0)),
        out_shape=jax.ShapeDtypeStruct((b, l, BR), F32),
        scratch_shapes=[pltpu.VMEM((BR, BR), F32)],
        compiler_params=_cparams("parallel", "arbitrary"),
        name="hgrn_bwd" if rev else "hgrn_fwd",
    )(qd, fl, vd, lb.reshape(1, BR), masks, bd)


def _hgrn_out_kernel(of_ref, ob_ref, gd_ref, g_ref, bd_ref, o_ref):
    o = of_ref[...] + ob_ref[...]
    ms = jnp.dot(o * o, bd_ref[...], precision=HIGHEST, preferred_element_type=F32) * (1.0 / HEAD_DIM)
    gd = gd_ref[...]
    o_ref[...] = (o * lax.rsqrt(ms + RMS_EPS) * g_ref[...] * (gd * _sigmoid(gd))).astype(o_ref.dtype)


def _hgrn_output(o_f, o_b, pd, norm_g, tm=1024):
    t = o_f.shape[0]
    row = pl.BlockSpec((tm, BR), lambda i: (i, 0))
    return pl.pallas_call(
        _hgrn_out_kernel, grid=(t // tm,),
        in_specs=[row, row, pl.BlockSpec((tm, BR), lambda i: (i, 4)),
                  pl.BlockSpec((1, BR), lambda i: (0, 0)), pl.BlockSpec((BR, BR), lambda i: (0, 0))],
        out_specs=row, out_shape=jax.ShapeDtypeStruct((t, BR), BF16),
        compiler_params=_cparams("parallel"), name="hgrn_out",
    )(o_f, o_b, pd, jnp.tile(norm_g, N_HEADS).reshape(1, BR), _same_head_matrix())


def _hgrn2_bidirectional(pd, lb_f, lb_b, norm_g, first_layer):
    b, l, _ = pd.shape
    o_f = _hgrn_scan(pd, pd, pd, (0, 1, 3), lb_f, first_layer, rev=False)
    o_b = _hgrn_scan(pd, pd, pd, (0, 2, 3), lb_b, first_layer, rev=True)
    y = _hgrn_output(o_f.reshape(b * l, BR), o_b.reshape(b * l, BR), pd.reshape(b * l, 5 * BR), norm_g)
    return y.reshape(b, l, BR)


def _merge_kernel(x_ref, ya_ref, yb_ref, yc_ref, yd_ref, wg_ref, bg_ref, wbr_ref, wo_ref, g_ref, b_ref, o_ref,
                  o16_ref):
    x = x_ref[...]
    xb = x.astype(BF16)
    merged = None
    for i, y_ref in enumerate((ya_ref, yb_ref, yc_ref, yd_ref)):
        gate = _sigmoid(jnp.dot(xb, wg_ref[i], preferred_element_type=F32) + bg_ref[i])
        term = gate * jnp.dot(y_ref[...].astype(BF16), wbr_ref[i], preferred_element_type=F32)
        merged = term if merged is None else merged + term
    mix = jnp.dot(merged.astype(BF16), wo_ref[...], preferred_element_type=F32)
    out = _ln_rows(DEEPNORM_ALPHA * x + mix, g_ref[...], b_ref[...])
    o_ref[...] = out
    o16_ref[...] = out.astype(BF16)


def _merge(x, ys, w_gate, b_gate, w_br, w_out, g, b, tm=1024):
    t, d = x.shape
    row = pl.BlockSpec((tm, d), lambda i: (i, 0))
    yrow = pl.BlockSpec((tm, BR), lambda i: (i, 0))
    vec = pl.BlockSpec((1, d), lambda i: (0, 0))

    def whole(a):
        return pl.BlockSpec(a.shape, lambda i: (0,) * a.ndim)

    bg = b_gate.reshape(b_gate.shape[0], 1, d)
    return pl.pallas_call(
        _merge_kernel, grid=(t // tm,),
        in_specs=[row, yrow, yrow, yrow, yrow, whole(w_gate), whole(bg), whole(w_br), whole(w_out), vec, vec],
        out_specs=[row, row], out_shape=[jax.ShapeDtypeStruct((t, d), F32), jax.ShapeDtypeStruct((t, d), BF16)],
        compiler_params=_cparams("parallel"), name="merge_ln1",
    )(x, *ys, w_gate, bg, w_br, w_out, g.reshape(1, d), b.reshape(1, d))


def _router_kernel(x_ref, whi_ref, wlo_ref, b_ref, tri_ref, e_ref, p_ref, cnt_ref, run_ref):
    @pl.when(pl.program_id(0) == 0)
    def _():
        run_ref[...] = jnp.zeros_like(run_ref)

    x_hi, x_lo = _split_bf16(x_ref[...])
    logits = _dot_split(x_hi, x_lo, whi_ref[...], wlo_ref[...]) + b_ref[...]
    lane = lax.broadcasted_iota(jnp.int32, logits.shape, 1)
    big = jnp.int32(LANES)
    is_group = (lane >= N_EXPERTS) & (lane < N_EXPERTS + N_GROUPS)
    lg = jnp.where(is_group, logits, -jnp.inf)
    mg = jnp.max(lg, axis=-1, keepdims=True)
    g_sel = jnp.min(jnp.where(lg == mg, lane, big), axis=-1, keepdims=True) - N_EXPERTS
    p_g = 1.0 / jnp.sum(jnp.exp(lg - mg), axis=-1, keepdims=True)
    in_group = (lane < N_EXPERTS) & ((lane // EXPERTS_PER_GROUP) == g_sel)
    le = jnp.where(in_group, logits, -jnp.inf)
    v1 = jnp.max(le, axis=-1, keepdims=True)
    i1 = jnp.min(jnp.where(le == v1, lane, big), axis=-1, keepdims=True)
    le2 = jnp.where(lane == i1, -jnp.inf, le)
    v2 = jnp.max(le2, axis=-1, keepdims=True)
    i2 = jnp.min(jnp.where(le2 == v2, lane, big), axis=-1, keepdims=True)
    e2 = jnp.exp(v2 - v1)
    w1 = p_g / (1.0 + e2)
    w2 = p_g * e2 / (1.0 + e2)
    hot1 = lane == i1
    hot2 = lane == i2
    hot = (hot1 | hot2).astype(BF16)
    before = jnp.dot(tri_ref[...], hot, preferred_element_type=F32) + run_ref[...]
    rank1 = jnp.sum(jnp.where(hot1, before, 0.0), axis=-1, keepdims=True).astype(jnp.int32)
    rank2 = jnp.sum(jnp.where(hot2, before, 0.0), axis=-1, keepdims=True).astype(jnp.int32)
    run = run_ref[...] + jnp.sum(hot.astype(F32), axis=0, keepdims=True)
    run_ref[...] = run
    cnt_ref[...] = run.astype(jnp.int32)
    e_ref[...] = jnp.where(lane == 0, i1, jnp.where(lane == 1, i2, jnp.where(lane == 2, rank1,
                                                                               jnp.where(lane == 3, rank2, 0))))
    p_ref[...] = jnp.where(lane == 0, w1, jnp.where(lane == 1, w2, 0.0))


def _router(x, wg, bg, we, be, tm=512):
    t, d = x.shape
    w = jnp.zeros((d, LANES), F32).at[:, :N_EXPERTS].set(we).at[:, N_EXPERTS:N_EXPERTS + N_GROUPS].set(wg)
    bias = jnp.zeros((1, LANES), F32).at[0, :N_EXPERTS].set(be).at[0, N_EXPERTS:N_EXPERTS + N_GROUPS].set(bg)
    w_hi, w_lo = _split_bf16(w)
    tri = np.tril(np.ones((tm, tm), np.float32), -1).astype(BF16)
    out = pl.BlockSpec((tm, LANES), lambda i: (i, 0))
    vec = pl.BlockSpec((1, LANES), lambda i: (0, 0))
    wspec = pl.BlockSpec((d, LANES), lambda i: (0, 0))
    return pl.pallas_call(
        _router_kernel, grid=(t // tm,),
        in_specs=[pl.BlockSpec((tm, d), lambda i: (i, 0)), wspec, wspec, vec,
                  pl.BlockSpec((tm, tm), lambda i: (0, 0))],
        out_specs=[out, out, vec],
        out_shape=[jax.ShapeDtypeStruct((t, LANES), jnp.int32), jax.ShapeDtypeStruct((t, LANES), F32),
                   jax.ShapeDtypeStruct((1, LANES), jnp.int32)],
        scratch_shapes=[pltpu.VMEM((1, LANES), F32)],
        compiler_params=_cparams("arbitrary"), name="moe_router",
    )(x, w_hi, w_lo, bias, tri)


def _expert_work_items(counts, n_blocks):
    end = jnp.cumsum(counts)
    start = end - counts
    first_blk = start // MOE_ROWS
    n_e = jnp.where(counts > 0, (end - 1) // MOE_ROWS - first_blk + 1, 0)
    item_end = jnp.cumsum(n_e)
    item_start = item_end - n_e
    w = jnp.arange(n_blocks + N_EXPERTS, dtype=jnp.int32)
    live = w < item_end[-1]
    wc = jnp.minimum(w, item_end[-1] - 1)
    e_w = jnp.sum((item_end[None, :] <= wc[:, None]).astype(jnp.int32), axis=1)
    blk = first_blk[e_w] + wc - item_start[e_w]
    lo = jnp.where(live, jnp.maximum(start[e_w], blk * MOE_ROWS) - blk * MOE_ROWS, 0)
    hi = jnp.where(live, jnp.minimum(end[e_w], (blk + 1) * MOE_ROWS) - blk * MOE_ROWS, 0)
    first = jnp.concatenate([jnp.ones((1,), jnp.int32), (blk[1:] != blk[:-1]).astype(jnp.int32)])
    return jnp.stack([e_w, blk, lo, hi, first]).astype(jnp.int32)


def _expert_kernel(items_ref, x_ref, w13_ref, w2_ref, o_ref):
    w = pl.program_id(0)
    lo = items_ref[2, w]
    hi = items_ref[3, w]

    @pl.when(hi > lo)
    def _():
        h = jnp.dot(x_ref[...], w13_ref[0], preferred_element_type=F32)
        hg = h[:, :D_FF_E]
        act = hg * _sigmoid(hg) * h[:, D_FF_E:]
        y = jnp.dot(act.astype(BF16), w2_ref[0], preferred_element_type=F32).astype(o_ref.dtype)
        row = lax.broadcasted_iota(jnp.int32, (MOE_ROWS, 1), 0)
        mine = (row >= lo) & (row < hi)

        @pl.when(items_ref[4, w] == 1)
        def _():
            o_ref[...] = jnp.where(mine, y, jnp.zeros_like(y))

        @pl.when(items_ref[4, w] == 0)
        def _():
            o_ref[...] = jnp.where(mine, y, o_ref[...])


def _expert_blocks(xs, items, w13, w2):
    n_rows, d = xs.shape
    grid_spec = pltpu.PrefetchScalarGridSpec(
        num_scalar_prefetch=1, grid=(items.shape[1],),
        in_specs=[pl.BlockSpec((MOE_ROWS, d), lambda w, it: (it[1, w], 0)),
                  pl.BlockSpec((1, d, 2 * D_FF_E), lambda w, it: (it[0, w], 0, 0)),
                  pl.BlockSpec((1, D_FF_E, d), lambda w, it: (it[0, w], 0, 0))],
        out_specs=pl.BlockSpec((MOE_ROWS, d), lambda w, it: (it[1, w], 0)))
    return pl.pallas_call(
        _expert_kernel, grid_spec=grid_spec, out_shape=jax.ShapeDtypeStruct((n_rows, d), BF16),
        compiler_params=_cparams("arbitrary"), name="moe_experts",
    )(items, xs, w13, w2)


def _combine_kernel(x_ref, y0_ref, y1_ref, p_ref, g_ref, b_ref, o_ref):
    p = p_ref[...]
    ffn = p[:, 0:1] * y0_ref[...].astype(F32) + p[:, 1:2] * y1_ref[...].astype(F32)
    o_ref[...] = _ln_rows(DEEPNORM_ALPHA * x_ref[...] + ffn, g_ref[...], b_ref[...])


def _combine(x, y0, y1, p, g, b, tm=1024):
    t, d = x.shape
    row = pl.BlockSpec((tm, d), lambda i: (i, 0))
    vec = pl.BlockSpec((1, d), lambda i: (0, 0))
    return pl.pallas_call(
        _combine_kernel, grid=(t // tm,),
        in_specs=[row, row, row, pl.BlockSpec((tm, LANES), lambda i: (i, 0)), vec, vec],
        out_specs=row, out_shape=jax.ShapeDtypeStruct((t, d), F32),
        compiler_params=_cparams("parallel"), name="moe_combine_ln2",
    )(x, y0, y1, p, g.reshape(1, d), b.reshape(1, d))


def _hierarchical_moe(x, x16, wg, bg, we, be, w13, w2, g, b):
    t, d = x.shape
    e_lanes, p_lanes, cnt = _router(x, wg, bg, we, be)
    tk = t * TOP_K
    e_k = e_lanes[:, :TOP_K]
    counts = cnt[0, :N_EXPERTS]
    start = jnp.cumsum(counts) - counts
    dest = start[e_k] + e_lanes[:, TOP_K:2 * TOP_K]
    order = jnp.argsort(e_k.reshape(tk)).astype(jnp.int32)
    xs = jnp.take(x16, order // TOP_K, axis=0, mode='clip')
    ys = _expert_blocks(xs, _expert_work_items(counts, tk // MOE_ROWS), w13, w2)
    y0 = jnp.take(ys, dest[:, 0], axis=0, mode='clip')
    y1 = jnp.take(ys, dest[:, 1], axis=0, mode='clip')
    return _combine(x, y0, y1, p_lanes, g, b)


def _trunk(x, prm):
    b, l, d = x.shape
    t = b * l
    x = _layer_norm(x.reshape(t, d), prm['ln_in_g'], prm['ln_in_b'])
    tc, ts = _fft_twiddles(l)
    for i in range(DEPTH):
        pa, pb, pc, pd = _input_projection(x, prm['w_in'][i])
        y_a = _neighbourhood_attention(pa.reshape(b, l, -1), prm['attn_bias'][i])
        y_b = _short_conv(pb.reshape(b, l, -1), prm['conv_w'][i])
        y_c = _fourier_mix(pc.reshape(b, l, -1), tc, ts)
        y_d = _hgrn2_bidirectional(pd.reshape(b, l, -1), prm['lb'][0, i], prm['lb'][1, i],
                                   prm['hgrn_norm_g'][i], i == 0)
        ys = [y.reshape(t, BR) for y in (y_a, y_b, y_c, y_d)]
        x, x16 = _merge(x, ys, prm['w_gate'][i], prm['b_gate'][i], prm['w_br'][i], prm['w_out'][i],
                        prm['ln1_g'][i], prm['ln1_b'][i])
        x = _hierarchical_moe(x, x16, prm['router_g_w'][i], prm['router_g_b'][i], prm['router_e_w'][i],
                              prm['router_e_b'][i], prm['w13'][i], prm['w2'][i], prm['ln2_g'][i], prm['ln2_b'][i])
    return x.reshape(b, l, d)


def kernel(x_prompt, x_sample, ln_in_g, ln_in_b, w_in, na_rpb, conv_w, hgrn_lb, hgrn_norm_g, w_gate, b_gate, w_br,
           w_out, ln1_g, ln1_b, router_g_w, router_g_b, router_e_w, router_e_b, w13, w2, ln2_g, ln2_b):
    lb = jnp.cumsum(jax.nn.softmax(hgrn_lb.astype(F32), axis=1), axis=1)
    lb = lb - lb[:, :1]
    prm = dict(
        ln_in_g=ln_in_g, ln_in_b=ln_in_b,
        w_in=jnp.stack([_fold_channel_dft(w_in[i]) for i in range(DEPTH)]).astype(BF16),
        attn_bias=jnp.stack([_attn_bias_table(na_rpb[i]) for i in range(DEPTH)]),
        conv_w=conv_w, lb=lb, hgrn_norm_g=hgrn_norm_g,
        w_gate=w_gate.astype(BF16), b_gate=b_gate, w_br=w_br.astype(BF16), w_out=w_out.astype(BF16),
        ln1_g=ln1_g, ln1_b=ln1_b, router_g_w=router_g_w, router_g_b=router_g_b,
        router_e_w=router_e_w, router_e_b=router_e_b, w13=w13.astype(BF16), w2=w2.astype(BF16),
        ln2_g=ln2_g, ln2_b=ln2_b)
    return _trunk(x_prompt, prm), _trunk(x_sample, prm)
```

```python
import functools

import numpy as np
import jax
import jax.numpy as jnp
from jax import lax
from jax.experimental import pallas as pl
from jax.experimental.pallas import tpu as pltpu

F32 = jnp.float32
BF16 = jnp.bfloat16
HIGHEST = lax.Precision.HIGHEST

D_MODEL = 1024
DEPTH = 4
GRID_W = 64
BR = 256
N_HEADS = 4
HEAD_DIM = 64
NA_ROWS = 8
NA_COLS = 16
FN_GROUP_DIM = 64
N_GROUPS = 4
EXPERTS_PER_GROUP = 8
N_EXPERTS = N_GROUPS * EXPERTS_PER_GROUP
TOP_K = 2
D_FF_E = 512
DEEPNORM_ALPHA = (2 * DEPTH) ** 0.25
LN_EPS = 1e-5
RMS_EPS = 1e-6
NEG_INF = -1e30

LANES = 128
VMEM_LIMIT_BYTES = 56 * 1024 * 1024
MOE_ROWS = 512
HG_CHUNK = 128
FFT_L1 = 64

COL_QA, COL_KA, COL_VA, COL_XB, COL_GB, COL_GC, COL_XF, COL_QD, COL_FF, COL_FB, COL_VD, COL_GD = range(12)
PROJ_GROUPS = ((0, 3), (3, 6), (6, 8), (8, 13))


def _cparams(*sem):
    return pltpu.CompilerParams(dimension_semantics=sem, vmem_limit_bytes=VMEM_LIMIT_BYTES)


def _sigmoid(x):
    return 1.0 / (1.0 + jnp.exp(-x))


def _split_bf16(x):
    hi = x.astype(BF16)
    return hi, (x - hi.astype(F32)).astype(BF16)


def _dot_split(a_hi, a_lo, b_hi, b_lo):
    dot = functools.partial(jnp.dot, preferred_element_type=F32)
    return dot(a_hi, b_hi) + (dot(a_lo, b_hi) + dot(a_hi, b_lo))


def _ln_rows(x, g, b):
    mu = jnp.mean(x, axis=-1, keepdims=True)
    xc = x - mu
    var = jnp.mean(xc * xc, axis=-1, keepdims=True)
    return xc * lax.rsqrt(var + LN_EPS) * g + b


def _head_of_lane(shape):
    return lax.broadcasted_iota(jnp.int32, shape, len(shape) - 1) // HEAD_DIM


def _head_stack(x):
    head = _head_of_lane(x.shape)
    return jnp.concatenate([jnp.where(head == h, x, jnp.zeros_like(x)) for h in range(N_HEADS)], axis=0)


def _same_head_matrix():
    h = np.arange(BR) // HEAD_DIM
    return (h[:, None] == h[None, :]).astype(np.float32)


def _ln_kernel(x_ref, g_ref, b_ref, o_ref):
    o_ref[...] = _ln_rows(x_ref[...], g_ref[...], b_ref[...])


def _layer_norm(x, g, b, tm=1024):
    t, d = x.shape
    row = pl.BlockSpec((tm, d), lambda i: (i, 0))
    vec = pl.BlockSpec((1, d), lambda i: (0, 0))
    return pl.pallas_call(
        _ln_kernel, grid=(t // tm,), in_specs=[row, vec, vec], out_specs=row,
        out_shape=jax.ShapeDtypeStruct((t, d), F32), compiler_params=_cparams("parallel"), name="ln_in",
    )(x, g.reshape(1, d), b.reshape(1, d))


def _proj_kernel(x_ref, w_ref, pa_ref, pb_ref, pc_ref, pd_ref):
    xb = x_ref[...].astype(BF16)

    def cols(lo, hi):
        return jnp.dot(xb, w_ref[:, lo * BR:hi * BR], preferred_element_type=F32)

    for ref, (lo, hi) in zip((pa_ref, pb_ref, pc_ref, pd_ref), PROJ_GROUPS):
        ref[...] = cols(lo, hi).astype(ref.dtype)


def _fold_channel_dft(w_in):
    cg, sg = _dft_cos_sin(FN_GROUP_DIM)
    eye = np.eye(BR // FN_GROUP_DIM, dtype=np.float32)
    w_xf = w_in[:, COL_XF * BR:(COL_XF + 1) * BR]
    w_re = jnp.dot(w_xf, np.kron(eye, cg), precision=HIGHEST)
    w_im = -jnp.dot(w_xf, np.kron(eye, sg), precision=HIGHEST)
    return jnp.concatenate([w_in[:, :COL_XF * BR], w_re, w_im, w_in[:, (COL_XF + 1) * BR:]], axis=1)


def _input_projection(x, w_in, tm=1024):
    t, d = x.shape
    n = w_in.shape[1]
    widths = tuple((hi - lo) * BR for lo, hi in PROJ_GROUPS)
    dtypes = (BF16, F32, F32, F32)
    return pl.pallas_call(
        _proj_kernel, grid=(t // tm,),
        in_specs=[pl.BlockSpec((tm, d), lambda i: (i, 0)), pl.BlockSpec((d, n), lambda i: (0, 0))],
        out_specs=[pl.BlockSpec((tm, w), lambda i: (i, 0)) for w in widths],
        out_shape=[jax.ShapeDtypeStruct((t, w), dt) for w, dt in zip(widths, dtypes)],
        compiler_params=_cparams("parallel"), name="in_proj",
    )(x, w_in)


def _attn_bias_table(rpb):
    c = np.arange(GRID_W)
    win_c0 = np.clip(c - NA_COLS // 2, 0, GRID_W - NA_COLS)
    valid = (c[None, :] >= win_c0[:, None]) & (c[None, :] < win_c0[:, None] + NA_COLS)
    dc = np.clip(c[None, :] - c[:, None], -(NA_COLS - 1), NA_COLS - 1) + NA_COLS - 1
    dr = np.arange(NA_ROWS)[None, :] - np.arange(NA_ROWS)[:, None] + NA_ROWS - 1
    row_hot = (dr[:, :, None] == np.arange(2 * NA_ROWS - 1)).astype(np.float32)
    col_hot = (dc[:, :, None] == np.arange(2 * NA_COLS - 1)).astype(np.float32)
    t = jnp.einsum('oia,hab,qkb->ohqik', row_hot, rpb.astype(F32), col_hot, precision=HIGHEST)
    t = jnp.where(valid[None, None, :, None, :], t, NEG_INF)
    return t.reshape(NA_ROWS, N_HEADS * GRID_W, NA_ROWS * GRID_W)


def _attn_kernel(q_ref, k_ref, v_ref, bias_ref, o_ref, *, n_rows, rows_per_tile):
    i = pl.program_id(1)
    head = _head_of_lane((GRID_W, BR))
    scale = HEAD_DIM ** -0.5
    span = NA_ROWS * GRID_W
    for j in range(rows_per_tile):
        r = i * rows_per_tile + j
        r0 = jnp.clip(r - NA_ROWS // 2, 0, n_rows - NA_ROWS)
        q = q_ref[0, j * GRID_W:(j + 1) * GRID_W, :]
        qs = _head_stack(q)
        start = pl.multiple_of(r0 * GRID_W, GRID_W)
        ks = k_ref[0, pl.ds(start, span), :]
        vs = v_ref[0, pl.ds(start, span), :]
        s = lax.dot_general(qs, ks, (((1,), (1,)), ((), ())), preferred_element_type=F32)
        s = s * scale + bias_ref[r - r0]
        m = jnp.max(s, axis=-1, keepdims=True)
        e = jnp.exp(s - m)
        p = e / jnp.sum(e, axis=-1, keepdims=True)
        o4 = jnp.dot(p.astype(BF16), vs, preferred_element_type=F32)
        o = jnp.zeros((GRID_W, BR), F32)
        for h in range(N_HEADS):
            o = o + jnp.where(head == h, o4[h * GRID_W:(h + 1) * GRID_W, :], 0.0)
        o_ref[0, j * GRID_W:(j + 1) * GRID_W, :] = o.astype(o_ref.dtype)


def _neighbourhood_attention(pa, bias, rows_per_tile=8):
    b, l, _ = pa.shape
    n_rows = l // GRID_W
    tq = rows_per_tile * GRID_W
    kern = functools.partial(_attn_kernel, n_rows=n_rows, rows_per_tile=rows_per_tile)
    return pl.pallas_call(
        kern, grid=(b, l // tq),
        in_specs=[pl.BlockSpec((1, tq, BR), lambda bi, i: (bi, i, 0)),
                  pl.BlockSpec((1, l, BR), lambda bi, i: (bi, 0, 1)),
                  pl.BlockSpec((1, l, BR), lambda bi, i: (bi, 0, 2)),
                  pl.BlockSpec(bias.shape, lambda bi, i: (0, 0, 0))],
        out_specs=pl.BlockSpec((1, tq, BR), lambda bi, i: (bi, i, 0)),
        out_shape=jax.ShapeDtypeStruct((b, l, BR), BF16),
        compiler_params=_cparams("parallel", "arbitrary"), name="nbr_attn",
    )(pa, pa, pa, bias)


def _conv_kernel(xb_ref, gb_ref, gc_ref, xbp_ref, gcp_ref, xbn_ref, gcn_ref, w_ref, o_ref, *, n_tiles):
    i = pl.program_id(1)
    tl = xb_ref.shape[1]
    u = gc_ref[0] * xb_ref[0]
    u_before = jnp.where(i > 0, gcp_ref[0, 7:8, :] * xbp_ref[0, 7:8, :], 0.0)
    u_after = jnp.where(i < n_tiles - 1, gcn_ref[0, 0:1, :] * xbn_ref[0, 0:1, :], 0.0)
    row = lax.broadcasted_iota(jnp.int32, (tl, 1), 0)
    u_prev = jnp.where(row == 0, u_before, pltpu.roll(u, 1, 0))
    u_next = jnp.where(row == tl - 1, u_after, pltpu.roll(u, tl - 1, 0))
    conv = w_ref[0:1, :] * u_prev + w_ref[1:2, :] * u + w_ref[2:3, :] * u_next
    o_ref[0] = (gb_ref[0] * conv).astype(o_ref.dtype)


def _short_conv(pb, conv_w, tl=1024):
    b, l, _ = pb.shape
    n_tiles = l // tl
    per8 = tl // 8
    last8 = l // 8 - 1

    def main(col):
        return pl.BlockSpec((1, tl, BR), lambda bi, i: (bi, i, col))

    def before(col):
        return pl.BlockSpec((1, 8, BR), lambda bi, i: (bi, jnp.maximum(i * per8 - 1, 0), col))

    def after(col):
        return pl.BlockSpec((1, 8, BR), lambda bi, i: (bi, jnp.minimum((i + 1) * per8, last8), col))

    return pl.pallas_call(
        functools.partial(_conv_kernel, n_tiles=n_tiles), grid=(b, n_tiles),
        in_specs=[main(0), main(1), main(2), before(0), before(2), after(0), after(2),
                  pl.BlockSpec(conv_w.shape, lambda bi, i: (0, 0))],
        out_specs=pl.BlockSpec((1, tl, BR), lambda bi, i: (bi, i, 0)),
        out_shape=jax.ShapeDtypeStruct((b, l, BR), BF16),
        compiler_params=_cparams("parallel", "parallel"), name="short_conv",
    )(pb, pb, pb, pb, pb, pb, pb, conv_w)


def _dft_cos_sin(n):
    ang = 2.0 * np.pi * ((np.arange(n)[:, None] * np.arange(n)[None, :]) % n) / n
    return np.cos(ang).astype(np.float32), np.sin(ang).astype(np.float32)


def _fft_major_kernel(g_ref, m_ref, tc_ref, ts_ref, br_ref, bi_ref):
    dot = functools.partial(jnp.dot, precision=HIGHEST, preferred_element_type=F32)
    for s in range(8):
        g = g_ref[0, :, s, :]
        swapped = jnp.concatenate([g[:, BR:], -g[:, :BR]], axis=1)
        a = dot(m_ref[0], g) + dot(m_ref[1], swapped)
        ar, ai = a[:, :BR], a[:, BR:]
        tc = tc_ref[:, s, :]
        ts = ts_ref[:, s, :]
        br_ref[0, :, s, :] = ar * tc + ai * ts
        bi_ref[0, :, s, :] = ai * tc - ar * ts


def _fft_minor_kernel(br_ref, bi_ref, m_ref, o_ref, *, scale):
    c_hi, c_lo, s_hi, s_lo = m_ref[0], m_ref[1], m_ref[2], m_ref[3]
    for j in range(8):
        y = (_dot_split(c_hi, c_lo, *_split_bf16(br_ref[0, j]))
             + _dot_split(s_hi, s_lo, *_split_bf16(bi_ref[0, j])))
        o_ref[0, :, j, :] = (y * scale).astype(o_ref.dtype)


def _fft_twiddles(l):
    l1 = FFT_L1
    l2 = l // l1
    idx = (jnp.arange(l1, dtype=jnp.int32)[:, None] * jnp.arange(l2, dtype=jnp.int32)[None, :]) % l
    ang = idx.astype(F32) * (2.0 * np.pi / l)
    tc = jnp.broadcast_to(jnp.cos(ang)[:, :, None], (l1, l2, BR))
    ts = jnp.broadcast_to(jnp.sin(ang)[:, :, None], (l1, l2, BR))
    return tc, ts


def _fourier_mix(pc, tc, ts):
    b, l, _ = pc.shape
    l1 = FFT_L1
    l2 = l // l1

    def split_parts(n):
        parts = []
        for m in _dft_cos_sin(n):
            hi = jnp.asarray(m).astype(BF16)
            parts += [hi, (jnp.asarray(m) - hi.astype(F32)).astype(BF16)]
        return jnp.stack(parts)

    def strip(width):
        return pl.BlockSpec((1, l1, 8, width), lambda bi, i: (bi, 0, i, 0))

    tspec = pl.BlockSpec((l1, 8, BR), lambda bi, i: (0, i, 0))
    mspec = pl.BlockSpec((2, l1, l1), lambda bi, i: (0, 0, 0))
    br, bi_ = pl.pallas_call(
        _fft_major_kernel, grid=(b, l2 // 8),
        in_specs=[strip(2 * BR), mspec, tspec, tspec], out_specs=[strip(BR), strip(BR)],
        out_shape=[jax.ShapeDtypeStruct((b, l1, l2, BR), F32)] * 2,
        compiler_params=_cparams("parallel", "parallel"), name="fft_major",
    )(pc.reshape(b, l1, l2, 2 * BR), np.stack(_dft_cos_sin(l1)), tc, ts)

    bspec = pl.BlockSpec((1, 8, l2, BR), lambda bi, i: (bi, i, 0, 0))
    m2spec = pl.BlockSpec((4, l2, l2), lambda bi, i: (0, 0, 0))
    scale = 1.0 / float(np.sqrt(l * FN_GROUP_DIM))
    y = pl.pallas_call(
        functools.partial(_fft_minor_kernel, scale=scale), grid=(b, l1 // 8),
        in_specs=[bspec, bspec, m2spec],
        out_specs=pl.BlockSpec((1, l2, 8, BR), lambda bi, i: (bi, 0, i, 0)),
        out_shape=jax.ShapeDtypeStruct((b, l2, l1, BR), F32),
        compiler_params=_cparams("parallel", "parallel"), name="fft_minor",
    )(br, bi_, split_parts(l2))
    return y.reshape(b, l, BR)


def _hgrn_level_masks():
    t = np.arange(HG_CHUNK)
    out = []
    half = 8
    while 2 * half < HG_CHUNK:
        same = (t[:, None] // (2 * half)) == (t[None, :] // (2 * half))
        out.append(np.tile(same, (1, N_HEADS)).astype(np.float32))
        half *= 2
    return np.stack(out)


def _hgrn_kernel(qd_ref, fl_ref, vd_ref, lb_ref, mask_ref, bd_ref, o_ref, st_ref, *, first_layer, chunks_per_tile,
                 rev):
    c = HG_CHUNK

    @pl.when(pl.program_id(1) == 0)
    def _():
        st_ref[...] = jnp.zeros_like(st_ref)

    bd = bd_ref[...]
    bd16 = bd.astype(BF16)
    lb = lb_ref[...]
    row = lax.broadcasted_iota(jnp.int32, (c, 1), 0)
    sub = row % 8
    nt = (((1,), (1,)), ((), ()))

    def roll8(x, j):
        return pltpu.roll(x.reshape(c // 8, 8, BR), j % 8, 1).reshape(c, BR)

    def earlier(x, j):
        return roll8(x, -j if rev else j)

    def later(x, j):
        return roll8(x, j if rev else -j)

    def has_earlier(j):
        return (sub + j <= 7) if rev else (sub >= j)

    def has_later(j):
        return (sub >= j) if rev else (sub + j <= 7)

    for ci in (reversed(range(chunks_per_tile)) if rev else range(chunks_per_tile)):
        sl = slice(ci * c, (ci + 1) * c)
        qd = qd_ref[0, sl, :]
        fl = fl_ref[0, sl, :]
        v = vd_ref[0, sl, :]
        q = qd * _sigmoid(qd)
        if first_layer:
            f = _sigmoid(fl)
            k = _sigmoid(-fl)
        else:
            f = lb + (1.0 - lb) * _sigmoid(fl)
            k = (1.0 - lb) * _sigmoid(-fl)

        o = jnp.dot((q * k).astype(BF16), bd16, preferred_element_type=F32) * v
        g = f
        for d in range(1, 8):
            if d > 1:
                g = g * earlier(f, d - 1)
            p = jnp.where(has_earlier(d), q * g * earlier(k, d), 0.0)
            o = o + jnp.dot(p.astype(BF16), bd16, preferred_element_type=F32) * earlier(v, d)

        qf = f
        for j in (1, 2, 4):
            qf = qf * jnp.where(has_earlier(j), earlier(qf, j), 1.0)
        kf = jnp.where(has_later(1), later(f, 1), 1.0)
        for j in (1, 2, 4):
            kf = kf * jnp.where(has_later(j), later(kf, j), 1.0)

        scores = None
        half = 8
        level = 0
        while half < c:
            late = ((row // half) % 2) == (0 if rev else 1)
            qb = jnp.where(late, q * qf, 0.0).astype(BF16)
            kb = jnp.where(late, 0.0, k * kf).astype(BF16)
            x = lax.dot_general(qb, _head_stack(kb), nt, preferred_element_type=F32)
            if 2 * half < c:
                x = x * mask_ref[level]
            scores = x if scores is None else scores + x
            blocks = c // (2 * half)
            qf3 = qf.reshape(blocks, 2 * half, BR)
            end_early = half if rev else half - 1
            end_late = 0 if rev else 2 * half - 1
            tot_early = jnp.broadcast_to(qf3[:, end_early:end_early + 1, :], qf3.shape).reshape(c, BR)
            tot_late = jnp.broadcast_to(qf3[:, end_late:end_late + 1, :], qf3.shape).reshape(c, BR)
            qf = qf * jnp.where(late, tot_early, 1.0)
            kf = kf * jnp.where(late, 1.0, tot_late)
            half *= 2
            level += 1

        o = o + jnp.dot(scores.astype(BF16), _head_stack(v.astype(BF16)), preferred_element_type=F32)

        st = st_ref[...]
        o = o + lax.dot_general((q * qf).astype(BF16), st.astype(BF16), nt, preferred_element_type=F32)
        o_ref[0, sl, :] = o
        upd = jnp.dot(v.T.astype(BF16), (k * kf).astype(BF16), preferred_element_type=F32)
        last = 0 if rev else c - 1
        st_ref[...] = st * qf[last:last + 1, :] + upd * bd


def _hgrn_scan(qd, fl, vd, cols, lb, first_layer, rev, chunks_per_tile=4):
    b, l, _ = qd.shape
    tile = HG_CHUNK * chunks_per_tile
    n_tiles = l // tile
    masks = _hgrn_level_masks()
    bd = _same_head_matrix()

    def tile_index(i):
        return n_tiles - 1 - i if rev else i

    def seq(col):
        return pl.BlockSpec((1, tile, BR), lambda bi, i: (bi, tile_index(i), col))

    return pl.pallas_call(
        functools.partial(_hgrn_kernel, first_layer=first_layer, chunks_per_tile=chunks_per_tile, rev=rev),
        grid=(b, n_tiles),
        in_specs=[seq(cols[0]), seq(cols[1]), seq(cols[2]),
                  pl.BlockSpec((1, BR), lambda bi, i: (0, 0)),
                  pl.BlockSpec(masks.shape, lambda bi, i: (0, 0, 0)),
                  pl.BlockSpec((BR, BR), lambda bi, i: (0, 0))],
        out_specs=pl.BlockSpec((1, tile, BR), lambda bi, i: (bi, tile_index(i), 0)),
        out_shape=jax.ShapeDtypeStruct((b, l, BR), F32),
        scratch_shapes=[pltpu.VMEM((BR, BR), F32)],
        compiler_params=_cparams("parallel", "arbitrary"),
        name="hgrn_bwd" if rev else "hgrn_fwd",
    )(qd, fl, vd, lb.reshape(1, BR), masks, bd)


def _hgrn2_bidirectional(pd, lb_f, lb_b, first_layer):
    o_f = _hgrn_scan(pd, pd, pd, (0, 1, 3), lb_f, first_layer, rev=False)
    o_b = _hgrn_scan(pd, pd, pd, (0, 2, 3), lb_b, first_layer, rev=True)
    return o_f, o_b


def _route(x, w_hi, w_lo, bias, tri, run):
    x_hi, x_lo = _split_bf16(x)
    logits = _dot_split(x_hi, x_lo, w_hi, w_lo) + bias
    lane = lax.broadcasted_iota(jnp.int32, logits.shape, 1)
    big = jnp.int32(LANES)
    is_group = (lane >= N_EXPERTS) & (lane < N_EXPERTS + N_GROUPS)
    lg = jnp.where(is_group, logits, -jnp.inf)
    mg = jnp.max(lg, axis=-1, keepdims=True)
    g_sel = jnp.min(jnp.where(lg == mg, lane, big), axis=-1, keepdims=True) - N_EXPERTS
    p_g = 1.0 / jnp.sum(jnp.exp(lg - mg), axis=-1, keepdims=True)
    in_group = (lane < N_EXPERTS) & ((lane // EXPERTS_PER_GROUP) == g_sel)
    le = jnp.where(in_group, logits, -jnp.inf)
    v1 = jnp.max(le, axis=-1, keepdims=True)
    i1 = jnp.min(jnp.where(le == v1, lane, big), axis=-1, keepdims=True)
    le2 = jnp.where(lane == i1, -jnp.inf, le)
    v2 = jnp.max(le2, axis=-1, keepdims=True)
    i2 = jnp.min(jnp.where(le2 == v2, lane, big), axis=-1, keepdims=True)
    e2 = jnp.exp(v2 - v1)
    w1 = p_g / (1.0 + e2)
    w2 = p_g * e2 / (1.0 + e2)
    hot1 = lane == i1
    hot2 = lane == i2
    hot = (hot1 | hot2).astype(BF16)
    before = jnp.dot(tri, hot, preferred_element_type=F32) + run
    rank1 = jnp.sum(jnp.where(hot1, before, 0.0), axis=-1, keepdims=True).astype(jnp.int32)
    rank2 = jnp.sum(jnp.where(hot2, before, 0.0), axis=-1, keepdims=True).astype(jnp.int32)
    run = run + jnp.sum(hot.astype(F32), axis=0, keepdims=True)
    e = jnp.where(lane == 0, i1, jnp.where(lane == 1, i2, jnp.where(lane == 2, rank1,
                                                                     jnp.where(lane == 3, rank2, 0))))
    p = jnp.where(lane == 0, w1, jnp.where(lane == 1, w2, 0.0))
    return e, p, run


def _merge_kernel(x_ref, ya_ref, yb_ref, yc_ref, of_ref, ob_ref, gd_ref, ng_ref, bd_ref, wg_ref, bg_ref, wbr_ref,
                  wo_ref, g_ref, b_ref, rhi_ref, rlo_ref, rb_ref, tri_ref, o_ref, o16_ref, e_ref, p_ref, cnt_ref,
                  run_ref):
    @pl.when(pl.program_id(0) == 0)
    def _():
        run_ref[...] = jnp.zeros_like(run_ref)

    od = of_ref[...] + ob_ref[...]
    ms = jnp.dot(od * od, bd_ref[...], precision=HIGHEST, preferred_element_type=F32) * (1.0 / HEAD_DIM)
    gd = gd_ref[...]
    y_d = od * lax.rsqrt(ms + RMS_EPS) * ng_ref[...] * (gd * _sigmoid(gd))

    x = x_ref[...]
    xb = x.astype(BF16)
    merged = None
    for i, y in enumerate((ya_ref[...], yb_ref[...], yc_ref[...], y_d)):
        gate = _sigmoid(jnp.dot(xb, wg_ref[i], preferred_element_type=F32) + bg_ref[i])
        term = gate * jnp.dot(y.astype(BF16), wbr_ref[i], preferred_element_type=F32)
        merged = term if merged is None else merged + term
    mix = jnp.dot(merged.astype(BF16), wo_ref[...], preferred_element_type=F32)
    out = _ln_rows(DEEPNORM_ALPHA * x + mix, g_ref[...], b_ref[...])
    o_ref[...] = out
    o16_ref[...] = out.astype(BF16)
    e, p, run = _route(out, rhi_ref[...], rlo_ref[...], rb_ref[...], tri_ref[...], run_ref[...])
    e_ref[...] = e
    p_ref[...] = p
    run_ref[...] = run
    cnt_ref[...] = run.astype(jnp.int32)


def _merge(x, y_a, y_b, y_c, o_f, o_b, pd, norm_g, w_gate, b_gate, w_br, w_out, g, b, wg, bg, we, be, tm=512):
    t, d = x.shape
    row = pl.BlockSpec((tm, d), lambda i: (i, 0))
    yrow = pl.BlockSpec((tm, BR), lambda i: (i, 0))
    vec = pl.BlockSpec((1, d), lambda i: (0, 0))
    lanes = pl.BlockSpec((tm, LANES), lambda i: (i, 0))
    lvec = pl.BlockSpec((1, LANES), lambda i: (0, 0))

    def whole(a):
        return pl.BlockSpec(a.shape, lambda i: (0,) * a.ndim)

    bgate = b_gate.reshape(b_gate.shape[0], 1, d)
    rw = jnp.zeros((d, LANES), F32).at[:, :N_EXPERTS].set(we).at[:, N_EXPERTS:N_EXPERTS + N_GROUPS].set(wg)
    rb = jnp.zeros((1, LANES), F32).at[0, :N_EXPERTS].set(be).at[0, N_EXPERTS:N_EXPERTS + N_GROUPS].set(bg)
    r_hi, r_lo = _split_bf16(rw)
    tri = np.tril(np.ones((tm, tm), np.float32), -1).astype(BF16)
    ng = jnp.tile(norm_g, N_HEADS).reshape(1, BR)
    bd = _same_head_matrix()
    return pl.pallas_call(
        _merge_kernel, grid=(t // tm,),
        in_specs=[row, yrow, yrow, yrow, yrow, yrow, pl.BlockSpec((tm, BR), lambda i: (i, 4)), whole(ng), whole(bd),
                  whole(w_gate), whole(bgate), whole(w_br), whole(w_out), vec, vec,
                  whole(r_hi), whole(r_lo), lvec, whole(tri)],
        out_specs=[row, row, lanes, lanes, lvec],
        out_shape=[jax.ShapeDtypeStruct((t, d), F32), jax.ShapeDtypeStruct((t, d), BF16),
                   jax.ShapeDtypeStruct((t, LANES), jnp.int32), jax.ShapeDtypeStruct((t, LANES), F32),
                   jax.ShapeDtypeStruct((1, LANES), jnp.int32)],
        scratch_shapes=[pltpu.VMEM((1, LANES), F32)],
        compiler_params=_cparams("arbitrary"), name="merge_ln1_route",
    )(x, y_a, y_b, y_c, o_f, o_b, pd, ng, bd, w_gate, bgate, w_br, w_out, g.reshape(1, d), b.reshape(1, d),
      r_hi, r_lo, rb, tri)


def _expert_work_items(counts, n_blocks):
    end = jnp.cumsum(counts)
    start = end - counts
    first_blk = start // MOE_ROWS
    n_e = jnp.where(counts > 0, (end - 1) // MOE_ROWS - first_blk + 1, 0)
    item_end = jnp.cumsum(n_e)
    item_start = item_end - n_e
    w = jnp.arange(n_blocks + N_EXPERTS, dtype=jnp.int32)
    live = w < item_end[-1]
    wc = jnp.minimum(w, item_end[-1] - 1)
    e_w = jnp.sum((item_end[None, :] <= wc[:, None]).astype(jnp.int32), axis=1)
    blk = first_blk[e_w] + wc - item_start[e_w]
    lo = jnp.where(live, jnp.maximum(start[e_w], blk * MOE_ROWS) - blk * MOE_ROWS, 0)
    hi = jnp.where(live, jnp.minimum(end[e_w], (blk + 1) * MOE_ROWS) - blk * MOE_ROWS, 0)
    first = jnp.concatenate([jnp.ones((1,), jnp.int32), (blk[1:] != blk[:-1]).astype(jnp.int32)])
    return jnp.stack([e_w, blk, lo, hi, first]).astype(jnp.int32)


def _expert_kernel(items_ref, x_ref, w13_ref, w2_ref, o_ref):
    w = pl.program_id(0)
    lo = items_ref[2, w]
    hi = items_ref[3, w]

    @pl.when(hi > lo)
    def _():
        h = jnp.dot(x_ref[...], w13_ref[0], preferred_element_type=F32)
        hg = h[:, :D_FF_E]
        act = hg * _sigmoid(hg) * h[:, D_FF_E:]
        y = jnp.dot(act.astype(BF16), w2_ref[0], preferred_element_type=F32).astype(o_ref.dtype)
        row = lax.broadcasted_iota(jnp.int32, (MOE_ROWS, 1), 0)
        mine = (row >= lo) & (row < hi)

        @pl.when(items_ref[4, w] == 1)
        def _():
            o_ref[...] = jnp.where(mine, y, jnp.zeros_like(y))

        @pl.when(items_ref[4, w] == 0)
        def _():
            o_ref[...] = jnp.where(mine, y, o_ref[...])


def _expert_blocks(xs, items, w13, w2):
    n_rows, d = xs.shape
    grid_spec = pltpu.PrefetchScalarGridSpec(
        num_scalar_prefetch=1, grid=(items.shape[1],),
        in_specs=[pl.BlockSpec((MOE_ROWS, d), lambda w, it: (it[1, w], 0)),
                  pl.BlockSpec((1, d, 2 * D_FF_E), lambda w, it: (it[0, w], 0, 0)),
                  pl.BlockSpec((1, D_FF_E, d), lambda w, it: (it[0, w], 0, 0))],
        out_specs=pl.BlockSpec((MOE_ROWS, d), lambda w, it: (it[1, w], 0)))
    return pl.pallas_call(
        _expert_kernel, grid_spec=grid_spec, out_shape=jax.ShapeDtypeStruct((n_rows, d), BF16),
        compiler_params=_cparams("arbitrary"), name="moe_experts",
    )(items, xs, w13, w2)


def _combine_kernel(x_ref, y0_ref, y1_ref, p_ref, g_ref, b_ref, o_ref):
    p = p_ref[...]
    ffn = p[:, 0:1] * y0_ref[...].astype(F32) + p[:, 1:2] * y1_ref[...].astype(F32)
    o_ref[...] = _ln_rows(DEEPNORM_ALPHA * x_ref[...] + ffn, g_ref[...], b_ref[...])


def _combine(x, y0, y1, p, g, b, tm=1024):
    t, d = x.shape
    row = pl.BlockSpec((tm, d), lambda i: (i, 0))
    vec = pl.BlockSpec((1, d), lambda i: (0, 0))
    return pl.pallas_call(
        _combine_kernel, grid=(t // tm,),
        in_specs=[row, row, row, pl.BlockSpec((tm, LANES), lambda i: (i, 0)), vec, vec],
        out_specs=row, out_shape=jax.ShapeDtypeStruct((t, d), F32),
        compiler_params=_cparams("parallel"), name="moe_combine_ln2",
    )(x, y0, y1, p, g.reshape(1, d), b.reshape(1, d))


def _hierarchical_moe(x, x16, e_lanes, p_lanes, cnt, w13, w2, g, b):
    t, d = x.shape
    tk = t * TOP_K
    e_k = e_lanes[:, :TOP_K]
    counts = cnt[0, :N_EXPERTS]
    start = jnp.cumsum(counts) - counts
    dest = start[e_k] + e_lanes[:, TOP_K:2 * TOP_K]
    order = jnp.argsort(e_k.reshape(tk)).astype(jnp.int32)
    xs = jnp.take(x16, order // TOP_K, axis=0, mode='clip')
    ys = _expert_blocks(xs, _expert_work_items(counts, tk // MOE_ROWS), w13, w2)
    y0 = jnp.take(ys, dest[:, 0], axis=0, mode='clip')
    y1 = jnp.take(ys, dest[:, 1], axis=0, mode='clip')
    return _combine(x, y0, y1, p_lanes, g, b)


def _trunk(x, prm):
    b, l, d = x.shape
    t = b * l
    x = _layer_norm(x.reshape(t, d), prm['ln_in_g'], prm['ln_in_b'])
    tc, ts = _fft_twiddles(l)
    for i in range(DEPTH):
        pa, pb, pc, pd = _input_projection(x, prm['w_in'][i])
        y_a = _neighbourhood_attention(pa.reshape(b, l, -1), prm['attn_bias'][i])
        y_b = _short_conv(pb.reshape(b, l, -1), prm['conv_w'][i])
        y_c = _fourier_mix(pc.reshape(b, l, -1), tc, ts)
        o_f, o_b = _hgrn2_bidirectional(pd.reshape(b, l, -1), prm['lb'][0, i], prm['lb'][1, i], i == 0)
        x, x16, e_lanes, p_lanes, cnt = _merge(
            x, y_a.reshape(t, BR), y_b.reshape(t, BR), y_c.reshape(t, BR), o_f.reshape(t, BR), o_b.reshape(t, BR),
            pd, prm['hgrn_norm_g'][i], prm['w_gate'][i], prm['b_gate'][i], prm['w_br'][i], prm['w_out'][i],
            prm['ln1_g'][i], prm['ln1_b'][i], prm['router_g_w'][i], prm['router_g_b'][i], prm['router_e_w'][i],
            prm['router_e_b'][i])
        x = _hierarchical_moe(x, x16, e_lanes, p_lanes, cnt, prm['w13'][i], prm['w2'][i],
                              prm['ln2_g'][i], prm['ln2_b'][i])
    return x.reshape(b, l, d)


def kernel(x_prompt, x_sample, ln_in_g, ln_in_b, w_in, na_rpb, conv_w, hgrn_lb, hgrn_norm_g, w_gate, b_gate, w_br,
           w_out, ln1_g, ln1_b, router_g_w, router_g_b, router_e_w, router_e_b, w13, w2, ln2_g, ln2_b):
    lb = jnp.cumsum(jax.nn.softmax(hgrn_lb.astype(F32), axis=1), axis=1)
    lb = lb - lb[:, :1]
    prm = dict(
        ln_in_g=ln_in_g, ln_in_b=ln_in_b,
        w_in=jnp.stack([_fold_channel_dft(w_in[i]) for i in range(DEPTH)]).astype(BF16),
        attn_bias=jnp.stack([_attn_bias_table(na_rpb[i]) for i in range(DEPTH)]),
        conv_w=conv_w, lb=lb, hgrn_norm_g=hgrn_norm_g,
        w_gate=w_gate.astype(BF16), b_gate=b_gate, w_br=w_br.astype(BF16), w_out=w_out.astype(BF16),
        ln1_g=ln1_g, ln1_b=ln1_b, router_g_w=router_g_w, router_g_b=router_g_b,
        router_e_w=router_e_w, router_e_b=router_e_b, w13=w13.astype(BF16), w2=w2.astype(BF16),
        ln2_g=ln2_g, ln2_b=ln2_b)
    return _trunk(x_prompt, prm), _trunk(x_sample, prm)
```

```python
import functools

import numpy as np
import jax
import jax.numpy as jnp
from jax import lax
from jax.experimental import pallas as pl
from jax.experimental.pallas import tpu as pltpu

F32 = jnp.float32
BF16 = jnp.bfloat16
HIGHEST = lax.Precision.HIGHEST

D_MODEL = 1024
DEPTH = 4
GRID_W = 64
BR = 256
N_HEADS = 4
HEAD_DIM = 64
NA_ROWS = 8
NA_COLS = 16
FN_GROUP_DIM = 64
N_GROUPS = 4
EXPERTS_PER_GROUP = 8
N_EXPERTS = N_GROUPS * EXPERTS_PER_GROUP
TOP_K = 2
D_FF_E = 512
DEEPNORM_ALPHA = (2 * DEPTH) ** 0.25
LN_EPS = 1e-5
RMS_EPS = 1e-6
NEG_INF = -1e30

LANES = 128
VMEM_LIMIT_BYTES = 56 * 1024 * 1024
MOE_ROWS = 512
HG_CHUNK = 128
FFT_L1 = 64

COL_QA, COL_KA, COL_VA, COL_XB, COL_GB, COL_GC, COL_XF, COL_QD, COL_FF, COL_FB, COL_VD, COL_GD = range(12)
PROJ_GROUPS = ((0, 3), (3, 6), (6, 8), (8, 13))


def _cparams(*sem):
    return pltpu.CompilerParams(dimension_semantics=sem, vmem_limit_bytes=VMEM_LIMIT_BYTES)


def _sigmoid(x):
    return 1.0 / (1.0 + jnp.exp(-x))


def _split_bf16(x):
    hi = x.astype(BF16)
    return hi, (x - hi.astype(F32)).astype(BF16)


def _dot_split(a_hi, a_lo, b_hi, b_lo):
    dot = functools.partial(jnp.dot, preferred_element_type=F32)
    return dot(a_hi, b_hi) + (dot(a_lo, b_hi) + dot(a_hi, b_lo))


def _ln_rows(x, g, b):
    mu = jnp.mean(x, axis=-1, keepdims=True)
    xc = x - mu
    var = jnp.mean(xc * xc, axis=-1, keepdims=True)
    return xc * lax.rsqrt(var + LN_EPS) * g + b


def _head_of_lane(shape):
    return lax.broadcasted_iota(jnp.int32, shape, len(shape) - 1) // HEAD_DIM


def _head_stack(x):
    head = _head_of_lane(x.shape)
    return jnp.concatenate([jnp.where(head == h, x, jnp.zeros_like(x)) for h in range(N_HEADS)], axis=0)


def _same_head_matrix():
    h = np.arange(BR) // HEAD_DIM
    return (h[:, None] == h[None, :]).astype(np.float32)


def _moe_residual(x_ref, y0_ref, y1_ref, p_ref):
    p = p_ref[...]
    ffn = p[:, 0:1] * y0_ref[...].astype(F32) + p[:, 1:2] * y1_ref[...].astype(F32)
    return DEEPNORM_ALPHA * x_ref[...] + ffn


def _proj_kernel(*refs, after_moe):
    if after_moe:
        x_ref, y0_ref, y1_ref, p_ref, g_ref, b_ref, w_ref, xo_ref, *out_refs = refs
        pre = _moe_residual(x_ref, y0_ref, y1_ref, p_ref)
    else:
        x_ref, g_ref, b_ref, w_ref, xo_ref, *out_refs = refs
        pre = x_ref[...]
    x = _ln_rows(pre, g_ref[...], b_ref[...])
    xo_ref[...] = x
    xb = x.astype(BF16)
    for ref, (lo, hi) in zip(out_refs, PROJ_GROUPS):
        ref[...] = jnp.dot(xb, w_ref[:, lo * BR:hi * BR], preferred_element_type=F32).astype(ref.dtype)


def _fold_channel_dft(w_in):
    cg, sg = _dft_cos_sin(FN_GROUP_DIM)
    eye = np.eye(BR // FN_GROUP_DIM, dtype=np.float32)
    w_xf = w_in[:, COL_XF * BR:(COL_XF + 1) * BR]
    w_re = jnp.dot(w_xf, np.kron(eye, cg), precision=HIGHEST)
    w_im = -jnp.dot(w_xf, np.kron(eye, sg), precision=HIGHEST)
    return jnp.concatenate([w_in[:, :COL_XF * BR], w_re, w_im, w_in[:, (COL_XF + 1) * BR:]], axis=1)


def _input_projection(x, moe, g, b, w_in, tm=512):
    t, d = x.shape
    n = w_in.shape[1]
    widths = (d,) + tuple((hi - lo) * BR for lo, hi in PROJ_GROUPS)
    dtypes = (F32, BF16, F32, F32, F32)
    row = pl.BlockSpec((tm, d), lambda i: (i, 0))
    vec = pl.BlockSpec((1, d), lambda i: (0, 0))
    acts, act_specs = [x], [row]
    if moe is not None:
        acts += list(moe)
        act_specs += [row, row, pl.BlockSpec((tm, LANES), lambda i: (i, 0))]
    return pl.pallas_call(
        functools.partial(_proj_kernel, after_moe=moe is not None), grid=(t // tm,),
        in_specs=act_specs + [vec, vec, pl.BlockSpec((d, n), lambda i: (0, 0))],
        out_specs=[pl.BlockSpec((tm, w), lambda i: (i, 0)) for w in widths],
        out_shape=[jax.ShapeDtypeStruct((t, w), dt) for w, dt in zip(widths, dtypes)],
        compiler_params=_cparams("parallel"), name="ln_in_proj",
    )(*acts, g.reshape(1, d), b.reshape(1, d), w_in)


def _attn_bias_table(rpb):
    c = np.arange(GRID_W)
    win_c0 = np.clip(c - NA_COLS // 2, 0, GRID_W - NA_COLS)
    valid = (c[None, :] >= win_c0[:, None]) & (c[None, :] < win_c0[:, None] + NA_COLS)
    dc = np.clip(c[None, :] - c[:, None], -(NA_COLS - 1), NA_COLS - 1) + NA_COLS - 1
    dr = np.arange(NA_ROWS)[None, :] - np.arange(NA_ROWS)[:, None] + NA_ROWS - 1
    row_hot = (dr[:, :, None] == np.arange(2 * NA_ROWS - 1)).astype(np.float32)
    col_hot = (dc[:, :, None] == np.arange(2 * NA_COLS - 1)).astype(np.float32)
    t = jnp.einsum('oia,hab,qkb->ohqik', row_hot, rpb.astype(F32), col_hot, precision=HIGHEST)
    t = jnp.where(valid[None, None, :, None, :], t, NEG_INF)
    return t.reshape(NA_ROWS, N_HEADS * GRID_W, NA_ROWS * GRID_W)


def _attn_kernel(q_ref, k_ref, v_ref, bias_ref, o_ref, *, n_rows, rows_per_tile):
    i = pl.program_id(1)
    head = _head_of_lane((GRID_W, BR))
    scale = HEAD_DIM ** -0.5
    span = NA_ROWS * GRID_W
    for j in range(rows_per_tile):
        r = i * rows_per_tile + j
        r0 = jnp.clip(r - NA_ROWS // 2, 0, n_rows - NA_ROWS)
        q = q_ref[0, j * GRID_W:(j + 1) * GRID_W, :]
        qs = _head_stack(q)
        start = pl.multiple_of(r0 * GRID_W, GRID_W)
        ks = k_ref[0, pl.ds(start, span), :]
        vs = v_ref[0, pl.ds(start, span), :]
        s = lax.dot_general(qs, ks, (((1,), (1,)), ((), ())), preferred_element_type=F32)
        s = s * scale + bias_ref[r - r0]
        m = jnp.max(s, axis=-1, keepdims=True)
        e = jnp.exp(s - m)
        p = e / jnp.sum(e, axis=-1, keepdims=True)
        o4 = jnp.dot(p.astype(BF16), vs, preferred_element_type=F32)
        o = jnp.zeros((GRID_W, BR), F32)
        for h in range(N_HEADS):
            o = o + jnp.where(head == h, o4[h * GRID_W:(h + 1) * GRID_W, :], 0.0)
        o_ref[0, j * GRID_W:(j + 1) * GRID_W, :] = o.astype(o_ref.dtype)


def _neighbourhood_attention(pa, bias, rows_per_tile=8):
    b, l, _ = pa.shape
    n_rows = l // GRID_W
    tq = rows_per_tile * GRID_W
    kern = functools.partial(_attn_kernel, n_rows=n_rows, rows_per_tile=rows_per_tile)
    return pl.pallas_call(
        kern, grid=(b, l // tq),
        in_specs=[pl.BlockSpec((1, tq, BR), lambda bi, i: (bi, i, 0)),
                  pl.BlockSpec((1, l, BR), lambda bi, i: (bi, 0, 1)),
                  pl.BlockSpec((1, l, BR), lambda bi, i: (bi, 0, 2)),
                  pl.BlockSpec(bias.shape, lambda bi, i: (0, 0, 0))],
        out_specs=pl.BlockSpec((1, tq, BR), lambda bi, i: (bi, i, 0)),
        out_shape=jax.ShapeDtypeStruct((b, l, BR), BF16),
        compiler_params=_cparams("parallel", "arbitrary"), name="nbr_attn",
    )(pa, pa, pa, bias)


def _conv_kernel(xb_ref, gb_ref, gc_ref, xbp_ref, gcp_ref, xbn_ref, gcn_ref, w_ref, o_ref, *, n_tiles):
    i = pl.program_id(1)
    tl = xb_ref.shape[1]
    u = gc_ref[0] * xb_ref[0]
    u_before = jnp.where(i > 0, gcp_ref[0, 7:8, :] * xbp_ref[0, 7:8, :], 0.0)
    u_after = jnp.where(i < n_tiles - 1, gcn_ref[0, 0:1, :] * xbn_ref[0, 0:1, :], 0.0)
    row = lax.broadcasted_iota(jnp.int32, (tl, 1), 0)
    u_prev = jnp.where(row == 0, u_before, pltpu.roll(u, 1, 0))
    u_next = jnp.where(row == tl - 1, u_after, pltpu.roll(u, tl - 1, 0))
    conv = w_ref[0:1, :] * u_prev + w_ref[1:2, :] * u + w_ref[2:3, :] * u_next
    o_ref[0] = (gb_ref[0] * conv).astype(o_ref.dtype)


def _short_conv(pb, conv_w, tl=1024):
    b, l, _ = pb.shape
    n_tiles = l // tl
    per8 = tl // 8
    last8 = l // 8 - 1

    def main(col):
        return pl.BlockSpec((1, tl, BR), lambda bi, i: (bi, i, col))

    def before(col):
        return pl.BlockSpec((1, 8, BR), lambda bi, i: (bi, jnp.maximum(i * per8 - 1, 0), col))

    def after(col):
        return pl.BlockSpec((1, 8, BR), lambda bi, i: (bi, jnp.minimum((i + 1) * per8, last8), col))

    return pl.pallas_call(
        functools.partial(_conv_kernel, n_tiles=n_tiles), grid=(b, n_tiles),
        in_specs=[main(0), main(1), main(2), before(0), before(2), after(0), after(2),
                  pl.BlockSpec(conv_w.shape, lambda bi, i: (0, 0))],
        out_specs=pl.BlockSpec((1, tl, BR), lambda bi, i: (bi, i, 0)),
        out_shape=jax.ShapeDtypeStruct((b, l, BR), BF16),
        compiler_params=_cparams("parallel", "parallel"), name="short_conv",
    )(pb, pb, pb, pb, pb, pb, pb, conv_w)


def _dft_cos_sin(n):
    ang = 2.0 * np.pi * ((np.arange(n)[:, None] * np.arange(n)[None, :]) % n) / n
    return np.cos(ang).astype(np.float32), np.sin(ang).astype(np.float32)


def _fft_major_kernel(g_ref, m_ref, tc_ref, ts_ref, br_ref, bi_ref):
    dot = functools.partial(jnp.dot, precision=HIGHEST, preferred_element_type=F32)
    for s in range(8):
        g = g_ref[0, :, s, :]
        swapped = jnp.concatenate([g[:, BR:], -g[:, :BR]], axis=1)
        a = dot(m_ref[0], g) + dot(m_ref[1], swapped)
        ar, ai = a[:, :BR], a[:, BR:]
        tc = tc_ref[:, s, :]
        ts = ts_ref[:, s, :]
        br_ref[0, :, s, :] = ar * tc + ai * ts
        bi_ref[0, :, s, :] = ai * tc - ar * ts


def _fft_minor_kernel(br_ref, bi_ref, m_ref, o_ref, *, scale):
    c_hi, c_lo, s_hi, s_lo = m_ref[0], m_ref[1], m_ref[2], m_ref[3]
    for j in range(8):
        y = (_dot_split(c_hi, c_lo, *_split_bf16(br_ref[0, j]))
             + _dot_split(s_hi, s_lo, *_split_bf16(bi_ref[0, j])))
        o_ref[0, :, j, :] = (y * scale).astype(o_ref.dtype)


def _fft_twiddles(l):
    l1 = FFT_L1
    l2 = l // l1
    idx = (jnp.arange(l1, dtype=jnp.int32)[:, None] * jnp.arange(l2, dtype=jnp.int32)[None, :]) % l
    ang = idx.astype(F32) * (2.0 * np.pi / l)
    tc = jnp.broadcast_to(jnp.cos(ang)[:, :, None], (l1, l2, BR))
    ts = jnp.broadcast_to(jnp.sin(ang)[:, :, None], (l1, l2, BR))
    return tc, ts


def _fourier_mix(pc, tc, ts):
    b, l, _ = pc.shape
    l1 = FFT_L1
    l2 = l // l1

    def split_parts(n):
        parts = []
        for m in _dft_cos_sin(n):
            hi = jnp.asarray(m).astype(BF16)
            parts += [hi, (jnp.asarray(m) - hi.astype(F32)).astype(BF16)]
        return jnp.stack(parts)

    def strip(width):
        return pl.BlockSpec((1, l1, 8, width), lambda bi, i: (bi, 0, i, 0))

    tspec = pl.BlockSpec((l1, 8, BR), lambda bi, i: (0, i, 0))
    mspec = pl.BlockSpec((2, l1, l1), lambda bi, i: (0, 0, 0))
    br, bi_ = pl.pallas_call(
        _fft_major_kernel, grid=(b, l2 // 8),
        in_specs=[strip(2 * BR), mspec, tspec, tspec], out_specs=[strip(BR), strip(BR)],
        out_shape=[jax.ShapeDtypeStruct((b, l1, l2, BR), F32)] * 2,
        compiler_params=_cparams("parallel", "parallel"), name="fft_major",
    )(pc.reshape(b, l1, l2, 2 * BR), np.stack(_dft_cos_sin(l1)), tc, ts)

    bspec = pl.BlockSpec((1, 8, l2, BR), lambda bi, i: (bi, i, 0, 0))
    m2spec = pl.BlockSpec((4, l2, l2), lambda bi, i: (0, 0, 0))
    scale = 1.0 / float(np.sqrt(l * FN_GROUP_DIM))
    y = pl.pallas_call(
        functools.partial(_fft_minor_kernel, scale=scale), grid=(b, l1 // 8),
        in_specs=[bspec, bspec, m2spec],
        out_specs=pl.BlockSpec((1, l2, 8, BR), lambda bi, i: (bi, 0, i, 0)),
        out_shape=jax.ShapeDtypeStruct((b, l2, l1, BR), F32),
        compiler_params=_cparams("parallel", "parallel"), name="fft_minor",
    )(br, bi_, split_parts(l2))
    return y.reshape(b, l, BR)


def _hgrn_level_masks():
    t = np.arange(HG_CHUNK)
    out = []
    half = 8
    while 2 * half < HG_CHUNK:
        same = (t[:, None] // (2 * half)) == (t[None, :] // (2 * half))
        out.append(np.tile(same, (1, N_HEADS)).astype(np.float32))
        half *= 2
    return np.stack(out)


def _hgrn_kernel(qd_ref, fl_ref, vd_ref, lb_ref, mask_ref, bd_ref, o_ref, st_ref, *, first_layer, chunks_per_tile,
                 rev):
    c = HG_CHUNK

    @pl.when(pl.program_id(1) == 0)
    def _():
        st_ref[...] = jnp.zeros_like(st_ref)

    bd = bd_ref[...]
    bd16 = bd.astype(BF16)
    lb = lb_ref[...]
    row = lax.broadcasted_iota(jnp.int32, (c, 1), 0)
    sub = row % 8
    nt = (((1,), (1,)), ((), ()))

    def roll8(x, j):
        return pltpu.roll(x.reshape(c // 8, 8, BR), j % 8, 1).reshape(c, BR)

    def earlier(x, j):
        return roll8(x, -j if rev else j)

    def later(x, j):
        return roll8(x, j if rev else -j)

    def has_earlier(j):
        return (sub + j <= 7) if rev else (sub >= j)

    def has_later(j):
        return (sub >= j) if rev else (sub + j <= 7)

    for ci in (reversed(range(chunks_per_tile)) if rev else range(chunks_per_tile)):
        sl = slice(ci * c, (ci + 1) * c)
        qd = qd_ref[0, sl, :]
        fl = fl_ref[0, sl, :]
        v = vd_ref[0, sl, :]
        q = qd * _sigmoid(qd)
        if first_layer:
            f = _sigmoid(fl)
            k = _sigmoid(-fl)
        else:
            f = lb + (1.0 - lb) * _sigmoid(fl)
            k = (1.0 - lb) * _sigmoid(-fl)

        o = jnp.dot((q * k).astype(BF16), bd16, preferred_element_type=F32) * v
        g = f
        for d in range(1, 8):
            if d > 1:
                g = g * earlier(f, d - 1)
            p = jnp.where(has_earlier(d), q * g * earlier(k, d), 0.0)
            o = o + jnp.dot(p.astype(BF16), bd16, preferred_element_type=F32) * earlier(v, d)

        qf = f
        for j in (1, 2, 4):
            qf = qf * jnp.where(has_earlier(j), earlier(qf, j), 1.0)
        kf = jnp.where(has_later(1), later(f, 1), 1.0)
        for j in (1, 2, 4):
            kf = kf * jnp.where(has_later(j), later(kf, j), 1.0)

        scores = None
        half = 8
        level = 0
        while half < c:
            late = ((row // half) % 2) == (0 if rev else 1)
            qb = jnp.where(late, q * qf, 0.0).astype(BF16)
            kb = jnp.where(late, 0.0, k * kf).astype(BF16)
            x = lax.dot_general(qb, _head_stack(kb), nt, preferred_element_type=F32)
            if 2 * half < c:
                x = x * mask_ref[level]
            scores = x if scores is None else scores + x
            blocks = c // (2 * half)
            qf3 = qf.reshape(blocks, 2 * half, BR)
            end_early = half if rev else half - 1
            end_late = 0 if rev else 2 * half - 1
            tot_early = jnp.broadcast_to(qf3[:, end_early:end_early + 1, :], qf3.shape).reshape(c, BR)
            tot_late = jnp.broadcast_to(qf3[:, end_late:end_late + 1, :], qf3.shape).reshape(c, BR)
            qf = qf * jnp.where(late, tot_early, 1.0)
            kf = kf * jnp.where(late, 1.0, tot_late)
            half *= 2
            level += 1

        o = o + jnp.dot(scores.astype(BF16), _head_stack(v.astype(BF16)), preferred_element_type=F32)

        st = st_ref[...]
        o = o + lax.dot_general((q * qf).astype(BF16), st.astype(BF16), nt, preferred_element_type=F32)
        o_ref[0, sl, :] = o
        upd = jnp.dot(v.T.astype(BF16), (k * kf).astype(BF16), preferred_element_type=F32)
        last = 0 if rev else c - 1
        st_ref[...] = st * qf[last:last + 1, :] + upd * bd


def _hgrn_scan(qd, fl, vd, cols, lb, first_layer, rev, chunks_per_tile=4):
    b, l, _ = qd.shape
    tile = HG_CHUNK * chunks_per_tile
    n_tiles = l // tile
    masks = _hgrn_level_masks()
    bd = _same_head_matrix()

    def tile_index(i):
        return n_tiles - 1 - i if rev else i

    def seq(col):
        return pl.BlockSpec((1, tile, BR), lambda bi, i: (bi, tile_index(i), col))

    return pl.pallas_call(
        functools.partial(_hgrn_kernel, first_layer=first_layer, chunks_per_tile=chunks_per_tile, rev=rev),
        grid=(b, n_tiles),
        in_specs=[seq(cols[0]), seq(cols[1]), seq(cols[2]),
                  pl.BlockSpec((1, BR), lambda bi, i: (0, 0)),
                  pl.BlockSpec(masks.shape, lambda bi, i: (0, 0, 0)),
                  pl.BlockSpec((BR, BR), lambda bi, i: (0, 0))],
        out_specs=pl.BlockSpec((1, tile, BR), lambda bi, i: (bi, tile_index(i), 0)),
        out_shape=jax.ShapeDtypeStruct((b, l, BR), F32),
        scratch_shapes=[pltpu.VMEM((BR, BR), F32)],
        compiler_params=_cparams("parallel", "arbitrary"),
        name="hgrn_bwd" if rev else "hgrn_fwd",
    )(qd, fl, vd, lb.reshape(1, BR), masks, bd)


def _hgrn2_bidirectional(pd, lb_f, lb_b, first_layer):
    o_f = _hgrn_scan(pd, pd, pd, (0, 1, 3), lb_f, first_layer, rev=False)
    o_b = _hgrn_scan(pd, pd, pd, (0, 2, 3), lb_b, first_layer, rev=True)
    return o_f, o_b


def _route(x, w_hi, w_lo, bias, tri, run):
    x_hi, x_lo = _split_bf16(x)
    logits = _dot_split(x_hi, x_lo, w_hi, w_lo) + bias
    lane = lax.broadcasted_iota(jnp.int32, logits.shape, 1)
    big = jnp.int32(LANES)
    is_group = (lane >= N_EXPERTS) & (lane < N_EXPERTS + N_GROUPS)
    lg = jnp.where(is_group, logits, -jnp.inf)
    mg = jnp.max(lg, axis=-1, keepdims=True)
    g_sel = jnp.min(jnp.where(lg == mg, lane, big), axis=-1, keepdims=True) - N_EXPERTS
    p_g = 1.0 / jnp.sum(jnp.exp(lg - mg), axis=-1, keepdims=True)
    in_group = (lane < N_EXPERTS) & ((lane // EXPERTS_PER_GROUP) == g_sel)
    le = jnp.where(in_group, logits, -jnp.inf)
    v1 = jnp.max(le, axis=-1, keepdims=True)
    i1 = jnp.min(jnp.where(le == v1, lane, big), axis=-1, keepdims=True)
    le2 = jnp.where(lane == i1, -jnp.inf, le)
    v2 = jnp.max(le2, axis=-1, keepdims=True)
    i2 = jnp.min(jnp.where(le2 == v2, lane, big), axis=-1, keepdims=True)
    e2 = jnp.exp(v2 - v1)
    w1 = p_g / (1.0 + e2)
    w2 = p_g * e2 / (1.0 + e2)
    hot1 = lane == i1
    hot2 = lane == i2
    hot = (hot1 | hot2).astype(BF16)
    before = jnp.dot(tri, hot, preferred_element_type=F32) + run
    rank1 = jnp.sum(jnp.where(hot1, before, 0.0), axis=-1, keepdims=True).astype(jnp.int32)
    rank2 = jnp.sum(jnp.where(hot2, before, 0.0), axis=-1, keepdims=True).astype(jnp.int32)
    run = run + jnp.sum(hot.astype(F32), axis=0, keepdims=True)
    e = jnp.where(lane == 0, i1, jnp.where(lane == 1, i2, jnp.where(lane == 2, rank1,
                                                                     jnp.where(lane == 3, rank2, 0))))
    p = jnp.where(lane == 0, w1, jnp.where(lane == 1, w2, 0.0))
    return e, p, run


def _merge_kernel(x_ref, ya_ref, yb_ref, yc_ref, of_ref, ob_ref, gd_ref, ng_ref, bd_ref, wg_ref, bg_ref, wbr_ref,
                  wo_ref, g_ref, b_ref, rhi_ref, rlo_ref, rb_ref, tri_ref, o_ref, o16_ref, e_ref, p_ref, cnt_ref,
                  run_ref):
    @pl.when(pl.program_id(0) == 0)
    def _():
        run_ref[...] = jnp.zeros_like(run_ref)

    od = of_ref[...] + ob_ref[...]
    ms = jnp.dot(od * od, bd_ref[...], precision=HIGHEST, preferred_element_type=F32) * (1.0 / HEAD_DIM)
    gd = gd_ref[...]
    y_d = od * lax.rsqrt(ms + RMS_EPS) * ng_ref[...] * (gd * _sigmoid(gd))

    x = x_ref[...]
    xb = x.astype(BF16)
    merged = None
    for i, y in enumerate((ya_ref[...], yb_ref[...], yc_ref[...], y_d)):
        gate = _sigmoid(jnp.dot(xb, wg_ref[i], preferred_element_type=F32) + bg_ref[i])
        term = gate * jnp.dot(y.astype(BF16), wbr_ref[i], preferred_element_type=F32)
        merged = term if merged is None else merged + term
    mix = jnp.dot(merged.astype(BF16), wo_ref[...], preferred_element_type=F32)
    out = _ln_rows(DEEPNORM_ALPHA * x + mix, g_ref[...], b_ref[...])
    o_ref[...] = out
    o16_ref[...] = out.astype(BF16)
    e, p, run = _route(out, rhi_ref[...], rlo_ref[...], rb_ref[...], tri_ref[...], run_ref[...])
    e_ref[...] = e.T[:e_ref.shape[0], :]
    p_ref[...] = p
    run_ref[...] = run
    cnt_ref[...] = run.astype(jnp.int32)


def _merge(x, y_a, y_b, y_c, o_f, o_b, pd, norm_g, w_gate, b_gate, w_br, w_out, g, b, wg, bg, we, be, tm=512):
    t, d = x.shape
    row = pl.BlockSpec((tm, d), lambda i: (i, 0))
    yrow = pl.BlockSpec((tm, BR), lambda i: (i, 0))
    vec = pl.BlockSpec((1, d), lambda i: (0, 0))
    lanes = pl.BlockSpec((tm, LANES), lambda i: (i, 0))
    lvec = pl.BlockSpec((1, LANES), lambda i: (0, 0))

    def whole(a):
        return pl.BlockSpec(a.shape, lambda i: (0,) * a.ndim)

    bgate = b_gate.reshape(b_gate.shape[0], 1, d)
    rw = jnp.zeros((d, LANES), F32).at[:, :N_EXPERTS].set(we).at[:, N_EXPERTS:N_EXPERTS + N_GROUPS].set(wg)
    rb = jnp.zeros((1, LANES), F32).at[0, :N_EXPERTS].set(be).at[0, N_EXPERTS:N_EXPERTS + N_GROUPS].set(bg)
    r_hi, r_lo = _split_bf16(rw)
    tri = np.tril(np.ones((tm, tm), np.float32), -1).astype(BF16)
    ng = jnp.tile(norm_g, N_HEADS).reshape(1, BR)
    bd = _same_head_matrix()
    return pl.pallas_call(
        _merge_kernel, grid=(t // tm,),
        in_specs=[row, yrow, yrow, yrow, yrow, yrow, pl.BlockSpec((tm, BR), lambda i: (i, 4)), whole(ng), whole(bd),
                  whole(w_gate), whole(bgate), whole(w_br), whole(w_out), vec, vec,
                  whole(r_hi), whole(r_lo), lvec, whole(tri)],
        out_specs=[row, row, pl.BlockSpec((8, tm), lambda i: (0, i)), lanes, lvec],
        out_shape=[jax.ShapeDtypeStruct((t, d), F32), jax.ShapeDtypeStruct((t, d), BF16),
                   jax.ShapeDtypeStruct((8, t), jnp.int32), jax.ShapeDtypeStruct((t, LANES), F32),
                   jax.ShapeDtypeStruct((1, LANES), jnp.int32)],
        scratch_shapes=[pltpu.VMEM((1, LANES), F32)],
        compiler_params=_cparams("arbitrary"), name="merge_ln1_route",
    )(x, y_a, y_b, y_c, o_f, o_b, pd, ng, bd, w_gate, bgate, w_br, w_out, g.reshape(1, d), b.reshape(1, d),
      r_hi, r_lo, rb, tri)


def _expert_work_items(counts, n_blocks):
    end = jnp.cumsum(counts)
    start = end - counts
    first_blk = start // MOE_ROWS
    n_e = jnp.where(counts > 0, (end - 1) // MOE_ROWS - first_blk + 1, 0)
    item_end = jnp.cumsum(n_e)
    item_start = item_end - n_e
    w = jnp.arange(n_blocks + N_EXPERTS, dtype=jnp.int32)
    live = w < item_end[-1]
    wc = jnp.minimum(w, item_end[-1] - 1)
    e_w = jnp.sum((item_end[None, :] <= wc[:, None]).astype(jnp.int32), axis=1)
    blk = first_blk[e_w] + wc - item_start[e_w]
    lo = jnp.where(live, jnp.maximum(start[e_w], blk * MOE_ROWS) - blk * MOE_ROWS, 0)
    hi = jnp.where(live, jnp.minimum(end[e_w], (blk + 1) * MOE_ROWS) - blk * MOE_ROWS, 0)
    first = jnp.concatenate([jnp.ones((1,), jnp.int32), (blk[1:] != blk[:-1]).astype(jnp.int32)])
    return jnp.stack([e_w, blk, lo, hi, first]).astype(jnp.int32)


def _expert_kernel(items_ref, x_ref, w13_ref, w2_ref, o_ref):
    w = pl.program_id(0)
    lo = items_ref[2, w]
    hi = items_ref[3, w]

    @pl.when(hi > lo)
    def _():
        h = jnp.dot(x_ref[...], w13_ref[0], preferred_element_type=F32)
        hg = h[:, :D_FF_E]
        act = hg * _sigmoid(hg) * h[:, D_FF_E:]
        y = jnp.dot(act.astype(BF16), w2_ref[0], preferred_element_type=F32).astype(o_ref.dtype)
        row = lax.broadcasted_iota(jnp.int32, (MOE_ROWS, 1), 0)
        mine = (row >= lo) & (row < hi)

        @pl.when(items_ref[4, w] == 1)
        def _():
            o_ref[...] = jnp.where(mine, y, jnp.zeros_like(y))

        @pl.when(items_ref[4, w] == 0)
        def _():
            o_ref[...] = jnp.where(mine, y, o_ref[...])


def _expert_blocks(xs, items, w13, w2):
    n_rows, d = xs.shape
    grid_spec = pltpu.PrefetchScalarGridSpec(
        num_scalar_prefetch=1, grid=(items.shape[1],),
        in_specs=[pl.BlockSpec((MOE_ROWS, d), lambda w, it: (it[1, w], 0)),
                  pl.BlockSpec((1, d, 2 * D_FF_E), lambda w, it: (it[0, w], 0, 0)),
                  pl.BlockSpec((1, D_FF_E, d), lambda w, it: (it[0, w], 0, 0))],
        out_specs=pl.BlockSpec((MOE_ROWS, d), lambda w, it: (it[1, w], 0)))
    return pl.pallas_call(
        _expert_kernel, grid_spec=grid_spec, out_shape=jax.ShapeDtypeStruct((n_rows, d), BF16),
        compiler_params=_cparams("arbitrary"), name="moe_experts",
    )(items, xs, w13, w2)


def _combine_kernel(x_ref, y0_ref, y1_ref, p_ref, g_ref, b_ref, o_ref):
    o_ref[...] = _ln_rows(_moe_residual(x_ref, y0_ref, y1_ref, p_ref), g_ref[...], b_ref[...])


def _combine(x, y0, y1, p, g, b, tm=1024):
    t, d = x.shape
    row = pl.BlockSpec((tm, d), lambda i: (i, 0))
    vec = pl.BlockSpec((1, d), lambda i: (0, 0))
    return pl.pallas_call(
        _combine_kernel, grid=(t // tm,),
        in_specs=[row, row, row, pl.BlockSpec((tm, LANES), lambda i: (i, 0)), vec, vec],
        out_specs=row, out_shape=jax.ShapeDtypeStruct((t, d), F32),
        compiler_params=_cparams("parallel"), name="moe_combine_ln2",
    )(x, y0, y1, p, g.reshape(1, d), b.reshape(1, d))


def _expert_outputs(x16, routing, cnt, w13, w2):
    t = x16.shape[0]
    tk = t * TOP_K
    counts = cnt[0, :N_EXPERTS]
    start = jnp.cumsum(counts) - counts
    dest0 = jnp.take(start, routing[0], mode='clip') + routing[2]
    dest1 = jnp.take(start, routing[1], mode='clip') + routing[3]
    order = jnp.argsort(jnp.stack([routing[0], routing[1]], axis=1).reshape(tk)).astype(jnp.int32)
    xs = jnp.take(x16, order // TOP_K, axis=0, mode='clip')
    ys = _expert_blocks(xs, _expert_work_items(counts, tk // MOE_ROWS), w13, w2)
    return jnp.take(ys, dest0, axis=0, mode='clip'), jnp.take(ys, dest1, axis=0, mode='clip')


def _trunk(x, prm):
    b, l, d = x.shape
    t = b * l
    x = x.reshape(t, d)
    moe, g, bias = None, prm['ln_in_g'], prm['ln_in_b']
    tc, ts = _fft_twiddles(l)
    for i in range(DEPTH):
        x, pa, pb, pc, pd = _input_projection(x, moe, g, bias, prm['w_in'][i])
        y_a = _neighbourhood_attention(pa.reshape(b, l, -1), prm['attn_bias'][i])
        y_b = _short_conv(pb.reshape(b, l, -1), prm['conv_w'][i])
        y_c = _fourier_mix(pc.reshape(b, l, -1), tc, ts)
        o_f, o_b = _hgrn2_bidirectional(pd.reshape(b, l, -1), prm['lb'][0, i], prm['lb'][1, i], i == 0)
        x, x16, routing, p_lanes, cnt = _merge(
            x, y_a.reshape(t, BR), y_b.reshape(t, BR), y_c.reshape(t, BR), o_f.reshape(t, BR), o_b.reshape(t, BR),
            pd, prm['hgrn_norm_g'][i], prm['w_gate'][i], prm['b_gate'][i], prm['w_br'][i], prm['w_out'][i],
            prm['ln1_g'][i], prm['ln1_b'][i], prm['router_g_w'][i], prm['router_g_b'][i], prm['router_e_w'][i],
            prm['router_e_b'][i])
        y0, y1 = _expert_outputs(x16, routing, cnt, prm['w13'][i], prm['w2'][i])
        moe, g, bias = (y0, y1, p_lanes), prm['ln2_g'][i], prm['ln2_b'][i]
    return _combine(x, *moe, g, bias).reshape(b, l, d)


def kernel(x_prompt, x_sample, ln_in_g, ln_in_b, w_in, na_rpb, conv_w, hgrn_lb, hgrn_norm_g, w_gate, b_gate, w_br,
           w_out, ln1_g, ln1_b, router_g_w, router_g_b, router_e_w, router_e_b, w13, w2, ln2_g, ln2_b):
    lb = jnp.cumsum(jax.nn.softmax(hgrn_lb.astype(F32), axis=1), axis=1)
    lb = lb - lb[:, :1]
    prm = dict(
        ln_in_g=ln_in_g, ln_in_b=ln_in_b,
        w_in=jnp.stack([_fold_channel_dft(w_in[i]) for i in range(DEPTH)]).astype(BF16),
        attn_bias=jnp.stack([_attn_bias_table(na_rpb[i]) for i in range(DEPTH)]),
        conv_w=conv_w, lb=lb, hgrn_norm_g=hgrn_norm_g,
        w_gate=w_gate.astype(BF16), b_gate=b_gate, w_br=w_br.astype(BF16), w_out=w_out.astype(BF16),
        ln1_g=ln1_g, ln1_b=ln1_b, router_g_w=router_g_w, router_g_b=router_g_b,
        router_e_w=router_e_w, router_e_b=router_e_b, w13=w13.astype(BF16), w2=w2.astype(BF16),
        ln2_g=ln2_g, ln2_b=ln2_b)
    return _trunk(x_prompt, prm), _trunk(x_sample, prm)
```

```python
import functools

import numpy as np
import jax
import jax.numpy as jnp
from jax import lax
from jax.experimental import pallas as pl
from jax.experimental.pallas import tpu as pltpu
from jax.experimental.pallas import tpu_sc as plsc

F32 = jnp.float32
BF16 = jnp.bfloat16
HIGHEST = lax.Precision.HIGHEST

D_MODEL = 1024
DEPTH = 4
GRID_W = 64
BR = 256
N_HEADS = 4
HEAD_DIM = 64
NA_ROWS = 8
NA_COLS = 16
FN_GROUP_DIM = 64
N_GROUPS = 4
EXPERTS_PER_GROUP = 8
N_EXPERTS = N_GROUPS * EXPERTS_PER_GROUP
TOP_K = 2
D_FF_E = 512
DEEPNORM_ALPHA = (2 * DEPTH) ** 0.25
LN_EPS = 1e-5
RMS_EPS = 1e-6
NEG_INF = -1e30

LANES = 128
VMEM_LIMIT_BYTES = 56 * 1024 * 1024
MOE_ROWS = 512
HG_CHUNK = 128
FFT_L1 = 64
SC_WINDOW = 64

COL_QA, COL_KA, COL_VA, COL_XB, COL_GB, COL_GC, COL_XF, COL_QD, COL_FF, COL_FB, COL_VD, COL_GD = range(12)
PROJ_GROUPS = ((0, 3), (3, 6), (6, 8), (8, 13))


def _cparams(*sem):
    return pltpu.CompilerParams(dimension_semantics=sem, vmem_limit_bytes=VMEM_LIMIT_BYTES)


def _sigmoid(x):
    return 1.0 / (1.0 + jnp.exp(-x))


def _split_bf16(x):
    hi = x.astype(BF16)
    return hi, (x - hi.astype(F32)).astype(BF16)


def _dot_split(a_hi, a_lo, b_hi, b_lo):
    dot = functools.partial(jnp.dot, preferred_element_type=F32)
    return dot(a_hi, b_hi) + (dot(a_lo, b_hi) + dot(a_hi, b_lo))


def _pack_bf16_pairs(x):
    m = x.shape[1] // 2
    hi = lax.bitcast_convert_type(x[:, :m].astype(BF16).astype(F32), jnp.uint32)
    lo = lax.bitcast_convert_type(x[:, m:].astype(BF16).astype(F32), jnp.uint32)
    return hi | (lo >> 16)


def _unpack_bf16_pairs(u):
    hi = lax.bitcast_convert_type(u & jnp.uint32(0xFFFF0000), F32)
    lo = lax.bitcast_convert_type(u << 16, F32)
    return jnp.concatenate([hi, lo], axis=1)


def _ln_rows(x, g, b):
    mu = jnp.mean(x, axis=-1, keepdims=True)
    xc = x - mu
    var = jnp.mean(xc * xc, axis=-1, keepdims=True)
    return xc * lax.rsqrt(var + LN_EPS) * g + b


def _head_of_lane(shape):
    return lax.broadcasted_iota(jnp.int32, shape, len(shape) - 1) // HEAD_DIM


def _head_stack(x):
    head = _head_of_lane(x.shape)
    return jnp.concatenate([jnp.where(head == h, x, jnp.zeros_like(x)) for h in range(N_HEADS)], axis=0)


def _same_head_matrix():
    h = np.arange(BR) // HEAD_DIM
    return (h[:, None] == h[None, :]).astype(np.float32)


def _moe_residual(x_ref, y0_ref, y1_ref, p_ref):
    p = p_ref[...]
    ffn = p[:, 0:1] * _unpack_bf16_pairs(y0_ref[...]) + p[:, 1:2] * _unpack_bf16_pairs(y1_ref[...])
    return DEEPNORM_ALPHA * x_ref[...] + ffn


def _proj_kernel(*refs, after_moe):
    if after_moe:
        x_ref, y0_ref, y1_ref, p_ref, g_ref, b_ref, w_ref, xo_ref, *out_refs = refs
        pre = _moe_residual(x_ref, y0_ref, y1_ref, p_ref)
    else:
        x_ref, g_ref, b_ref, w_ref, xo_ref, *out_refs = refs
        pre = x_ref[...]
    x = _ln_rows(pre, g_ref[...], b_ref[...])
    xo_ref[...] = x
    xb = x.astype(BF16)
    for ref, (lo, hi) in zip(out_refs, PROJ_GROUPS):
        ref[...] = jnp.dot(xb, w_ref[:, lo * BR:hi * BR], preferred_element_type=F32).astype(ref.dtype)


def _fold_channel_dft(w_in):
    cg, sg = _dft_cos_sin(FN_GROUP_DIM)
    eye = np.eye(BR // FN_GROUP_DIM, dtype=np.float32)
    w_xf = w_in[:, COL_XF * BR:(COL_XF + 1) * BR]
    w_re = jnp.dot(w_xf, np.kron(eye, cg), precision=HIGHEST)
    w_im = -jnp.dot(w_xf, np.kron(eye, sg), precision=HIGHEST)
    return jnp.concatenate([w_in[:, :COL_XF * BR], w_re, w_im, w_in[:, (COL_XF + 1) * BR:]], axis=1)


def _pair_specs(t, d, tm):
    return [pl.BlockSpec((tm, d // 2), lambda i: (i, 0)), pl.BlockSpec((tm, d // 2), lambda i: (i + t // tm, 0))]


def _input_projection(x, moe, g, b, w_in, tm=512):
    t, d = x.shape
    n = w_in.shape[1]
    widths = (d,) + tuple((hi - lo) * BR for lo, hi in PROJ_GROUPS)
    dtypes = (F32, BF16, F32, F32, F32)
    row = pl.BlockSpec((tm, d), lambda i: (i, 0))
    vec = pl.BlockSpec((1, d), lambda i: (0, 0))
    acts, act_specs = [x], [row]
    if moe is not None:
        y01, p = moe
        acts += [y01, y01, p]
        act_specs += _pair_specs(t, d, tm) + [pl.BlockSpec((tm, LANES), lambda i: (i, 0))]
    return pl.pallas_call(
        functools.partial(_proj_kernel, after_moe=moe is not None), grid=(t // tm,),
        in_specs=act_specs + [vec, vec, pl.BlockSpec((d, n), lambda i: (0, 0))],
        out_specs=[pl.BlockSpec((tm, w), lambda i: (i, 0)) for w in widths],
        out_shape=[jax.ShapeDtypeStruct((t, w), dt) for w, dt in zip(widths, dtypes)],
        compiler_params=_cparams("parallel"), name="ln_in_proj",
    )(*acts, g.reshape(1, d), b.reshape(1, d), w_in)


def _attn_bias_table(rpb):
    c = np.arange(GRID_W)
    win_c0 = np.clip(c - NA_COLS // 2, 0, GRID_W - NA_COLS)
    valid = (c[None, :] >= win_c0[:, None]) & (c[None, :] < win_c0[:, None] + NA_COLS)
    dc = np.clip(c[None, :] - c[:, None], -(NA_COLS - 1), NA_COLS - 1) + NA_COLS - 1
    dr = np.arange(NA_ROWS)[None, :] - np.arange(NA_ROWS)[:, None] + NA_ROWS - 1
    row_hot = (dr[:, :, None] == np.arange(2 * NA_ROWS - 1)).astype(np.float32)
    col_hot = (dc[:, :, None] == np.arange(2 * NA_COLS - 1)).astype(np.float32)
    t = jnp.einsum('oia,hab,qkb->ohqik', row_hot, rpb.astype(F32), col_hot, precision=HIGHEST)
    t = jnp.where(valid[None, None, :, None, :], t, NEG_INF)
    return t.reshape(NA_ROWS, N_HEADS * GRID_W, NA_ROWS * GRID_W)


def _attn_kernel(q_ref, k_ref, v_ref, bias_ref, o_ref, *, n_rows, rows_per_tile):
    i = pl.program_id(1)
    head = _head_of_lane((GRID_W, BR))
    scale = HEAD_DIM ** -0.5
    span = NA_ROWS * GRID_W
    for j in range(rows_per_tile):
        r = i * rows_per_tile + j
        r0 = jnp.clip(r - NA_ROWS // 2, 0, n_rows - NA_ROWS)
        q = q_ref[0, j * GRID_W:(j + 1) * GRID_W, :]
        qs = _head_stack(q)
        start = pl.multiple_of(r0 * GRID_W, GRID_W)
        ks = k_ref[0, pl.ds(start, span), :]
        vs = v_ref[0, pl.ds(start, span), :]
        s = lax.dot_general(qs, ks, (((1,), (1,)), ((), ())), preferred_element_type=F32)
        s = s * scale + bias_ref[r - r0]
        m = jnp.max(s, axis=-1, keepdims=True)
        e = jnp.exp(s - m)
        p = e / jnp.sum(e, axis=-1, keepdims=True)
        o4 = jnp.dot(p.astype(BF16), vs, preferred_element_type=F32)
        o = jnp.zeros((GRID_W, BR), F32)
        for h in range(N_HEADS):
            o = o + jnp.where(head == h, o4[h * GRID_W:(h + 1) * GRID_W, :], 0.0)
        o_ref[0, j * GRID_W:(j + 1) * GRID_W, :] = o.astype(o_ref.dtype)


def _neighbourhood_attention(pa, bias, rows_per_tile=8):
    b, l, _ = pa.shape
    n_rows = l // GRID_W
    tq = rows_per_tile * GRID_W
    kern = functools.partial(_attn_kernel, n_rows=n_rows, rows_per_tile=rows_per_tile)
    return pl.pallas_call(
        kern, grid=(b, l // tq),
        in_specs=[pl.BlockSpec((1, tq, BR), lambda bi, i: (bi, i, 0)),
                  pl.BlockSpec((1, l, BR), lambda bi, i: (bi, 0, 1)),
                  pl.BlockSpec((1, l, BR), lambda bi, i: (bi, 0, 2)),
                  pl.BlockSpec(bias.shape, lambda bi, i: (0, 0, 0))],
        out_specs=pl.BlockSpec((1, tq, BR), lambda bi, i: (bi, i, 0)),
        out_shape=jax.ShapeDtypeStruct((b, l, BR), BF16),
        compiler_params=_cparams("parallel", "arbitrary"), name="nbr_attn",
    )(pa, pa, pa, bias)


def _conv_kernel(xb_ref, gb_ref, gc_ref, xbp_ref, gcp_ref, xbn_ref, gcn_ref, w_ref, o_ref, *, n_tiles):
    i = pl.program_id(1)
    tl = xb_ref.shape[1]
    u = gc_ref[0] * xb_ref[0]
    u_before = jnp.where(i > 0, gcp_ref[0, 7:8, :] * xbp_ref[0, 7:8, :], 0.0)
    u_after = jnp.where(i < n_tiles - 1, gcn_ref[0, 0:1, :] * xbn_ref[0, 0:1, :], 0.0)
    row = lax.broadcasted_iota(jnp.int32, (tl, 1), 0)
    u_prev = jnp.where(row == 0, u_before, pltpu.roll(u, 1, 0))
    u_next = jnp.where(row == tl - 1, u_after, pltpu.roll(u, tl - 1, 0))
    conv = w_ref[0:1, :] * u_prev + w_ref[1:2, :] * u + w_ref[2:3, :] * u_next
    o_ref[0] = (gb_ref[0] * conv).astype(o_ref.dtype)


def _short_conv(pb, conv_w, tl=1024):
    b, l, _ = pb.shape
    n_tiles = l // tl
    per8 = tl // 8
    last8 = l // 8 - 1

    def main(col):
        return pl.BlockSpec((1, tl, BR), lambda bi, i: (bi, i, col))

    def before(col):
        return pl.BlockSpec((1, 8, BR), lambda bi, i: (bi, jnp.maximum(i * per8 - 1, 0), col))

    def after(col):
        return pl.BlockSpec((1, 8, BR), lambda bi, i: (bi, jnp.minimum((i + 1) * per8, last8), col))

    return pl.pallas_call(
        functools.partial(_conv_kernel, n_tiles=n_tiles), grid=(b, n_tiles),
        in_specs=[main(0), main(1), main(2), before(0), before(2), after(0), after(2),
                  pl.BlockSpec(conv_w.shape, lambda bi, i: (0, 0))],
        out_specs=pl.BlockSpec((1, tl, BR), lambda bi, i: (bi, i, 0)),
        out_shape=jax.ShapeDtypeStruct((b, l, BR), BF16),
        compiler_params=_cparams("parallel", "parallel"), name="short_conv",
    )(pb, pb, pb, pb, pb, pb, pb, conv_w)


def _dft_cos_sin(n):
    ang = 2.0 * np.pi * ((np.arange(n)[:, None] * np.arange(n)[None, :]) % n) / n
    return np.cos(ang).astype(np.float32), np.sin(ang).astype(np.float32)


def _fft_major_kernel(g_ref, m_ref, tc_ref, ts_ref, br_ref, bi_ref):
    dot = functools.partial(jnp.dot, precision=HIGHEST, preferred_element_type=F32)
    for s in range(8):
        g = g_ref[0, :, s, :]
        swapped = jnp.concatenate([g[:, BR:], -g[:, :BR]], axis=1)
        a = dot(m_ref[0], g) + dot(m_ref[1], swapped)
        ar, ai = a[:, :BR], a[:, BR:]
        tc = tc_ref[:, s, :]
        ts = ts_ref[:, s, :]
        br_ref[0, :, s, :] = ar * tc + ai * ts
        bi_ref[0, :, s, :] = ai * tc - ar * ts


def _fft_minor_kernel(br_ref, bi_ref, m_ref, o_ref, *, scale):
    c_hi, c_lo, s_hi, s_lo = m_ref[0], m_ref[1], m_ref[2], m_ref[3]
    for j in range(8):
        y = (_dot_split(c_hi, c_lo, *_split_bf16(br_ref[0, j]))
             + _dot_split(s_hi, s_lo, *_split_bf16(bi_ref[0, j])))
        o_ref[0, :, j, :] = (y * scale).astype(o_ref.dtype)


def _fft_twiddles(l):
    l1 = FFT_L1
    l2 = l // l1
    idx = (jnp.arange(l1, dtype=jnp.int32)[:, None] * jnp.arange(l2, dtype=jnp.int32)[None, :]) % l
    ang = idx.astype(F32) * (2.0 * np.pi / l)
    tc = jnp.broadcast_to(jnp.cos(ang)[:, :, None], (l1, l2, BR))
    ts = jnp.broadcast_to(jnp.sin(ang)[:, :, None], (l1, l2, BR))
    return tc, ts


def _fourier_mix(pc, tc, ts):
    b, l, _ = pc.shape
    l1 = FFT_L1
    l2 = l // l1

    def split_parts(n):
        parts = []
        for m in _dft_cos_sin(n):
            hi = jnp.asarray(m).astype(BF16)
            parts += [hi, (jnp.asarray(m) - hi.astype(F32)).astype(BF16)]
        return jnp.stack(parts)

    def strip(width):
        return pl.BlockSpec((1, l1, 8, width), lambda bi, i: (bi, 0, i, 0))

    tspec = pl.BlockSpec((l1, 8, BR), lambda bi, i: (0, i, 0))
    mspec = pl.BlockSpec((2, l1, l1), lambda bi, i: (0, 0, 0))
    br, bi_ = pl.pallas_call(
        _fft_major_kernel, grid=(b, l2 // 8),
        in_specs=[strip(2 * BR), mspec, tspec, tspec], out_specs=[strip(BR), strip(BR)],
        out_shape=[jax.ShapeDtypeStruct((b, l1, l2, BR), F32)] * 2,
        compiler_params=_cparams("parallel", "parallel"), name="fft_major",
    )(pc.reshape(b, l1, l2, 2 * BR), np.stack(_dft_cos_sin(l1)), tc, ts)

    bspec = pl.BlockSpec((1, 8, l2, BR), lambda bi, i: (bi, i, 0, 0))
    m2spec = pl.BlockSpec((4, l2, l2), lambda bi, i: (0, 0, 0))
    scale = 1.0 / float(np.sqrt(l * FN_GROUP_DIM))
    y = pl.pallas_call(
        functools.partial(_fft_minor_kernel, scale=scale), grid=(b, l1 // 8),
        in_specs=[bspec, bspec, m2spec],
        out_specs=pl.BlockSpec((1, l2, 8, BR), lambda bi, i: (bi, 0, i, 0)),
        out_shape=jax.ShapeDtypeStruct((b, l2, l1, BR), F32),
        compiler_params=_cparams("parallel", "parallel"), name="fft_minor",
    )(br, bi_, split_parts(l2))
    return y.reshape(b, l, BR)


def _hgrn_level_masks():
    t = np.arange(HG_CHUNK)
    out = []
    half = 8
    while 2 * half < HG_CHUNK:
        same = (t[:, None] // (2 * half)) == (t[None, :] // (2 * half))
        out.append(np.tile(same, (1, N_HEADS)).astype(np.float32))
        half *= 2
    return np.stack(out)


def _hgrn_kernel(qd_ref, fl_ref, vd_ref, lb_ref, mask_ref, bd_ref, o_ref, st_ref, *, first_layer, chunks_per_tile,
                 rev):
    c = HG_CHUNK

    @pl.when(pl.program_id(1) == 0)
    def _():
        st_ref[...] = jnp.zeros_like(st_ref)

    bd = bd_ref[...]
    bd16 = bd.astype(BF16)
    lb = lb_ref[...]
    row = lax.broadcasted_iota(jnp.int32, (c, 1), 0)
    sub = row % 8
    nt = (((1,), (1,)), ((), ()))

    def roll8(x, j):
        return pltpu.roll(x.reshape(c // 8, 8, BR), j % 8, 1).reshape(c, BR)

    def earlier(x, j):
        return roll8(x, -j if rev else j)

    def later(x, j):
        return roll8(x, j if rev else -j)

    def has_earlier(j):
        return (sub + j <= 7) if rev else (sub >= j)

    def has_later(j):
        return (sub >= j) if rev else (sub + j <= 7)

    for ci in (reversed(range(chunks_per_tile)) if rev else range(chunks_per_tile)):
        sl = slice(ci * c, (ci + 1) * c)
        qd = qd_ref[0, sl, :]
        fl = fl_ref[0, sl, :]
        v = vd_ref[0, sl, :]
        q = qd * _sigmoid(qd)
        if first_layer:
            f = _sigmoid(fl)
            k = _sigmoid(-fl)
        else:
            f = lb + (1.0 - lb) * _sigmoid(fl)
            k = (1.0 - lb) * _sigmoid(-fl)

        o = jnp.dot((q * k).astype(BF16), bd16, preferred_element_type=F32) * v
        g = f
        for d in range(1, 8):
            if d > 1:
                g = g * earlier(f, d - 1)
            p = jnp.where(has_earlier(d), q * g * earlier(k, d), 0.0)
            o = o + jnp.dot(p.astype(BF16), bd16, preferred_element_type=F32) * earlier(v, d)

        qf = f
        for j in (1, 2, 4):
            qf = qf * jnp.where(has_earlier(j), earlier(qf, j), 1.0)
        kf = jnp.where(has_later(1), later(f, 1), 1.0)
        for j in (1, 2, 4):
            kf = kf * jnp.where(has_later(j), later(kf, j), 1.0)

        scores = None
        half = 8
        level = 0
        while half < c:
            late = ((row // half) % 2) == (0 if rev else 1)
            qb = jnp.where(late, q * qf, 0.0).astype(BF16)
            kb = jnp.where(late, 0.0, k * kf).astype(BF16)
            x = lax.dot_general(qb, _head_stack(kb), nt, preferred_element_type=F32)
            if 2 * half < c:
                x = x * mask_ref[level]
            scores = x if scores is None else scores + x
            blocks = c // (2 * half)
            qf3 = qf.reshape(blocks, 2 * half, BR)
            end_early = half if rev else half - 1
            end_late = 0 if rev else 2 * half - 1
            tot_early = jnp.broadcast_to(qf3[:, end_early:end_early + 1, :], qf3.shape).reshape(c, BR)
            tot_late = jnp.broadcast_to(qf3[:, end_late:end_late + 1, :], qf3.shape).reshape(c, BR)
            qf = qf * jnp.where(late, tot_early, 1.0)
            kf = kf * jnp.where(late, 1.0, tot_late)
            half *= 2
            level += 1

        o = o + jnp.dot(scores.astype(BF16), _head_stack(v.astype(BF16)), preferred_element_type=F32)

        st = st_ref[...]
        o = o + lax.dot_general((q * qf).astype(BF16), st.astype(BF16), nt, preferred_element_type=F32)
        o_ref[0, sl, :] = o
        upd = jnp.dot(v.T.astype(BF16), (k * kf).astype(BF16), preferred_element_type=F32)
        last = 0 if rev else c - 1
        st_ref[...] = st * qf[last:last + 1, :] + upd * bd


def _hgrn_scan(qd, fl, vd, cols, lb, first_layer, rev, chunks_per_tile=4):
    b, l, _ = qd.shape
    tile = HG_CHUNK * chunks_per_tile
    n_tiles = l // tile
    masks = _hgrn_level_masks()
    bd = _same_head_matrix()

    def tile_index(i):
        return n_tiles - 1 - i if rev else i

    def seq(col):
        return pl.BlockSpec((1, tile, BR), lambda bi, i: (bi, tile_index(i), col))

    return pl.pallas_call(
        functools.partial(_hgrn_kernel, first_layer=first_layer, chunks_per_tile=chunks_per_tile, rev=rev),
        grid=(b, n_tiles),
        in_specs=[seq(cols[0]), seq(cols[1]), seq(cols[2]),
                  pl.BlockSpec((1, BR), lambda bi, i: (0, 0)),
                  pl.BlockSpec(masks.shape, lambda bi, i: (0, 0, 0)),
                  pl.BlockSpec((BR, BR), lambda bi, i: (0, 0))],
        out_specs=pl.BlockSpec((1, tile, BR), lambda bi, i: (bi, tile_index(i), 0)),
        out_shape=jax.ShapeDtypeStruct((b, l, BR), F32),
        scratch_shapes=[pltpu.VMEM((BR, BR), F32)],
        compiler_params=_cparams("parallel", "arbitrary"),
        name="hgrn_bwd" if rev else "hgrn_fwd",
    )(qd, fl, vd, lb.reshape(1, BR), masks, bd)


def _hgrn2_bidirectional(pd, lb_f, lb_b, first_layer):
    o_f = _hgrn_scan(pd, pd, pd, (0, 1, 3), lb_f, first_layer, rev=False)
    o_b = _hgrn_scan(pd, pd, pd, (0, 2, 3), lb_b, first_layer, rev=True)
    return o_f, o_b


def _route(x, w_hi, w_lo, bias, tri, run):
    x_hi, x_lo = _split_bf16(x)
    logits = _dot_split(x_hi, x_lo, w_hi, w_lo) + bias
    lane = lax.broadcasted_iota(jnp.int32, logits.shape, 1)
    big = jnp.int32(LANES)
    is_group = (lane >= N_EXPERTS) & (lane < N_EXPERTS + N_GROUPS)
    lg = jnp.where(is_group, logits, -jnp.inf)
    mg = jnp.max(lg, axis=-1, keepdims=True)
    g_sel = jnp.min(jnp.where(lg == mg, lane, big), axis=-1, keepdims=True) - N_EXPERTS
    p_g = 1.0 / jnp.sum(jnp.exp(lg - mg), axis=-1, keepdims=True)
    in_group = (lane < N_EXPERTS) & ((lane // EXPERTS_PER_GROUP) == g_sel)
    le = jnp.where(in_group, logits, -jnp.inf)
    v1 = jnp.max(le, axis=-1, keepdims=True)
    i1 = jnp.min(jnp.where(le == v1, lane, big), axis=-1, keepdims=True)
    le2 = jnp.where(lane == i1, -jnp.inf, le)
    v2 = jnp.max(le2, axis=-1, keepdims=True)
    i2 = jnp.min(jnp.where(le2 == v2, lane, big), axis=-1, keepdims=True)
    e2 = jnp.exp(v2 - v1)
    w1 = p_g / (1.0 + e2)
    w2 = p_g * e2 / (1.0 + e2)
    hot1 = lane == i1
    hot2 = lane == i2
    hot = (hot1 | hot2).astype(BF16)
    before = jnp.dot(tri, hot, preferred_element_type=F32) + run
    rank1 = jnp.sum(jnp.where(hot1, before, 0.0), axis=-1, keepdims=True).astype(jnp.int32)
    rank2 = jnp.sum(jnp.where(hot2, before, 0.0), axis=-1, keepdims=True).astype(jnp.int32)
    run = run + jnp.sum(hot.astype(F32), axis=0, keepdims=True)
    e = jnp.where(lane == 0, i1, jnp.where(lane == 1, i2, jnp.where(lane == 2, rank1,
                                                                     jnp.where(lane == 3, rank2, 0))))
    p = jnp.where(lane == 0, w1, jnp.where(lane == 1, w2, 0.0))
    return e, p, run


def _merge_kernel(x_ref, ya_ref, yb_ref, yc_ref, of_ref, ob_ref, gd_ref, ng_ref, bd_ref, wg_ref, bg_ref, wbr_ref,
                  wo_ref, g_ref, b_ref, rhi_ref, rlo_ref, rb_ref, tri_ref, o_ref, o16_ref, e_ref, p_ref, cnt_ref,
                  run_ref):
    @pl.when(pl.program_id(0) == 0)
    def _():
        run_ref[...] = jnp.zeros_like(run_ref)

    od = of_ref[...] + ob_ref[...]
    ms = jnp.dot(od * od, bd_ref[...], precision=HIGHEST, preferred_element_type=F32) * (1.0 / HEAD_DIM)
    gd = gd_ref[...]
    y_d = od * lax.rsqrt(ms + RMS_EPS) * ng_ref[...] * (gd * _sigmoid(gd))

    x = x_ref[...]
    xb = x.astype(BF16)
    merged = None
    for i, y in enumerate((ya_ref[...], yb_ref[...], yc_ref[...], y_d)):
        gate = _sigmoid(jnp.dot(xb, wg_ref[i], preferred_element_type=F32) + bg_ref[i])
        term = gate * jnp.dot(y.astype(BF16), wbr_ref[i], preferred_element_type=F32)
        merged = term if merged is None else merged + term
    mix = jnp.dot(merged.astype(BF16), wo_ref[...], preferred_element_type=F32)
    out = _ln_rows(DEEPNORM_ALPHA * x + mix, g_ref[...], b_ref[...])
    o_ref[...] = out
    o16_ref[...] = _pack_bf16_pairs(out)
    e, p, run = _route(out, rhi_ref[...], rlo_ref[...], rb_ref[...], tri_ref[...], run_ref[...])
    e_ref[...] = e.T[:e_ref.shape[0], :]
    p_ref[...] = p
    run_ref[...] = run
    cnt_ref[...] = run.astype(jnp.int32)


def _merge(x, y_a, y_b, y_c, o_f, o_b, pd, norm_g, w_gate, b_gate, w_br, w_out, g, b, wg, bg, we, be, tm=512):
    t, d = x.shape
    row = pl.BlockSpec((tm, d), lambda i: (i, 0))
    yrow = pl.BlockSpec((tm, BR), lambda i: (i, 0))
    vec = pl.BlockSpec((1, d), lambda i: (0, 0))
    lanes = pl.BlockSpec((tm, LANES), lambda i: (i, 0))
    lvec = pl.BlockSpec((1, LANES), lambda i: (0, 0))

    def whole(a):
        return pl.BlockSpec(a.shape, lambda i: (0,) * a.ndim)

    bgate = b_gate.reshape(b_gate.shape[0], 1, d)
    rw = jnp.zeros((d, LANES), F32).at[:, :N_EXPERTS].set(we).at[:, N_EXPERTS:N_EXPERTS + N_GROUPS].set(wg)
    rb = jnp.zeros((1, LANES), F32).at[0, :N_EXPERTS].set(be).at[0, N_EXPERTS:N_EXPERTS + N_GROUPS].set(bg)
    r_hi, r_lo = _split_bf16(rw)
    tri = np.tril(np.ones((tm, tm), np.float32), -1).astype(BF16)
    ng = jnp.tile(norm_g, N_HEADS).reshape(1, BR)
    bd = _same_head_matrix()
    return pl.pallas_call(
        _merge_kernel, grid=(t // tm,),
        in_specs=[row, yrow, yrow, yrow, yrow, yrow, pl.BlockSpec((tm, BR), lambda i: (i, 4)), whole(ng), whole(bd),
                  whole(w_gate), whole(bgate), whole(w_br), whole(w_out), vec, vec,
                  whole(r_hi), whole(r_lo), lvec, whole(tri)],
        out_specs=[row, pl.BlockSpec((tm, d // 2), lambda i: (i, 0)), pl.BlockSpec((8, tm), lambda i: (0, i)),
                   lanes, lvec],
        out_shape=[jax.ShapeDtypeStruct((t, d), F32), jax.ShapeDtypeStruct((t, d // 2), jnp.uint32),
                   jax.ShapeDtypeStruct((8, t), jnp.int32), jax.ShapeDtypeStruct((t, LANES), F32),
                   jax.ShapeDtypeStruct((1, LANES), jnp.int32)],
        scratch_shapes=[pltpu.VMEM((1, LANES), F32)],
        compiler_params=_cparams("arbitrary"), name="merge_ln1_route",
    )(x, y_a, y_b, y_c, o_f, o_b, pd, ng, bd, w_gate, bgate, w_br, w_out, g.reshape(1, d), b.reshape(1, d),
      r_hi, r_lo, rb, tri)


def _expert_work_items(counts, n_blocks):
    end = jnp.cumsum(counts)
    start = end - counts
    first_blk = start // MOE_ROWS
    n_e = jnp.where(counts > 0, (end - 1) // MOE_ROWS - first_blk + 1, 0)
    item_end = jnp.cumsum(n_e)
    item_start = item_end - n_e
    w = jnp.arange(n_blocks + N_EXPERTS, dtype=jnp.int32)
    live = w < item_end[-1]
    wc = jnp.minimum(w, item_end[-1] - 1)
    e_w = jnp.sum((item_end[None, :] <= wc[:, None]).astype(jnp.int32), axis=1)
    blk = first_blk[e_w] + wc - item_start[e_w]
    lo = jnp.where(live, jnp.maximum(start[e_w], blk * MOE_ROWS) - blk * MOE_ROWS, 0)
    hi = jnp.where(live, jnp.minimum(end[e_w], (blk + 1) * MOE_ROWS) - blk * MOE_ROWS, 0)
    first = jnp.concatenate([jnp.ones((1,), jnp.int32), (blk[1:] != blk[:-1]).astype(jnp.int32)])
    return jnp.stack([e_w, blk, lo, hi, first]).astype(jnp.int32)


def _expert_kernel(items_ref, x_ref, w13_ref, w2_ref, o_ref):
    w = pl.program_id(0)
    lo = items_ref[2, w]
    hi = items_ref[3, w]

    @pl.when(hi > lo)
    def _():
        x = _unpack_bf16_pairs(x_ref[...]).astype(BF16)
        h = jnp.dot(x, w13_ref[0], preferred_element_type=F32)
        hg = h[:, :D_FF_E]
        act = hg * _sigmoid(hg) * h[:, D_FF_E:]
        y = _pack_bf16_pairs(jnp.dot(act.astype(BF16), w2_ref[0], preferred_element_type=F32))
        row = lax.broadcasted_iota(jnp.int32, (MOE_ROWS, 1), 0)
        mine = (row >= lo) & (row < hi)

        @pl.when(items_ref[4, w] == 1)
        def _():
            o_ref[...] = jnp.where(mine, y, jnp.zeros_like(y))

        @pl.when(items_ref[4, w] == 0)
        def _():
            o_ref[...] = jnp.where(mine, y, o_ref[...])


def _expert_blocks(xs, items, w13, w2):
    n_rows, half = xs.shape
    d = 2 * half
    grid_spec = pltpu.PrefetchScalarGridSpec(
        num_scalar_prefetch=1, grid=(items.shape[1],),
        in_specs=[pl.BlockSpec((MOE_ROWS, half), lambda w, it: (it[1, w], 0)),
                  pl.BlockSpec((1, d, 2 * D_FF_E), lambda w, it: (it[0, w], 0, 0)),
                  pl.BlockSpec((1, D_FF_E, d), lambda w, it: (it[0, w], 0, 0))],
        out_specs=pl.BlockSpec((MOE_ROWS, half), lambda w, it: (it[1, w], 0)))
    return pl.pallas_call(
        _expert_kernel, grid_spec=grid_spec, out_shape=jax.ShapeDtypeStruct((n_rows, half), jnp.uint32),
        compiler_params=_cparams("arbitrary"), name="moe_experts",
    )(items, xs, w13, w2)


def _combine_kernel(x_ref, y0_ref, y1_ref, p_ref, g_ref, b_ref, o_ref):
    o_ref[...] = _ln_rows(_moe_residual(x_ref, y0_ref, y1_ref, p_ref), g_ref[...], b_ref[...])


def _combine(x, y01, p, g, b, tm=1024):
    t, d = x.shape
    row = pl.BlockSpec((tm, d), lambda i: (i, 0))
    vec = pl.BlockSpec((1, d), lambda i: (0, 0))
    return pl.pallas_call(
        _combine_kernel, grid=(t // tm,),
        in_specs=[row] + _pair_specs(t, d, tm) + [pl.BlockSpec((tm, LANES), lambda i: (i, 0)), vec, vec],
        out_specs=row, out_shape=jax.ShapeDtypeStruct((t, d), F32),
        compiler_params=_cparams("parallel"), name="moe_combine_ln2",
    )(x, y01, y01, p, g.reshape(1, d), b.reshape(1, d))


def _sc_gather_rows(src, idx):
    m = idx.shape[0]
    width = src.shape[1]
    info = plsc.get_sparse_core_info()
    n_workers = info.num_cores * info.num_subcores
    steps = m // (n_workers * SC_WINDOW)
    assert steps * n_workers * SC_WINDOW == m and steps % 2 == 0, (m, n_workers)
    mesh = plsc.VectorSubcoreMesh(core_axis_name="core", subcore_axis_name="subcore")

    @functools.partial(
        pl.kernel, mesh=mesh, out_type=jax.ShapeDtypeStruct((m, width), src.dtype),
        scratch_types=[pltpu.VMEM((steps, SC_WINDOW), jnp.int32), pltpu.VMEM((2, SC_WINDOW, width), src.dtype),
                       pltpu.SemaphoreType.DMA, pltpu.SemaphoreType.DMA])
    def gather(src_hbm, idx_hbm, out_hbm, idx_v, rows_v, sem0, sem1):
        wid = lax.axis_index("subcore") * info.num_cores + lax.axis_index("core")
        pltpu.sync_copy(idx_hbm.at[wid], idx_v)
        sems = (sem0, sem1)

        def fetch(j, slot):
            return pltpu.make_async_copy(src_hbm.at[idx_v.at[j]], rows_v.at[slot], sems[slot])

        def flush(j, slot):
            pltpu.sync_copy(rows_v.at[slot], out_hbm.at[pl.ds((wid * steps + j) * SC_WINDOW, SC_WINDOW)])

        fetch(0, 0).start()

        @pl.loop(0, steps, step=2)
        def _(j):
            fetch(j + 1, 1).start()
            fetch(j, 0).wait()
            flush(j, 0)

            @pl.when(j + 2 < steps)
            def _():
                fetch(j + 2, 0).start()

            fetch(j + 1, 1).wait()
            flush(j + 1, 1)

    return gather(src, idx.reshape(n_workers, steps, SC_WINDOW))


def _expert_outputs(x16, routing, cnt, w13, w2):
    t = x16.shape[0]
    tk = t * TOP_K
    counts = cnt[0, :N_EXPERTS]
    start = jnp.cumsum(counts) - counts
    dest0 = jnp.take(start, routing[0], mode='clip') + routing[2]
    dest1 = jnp.take(start, routing[1], mode='clip') + routing[3]
    order = jnp.argsort(jnp.stack([routing[0], routing[1]], axis=1).reshape(tk)).astype(jnp.int32)
    xs = _sc_gather_rows(x16, order // TOP_K)
    ys = _expert_blocks(xs, _expert_work_items(counts, tk // MOE_ROWS), w13, w2)
    return _sc_gather_rows(ys, jnp.concatenate([dest0, dest1]))


def _trunk(x, prm):
    b, l, d = x.shape
    t = b * l
    x = x.reshape(t, d)
    moe, g, bias = None, prm['ln_in_g'], prm['ln_in_b']
    tc, ts = _fft_twiddles(l)
    for i in range(DEPTH):
        x, pa, pb, pc, pd = _input_projection(x, moe, g, bias, prm['w_in'][i])
        y_a = _neighbourhood_attention(pa.reshape(b, l, -1), prm['attn_bias'][i])
        y_b = _short_conv(pb.reshape(b, l, -1), prm['conv_w'][i])
        y_c = _fourier_mix(pc.reshape(b, l, -1), tc, ts)
        o_f, o_b = _hgrn2_bidirectional(pd.reshape(b, l, -1), prm['lb'][0, i], prm['lb'][1, i], i == 0)
        x, x16, routing, p_lanes, cnt = _merge(
            x, y_a.reshape(t, BR), y_b.reshape(t, BR), y_c.reshape(t, BR), o_f.reshape(t, BR), o_b.reshape(t, BR),
            pd, prm['hgrn_norm_g'][i], prm['w_gate'][i], prm['b_gate'][i], prm['w_br'][i], prm['w_out'][i],
            prm['ln1_g'][i], prm['ln1_b'][i], prm['router_g_w'][i], prm['router_g_b'][i], prm['router_e_w'][i],
            prm['router_e_b'][i])
        y01 = _expert_outputs(x16, routing, cnt, prm['w13'][i], prm['w2'][i])
        moe, g, bias = (y01, p_lanes), prm['ln2_g'][i], prm['ln2_b'][i]
    return _combine(x, *moe, g, bias).reshape(b, l, d)


def kernel(x_prompt, x_sample, ln_in_g, ln_in_b, w_in, na_rpb, conv_w, hgrn_lb, hgrn_norm_g, w_gate, b_gate, w_br,
           w_out, ln1_g, ln1_b, router_g_w, router_g_b, router_e_w, router_e_b, w13, w2, ln2_g, ln2_b):
    lb = jnp.cumsum(jax.nn.softmax(hgrn_lb.astype(F32), axis=1), axis=1)
    lb = lb - lb[:, :1]
    prm = dict(
        ln_in_g=ln_in_g, ln_in_b=ln_in_b,
        w_in=jnp.stack([_fold_channel_dft(w_in[i]) for i in range(DEPTH)]).astype(BF16),
        attn_bias=jnp.stack([_attn_bias_table(na_rpb[i]) for i in range(DEPTH)]),
        conv_w=conv_w, lb=lb, hgrn_norm_g=hgrn_norm_g,
        w_gate=w_gate.astype(BF16), b_gate=b_gate, w_br=w_br.astype(BF16), w_out=w_out.astype(BF16),
        ln1_g=ln1_g, ln1_b=ln1_b, router_g_w=router_g_w, router_g_b=router_g_b,
        router_e_w=router_e_w, router_e_b=router_e_b, w13=w13.astype(BF16), w2=w2.astype(BF16),
        ln2_g=ln2_g, ln2_b=ln2_b)
    return _trunk(x_prompt, prm), _trunk(x_sample, prm)
```

```python
import functools

import numpy as np
import jax
import jax.numpy as jnp
from jax import lax
from jax.experimental import pallas as pl
from jax.experimental.pallas import tpu as pltpu
from jax.experimental.pallas import tpu_sc as plsc

F32 = jnp.float32
BF16 = jnp.bfloat16
HIGHEST = lax.Precision.HIGHEST

D_MODEL = 1024
DEPTH = 4
GRID_W = 64
BR = 256
N_HEADS = 4
HEAD_DIM = 64
NA_ROWS = 8
NA_COLS = 16
FN_GROUP_DIM = 64
N_GROUPS = 4
EXPERTS_PER_GROUP = 8
N_EXPERTS = N_GROUPS * EXPERTS_PER_GROUP
TOP_K = 2
D_FF_E = 512
DEEPNORM_ALPHA = (2 * DEPTH) ** 0.25
LN_EPS = 1e-5
RMS_EPS = 1e-6
NEG_INF = -1e30

LANES = 128
VMEM_LIMIT_BYTES = 56 * 1024 * 1024
MOE_ROWS = 512
HG_CHUNK = 128
FFT_L1 = 64
SC_WINDOW = 64

COL_QA, COL_KA, COL_VA, COL_XB, COL_GB, COL_GC, COL_XF, COL_QD, COL_FF, COL_FB, COL_VD, COL_GD = range(12)
PROJ_GROUPS = ((0, 3), (3, 6), (6, 8), (8, 13))


def _cparams(*sem):
    return pltpu.CompilerParams(dimension_semantics=sem, vmem_limit_bytes=VMEM_LIMIT_BYTES)


def _sigmoid(x):
    return 1.0 / (1.0 + jnp.exp(-x))


def _split_bf16(x):
    hi = x.astype(BF16)
    return hi, (x - hi.astype(F32)).astype(BF16)


def _dot_split(a_hi, a_lo, b_hi, b_lo):
    dot = functools.partial(jnp.dot, preferred_element_type=F32)
    return dot(a_hi, b_hi) + (dot(a_lo, b_hi) + dot(a_hi, b_lo))


def _pack_bf16_pairs(x):
    m = x.shape[1] // 2
    hi = lax.bitcast_convert_type(x[:, :m].astype(BF16).astype(F32), jnp.uint32)
    lo = lax.bitcast_convert_type(x[:, m:].astype(BF16).astype(F32), jnp.uint32)
    return hi | (lo >> 16)


def _unpack_bf16_pairs(u):
    hi = lax.bitcast_convert_type(u & jnp.uint32(0xFFFF0000), F32)
    lo = lax.bitcast_convert_type(u << 16, F32)
    return jnp.concatenate([hi, lo], axis=1)


def _ln_rows(x, g, b):
    mu = jnp.mean(x, axis=-1, keepdims=True)
    xc = x - mu
    var = jnp.mean(xc * xc, axis=-1, keepdims=True)
    return xc * lax.rsqrt(var + LN_EPS) * g + b


def _head_of_lane(shape):
    return lax.broadcasted_iota(jnp.int32, shape, len(shape) - 1) // HEAD_DIM


def _head_stack(x):
    head = _head_of_lane(x.shape)
    return jnp.concatenate([jnp.where(head == h, x, jnp.zeros_like(x)) for h in range(N_HEADS)], axis=0)


def _same_head_matrix():
    h = np.arange(BR) // HEAD_DIM
    return (h[:, None] == h[None, :]).astype(np.float32)


def _moe_residual(x_ref, y0_ref, y1_ref, p_ref):
    p = p_ref[...]
    ffn = p[:, 0:1] * _unpack_bf16_pairs(y0_ref[...]) + p[:, 1:2] * _unpack_bf16_pairs(y1_ref[...])
    return DEEPNORM_ALPHA * x_ref[...] + ffn


def _proj_kernel(*refs, after_moe):
    if after_moe:
        x_ref, y0_ref, y1_ref, p_ref, g_ref, b_ref, w_ref, xo_ref, *out_refs = refs
        pre = _moe_residual(x_ref, y0_ref, y1_ref, p_ref)
    else:
        x_ref, g_ref, b_ref, w_ref, xo_ref, *out_refs = refs
        pre = x_ref[...]
    x = _ln_rows(pre, g_ref[...], b_ref[...])
    xo_ref[...] = x
    xb = x.astype(BF16)
    for ref, (lo, hi) in zip(out_refs, PROJ_GROUPS):
        ref[...] = jnp.dot(xb, w_ref[:, lo * BR:hi * BR], preferred_element_type=F32).astype(ref.dtype)


def _fold_channel_dft(w_in):
    cg, sg = _dft_cos_sin(FN_GROUP_DIM)
    eye = np.eye(BR // FN_GROUP_DIM, dtype=np.float32)
    w_xf = w_in[:, COL_XF * BR:(COL_XF + 1) * BR]
    w_re = jnp.dot(w_xf, np.kron(eye, cg), precision=HIGHEST)
    w_im = -jnp.dot(w_xf, np.kron(eye, sg), precision=HIGHEST)
    return jnp.concatenate([w_in[:, :COL_XF * BR], w_re, w_im, w_in[:, (COL_XF + 1) * BR:]], axis=1)


def _pair_specs(t, d, tm):
    return [pl.BlockSpec((tm, d // 2), lambda i: (i, 0)), pl.BlockSpec((tm, d // 2), lambda i: (i + t // tm, 0))]


def _input_projection(x, moe, g, b, w_in, tm=512):
    t, d = x.shape
    n = w_in.shape[1]
    widths = (d,) + tuple((hi - lo) * BR for lo, hi in PROJ_GROUPS)
    dtypes = (F32, BF16, F32, F32, F32)
    row = pl.BlockSpec((tm, d), lambda i: (i, 0))
    vec = pl.BlockSpec((1, d), lambda i: (0, 0))
    acts, act_specs = [x], [row]
    if moe is not None:
        y01, p = moe
        acts += [y01, y01, p]
        act_specs += _pair_specs(t, d, tm) + [pl.BlockSpec((tm, LANES), lambda i: (i, 0))]
    return pl.pallas_call(
        functools.partial(_proj_kernel, after_moe=moe is not None), grid=(t // tm,),
        in_specs=act_specs + [vec, vec, pl.BlockSpec((d, n), lambda i: (0, 0))],
        out_specs=[pl.BlockSpec((tm, w), lambda i: (i, 0)) for w in widths],
        out_shape=[jax.ShapeDtypeStruct((t, w), dt) for w, dt in zip(widths, dtypes)],
        compiler_params=_cparams("parallel"), name="ln_in_proj",
    )(*acts, g.reshape(1, d), b.reshape(1, d), w_in)


def _attn_bias_table(rpb):
    c = np.arange(GRID_W)
    win_c0 = np.clip(c - NA_COLS // 2, 0, GRID_W - NA_COLS)
    valid = (c[None, :] >= win_c0[:, None]) & (c[None, :] < win_c0[:, None] + NA_COLS)
    dc = np.clip(c[None, :] - c[:, None], -(NA_COLS - 1), NA_COLS - 1) + NA_COLS - 1
    dr = np.arange(NA_ROWS)[None, :] - np.arange(NA_ROWS)[:, None] + NA_ROWS - 1
    row_hot = (dr[:, :, None] == np.arange(2 * NA_ROWS - 1)).astype(np.float32)
    col_hot = (dc[:, :, None] == np.arange(2 * NA_COLS - 1)).astype(np.float32)
    t = jnp.einsum('oia,hab,qkb->ohqik', row_hot, rpb.astype(F32), col_hot, precision=HIGHEST)
    t = jnp.where(valid[None, None, :, None, :], t, NEG_INF)
    return t.reshape(NA_ROWS, N_HEADS * GRID_W, NA_ROWS * GRID_W)


def _attn_kernel(q_ref, k_ref, v_ref, bias_ref, o_ref, *, n_rows, rows_per_tile):
    i = pl.program_id(1)
    head = _head_of_lane((GRID_W, BR))
    scale = HEAD_DIM ** -0.5
    span = NA_ROWS * GRID_W
    for j in range(rows_per_tile):
        r = i * rows_per_tile + j
        r0 = jnp.clip(r - NA_ROWS // 2, 0, n_rows - NA_ROWS)
        q = q_ref[0, j * GRID_W:(j + 1) * GRID_W, :]
        qs = _head_stack(q)
        start = pl.multiple_of(r0 * GRID_W, GRID_W)
        ks = k_ref[0, pl.ds(start, span), :]
        vs = v_ref[0, pl.ds(start, span), :]
        s = lax.dot_general(qs, ks, (((1,), (1,)), ((), ())), preferred_element_type=F32)
        s = s * scale + bias_ref[r - r0]
        m = jnp.max(s, axis=-1, keepdims=True)
        e = jnp.exp(s - m)
        p = e / jnp.sum(e, axis=-1, keepdims=True)
        o4 = jnp.dot(p.astype(BF16), vs, preferred_element_type=F32)
        o = jnp.zeros((GRID_W, BR), F32)
        for h in range(N_HEADS):
            o = o + jnp.where(head == h, o4[h * GRID_W:(h + 1) * GRID_W, :], 0.0)
        o_ref[0, j * GRID_W:(j + 1) * GRID_W, :] = o.astype(o_ref.dtype)


def _neighbourhood_attention(pa, bias, rows_per_tile=8):
    b, l, _ = pa.shape
    n_rows = l // GRID_W
    tq = rows_per_tile * GRID_W
    kern = functools.partial(_attn_kernel, n_rows=n_rows, rows_per_tile=rows_per_tile)
    return pl.pallas_call(
        kern, grid=(b, l // tq),
        in_specs=[pl.BlockSpec((1, tq, BR), lambda bi, i: (bi, i, 0)),
                  pl.BlockSpec((1, l, BR), lambda bi, i: (bi, 0, 1)),
                  pl.BlockSpec((1, l, BR), lambda bi, i: (bi, 0, 2)),
                  pl.BlockSpec(bias.shape, lambda bi, i: (0, 0, 0))],
        out_specs=pl.BlockSpec((1, tq, BR), lambda bi, i: (bi, i, 0)),
        out_shape=jax.ShapeDtypeStruct((b, l, BR), BF16),
        compiler_params=_cparams("parallel", "arbitrary"), name="nbr_attn",
    )(pa, pa, pa, bias)


def _conv_kernel(xb_ref, gb_ref, gc_ref, xbp_ref, gcp_ref, xbn_ref, gcn_ref, w_ref, o_ref, *, n_tiles):
    i = pl.program_id(1)
    tl = xb_ref.shape[1]
    u = gc_ref[0] * xb_ref[0]
    u_before = jnp.where(i > 0, gcp_ref[0, 7:8, :] * xbp_ref[0, 7:8, :], 0.0)
    u_after = jnp.where(i < n_tiles - 1, gcn_ref[0, 0:1, :] * xbn_ref[0, 0:1, :], 0.0)
    row = lax.broadcasted_iota(jnp.int32, (tl, 1), 0)
    u_prev = jnp.where(row == 0, u_before, pltpu.roll(u, 1, 0))
    u_next = jnp.where(row == tl - 1, u_after, pltpu.roll(u, tl - 1, 0))
    conv = w_ref[0:1, :] * u_prev + w_ref[1:2, :] * u + w_ref[2:3, :] * u_next
    o_ref[0] = (gb_ref[0] * conv).astype(o_ref.dtype)


def _short_conv(pb, conv_w, tl=1024):
    b, l, _ = pb.shape
    n_tiles = l // tl
    per8 = tl // 8
    last8 = l // 8 - 1

    def main(col):
        return pl.BlockSpec((1, tl, BR), lambda bi, i: (bi, i, col))

    def before(col):
        return pl.BlockSpec((1, 8, BR), lambda bi, i: (bi, jnp.maximum(i * per8 - 1, 0), col))

    def after(col):
        return pl.BlockSpec((1, 8, BR), lambda bi, i: (bi, jnp.minimum((i + 1) * per8, last8), col))

    return pl.pallas_call(
        functools.partial(_conv_kernel, n_tiles=n_tiles), grid=(b, n_tiles),
        in_specs=[main(0), main(1), main(2), before(0), before(2), after(0), after(2),
                  pl.BlockSpec(conv_w.shape, lambda bi, i: (0, 0))],
        out_specs=pl.BlockSpec((1, tl, BR), lambda bi, i: (bi, i, 0)),
        out_shape=jax.ShapeDtypeStruct((b, l, BR), BF16),
        compiler_params=_cparams("parallel", "parallel"), name="short_conv",
    )(pb, pb, pb, pb, pb, pb, pb, conv_w)


def _dft_cos_sin(n):
    ang = 2.0 * np.pi * ((np.arange(n)[:, None] * np.arange(n)[None, :]) % n) / n
    return np.cos(ang).astype(np.float32), np.sin(ang).astype(np.float32)


def _fft_major_kernel(g_ref, m_ref, tc_ref, ts_ref, br_ref, bi_ref):
    dot = functools.partial(jnp.dot, precision=HIGHEST, preferred_element_type=F32)
    for s in range(8):
        g = g_ref[0, :, s, :]
        swapped = jnp.concatenate([g[:, BR:], -g[:, :BR]], axis=1)
        a = dot(m_ref[0], g) + dot(m_ref[1], swapped)
        ar, ai = a[:, :BR], a[:, BR:]
        tc = tc_ref[:, s, :]
        ts = ts_ref[:, s, :]
        br_ref[0, :, s, :] = ar * tc + ai * ts
        bi_ref[0, :, s, :] = ai * tc - ar * ts


def _fft_minor_kernel(br_ref, bi_ref, m_ref, o_ref, *, scale):
    c_hi, c_lo, s_hi, s_lo = m_ref[0], m_ref[1], m_ref[2], m_ref[3]
    for j in range(8):
        y = (_dot_split(c_hi, c_lo, *_split_bf16(br_ref[0, j]))
             + _dot_split(s_hi, s_lo, *_split_bf16(bi_ref[0, j])))
        o_ref[0, :, j, :] = (y * scale).astype(o_ref.dtype)


def _fft_twiddles(l):
    l1 = FFT_L1
    l2 = l // l1
    idx = (jnp.arange(l1, dtype=jnp.int32)[:, None] * jnp.arange(l2, dtype=jnp.int32)[None, :]) % l
    ang = idx.astype(F32) * (2.0 * np.pi / l)
    tc = jnp.broadcast_to(jnp.cos(ang)[:, :, None], (l1, l2, BR))
    ts = jnp.broadcast_to(jnp.sin(ang)[:, :, None], (l1, l2, BR))
    return tc, ts


def _fourier_mix(pc, tc, ts):
    b, l, _ = pc.shape
    l1 = FFT_L1
    l2 = l // l1

    def split_parts(n):
        parts = []
        for m in _dft_cos_sin(n):
            hi = jnp.asarray(m).astype(BF16)
            parts += [hi, (jnp.asarray(m) - hi.astype(F32)).astype(BF16)]
        return jnp.stack(parts)

    def strip(width):
        return pl.BlockSpec((1, l1, 8, width), lambda bi, i: (bi, 0, i, 0))

    tspec = pl.BlockSpec((l1, 8, BR), lambda bi, i: (0, i, 0))
    mspec = pl.BlockSpec((2, l1, l1), lambda bi, i: (0, 0, 0))
    br, bi_ = pl.pallas_call(
        _fft_major_kernel, grid=(b, l2 // 8),
        in_specs=[strip(2 * BR), mspec, tspec, tspec], out_specs=[strip(BR), strip(BR)],
        out_shape=[jax.ShapeDtypeStruct((b, l1, l2, BR), F32)] * 2,
        compiler_params=_cparams("parallel", "parallel"), name="fft_major",
    )(pc.reshape(b, l1, l2, 2 * BR), np.stack(_dft_cos_sin(l1)), tc, ts)

    bspec = pl.BlockSpec((1, 8, l2, BR), lambda bi, i: (bi, i, 0, 0))
    m2spec = pl.BlockSpec((4, l2, l2), lambda bi, i: (0, 0, 0))
    scale = 1.0 / float(np.sqrt(l * FN_GROUP_DIM))
    y = pl.pallas_call(
        functools.partial(_fft_minor_kernel, scale=scale), grid=(b, l1 // 8),
        in_specs=[bspec, bspec, m2spec],
        out_specs=pl.BlockSpec((1, l2, 8, BR), lambda bi, i: (bi, 0, i, 0)),
        out_shape=jax.ShapeDtypeStruct((b, l2, l1, BR), F32),
        compiler_params=_cparams("parallel", "parallel"), name="fft_minor",
    )(br, bi_, split_parts(l2))
    return y.reshape(b, l, BR)


def _hgrn_level_masks():
    t = np.arange(HG_CHUNK)
    out = []
    half = 8
    while 2 * half < HG_CHUNK:
        same = (t[:, None] // (2 * half)) == (t[None, :] // (2 * half))
        out.append(np.tile(same, (1, N_HEADS)).astype(np.float32))
        half *= 2
    return np.stack(out)


def _hgrn_kernel(qd_ref, fl_ref, vd_ref, lb_ref, mask_ref, bd_ref, o_ref, st_ref, *, first_layer, chunks_per_tile,
                 rev):
    c = HG_CHUNK

    @pl.when(pl.program_id(1) == 0)
    def _():
        st_ref[...] = jnp.zeros_like(st_ref)

    bd = bd_ref[...]
    bd16 = bd.astype(BF16)
    lb = lb_ref[...]
    row = lax.broadcasted_iota(jnp.int32, (c, 1), 0)
    sub = row % 8
    nt = (((1,), (1,)), ((), ()))

    def roll8(x, j):
        return pltpu.roll(x.reshape(c // 8, 8, BR), j % 8, 1).reshape(c, BR)

    def earlier(x, j):
        return roll8(x, -j if rev else j)

    def later(x, j):
        return roll8(x, j if rev else -j)

    def has_earlier(j):
        return (sub + j <= 7) if rev else (sub >= j)

    def has_later(j):
        return (sub >= j) if rev else (sub + j <= 7)

    for ci in (reversed(range(chunks_per_tile)) if rev else range(chunks_per_tile)):
        sl = slice(ci * c, (ci + 1) * c)
        qd = qd_ref[0, sl, :]
        fl = fl_ref[0, sl, :]
        v = vd_ref[0, sl, :]
        q = qd * _sigmoid(qd)
        if first_layer:
            f = _sigmoid(fl)
            k = _sigmoid(-fl)
        else:
            f = lb + (1.0 - lb) * _sigmoid(fl)
            k = (1.0 - lb) * _sigmoid(-fl)

        o = jnp.dot((q * k).astype(BF16), bd16, preferred_element_type=F32) * v
        g = f
        for d in range(1, 8):
            if d > 1:
                g = g * earlier(f, d - 1)
            p = jnp.where(has_earlier(d), q * g * earlier(k, d), 0.0)
            o = o + jnp.dot(p.astype(BF16), bd16, preferred_element_type=F32) * earlier(v, d)

        qf = f
        for j in (1, 2, 4):
            qf = qf * jnp.where(has_earlier(j), earlier(qf, j), 1.0)
        kf = jnp.where(has_later(1), later(f, 1), 1.0)
        for j in (1, 2, 4):
            kf = kf * jnp.where(has_later(j), later(kf, j), 1.0)

        scores = None
        half = 8
        level = 0
        while half < c:
            late = ((row // half) % 2) == (0 if rev else 1)
            qb = jnp.where(late, q * qf, 0.0).astype(BF16)
            kb = jnp.where(late, 0.0, k * kf).astype(BF16)
            x = lax.dot_general(qb, _head_stack(kb), nt, preferred_element_type=F32)
            if 2 * half < c:
                x = x * mask_ref[level]
            scores = x if scores is None else scores + x
            blocks = c // (2 * half)
            qf3 = qf.reshape(blocks, 2 * half, BR)
            end_early = half if rev else half - 1
            end_late = 0 if rev else 2 * half - 1
            tot_early = jnp.broadcast_to(qf3[:, end_early:end_early + 1, :], qf3.shape).reshape(c, BR)
            tot_late = jnp.broadcast_to(qf3[:, end_late:end_late + 1, :], qf3.shape).reshape(c, BR)
            qf = qf * jnp.where(late, tot_early, 1.0)
            kf = kf * jnp.where(late, 1.0, tot_late)
            half *= 2
            level += 1

        o = o + jnp.dot(scores.astype(BF16), _head_stack(v.astype(BF16)), preferred_element_type=F32)

        st = st_ref[...]
        o = o + lax.dot_general((q * qf).astype(BF16), st.astype(BF16), nt, preferred_element_type=F32)
        o_ref[0, sl, :] = o
        upd = jnp.dot(v.T.astype(BF16), (k * kf).astype(BF16), preferred_element_type=F32)
        last = 0 if rev else c - 1
        st_ref[...] = st * qf[last:last + 1, :] + upd * bd


def _hgrn_scan(qd, fl, vd, cols, lb, first_layer, rev, chunks_per_tile=4):
    b, l, _ = qd.shape
    tile = HG_CHUNK * chunks_per_tile
    n_tiles = l // tile
    masks = _hgrn_level_masks()
    bd = _same_head_matrix()

    def tile_index(i):
        return n_tiles - 1 - i if rev else i

    def seq(col):
        return pl.BlockSpec((1, tile, BR), lambda bi, i: (bi, tile_index(i), col))

    return pl.pallas_call(
        functools.partial(_hgrn_kernel, first_layer=first_layer, chunks_per_tile=chunks_per_tile, rev=rev),
        grid=(b, n_tiles),
        in_specs=[seq(cols[0]), seq(cols[1]), seq(cols[2]),
                  pl.BlockSpec((1, BR), lambda bi, i: (0, 0)),
                  pl.BlockSpec(masks.shape, lambda bi, i: (0, 0, 0)),
                  pl.BlockSpec((BR, BR), lambda bi, i: (0, 0))],
        out_specs=pl.BlockSpec((1, tile, BR), lambda bi, i: (bi, tile_index(i), 0)),
        out_shape=jax.ShapeDtypeStruct((b, l, BR), F32),
        scratch_shapes=[pltpu.VMEM((BR, BR), F32)],
        compiler_params=_cparams("parallel", "arbitrary"),
        name="hgrn_bwd" if rev else "hgrn_fwd",
    )(qd, fl, vd, lb.reshape(1, BR), masks, bd)


def _hgrn2_bidirectional(pd, lb_f, lb_b, first_layer):
    o_f = _hgrn_scan(pd, pd, pd, (0, 1, 3), lb_f, first_layer, rev=False)
    o_b = _hgrn_scan(pd, pd, pd, (0, 2, 3), lb_b, first_layer, rev=True)
    return o_f, o_b


def _route(x, w_hi, w_lo, bias, tri, run):
    x_hi, x_lo = _split_bf16(x)
    logits = _dot_split(x_hi, x_lo, w_hi, w_lo) + bias
    lane = lax.broadcasted_iota(jnp.int32, logits.shape, 1)
    big = jnp.int32(LANES)
    is_group = (lane >= N_EXPERTS) & (lane < N_EXPERTS + N_GROUPS)
    lg = jnp.where(is_group, logits, -jnp.inf)
    mg = jnp.max(lg, axis=-1, keepdims=True)
    g_sel = jnp.min(jnp.where(lg == mg, lane, big), axis=-1, keepdims=True) - N_EXPERTS
    p_g = 1.0 / jnp.sum(jnp.exp(lg - mg), axis=-1, keepdims=True)
    in_group = (lane < N_EXPERTS) & ((lane // EXPERTS_PER_GROUP) == g_sel)
    le = jnp.where(in_group, logits, -jnp.inf)
    v1 = jnp.max(le, axis=-1, keepdims=True)
    i1 = jnp.min(jnp.where(le == v1, lane, big), axis=-1, keepdims=True)
    le2 = jnp.where(lane == i1, -jnp.inf, le)
    v2 = jnp.max(le2, axis=-1, keepdims=True)
    i2 = jnp.min(jnp.where(le2 == v2, lane, big), axis=-1, keepdims=True)
    e2 = jnp.exp(v2 - v1)
    w1 = p_g / (1.0 + e2)
    w2 = p_g * e2 / (1.0 + e2)
    hot1 = lane == i1
    hot2 = lane == i2
    hot = (hot1 | hot2).astype(BF16)
    before = jnp.dot(tri, hot, preferred_element_type=F32) + run
    rank1 = jnp.sum(jnp.where(hot1, before, 0.0), axis=-1, keepdims=True).astype(jnp.int32)
    rank2 = jnp.sum(jnp.where(hot2, before, 0.0), axis=-1, keepdims=True).astype(jnp.int32)
    run = run + jnp.sum(hot.astype(F32), axis=0, keepdims=True)
    e = jnp.where(lane == 0, i1, jnp.where(lane == 1, i2, jnp.where(lane == 2, rank1,
                                                                     jnp.where(lane == 3, rank2, 0))))
    p = jnp.where(lane == 0, w1, jnp.where(lane == 1, w2, 0.0))
    return e, p, run


def _merge_kernel(x_ref, ya_ref, yb_ref, yc_ref, of_ref, ob_ref, gd_ref, ng_ref, bd_ref, wg_ref, bg_ref, wbr_ref,
                  wo_ref, g_ref, b_ref, rhi_ref, rlo_ref, rb_ref, tri_ref, o_ref, o16_ref, e_ref, p_ref, cnt_ref,
                  run_ref):
    @pl.when(pl.program_id(0) == 0)
    def _():
        run_ref[...] = jnp.zeros_like(run_ref)

    od = of_ref[...] + ob_ref[...]
    ms = jnp.dot(od * od, bd_ref[...], precision=HIGHEST, preferred_element_type=F32) * (1.0 / HEAD_DIM)
    gd = gd_ref[...]
    y_d = od * lax.rsqrt(ms + RMS_EPS) * ng_ref[...] * (gd * _sigmoid(gd))

    x = x_ref[...]
    xb = x.astype(BF16)
    merged = None
    for i, y in enumerate((ya_ref[...], yb_ref[...], yc_ref[...], y_d)):
        gate = _sigmoid(jnp.dot(xb, wg_ref[i], preferred_element_type=F32) + bg_ref[i])
        term = gate * jnp.dot(y.astype(BF16), wbr_ref[i], preferred_element_type=F32)
        merged = term if merged is None else merged + term
    mix = jnp.dot(merged.astype(BF16), wo_ref[...], preferred_element_type=F32)
    out = _ln_rows(DEEPNORM_ALPHA * x + mix, g_ref[...], b_ref[...])
    o_ref[...] = out
    o16_ref[...] = _pack_bf16_pairs(out)
    e, p, run = _route(out, rhi_ref[...], rlo_ref[...], rb_ref[...], tri_ref[...], run_ref[...])
    e_ref[...] = e.T[:e_ref.shape[0], :]
    p_ref[...] = p
    run_ref[...] = run
    cnt_ref[...] = run.astype(jnp.int32)


def _merge(x, y_a, y_b, y_c, o_f, o_b, pd, norm_g, w_gate, b_gate, w_br, w_out, g, b, wg, bg, we, be, tm=512):
    t, d = x.shape
    row = pl.BlockSpec((tm, d), lambda i: (i, 0))
    yrow = pl.BlockSpec((tm, BR), lambda i: (i, 0))
    vec = pl.BlockSpec((1, d), lambda i: (0, 0))
    lanes = pl.BlockSpec((tm, LANES), lambda i: (i, 0))
    lvec = pl.BlockSpec((1, LANES), lambda i: (0, 0))

    def whole(a):
        return pl.BlockSpec(a.shape, lambda i: (0,) * a.ndim)

    bgate = b_gate.reshape(b_gate.shape[0], 1, d)
    rw = jnp.zeros((d, LANES), F32).at[:, :N_EXPERTS].set(we).at[:, N_EXPERTS:N_EXPERTS + N_GROUPS].set(wg)
    rb = jnp.zeros((1, LANES), F32).at[0, :N_EXPERTS].set(be).at[0, N_EXPERTS:N_EXPERTS + N_GROUPS].set(bg)
    r_hi, r_lo = _split_bf16(rw)
    tri = np.tril(np.ones((tm, tm), np.float32), -1).astype(BF16)
    ng = jnp.tile(norm_g, N_HEADS).reshape(1, BR)
    bd = _same_head_matrix()
    return pl.pallas_call(
        _merge_kernel, grid=(t // tm,),
        in_specs=[row, yrow, yrow, yrow, yrow, yrow, pl.BlockSpec((tm, BR), lambda i: (i, 4)), whole(ng), whole(bd),
                  whole(w_gate), whole(bgate), whole(w_br), whole(w_out), vec, vec,
                  whole(r_hi), whole(r_lo), lvec, whole(tri)],
        out_specs=[row, pl.BlockSpec((tm, d // 2), lambda i: (i, 0)), pl.BlockSpec((8, tm), lambda i: (0, i)),
                   lanes, lvec],
        out_shape=[jax.ShapeDtypeStruct((t, d), F32), jax.ShapeDtypeStruct((t, d // 2), jnp.uint32),
                   jax.ShapeDtypeStruct((8, t), jnp.int32), jax.ShapeDtypeStruct((t, LANES), F32),
                   jax.ShapeDtypeStruct((1, LANES), jnp.int32)],
        scratch_shapes=[pltpu.VMEM((1, LANES), F32)],
        compiler_params=_cparams("arbitrary"), name="merge_ln1_route",
    )(x, y_a, y_b, y_c, o_f, o_b, pd, ng, bd, w_gate, bgate, w_br, w_out, g.reshape(1, d), b.reshape(1, d),
      r_hi, r_lo, rb, tri)


def _expert_work_items(counts, n_blocks):
    end = jnp.cumsum(counts)
    start = end - counts
    first_blk = start // MOE_ROWS
    n_e = jnp.where(counts > 0, (end - 1) // MOE_ROWS - first_blk + 1, 0)
    item_end = jnp.cumsum(n_e)
    item_start = item_end - n_e
    w = jnp.arange(n_blocks + N_EXPERTS, dtype=jnp.int32)
    live = w < item_end[-1]
    wc = jnp.minimum(w, item_end[-1] - 1)
    e_w = jnp.sum((item_end[None, :] <= wc[:, None]).astype(jnp.int32), axis=1)
    blk = first_blk[e_w] + wc - item_start[e_w]
    lo = jnp.where(live, jnp.maximum(start[e_w], blk * MOE_ROWS) - blk * MOE_ROWS, 0)
    hi = jnp.where(live, jnp.minimum(end[e_w], (blk + 1) * MOE_ROWS) - blk * MOE_ROWS, 0)
    first = jnp.concatenate([jnp.ones((1,), jnp.int32), (blk[1:] != blk[:-1]).astype(jnp.int32)])
    return jnp.stack([e_w, blk, lo, hi, first]).astype(jnp.int32)


def _expert_kernel(items_ref, x_ref, w13_ref, w2_ref, o_ref):
    w = pl.program_id(0)
    lo = items_ref[2, w]
    hi = items_ref[3, w]

    @pl.when(hi > lo)
    def _():
        x = _unpack_bf16_pairs(x_ref[...]).astype(BF16)
        h = jnp.dot(x, w13_ref[0], preferred_element_type=F32)
        hg = h[:, :D_FF_E]
        act = hg * _sigmoid(hg) * h[:, D_FF_E:]
        y = _pack_bf16_pairs(jnp.dot(act.astype(BF16), w2_ref[0], preferred_element_type=F32))
        row = lax.broadcasted_iota(jnp.int32, (MOE_ROWS, 1), 0)
        mine = (row >= lo) & (row < hi)

        @pl.when(items_ref[4, w] == 1)
        def _():
            o_ref[...] = jnp.where(mine, y, jnp.zeros_like(y))

        @pl.when(items_ref[4, w] == 0)
        def _():
            o_ref[...] = jnp.where(mine, y, o_ref[...])


def _expert_blocks(xs, items, w13, w2):
    n_rows, half = xs.shape
    d = 2 * half
    grid_spec = pltpu.PrefetchScalarGridSpec(
        num_scalar_prefetch=1, grid=(items.shape[1],),
        in_specs=[pl.BlockSpec((MOE_ROWS, half), lambda w, it: (it[1, w], 0)),
                  pl.BlockSpec((1, d, 2 * D_FF_E), lambda w, it: (it[0, w], 0, 0)),
                  pl.BlockSpec((1, D_FF_E, d), lambda w, it: (it[0, w], 0, 0))],
        out_specs=pl.BlockSpec((MOE_ROWS, half), lambda w, it: (it[1, w], 0)))
    return pl.pallas_call(
        _expert_kernel, grid_spec=grid_spec, out_shape=jax.ShapeDtypeStruct((n_rows, half), jnp.uint32),
        compiler_params=_cparams("arbitrary"), name="moe_experts",
    )(items, xs, w13, w2)


def _combine_kernel(x_ref, y0_ref, y1_ref, p_ref, g_ref, b_ref, o_ref):
    o_ref[...] = _ln_rows(_moe_residual(x_ref, y0_ref, y1_ref, p_ref), g_ref[...], b_ref[...])


def _combine(x, y01, p, g, b, tm=1024):
    t, d = x.shape
    row = pl.BlockSpec((tm, d), lambda i: (i, 0))
    vec = pl.BlockSpec((1, d), lambda i: (0, 0))
    return pl.pallas_call(
        _combine_kernel, grid=(t // tm,),
        in_specs=[row] + _pair_specs(t, d, tm) + [pl.BlockSpec((tm, LANES), lambda i: (i, 0)), vec, vec],
        out_specs=row, out_shape=jax.ShapeDtypeStruct((t, d), F32),
        compiler_params=_cparams("parallel"), name="moe_combine_ln2",
    )(x, y01, y01, p, g.reshape(1, d), b.reshape(1, d))


def _dest_kernel(start_ref, r_ref, o_ref):
    r = r_ref[...]
    e = r[0:TOP_K, :]
    base = jnp.zeros_like(e)
    for x in range(N_EXPERTS):
        base = jnp.where(e == x, start_ref[x], base)
    pad = jnp.zeros((r.shape[0] - TOP_K, r.shape[1]), jnp.int32)
    o_ref[...] = jnp.concatenate([base + r[TOP_K:2 * TOP_K, :], pad], axis=0)


def _slot_rows(routing, start, tl=8192):
    rows, t = routing.shape
    tl = min(tl, t)
    grid_spec = pltpu.PrefetchScalarGridSpec(
        num_scalar_prefetch=1, grid=(t // tl,),
        in_specs=[pl.BlockSpec((rows, tl), lambda i, st: (0, i))],
        out_specs=pl.BlockSpec((rows, tl), lambda i, st: (0, i)))
    return pl.pallas_call(
        _dest_kernel, grid_spec=grid_spec, out_shape=jax.ShapeDtypeStruct((rows, t), jnp.int32),
        compiler_params=_cparams("parallel"), name="moe_slot_rows",
    )(start, routing)


def _sc_workers():
    info = plsc.get_sparse_core_info()
    return info.num_cores, info.num_cores * info.num_subcores


def _sc_move_rows(src, dest, scatter):
    t = dest.shape[1]
    width = src.shape[1]
    n_cores, n_workers = _sc_workers()
    steps = t // (n_workers * SC_WINDOW)
    assert steps * n_workers * SC_WINDOW == t and steps % 2 == 0, (t, n_workers)
    mesh = plsc.VectorSubcoreMesh(core_axis_name="core", subcore_axis_name="subcore")

    @functools.partial(
        pl.kernel, mesh=mesh, out_type=jax.ShapeDtypeStruct((TOP_K * t, width), src.dtype),
        scratch_types=[pltpu.VMEM((TOP_K, steps, SC_WINDOW), jnp.int32),
                       pltpu.VMEM((2, SC_WINDOW, width), src.dtype),
                       pltpu.SemaphoreType.DMA, pltpu.SemaphoreType.DMA])
    def move(src_hbm, idx_hbm, out_hbm, idx_v, rows_v, sem0, sem1):
        wid = lax.axis_index("subcore") * n_cores + lax.axis_index("core")
        for k in range(TOP_K):
            pltpu.sync_copy(idx_hbm.at[k, wid], idx_v.at[k])
        sems = (sem0, sem1)

        def window(j):
            return pl.ds((wid * steps + j) * SC_WINDOW, SC_WINDOW)

        def run(fetch, flush):
            fetch(0, 0).start()

            @pl.loop(0, steps, step=2)
            def _(j):
                fetch(j + 1, 1).start()
                fetch(j, 0).wait()
                flush(j, 0)

                @pl.when(j + 2 < steps)
                def _():
                    fetch(j + 2, 0).start()

                fetch(j + 1, 1).wait()
                flush(j + 1, 1)

        if scatter:
            def fetch(j, slot):
                return pltpu.make_async_copy(src_hbm.at[window(j)], rows_v.at[slot], sems[slot])

            def flush(j, slot):
                for k in range(TOP_K):
                    pltpu.sync_copy(rows_v.at[slot], out_hbm.at[idx_v.at[k, j]])

            run(fetch, flush)
        else:
            for k in range(TOP_K):
                def fetch(j, slot, k=k):
                    return pltpu.make_async_copy(src_hbm.at[idx_v.at[k, j]], rows_v.at[slot], sems[slot])

                def flush(j, slot, k=k):
                    pltpu.sync_copy(rows_v.at[slot], out_hbm.at[pl.ds(k * t + (wid * steps + j) * SC_WINDOW,
                                                                      SC_WINDOW)])

                run(fetch, flush)

    return move(src, dest.reshape(dest.shape[0], n_workers, steps, SC_WINDOW))


def _expert_outputs(x16, routing, cnt, w13, w2):
    t = x16.shape[0]
    counts = cnt[0, :N_EXPERTS]
    start = jnp.cumsum(counts) - counts
    dest = _slot_rows(routing, start)
    xs = _sc_move_rows(x16, dest, scatter=True)
    ys = _expert_blocks(xs, _expert_work_items(counts, t * TOP_K // MOE_ROWS), w13, w2)
    return _sc_move_rows(ys, dest, scatter=False)


def _trunk(x, prm):
    b, l, d = x.shape
    t = b * l
    x = x.reshape(t, d)
    moe, g, bias = None, prm['ln_in_g'], prm['ln_in_b']
    tc, ts = _fft_twiddles(l)
    for i in range(DEPTH):
        x, pa, pb, pc, pd = _input_projection(x, moe, g, bias, prm['w_in'][i])
        y_a = _neighbourhood_attention(pa.reshape(b, l, -1), prm['attn_bias'][i])
        y_b = _short_conv(pb.reshape(b, l, -1), prm['conv_w'][i])
        y_c = _fourier_mix(pc.reshape(b, l, -1), tc, ts)
        o_f, o_b = _hgrn2_bidirectional(pd.reshape(b, l, -1), prm['lb'][0, i], prm['lb'][1, i], i == 0)
        x, x16, routing, p_lanes, cnt = _merge(
            x, y_a.reshape(t, BR), y_b.reshape(t, BR), y_c.reshape(t, BR), o_f.reshape(t, BR), o_b.reshape(t, BR),
            pd, prm['hgrn_norm_g'][i], prm['w_gate'][i], prm['b_gate'][i], prm['w_br'][i], prm['w_out'][i],
            prm['ln1_g'][i], prm['ln1_b'][i], prm['router_g_w'][i], prm['router_g_b'][i], prm['router_e_w'][i],
            prm['router_e_b'][i])
        y01 = _expert_outputs(x16, routing, cnt, prm['w13'][i], prm['w2'][i])
        moe, g, bias = (y01, p_lanes), prm['ln2_g'][i], prm['ln2_b'][i]
    return _combine(x, *moe, g, bias).reshape(b, l, d)


def kernel(x_prompt, x_sample, ln_in_g, ln_in_b, w_in, na_rpb, conv_w, hgrn_lb, hgrn_norm_g, w_gate, b_gate, w_br,
           w_out, ln1_g, ln1_b, router_g_w, router_g_b, router_e_w, router_e_b, w13, w2, ln2_g, ln2_b):
    lb = jnp.cumsum(jax.nn.softmax(hgrn_lb.astype(F32), axis=1), axis=1)
    lb = lb - lb[:, :1]
    prm = dict(
        ln_in_g=ln_in_g, ln_in_b=ln_in_b,
        w_in=jnp.stack([_fold_channel_dft(w_in[i]) for i in range(DEPTH)]).astype(BF16),
        attn_bias=jnp.stack([_attn_bias_table(na_rpb[i]) for i in range(DEPTH)]),
        conv_w=conv_w, lb=lb, hgrn_norm_g=hgrn_norm_g,
        w_gate=w_gate.astype(BF16), b_gate=b_gate, w_br=w_br.astype(BF16), w_out=w_out.astype(BF16),
        ln1_g=ln1_g, ln1_b=ln1_b, router_g_w=router_g_w, router_g_b=router_g_b,
        router_e_w=router_e_w, router_e_b=router_e_b, w13=w13.astype(BF16), w2=w2.astype(BF16),
        ln2_g=ln2_g, ln2_b=ln2_b)
    return _trunk(x_prompt, prm), _trunk(x_sample, prm)
```

```python
import functools

import numpy as np
import jax
import jax.numpy as jnp
from jax import lax
from jax.experimental import pallas as pl
from jax.experimental.pallas import tpu as pltpu
from jax.experimental.pallas import tpu_sc as plsc

F32 = jnp.float32
BF16 = jnp.bfloat16
HIGHEST = lax.Precision.HIGHEST

D_MODEL = 1024
DEPTH = 4
GRID_W = 64
BR = 256
N_HEADS = 4
HEAD_DIM = 64
NA_ROWS = 8
NA_COLS = 16
FN_GROUP_DIM = 64
N_GROUPS = 4
EXPERTS_PER_GROUP = 8
N_EXPERTS = N_GROUPS * EXPERTS_PER_GROUP
TOP_K = 2
D_FF_E = 512
DEEPNORM_ALPHA = (2 * DEPTH) ** 0.25
LN_EPS = 1e-5
RMS_EPS = 1e-6
NEG_INF = -1e30

LANES = 128
VMEM_LIMIT_BYTES = 56 * 1024 * 1024
MOE_ROWS = 512
HG_CHUNK = 128
FFT_L1 = 64
SC_WINDOW = 64

COL_QA, COL_KA, COL_VA, COL_XB, COL_GB, COL_GC, COL_XF, COL_QD, COL_FF, COL_FB, COL_VD, COL_GD = range(12)
PROJ_GROUPS = ((0, 3), (3, 6), (6, 8), (8, 13))


def _cparams(*sem):
    return pltpu.CompilerParams(dimension_semantics=sem, vmem_limit_bytes=VMEM_LIMIT_BYTES)


def _sigmoid(x):
    return 1.0 / (1.0 + jnp.exp(-x))


def _split_bf16(x):
    hi = x.astype(BF16)
    return hi, (x - hi.astype(F32)).astype(BF16)


def _dot_split(a_hi, a_lo, b_hi, b_lo):
    dot = functools.partial(jnp.dot, preferred_element_type=F32)
    return dot(a_hi, b_hi) + (dot(a_lo, b_hi) + dot(a_hi, b_lo))


def _pack_bf16_pairs(x):
    m = x.shape[1] // 2
    hi = lax.bitcast_convert_type(x[:, :m].astype(BF16).astype(F32), jnp.uint32)
    lo = lax.bitcast_convert_type(x[:, m:].astype(BF16).astype(F32), jnp.uint32)
    return hi | (lo >> 16)


def _unpack_bf16_pairs(u):
    hi = lax.bitcast_convert_type(u & jnp.uint32(0xFFFF0000), F32)
    lo = lax.bitcast_convert_type(u << 16, F32)
    return jnp.concatenate([hi, lo], axis=1)


def _ln_rows(x, g, b):
    mu = jnp.mean(x, axis=-1, keepdims=True)
    xc = x - mu
    var = jnp.mean(xc * xc, axis=-1, keepdims=True)
    return xc * lax.rsqrt(var + LN_EPS) * g + b


def _head_of_lane(shape):
    return lax.broadcasted_iota(jnp.int32, shape, len(shape) - 1) // HEAD_DIM


def _head_stack(x):
    head = _head_of_lane(x.shape)
    return jnp.concatenate([jnp.where(head == h, x, jnp.zeros_like(x)) for h in range(N_HEADS)], axis=0)


def _same_head_matrix():
    h = np.arange(BR) // HEAD_DIM
    return (h[:, None] == h[None, :]).astype(np.float32)


def _moe_residual(x_ref, y0_ref, y1_ref, p_ref):
    p = p_ref[...]
    ffn = p[:, 0:1] * _unpack_bf16_pairs(y0_ref[...]) + p[:, 1:2] * _unpack_bf16_pairs(y1_ref[...])
    return DEEPNORM_ALPHA * x_ref[...] + ffn


def _proj_kernel(*refs, after_moe):
    if after_moe:
        x_ref, y0_ref, y1_ref, p_ref, g_ref, b_ref, w_ref, xo_ref, *out_refs = refs
        pre = _moe_residual(x_ref, y0_ref, y1_ref, p_ref)
    else:
        x_ref, g_ref, b_ref, w_ref, xo_ref, *out_refs = refs
        pre = x_ref[...]
    x = _ln_rows(pre, g_ref[...], b_ref[...])
    xo_ref[...] = x
    xb = x.astype(BF16)
    for ref, (lo, hi) in zip(out_refs, PROJ_GROUPS):
        ref[...] = jnp.dot(xb, w_ref[:, lo * BR:hi * BR], preferred_element_type=F32).astype(ref.dtype)


def _fold_channel_dft(w_in):
    cg, sg = _dft_cos_sin(FN_GROUP_DIM)
    eye = np.eye(BR // FN_GROUP_DIM, dtype=np.float32)
    w_xf = w_in[:, COL_XF * BR:(COL_XF + 1) * BR]
    w_re = jnp.dot(w_xf, np.kron(eye, cg), precision=HIGHEST)
    w_im = -jnp.dot(w_xf, np.kron(eye, sg), precision=HIGHEST)
    return jnp.concatenate([w_in[:, :COL_XF * BR], w_re, w_im, w_in[:, (COL_XF + 1) * BR:]], axis=1)


def _pair_specs(t, d, tm):
    return [pl.BlockSpec((tm, d // 2), lambda i: (i, 0)), pl.BlockSpec((tm, d // 2), lambda i: (i + t // tm, 0))]


def _input_projection(x, moe, g, b, w_in, tm=512):
    t, d = x.shape
    n = w_in.shape[1]
    widths = (d,) + tuple((hi - lo) * BR for lo, hi in PROJ_GROUPS)
    dtypes = (F32, BF16, F32, F32, F32)
    row = pl.BlockSpec((tm, d), lambda i: (i, 0))
    vec = pl.BlockSpec((1, d), lambda i: (0, 0))
    acts, act_specs = [x], [row]
    if moe is not None:
        y01, p = moe
        acts += [y01, y01, p]
        act_specs += _pair_specs(t, d, tm) + [pl.BlockSpec((tm, LANES), lambda i: (i, 0))]
    return pl.pallas_call(
        functools.partial(_proj_kernel, after_moe=moe is not None), grid=(t // tm,),
        in_specs=act_specs + [vec, vec, pl.BlockSpec((d, n), lambda i: (0, 0))],
        out_specs=[pl.BlockSpec((tm, w), lambda i: (i, 0)) for w in widths],
        out_shape=[jax.ShapeDtypeStruct((t, w), dt) for w, dt in zip(widths, dtypes)],
        compiler_params=_cparams("parallel"), name="ln_in_proj",
    )(*acts, g.reshape(1, d), b.reshape(1, d), w_in)


def _attn_bias_table(rpb):
    c = np.arange(GRID_W)
    win_c0 = np.clip(c - NA_COLS // 2, 0, GRID_W - NA_COLS)
    valid = (c[None, :] >= win_c0[:, None]) & (c[None, :] < win_c0[:, None] + NA_COLS)
    dc = np.clip(c[None, :] - c[:, None], -(NA_COLS - 1), NA_COLS - 1) + NA_COLS - 1
    dr = np.arange(NA_ROWS)[None, :] - np.arange(NA_ROWS)[:, None] + NA_ROWS - 1
    row_hot = (dr[:, :, None] == np.arange(2 * NA_ROWS - 1)).astype(np.float32)
    col_hot = (dc[:, :, None] == np.arange(2 * NA_COLS - 1)).astype(np.float32)
    t = jnp.einsum('oia,hab,qkb->ohqik', row_hot, rpb.astype(F32), col_hot, precision=HIGHEST)
    t = jnp.where(valid[None, None, :, None, :], t, NEG_INF)
    return t.reshape(NA_ROWS, N_HEADS * GRID_W, NA_ROWS * GRID_W)


def _attn_kernel(q_ref, k_ref, v_ref, bias_ref, o_ref, *, n_rows, rows_per_tile):
    i = pl.program_id(1)
    head = _head_of_lane((GRID_W, BR))
    scale = HEAD_DIM ** -0.5
    span = NA_ROWS * GRID_W
    for j in range(rows_per_tile):
        r = i * rows_per_tile + j
        r0 = jnp.clip(r - NA_ROWS // 2, 0, n_rows - NA_ROWS)
        q = q_ref[0, j * GRID_W:(j + 1) * GRID_W, :]
        qs = _head_stack(q)
        start = pl.multiple_of(r0 * GRID_W, GRID_W)
        ks = k_ref[0, pl.ds(start, span), :]
        vs = v_ref[0, pl.ds(start, span), :]
        s = lax.dot_general(qs, ks, (((1,), (1,)), ((), ())), preferred_element_type=F32)
        s = s * scale + bias_ref[r - r0]
        m = jnp.max(s, axis=-1, keepdims=True)
        e = jnp.exp(s - m)
        p = e / jnp.sum(e, axis=-1, keepdims=True)
        o4 = jnp.dot(p.astype(BF16), vs, preferred_element_type=F32)
        o = jnp.zeros((GRID_W, BR), F32)
        for h in range(N_HEADS):
            o = o + jnp.where(head == h, o4[h * GRID_W:(h + 1) * GRID_W, :], 0.0)
        o_ref[0, j * GRID_W:(j + 1) * GRID_W, :] = o.astype(o_ref.dtype)


def _neighbourhood_attention(pa, bias, rows_per_tile=8):
    b, l, _ = pa.shape
    n_rows = l // GRID_W
    tq = rows_per_tile * GRID_W
    kern = functools.partial(_attn_kernel, n_rows=n_rows, rows_per_tile=rows_per_tile)
    return pl.pallas_call(
        kern, grid=(b, l // tq),
        in_specs=[pl.BlockSpec((1, tq, BR), lambda bi, i: (bi, i, 0)),
                  pl.BlockSpec((1, l, BR), lambda bi, i: (bi, 0, 1)),
                  pl.BlockSpec((1, l, BR), lambda bi, i: (bi, 0, 2)),
                  pl.BlockSpec(bias.shape, lambda bi, i: (0, 0, 0))],
        out_specs=pl.BlockSpec((1, tq, BR), lambda bi, i: (bi, i, 0)),
        out_shape=jax.ShapeDtypeStruct((b, l, BR), BF16),
        compiler_params=_cparams("parallel", "arbitrary"), name="nbr_attn",
    )(pa, pa, pa, bias)


def _conv_kernel(xb_ref, gb_ref, gc_ref, xbp_ref, gcp_ref, xbn_ref, gcn_ref, w_ref, o_ref, *, n_tiles):
    i = pl.program_id(1)
    tl = xb_ref.shape[1]
    u = gc_ref[0] * xb_ref[0]
    u_before = jnp.where(i > 0, gcp_ref[0, 7:8, :] * xbp_ref[0, 7:8, :], 0.0)
    u_after = jnp.where(i < n_tiles - 1, gcn_ref[0, 0:1, :] * xbn_ref[0, 0:1, :], 0.0)
    row = lax.broadcasted_iota(jnp.int32, (tl, 1), 0)
    u_prev = jnp.where(row == 0, u_before, pltpu.roll(u, 1, 0))
    u_next = jnp.where(row == tl - 1, u_after, pltpu.roll(u, tl - 1, 0))
    conv = w_ref[0:1, :] * u_prev + w_ref[1:2, :] * u + w_ref[2:3, :] * u_next
    o_ref[0] = (gb_ref[0] * conv).astype(o_ref.dtype)


def _short_conv(pb, conv_w, tl=1024):
    b, l, _ = pb.shape
    n_tiles = l // tl
    per8 = tl // 8
    last8 = l // 8 - 1

    def main(col):
        return pl.BlockSpec((1, tl, BR), lambda bi, i: (bi, i, col))

    def before(col):
        return pl.BlockSpec((1, 8, BR), lambda bi, i: (bi, jnp.maximum(i * per8 - 1, 0), col))

    def after(col):
        return pl.BlockSpec((1, 8, BR), lambda bi, i: (bi, jnp.minimum((i + 1) * per8, last8), col))

    return pl.pallas_call(
        functools.partial(_conv_kernel, n_tiles=n_tiles), grid=(b, n_tiles),
        in_specs=[main(0), main(1), main(2), before(0), before(2), after(0), after(2),
                  pl.BlockSpec(conv_w.shape, lambda bi, i: (0, 0))],
        out_specs=pl.BlockSpec((1, tl, BR), lambda bi, i: (bi, i, 0)),
        out_shape=jax.ShapeDtypeStruct((b, l, BR), BF16),
        compiler_params=_cparams("parallel", "parallel"), name="short_conv",
    )(pb, pb, pb, pb, pb, pb, pb, conv_w)


def _dft_cos_sin(n):
    ang = 2.0 * np.pi * ((np.arange(n)[:, None] * np.arange(n)[None, :]) % n) / n
    return np.cos(ang).astype(np.float32), np.sin(ang).astype(np.float32)


def _fft_major_kernel(g_ref, m_ref, tc_ref, ts_ref, br_ref, bi_ref):
    for s in range(8):
        g = g_ref[0, :, s, :]
        swapped = jnp.concatenate([g[:, BR:], -g[:, :BR]], axis=1)
        a = jnp.dot(m_ref[...], jnp.concatenate([g, swapped], axis=0), precision=HIGHEST,
                    preferred_element_type=F32)
        ar, ai = a[:, :BR], a[:, BR:]
        tc = tc_ref[:, s, :]
        ts = ts_ref[:, s, :]
        br_ref[0, :, s, :] = ar * tc + ai * ts
        bi_ref[0, :, s, :] = ai * tc - ar * ts


def _fft_minor_kernel(br_ref, bi_ref, m_ref, o_ref, *, scale):
    l2 = br_ref.shape[2]
    group = m_ref.shape[1] // l2
    for j0 in range(0, 8, group):
        rhs = jnp.concatenate([br_ref[0, j0:j0 + group].reshape(group * l2, BR),
                               bi_ref[0, j0:j0 + group].reshape(group * l2, BR)], axis=0)
        y = _dot_split(m_ref[0], m_ref[1], *_split_bf16(rhs)) * scale
        for j in range(group):
            o_ref[0, :, j0 + j, :] = y[j * l2:(j + 1) * l2].astype(o_ref.dtype)


def _fft_twiddles(l):
    l1 = FFT_L1
    l2 = l // l1
    idx = (jnp.arange(l1, dtype=jnp.int32)[:, None] * jnp.arange(l2, dtype=jnp.int32)[None, :]) % l
    ang = idx.astype(F32) * (2.0 * np.pi / l)
    tc = jnp.broadcast_to(jnp.cos(ang)[:, :, None], (l1, l2, BR))
    ts = jnp.broadcast_to(jnp.sin(ang)[:, :, None], (l1, l2, BR))
    return tc, ts


def _fourier_mix(pc, tc, ts):
    b, l, _ = pc.shape
    l1 = FFT_L1
    l2 = l // l1

    c1, s1 = _dft_cos_sin(l1)
    c2, s2 = _dft_cos_sin(l2)
    group = min(max(1, 128 // l2), 8)
    eye = np.eye(group, dtype=np.float32)
    minor = jnp.asarray(np.concatenate([np.kron(eye, c2), np.kron(eye, s2)], axis=1))
    minor_hi = minor.astype(BF16)
    minor_parts = jnp.stack([minor_hi, (minor - minor_hi.astype(F32)).astype(BF16)])

    def strip(width):
        return pl.BlockSpec((1, l1, 8, width), lambda bi, i: (bi, 0, i, 0))

    tspec = pl.BlockSpec((l1, 8, BR), lambda bi, i: (0, i, 0))
    mspec = pl.BlockSpec((l1, 2 * l1), lambda bi, i: (0, 0))
    br, bi_ = pl.pallas_call(
        _fft_major_kernel, grid=(b, l2 // 8),
        in_specs=[strip(2 * BR), mspec, tspec, tspec], out_specs=[strip(BR), strip(BR)],
        out_shape=[jax.ShapeDtypeStruct((b, l1, l2, BR), F32)] * 2,
        compiler_params=_cparams("parallel", "parallel"), name="fft_major",
    )(pc.reshape(b, l1, l2, 2 * BR), np.concatenate([c1, s1], axis=1), tc, ts)

    bspec = pl.BlockSpec((1, 8, l2, BR), lambda bi, i: (bi, i, 0, 0))
    m2spec = pl.BlockSpec(minor_parts.shape, lambda bi, i: (0, 0, 0))
    scale = 1.0 / float(np.sqrt(l * FN_GROUP_DIM))
    y = pl.pallas_call(
        functools.partial(_fft_minor_kernel, scale=scale), grid=(b, l1 // 8),
        in_specs=[bspec, bspec, m2spec],
        out_specs=pl.BlockSpec((1, l2, 8, BR), lambda bi, i: (bi, 0, i, 0)),
        out_shape=jax.ShapeDtypeStruct((b, l2, l1, BR), F32),
        compiler_params=_cparams("parallel", "parallel"), name="fft_minor",
    )(br, bi_, minor_parts)
    return y.reshape(b, l, BR)


def _hgrn_level_masks():
    t = np.arange(HG_CHUNK)
    out = []
    half = 8
    while 2 * half < HG_CHUNK:
        same = (t[:, None] // (2 * half)) == (t[None, :] // (2 * half))
        out.append(np.tile(same, (1, N_HEADS)).astype(np.float32))
        half *= 2
    return np.stack(out)


def _hgrn_kernel(qd_ref, fl_ref, vd_ref, lb_ref, mask_ref, bd_ref, o_ref, st_ref, *, first_layer, chunks_per_tile,
                 rev):
    c = HG_CHUNK

    @pl.when(pl.program_id(1) == 0)
    def _():
        st_ref[...] = jnp.zeros_like(st_ref)

    bd = bd_ref[...]
    bd16 = bd.astype(BF16)
    lb = lb_ref[...]
    row = lax.broadcasted_iota(jnp.int32, (c, 1), 0)
    sub = row % 8
    nt = (((1,), (1,)), ((), ()))

    def roll8(x, j):
        return pltpu.roll(x.reshape(c // 8, 8, BR), j % 8, 1).reshape(c, BR)

    def earlier(x, j):
        return roll8(x, -j if rev else j)

    def later(x, j):
        return roll8(x, j if rev else -j)

    def has_earlier(j):
        return (sub + j <= 7) if rev else (sub >= j)

    def has_later(j):
        return (sub >= j) if rev else (sub + j <= 7)

    for ci in (reversed(range(chunks_per_tile)) if rev else range(chunks_per_tile)):
        sl = slice(ci * c, (ci + 1) * c)
        qd = qd_ref[0, sl, :]
        fl = fl_ref[0, sl, :]
        v = vd_ref[0, sl, :]
        q = qd * _sigmoid(qd)
        if first_layer:
            f = _sigmoid(fl)
            k = _sigmoid(-fl)
        else:
            f = lb + (1.0 - lb) * _sigmoid(fl)
            k = (1.0 - lb) * _sigmoid(-fl)

        o = jnp.dot((q * k).astype(BF16), bd16, preferred_element_type=F32) * v
        g = f
        for d in range(1, 8):
            if d > 1:
                g = g * earlier(f, d - 1)
            p = jnp.where(has_earlier(d), q * g * earlier(k, d), 0.0)
            o = o + jnp.dot(p.astype(BF16), bd16, preferred_element_type=F32) * earlier(v, d)

        qf = f
        for j in (1, 2, 4):
            qf = qf * jnp.where(has_earlier(j), earlier(qf, j), 1.0)
        kf = jnp.where(has_later(1), later(f, 1), 1.0)
        for j in (1, 2, 4):
            kf = kf * jnp.where(has_later(j), later(kf, j), 1.0)

        scores = None
        half = 8
        level = 0
        while half < c:
            late = ((row // half) % 2) == (0 if rev else 1)
            qb = jnp.where(late, q * qf, 0.0).astype(BF16)
            kb = jnp.where(late, 0.0, k * kf).astype(BF16)
            x = lax.dot_general(qb, _head_stack(kb), nt, preferred_element_type=F32)
            if 2 * half < c:
                x = x * mask_ref[level]
            scores = x if scores is None else scores + x
            blocks = c // (2 * half)
            qf3 = qf.reshape(blocks, 2 * half, BR)
            end_early = half if rev else half - 1
            end_late = 0 if rev else 2 * half - 1
            tot_early = jnp.broadcast_to(qf3[:, end_early:end_early + 1, :], qf3.shape).reshape(c, BR)
            tot_late = jnp.broadcast_to(qf3[:, end_late:end_late + 1, :], qf3.shape).reshape(c, BR)
            qf = qf * jnp.where(late, tot_early, 1.0)
            kf = kf * jnp.where(late, 1.0, tot_late)
            half *= 2
            level += 1

        o = o + jnp.dot(scores.astype(BF16), _head_stack(v.astype(BF16)), preferred_element_type=F32)

        st = st_ref[...]
        o = o + lax.dot_general((q * qf).astype(BF16), st.astype(BF16), nt, preferred_element_type=F32)
        o_ref[0, sl, :] = o
        upd = jnp.dot(v.T.astype(BF16), (k * kf).astype(BF16), preferred_element_type=F32)
        last = 0 if rev else c - 1
        st_ref[...] = st * qf[last:last + 1, :] + upd * bd


def _hgrn_scan(qd, fl, vd, cols, lb, first_layer, rev, chunks_per_tile=8):
    b, l, _ = qd.shape
    tile = HG_CHUNK * chunks_per_tile
    n_tiles = l // tile
    masks = _hgrn_level_masks()
    bd = _same_head_matrix()

    def tile_index(i):
        return n_tiles - 1 - i if rev else i

    def seq(col):
        return pl.BlockSpec((1, tile, BR), lambda bi, i: (bi, tile_index(i), col))

    return pl.pallas_call(
        functools.partial(_hgrn_kernel, first_layer=first_layer, chunks_per_tile=chunks_per_tile, rev=rev),
        grid=(b, n_tiles),
        in_specs=[seq(cols[0]), seq(cols[1]), seq(cols[2]),
                  pl.BlockSpec((1, BR), lambda bi, i: (0, 0)),
                  pl.BlockSpec(masks.shape, lambda bi, i: (0, 0, 0)),
                  pl.BlockSpec((BR, BR), lambda bi, i: (0, 0))],
        out_specs=pl.BlockSpec((1, tile, BR), lambda bi, i: (bi, tile_index(i), 0)),
        out_shape=jax.ShapeDtypeStruct((b, l, BR), F32),
        scratch_shapes=[pltpu.VMEM((BR, BR), F32)],
        compiler_params=_cparams("parallel", "arbitrary"),
        name="hgrn_bwd" if rev else "hgrn_fwd",
    )(qd, fl, vd, lb.reshape(1, BR), masks, bd)


def _hgrn2_bidirectional(pd, lb_f, lb_b, first_layer):
    o_f = _hgrn_scan(pd, pd, pd, (0, 1, 3), lb_f, first_layer, rev=False)
    o_b = _hgrn_scan(pd, pd, pd, (0, 2, 3), lb_b, first_layer, rev=True)
    return o_f, o_b


def _route(x, w_hi, w_lo, bias, tri, run):
    x_hi, x_lo = _split_bf16(x)
    logits = _dot_split(x_hi, x_lo, w_hi, w_lo) + bias
    lane = lax.broadcasted_iota(jnp.int32, logits.shape, 1)
    big = jnp.int32(LANES)
    is_group = (lane >= N_EXPERTS) & (lane < N_EXPERTS + N_GROUPS)
    lg = jnp.where(is_group, logits, -jnp.inf)
    mg = jnp.max(lg, axis=-1, keepdims=True)
    g_sel = jnp.min(jnp.where(lg == mg, lane, big), axis=-1, keepdims=True) - N_EXPERTS
    p_g = 1.0 / jnp.sum(jnp.exp(lg - mg), axis=-1, keepdims=True)
    in_group = (lane < N_EXPERTS) & ((lane // EXPERTS_PER_GROUP) == g_sel)
    le = jnp.where(in_group, logits, -jnp.inf)
    v1 = jnp.max(le, axis=-1, keepdims=True)
    i1 = jnp.min(jnp.where(le == v1, lane, big), axis=-1, keepdims=True)
    le2 = jnp.where(lane == i1, -jnp.inf, le)
    v2 = jnp.max(le2, axis=-1, keepdims=True)
    i2 = jnp.min(jnp.where(le2 == v2, lane, big), axis=-1, keepdims=True)
    e2 = jnp.exp(v2 - v1)
    w1 = p_g / (1.0 + e2)
    w2 = p_g * e2 / (1.0 + e2)
    hot1 = lane == i1
    hot2 = lane == i2
    hot = (hot1 | hot2).astype(BF16)
    before = jnp.dot(tri, hot, preferred_element_type=F32) + run
    rank1 = jnp.sum(jnp.where(hot1, before, 0.0), axis=-1, keepdims=True).astype(jnp.int32)
    rank2 = jnp.sum(jnp.where(hot2, before, 0.0), axis=-1, keepdims=True).astype(jnp.int32)
    run = run + jnp.sum(hot.astype(F32), axis=0, keepdims=True)
    e = jnp.where(lane == 0, i1, jnp.where(lane == 1, i2, jnp.where(lane == 2, rank1,
                                                                     jnp.where(lane == 3, rank2, 0))))
    p = jnp.where(lane == 0, w1, jnp.where(lane == 1, w2, 0.0))
    return e, p, run


def _merge_kernel(x_ref, ya_ref, yb_ref, yc_ref, of_ref, ob_ref, gd_ref, ng_ref, bd_ref, wg_ref, bg_ref, wbr_ref,
                  wo_ref, g_ref, b_ref, rhi_ref, rlo_ref, rb_ref, tri_ref, o_ref, o16_ref, e_ref, p_ref, cnt_ref,
                  run_ref):
    @pl.when(pl.program_id(0) == 0)
    def _():
        run_ref[...] = jnp.zeros_like(run_ref)

    od = of_ref[...] + ob_ref[...]
    ms = jnp.dot(od * od, bd_ref[...], precision=HIGHEST, preferred_element_type=F32) * (1.0 / HEAD_DIM)
    gd = gd_ref[...]
    y_d = od * lax.rsqrt(ms + RMS_EPS) * ng_ref[...] * (gd * _sigmoid(gd))

    x = x_ref[...]
    xb = x.astype(BF16)
    merged = None
    for i, y in enumerate((ya_ref[...], yb_ref[...], yc_ref[...], y_d)):
        gate = _sigmoid(jnp.dot(xb, wg_ref[i], preferred_element_type=F32) + bg_ref[i])
        term = gate * jnp.dot(y.astype(BF16), wbr_ref[i], preferred_element_type=F32)
        merged = term if merged is None else merged + term
    mix = jnp.dot(merged.astype(BF16), wo_ref[...], preferred_element_type=F32)
    out = _ln_rows(DEEPNORM_ALPHA * x + mix, g_ref[...], b_ref[...])
    o_ref[...] = out
    o16_ref[...] = _pack_bf16_pairs(out)
    e, p, run = _route(out, rhi_ref[...], rlo_ref[...], rb_ref[...], tri_ref[...], run_ref[...])
    e_ref[...] = e.T[:e_ref.shape[0], :]
    p_ref[...] = p
    run_ref[...] = run
    cnt_ref[...] = run.astype(jnp.int32)


def _merge(x, y_a, y_b, y_c, o_f, o_b, pd, norm_g, w_gate, b_gate, w_br, w_out, g, b, wg, bg, we, be, tm=512):
    t, d = x.shape
    row = pl.BlockSpec((tm, d), lambda i: (i, 0))
    yrow = pl.BlockSpec((tm, BR), lambda i: (i, 0))
    vec = pl.BlockSpec((1, d), lambda i: (0, 0))
    lanes = pl.BlockSpec((tm, LANES), lambda i: (i, 0))
    lvec = pl.BlockSpec((1, LANES), lambda i: (0, 0))

    def whole(a):
        return pl.BlockSpec(a.shape, lambda i: (0,) * a.ndim)

    bgate = b_gate.reshape(b_gate.shape[0], 1, d)
    rw = jnp.zeros((d, LANES), F32).at[:, :N_EXPERTS].set(we).at[:, N_EXPERTS:N_EXPERTS + N_GROUPS].set(wg)
    rb = jnp.zeros((1, LANES), F32).at[0, :N_EXPERTS].set(be).at[0, N_EXPERTS:N_EXPERTS + N_GROUPS].set(bg)
    r_hi, r_lo = _split_bf16(rw)
    tri = np.tril(np.ones((tm, tm), np.float32), -1).astype(BF16)
    ng = jnp.tile(norm_g, N_HEADS).reshape(1, BR)
    bd = _same_head_matrix()
    return pl.pallas_call(
        _merge_kernel, grid=(t // tm,),
        in_specs=[row, yrow, yrow, yrow, yrow, yrow, pl.BlockSpec((tm, BR), lambda i: (i, 4)), whole(ng), whole(bd),
                  whole(w_gate), whole(bgate), whole(w_br), whole(w_out), vec, vec,
                  whole(r_hi), whole(r_lo), lvec, whole(tri)],
        out_specs=[row, pl.BlockSpec((tm, d // 2), lambda i: (i, 0)), pl.BlockSpec((8, tm), lambda i: (0, i)),
                   lanes, lvec],
        out_shape=[jax.ShapeDtypeStruct((t, d), F32), jax.ShapeDtypeStruct((t, d // 2), jnp.uint32),
                   jax.ShapeDtypeStruct((8, t), jnp.int32), jax.ShapeDtypeStruct((t, LANES), F32),
                   jax.ShapeDtypeStruct((1, LANES), jnp.int32)],
        scratch_shapes=[pltpu.VMEM((1, LANES), F32)],
        compiler_params=_cparams("arbitrary"), name="merge_ln1_route",
    )(x, y_a, y_b, y_c, o_f, o_b, pd, ng, bd, w_gate, bgate, w_br, w_out, g.reshape(1, d), b.reshape(1, d),
      r_hi, r_lo, rb, tri)


def _expert_work_items(counts, n_blocks):
    end = jnp.cumsum(counts)
    start = end - counts
    first_blk = start // MOE_ROWS
    n_e = jnp.where(counts > 0, (end - 1) // MOE_ROWS - first_blk + 1, 0)
    item_end = jnp.cumsum(n_e)
    item_start = item_end - n_e
    w = jnp.arange(n_blocks + N_EXPERTS, dtype=jnp.int32)
    live = w < item_end[-1]
    wc = jnp.minimum(w, item_end[-1] - 1)
    e_w = jnp.sum((item_end[None, :] <= wc[:, None]).astype(jnp.int32), axis=1)
    blk = first_blk[e_w] + wc - item_start[e_w]
    lo = jnp.where(live, jnp.maximum(start[e_w], blk * MOE_ROWS) - blk * MOE_ROWS, 0)
    hi = jnp.where(live, jnp.minimum(end[e_w], (blk + 1) * MOE_ROWS) - blk * MOE_ROWS, 0)
    first = jnp.concatenate([jnp.ones((1,), jnp.int32), (blk[1:] != blk[:-1]).astype(jnp.int32)])
    return jnp.stack([e_w, blk, lo, hi, first]).astype(jnp.int32)


def _expert_kernel(items_ref, x_ref, w13_ref, w2_ref, o_ref):
    w = pl.program_id(0)
    lo = items_ref[2, w]
    hi = items_ref[3, w]

    @pl.when(hi > lo)
    def _():
        x = _unpack_bf16_pairs(x_ref[...]).astype(BF16)
        h = jnp.dot(x, w13_ref[0], preferred_element_type=F32)
        hg = h[:, :D_FF_E]
        act = hg * _sigmoid(hg) * h[:, D_FF_E:]
        y = _pack_bf16_pairs(jnp.dot(act.astype(BF16), w2_ref[0], preferred_element_type=F32))
        row = lax.broadcasted_iota(jnp.int32, (MOE_ROWS, 1), 0)
        mine = (row >= lo) & (row < hi)

        @pl.when(items_ref[4, w] == 1)
        def _():
            o_ref[...] = jnp.where(mine, y, jnp.zeros_like(y))

        @pl.when(items_ref[4, w] == 0)
        def _():
            o_ref[...] = jnp.where(mine, y, o_ref[...])


def _expert_blocks(xs, items, w13, w2):
    n_rows, half = xs.shape
    d = 2 * half
    grid_spec = pltpu.PrefetchScalarGridSpec(
        num_scalar_prefetch=1, grid=(items.shape[1],),
        in_specs=[pl.BlockSpec((MOE_ROWS, half), lambda w, it: (it[1, w], 0)),
                  pl.BlockSpec((1, d, 2 * D_FF_E), lambda w, it: (it[0, w], 0, 0)),
                  pl.BlockSpec((1, D_FF_E, d), lambda w, it: (it[0, w], 0, 0))],
        out_specs=pl.BlockSpec((MOE_ROWS, half), lambda w, it: (it[1, w], 0)))
    return pl.pallas_call(
        _expert_kernel, grid_spec=grid_spec, out_shape=jax.ShapeDtypeStruct((n_rows, half), jnp.uint32),
        compiler_params=_cparams("arbitrary"), name="moe_experts",
    )(items, xs, w13, w2)


def _combine_kernel(x_ref, y0_ref, y1_ref, p_ref, g_ref, b_ref, o_ref):
    o_ref[...] = _ln_rows(_moe_residual(x_ref, y0_ref, y1_ref, p_ref), g_ref[...], b_ref[...])


def _combine(x, y01, p, g, b, tm=1024):
    t, d = x.shape
    row = pl.BlockSpec((tm, d), lambda i: (i, 0))
    vec = pl.BlockSpec((1, d), lambda i: (0, 0))
    return pl.pallas_call(
        _combine_kernel, grid=(t // tm,),
        in_specs=[row] + _pair_specs(t, d, tm) + [pl.BlockSpec((tm, LANES), lambda i: (i, 0)), vec, vec],
        out_specs=row, out_shape=jax.ShapeDtypeStruct((t, d), F32),
        compiler_params=_cparams("parallel"), name="moe_combine_ln2",
    )(x, y01, y01, p, g.reshape(1, d), b.reshape(1, d))


def _dest_kernel(start_ref, r_ref, o_ref):
    r = r_ref[...]
    e = r[0:TOP_K, :]
    base = jnp.zeros_like(e)
    for x in range(N_EXPERTS):
        base = jnp.where(e == x, start_ref[x], base)
    pad = jnp.zeros((r.shape[0] - TOP_K, r.shape[1]), jnp.int32)
    o_ref[...] = jnp.concatenate([base + r[TOP_K:2 * TOP_K, :], pad], axis=0)


def _slot_rows(routing, start, tl=8192):
    rows, t = routing.shape
    tl = min(tl, t)
    grid_spec = pltpu.PrefetchScalarGridSpec(
        num_scalar_prefetch=1, grid=(t // tl,),
        in_specs=[pl.BlockSpec((rows, tl), lambda i, st: (0, i))],
        out_specs=pl.BlockSpec((rows, tl), lambda i, st: (0, i)))
    return pl.pallas_call(
        _dest_kernel, grid_spec=grid_spec, out_shape=jax.ShapeDtypeStruct((rows, t), jnp.int32),
        compiler_params=_cparams("parallel"), name="moe_slot_rows",
    )(start, routing)


def _sc_workers():
    info = plsc.get_sparse_core_info()
    return info.num_cores, info.num_cores * info.num_subcores


def _sc_move_rows(src, dest, scatter):
    t = dest.shape[1]
    width = src.shape[1]
    n_cores, n_workers = _sc_workers()
    steps = t // (n_workers * SC_WINDOW)
    assert steps * n_workers * SC_WINDOW == t and steps % 2 == 0, (t, n_workers)
    mesh = plsc.VectorSubcoreMesh(core_axis_name="core", subcore_axis_name="subcore")

    @functools.partial(
        pl.kernel, mesh=mesh, out_type=jax.ShapeDtypeStruct((TOP_K * t, width), src.dtype),
        scratch_types=[pltpu.VMEM((TOP_K, steps, SC_WINDOW), jnp.int32),
                       pltpu.VMEM((2, SC_WINDOW, width), src.dtype),
                       pltpu.SemaphoreType.DMA, pltpu.SemaphoreType.DMA])
    def move(src_hbm, idx_hbm, out_hbm, idx_v, rows_v, sem0, sem1):
        wid = lax.axis_index("subcore") * n_cores + lax.axis_index("core")
        for k in range(TOP_K):
            pltpu.sync_copy(idx_hbm.at[k, wid], idx_v.at[k])
        sems = (sem0, sem1)

        def window(j):
            return pl.ds((wid * steps + j) * SC_WINDOW, SC_WINDOW)

        def run(fetch, flush):
            fetch(0, 0).start()

            @pl.loop(0, steps, step=2)
            def _(j):
                fetch(j + 1, 1).start()
                fetch(j, 0).wait()
                flush(j, 0)

                @pl.when(j + 2 < steps)
                def _():
                    fetch(j + 2, 0).start()

                fetch(j + 1, 1).wait()
                flush(j + 1, 1)

        if scatter:
            def fetch(j, slot):
                return pltpu.make_async_copy(src_hbm.at[window(j)], rows_v.at[slot], sems[slot])

            def flush(j, slot):
                for k in range(TOP_K):
                    pltpu.sync_copy(rows_v.at[slot], out_hbm.at[idx_v.at[k, j]])

            run(fetch, flush)
        else:
            for k in range(TOP_K):
                def fetch(j, slot, k=k):
                    return pltpu.make_async_copy(src_hbm.at[idx_v.at[k, j]], rows_v.at[slot], sems[slot])

                def flush(j, slot, k=k):
                    pltpu.sync_copy(rows_v.at[slot], out_hbm.at[pl.ds(k * t + (wid * steps + j) * SC_WINDOW,
                                                                      SC_WINDOW)])

                run(fetch, flush)

    return move(src, dest.reshape(dest.shape[0], n_workers, steps, SC_WINDOW))


def _expert_outputs(x16, routing, cnt, w13, w2):
    t = x16.shape[0]
    counts = cnt[0, :N_EXPERTS]
    start = jnp.cumsum(counts) - counts
    dest = _slot_rows(routing, start)
    xs = _sc_move_rows(x16, dest, scatter=True)
    ys = _expert_blocks(xs, _expert_work_items(counts, t * TOP_K // MOE_ROWS), w13, w2)
    return _sc_move_rows(ys, dest, scatter=False)


def _trunk(x, prm):
    b, l, d = x.shape
    t = b * l
    x = x.reshape(t, d)
    moe, g, bias = None, prm['ln_in_g'], prm['ln_in_b']
    tc, ts = _fft_twiddles(l)
    for i in range(DEPTH):
        x, pa, pb, pc, pd = _input_projection(x, moe, g, bias, prm['w_in'][i])
        y_a = _neighbourhood_attention(pa.reshape(b, l, -1), prm['attn_bias'][i])
        y_b = _short_conv(pb.reshape(b, l, -1), prm['conv_w'][i])
        y_c = _fourier_mix(pc.reshape(b, l, -1), tc, ts)
        o_f, o_b = _hgrn2_bidirectional(pd.reshape(b, l, -1), prm['lb'][0, i], prm['lb'][1, i], i == 0)
        x, x16, routing, p_lanes, cnt = _merge(
            x, y_a.reshape(t, BR), y_b.reshape(t, BR), y_c.reshape(t, BR), o_f.reshape(t, BR), o_b.reshape(t, BR),
            pd, prm['hgrn_norm_g'][i], prm['w_gate'][i], prm['b_gate'][i], prm['w_br'][i], prm['w_out'][i],
            prm['ln1_g'][i], prm['ln1_b'][i], prm['router_g_w'][i], prm['router_g_b'][i], prm['router_e_w'][i],
            prm['router_e_b'][i])
        y01 = _expert_outputs(x16, routing, cnt, prm['w13'][i], prm['w2'][i])
        moe, g, bias = (y01, p_lanes), prm['ln2_g'][i], prm['ln2_b'][i]
    return _combine(x, *moe, g, bias).reshape(b, l, d)


def kernel(x_prompt, x_sample, ln_in_g, ln_in_b, w_in, na_rpb, conv_w, hgrn_lb, hgrn_norm_g, w_gate, b_gate, w_br,
           w_out, ln1_g, ln1_b, router_g_w, router_g_b, router_e_w, router_e_b, w13, w2, ln2_g, ln2_b):
    lb = jnp.cumsum(jax.nn.softmax(hgrn_lb.astype(F32), axis=1), axis=1)
    lb = lb - lb[:, :1]
    prm = dict(
        ln_in_g=ln_in_g, ln_in_b=ln_in_b,
        w_in=jnp.stack([_fold_channel_dft(w_in[i]) for i in range(DEPTH)]).astype(BF16),
        attn_bias=jnp.stack([_attn_bias_table(na_rpb[i]) for i in range(DEPTH)]),
        conv_w=conv_w, lb=lb, hgrn_norm_g=hgrn_norm_g,
        w_gate=w_gate.astype(BF16), b_gate=b_gate, w_br=w_br.astype(BF16), w_out=w_out.astype(BF16),
        ln1_g=ln1_g, ln1_b=ln1_b, router_g_w=router_g_w, router_g_b=router_g_b,
        router_e_w=router_e_w, router_e_b=router_e_b, w13=w13.astype(BF16), w2=w2.astype(BF16),
        ln2_g=ln2_g, ln2_b=ln2_b)
    return _trunk(x_prompt, prm), _trunk(x_sample, prm)
```

```python
import functools

import numpy as np
import jax
import jax.numpy as jnp
from jax import lax
from jax.experimental import pallas as pl
from jax.experimental.pallas import tpu as pltpu
from jax.experimental.pallas import tpu_sc as plsc

F32 = jnp.float32
BF16 = jnp.bfloat16
HIGHEST = lax.Precision.HIGHEST

D_MODEL = 1024
DEPTH = 4
GRID_W = 64
BR = 256
N_HEADS = 4
HEAD_DIM = 64
NA_ROWS = 8
NA_COLS = 16
FN_GROUP_DIM = 64
N_GROUPS = 4
EXPERTS_PER_GROUP = 8
N_EXPERTS = N_GROUPS * EXPERTS_PER_GROUP
TOP_K = 2
D_FF_E = 512
DEEPNORM_ALPHA = (2 * DEPTH) ** 0.25
LN_EPS = 1e-5
RMS_EPS = 1e-6
NEG_INF = -1e30

LANES = 128
VMEM_LIMIT_BYTES = 56 * 1024 * 1024
MOE_ROWS = 512
HG_CHUNK = 128
FFT_L1 = 64
SC_WINDOW = 64

COL_QA, COL_KA, COL_VA, COL_XB, COL_GB, COL_GC, COL_XF, COL_QD, COL_FF, COL_FB, COL_VD, COL_GD = range(12)
PROJ_GROUPS = ((0, 3), (3, 6), (6, 8), (8, 13))


def _cparams(*sem):
    return pltpu.CompilerParams(dimension_semantics=sem, vmem_limit_bytes=VMEM_LIMIT_BYTES)


def _sigmoid(x):
    return 1.0 / (1.0 + jnp.exp(-x))


def _split_bf16(x):
    hi = x.astype(BF16)
    return hi, (x - hi.astype(F32)).astype(BF16)


def _dot_split(a_hi, a_lo, b_hi, b_lo):
    dot = functools.partial(jnp.dot, preferred_element_type=F32)
    return dot(a_hi, b_hi) + (dot(a_lo, b_hi) + dot(a_hi, b_lo))


def _pack_bf16_pairs(x):
    m = x.shape[1] // 2
    hi = lax.bitcast_convert_type(x[:, :m].astype(BF16).astype(F32), jnp.uint32)
    lo = lax.bitcast_convert_type(x[:, m:].astype(BF16).astype(F32), jnp.uint32)
    return hi | (lo >> 16)


def _unpack_bf16_pairs(u):
    hi = lax.bitcast_convert_type(u & jnp.uint32(0xFFFF0000), F32)
    lo = lax.bitcast_convert_type(u << 16, F32)
    return jnp.concatenate([hi, lo], axis=1)


def _ln_rows(x, g, b):
    mu = jnp.mean(x, axis=-1, keepdims=True)
    xc = x - mu
    var = jnp.mean(xc * xc, axis=-1, keepdims=True)
    return xc * lax.rsqrt(var + LN_EPS) * g + b


def _head_of_lane(shape):
    return lax.broadcasted_iota(jnp.int32, shape, len(shape) - 1) // HEAD_DIM


def _head_stack(x):
    head = _head_of_lane(x.shape)
    return jnp.concatenate([jnp.where(head == h, x, jnp.zeros_like(x)) for h in range(N_HEADS)], axis=0)


def _same_head_matrix():
    h = np.arange(BR) // HEAD_DIM
    return (h[:, None] == h[None, :]).astype(np.float32)


def _moe_residual(x_ref, y0_ref, y1_ref, p_ref):
    p = p_ref[...]
    ffn = p[:, 0:1] * _unpack_bf16_pairs(y0_ref[...]) + p[:, 1:2] * _unpack_bf16_pairs(y1_ref[...])
    return DEEPNORM_ALPHA * x_ref[...] + ffn


def _proj_kernel(*refs, after_moe):
    if after_moe:
        x_ref, y0_ref, y1_ref, p_ref, g_ref, b_ref, w_ref, xo_ref, *out_refs = refs
        pre = _moe_residual(x_ref, y0_ref, y1_ref, p_ref)
    else:
        x_ref, g_ref, b_ref, w_ref, xo_ref, *out_refs = refs
        pre = x_ref[...]
    x = _ln_rows(pre, g_ref[...], b_ref[...])
    xo_ref[...] = x
    xb = x.astype(BF16)
    for ref, (lo, hi) in zip(out_refs, PROJ_GROUPS):
        ref[...] = jnp.dot(xb, w_ref[:, lo * BR:hi * BR], preferred_element_type=F32).astype(ref.dtype)


def _fold_channel_dft(w_in):
    cg, sg = _dft_cos_sin(FN_GROUP_DIM)
    eye = np.eye(BR // FN_GROUP_DIM, dtype=np.float32)
    w_xf = w_in[:, COL_XF * BR:(COL_XF + 1) * BR]
    w_re = jnp.dot(w_xf, np.kron(eye, cg), precision=HIGHEST)
    w_im = -jnp.dot(w_xf, np.kron(eye, sg), precision=HIGHEST)
    return jnp.concatenate([w_in[:, :COL_XF * BR], w_re, w_im, w_in[:, (COL_XF + 1) * BR:]], axis=1)


def _pair_specs(t, d, tm):
    return [pl.BlockSpec((tm, d // 2), lambda i: (i, 0)), pl.BlockSpec((tm, d // 2), lambda i: (i + t // tm, 0))]


def _input_projection(x, moe, g, b, w_in, tm=512):
    t, d = x.shape
    n = w_in.shape[1]
    widths = (d,) + tuple((hi - lo) * BR for lo, hi in PROJ_GROUPS)
    dtypes = (F32, BF16, BF16, F32, BF16)
    row = pl.BlockSpec((tm, d), lambda i: (i, 0))
    vec = pl.BlockSpec((1, d), lambda i: (0, 0))
    acts, act_specs = [x], [row]
    if moe is not None:
        y01, p = moe
        acts += [y01, y01, p]
        act_specs += _pair_specs(t, d, tm) + [pl.BlockSpec((tm, LANES), lambda i: (i, 0))]
    return pl.pallas_call(
        functools.partial(_proj_kernel, after_moe=moe is not None), grid=(t // tm,),
        in_specs=act_specs + [vec, vec, pl.BlockSpec((d, n), lambda i: (0, 0))],
        out_specs=[pl.BlockSpec((tm, w), lambda i: (i, 0)) for w in widths],
        out_shape=[jax.ShapeDtypeStruct((t, w), dt) for w, dt in zip(widths, dtypes)],
        compiler_params=_cparams("parallel"), name="ln_in_proj",
    )(*acts, g.reshape(1, d), b.reshape(1, d), w_in)


def _attn_bias_table(rpb):
    c = np.arange(GRID_W)
    win_c0 = np.clip(c - NA_COLS // 2, 0, GRID_W - NA_COLS)
    valid = (c[None, :] >= win_c0[:, None]) & (c[None, :] < win_c0[:, None] + NA_COLS)
    dc = np.clip(c[None, :] - c[:, None], -(NA_COLS - 1), NA_COLS - 1) + NA_COLS - 1
    dr = np.arange(NA_ROWS)[None, :] - np.arange(NA_ROWS)[:, None] + NA_ROWS - 1
    row_hot = (dr[:, :, None] == np.arange(2 * NA_ROWS - 1)).astype(np.float32)
    col_hot = (dc[:, :, None] == np.arange(2 * NA_COLS - 1)).astype(np.float32)
    t = jnp.einsum('oia,hab,qkb->ohqik', row_hot, rpb.astype(F32), col_hot, precision=HIGHEST)
    t = jnp.where(valid[None, None, :, None, :], t, NEG_INF)
    return t.reshape(NA_ROWS, N_HEADS * GRID_W, NA_ROWS * GRID_W)


def _attn_kernel(q_ref, k_ref, v_ref, bias_ref, o_ref, *, n_rows, rows_per_tile):
    i = pl.program_id(1)
    head = _head_of_lane((GRID_W, BR))
    scale = HEAD_DIM ** -0.5
    span = NA_ROWS * GRID_W
    for j in range(rows_per_tile):
        r = i * rows_per_tile + j
        r0 = jnp.clip(r - NA_ROWS // 2, 0, n_rows - NA_ROWS)
        q = q_ref[0, j * GRID_W:(j + 1) * GRID_W, :]
        qs = _head_stack(q)
        start = pl.multiple_of(r0 * GRID_W, GRID_W)
        ks = k_ref[0, pl.ds(start, span), :]
        vs = v_ref[0, pl.ds(start, span), :]
        s = lax.dot_general(qs, ks, (((1,), (1,)), ((), ())), preferred_element_type=F32)
        s = s * scale + bias_ref[r - r0]
        m = jnp.max(s, axis=-1, keepdims=True)
        e = jnp.exp(s - m)
        p = e / jnp.sum(e, axis=-1, keepdims=True)
        o4 = jnp.dot(p.astype(BF16), vs, preferred_element_type=F32)
        o = jnp.zeros((GRID_W, BR), F32)
        for h in range(N_HEADS):
            o = o + jnp.where(head == h, o4[h * GRID_W:(h + 1) * GRID_W, :], 0.0)
        o_ref[0, j * GRID_W:(j + 1) * GRID_W, :] = o.astype(o_ref.dtype)


def _neighbourhood_attention(pa, bias, rows_per_tile=8):
    b, l, _ = pa.shape
    n_rows = l // GRID_W
    tq = rows_per_tile * GRID_W
    kern = functools.partial(_attn_kernel, n_rows=n_rows, rows_per_tile=rows_per_tile)
    return pl.pallas_call(
        kern, grid=(b, l // tq),
        in_specs=[pl.BlockSpec((1, tq, BR), lambda bi, i: (bi, i, 0)),
                  pl.BlockSpec((1, l, BR), lambda bi, i: (bi, 0, 1)),
                  pl.BlockSpec((1, l, BR), lambda bi, i: (bi, 0, 2)),
                  pl.BlockSpec(bias.shape, lambda bi, i: (0, 0, 0))],
        out_specs=pl.BlockSpec((1, tq, BR), lambda bi, i: (bi, i, 0)),
        out_shape=jax.ShapeDtypeStruct((b, l, BR), BF16),
        compiler_params=_cparams("parallel", "arbitrary"), name="nbr_attn",
    )(pa, pa, pa, bias)


def _conv_kernel(xb_ref, gb_ref, gc_ref, xbp_ref, gcp_ref, xbn_ref, gcn_ref, w_ref, o_ref, *, n_tiles):
    i = pl.program_id(1)
    tl = xb_ref.shape[1]
    last = xbp_ref.shape[1] - 1

    def gated(gc, xb):
        return gc.astype(F32) * xb.astype(F32)

    u = gated(gc_ref[0], xb_ref[0])
    u_before = jnp.where(i > 0, gated(gcp_ref[0, last:, :], xbp_ref[0, last:, :]), 0.0)
    u_after = jnp.where(i < n_tiles - 1, gated(gcn_ref[0, 0:1, :], xbn_ref[0, 0:1, :]), 0.0)
    row = lax.broadcasted_iota(jnp.int32, (tl, 1), 0)
    u_prev = jnp.where(row == 0, u_before, pltpu.roll(u, 1, 0))
    u_next = jnp.where(row == tl - 1, u_after, pltpu.roll(u, tl - 1, 0))
    conv = w_ref[0:1, :] * u_prev + w_ref[1:2, :] * u + w_ref[2:3, :] * u_next
    o_ref[0] = (gb_ref[0].astype(F32) * conv).astype(o_ref.dtype)


def _short_conv(pb, conv_w, tl=1024):
    b, l, _ = pb.shape
    n_tiles = l // tl
    halo = 16
    per_tile = tl // halo
    last_halo = l // halo - 1

    def main(col):
        return pl.BlockSpec((1, tl, BR), lambda bi, i: (bi, i, col))

    def before(col):
        return pl.BlockSpec((1, halo, BR), lambda bi, i: (bi, jnp.maximum(i * per_tile - 1, 0), col))

    def after(col):
        return pl.BlockSpec((1, halo, BR), lambda bi, i: (bi, jnp.minimum((i + 1) * per_tile, last_halo), col))

    return pl.pallas_call(
        functools.partial(_conv_kernel, n_tiles=n_tiles), grid=(b, n_tiles),
        in_specs=[main(0), main(1), main(2), before(0), before(2), after(0), after(2),
                  pl.BlockSpec(conv_w.shape, lambda bi, i: (0, 0))],
        out_specs=pl.BlockSpec((1, tl, BR), lambda bi, i: (bi, i, 0)),
        out_shape=jax.ShapeDtypeStruct((b, l, BR), BF16),
        compiler_params=_cparams("parallel", "parallel"), name="short_conv",
    )(pb, pb, pb, pb, pb, pb, pb, conv_w)


def _dft_cos_sin(n):
    ang = 2.0 * np.pi * ((np.arange(n)[:, None] * np.arange(n)[None, :]) % n) / n
    return np.cos(ang).astype(np.float32), np.sin(ang).astype(np.float32)


def _fft_major_kernel(g_ref, m_ref, tc_ref, ts_ref, br_ref, bi_ref):
    for s in range(8):
        g = g_ref[0, :, s, :]
        swapped = jnp.concatenate([g[:, BR:], -g[:, :BR]], axis=1)
        a = jnp.dot(m_ref[...], jnp.concatenate([g, swapped], axis=0), precision=HIGHEST,
                    preferred_element_type=F32)
        ar, ai = a[:, :BR], a[:, BR:]
        tc = tc_ref[:, s, :]
        ts = ts_ref[:, s, :]
        br_ref[0, :, s, :] = ar * tc + ai * ts
        bi_ref[0, :, s, :] = ai * tc - ar * ts


def _fft_minor_kernel(br_ref, bi_ref, m_ref, o_ref, *, scale):
    l2 = br_ref.shape[2]
    group = m_ref.shape[1] // l2
    for j0 in range(0, 8, group):
        rhs = jnp.concatenate([br_ref[0, j0:j0 + group].reshape(group * l2, BR),
                               bi_ref[0, j0:j0 + group].reshape(group * l2, BR)], axis=0)
        y = _dot_split(m_ref[0], m_ref[1], *_split_bf16(rhs)) * scale
        for j in range(group):
            o_ref[0, :, j0 + j, :] = y[j * l2:(j + 1) * l2].astype(o_ref.dtype)


def _fft_twiddles(l):
    l1 = FFT_L1
    l2 = l // l1
    idx = (jnp.arange(l1, dtype=jnp.int32)[:, None] * jnp.arange(l2, dtype=jnp.int32)[None, :]) % l
    ang = idx.astype(F32) * (2.0 * np.pi / l)
    tc = jnp.broadcast_to(jnp.cos(ang)[:, :, None], (l1, l2, BR))
    ts = jnp.broadcast_to(jnp.sin(ang)[:, :, None], (l1, l2, BR))
    return tc, ts


def _fourier_mix(pc, tc, ts):
    b, l, _ = pc.shape
    l1 = FFT_L1
    l2 = l // l1

    c1, s1 = _dft_cos_sin(l1)
    c2, s2 = _dft_cos_sin(l2)
    group = min(max(1, 128 // l2), 8)
    eye = np.eye(group, dtype=np.float32)
    minor = jnp.asarray(np.concatenate([np.kron(eye, c2), np.kron(eye, s2)], axis=1))
    minor_hi = minor.astype(BF16)
    minor_parts = jnp.stack([minor_hi, (minor - minor_hi.astype(F32)).astype(BF16)])

    def strip(width):
        return pl.BlockSpec((1, l1, 8, width), lambda bi, i: (bi, 0, i, 0))

    tspec = pl.BlockSpec((l1, 8, BR), lambda bi, i: (0, i, 0))
    mspec = pl.BlockSpec((l1, 2 * l1), lambda bi, i: (0, 0))
    br, bi_ = pl.pallas_call(
        _fft_major_kernel, grid=(b, l2 // 8),
        in_specs=[strip(2 * BR), mspec, tspec, tspec], out_specs=[strip(BR), strip(BR)],
        out_shape=[jax.ShapeDtypeStruct((b, l1, l2, BR), F32)] * 2,
        compiler_params=_cparams("parallel", "parallel"), name="fft_major",
    )(pc.reshape(b, l1, l2, 2 * BR), np.concatenate([c1, s1], axis=1), tc, ts)

    bspec = pl.BlockSpec((1, 8, l2, BR), lambda bi, i: (bi, i, 0, 0))
    m2spec = pl.BlockSpec(minor_parts.shape, lambda bi, i: (0, 0, 0))
    scale = 1.0 / float(np.sqrt(l * FN_GROUP_DIM))
    y = pl.pallas_call(
        functools.partial(_fft_minor_kernel, scale=scale), grid=(b, l1 // 8),
        in_specs=[bspec, bspec, m2spec],
        out_specs=pl.BlockSpec((1, l2, 8, BR), lambda bi, i: (bi, 0, i, 0)),
        out_shape=jax.ShapeDtypeStruct((b, l2, l1, BR), F32),
        compiler_params=_cparams("parallel", "parallel"), name="fft_minor",
    )(br, bi_, minor_parts)
    return y.reshape(b, l, BR)


def _hgrn_level_masks():
    t = np.arange(HG_CHUNK)
    out = []
    half = 8
    while 2 * half < HG_CHUNK:
        same = (t[:, None] // (2 * half)) == (t[None, :] // (2 * half))
        out.append(np.tile(same, (1, N_HEADS)).astype(np.float32))
        half *= 2
    return np.stack(out)


def _hgrn_kernel(qd_ref, fl_ref, vd_ref, lb_ref, mask_ref, bd_ref, o_ref, st_ref, *, first_layer, chunks_per_tile,
                 rev):
    c = HG_CHUNK

    @pl.when(pl.program_id(1) == 0)
    def _():
        st_ref[...] = jnp.zeros_like(st_ref)

    bd = bd_ref[...]
    bd16 = bd.astype(BF16)
    lb = lb_ref[...]
    row = lax.broadcasted_iota(jnp.int32, (c, 1), 0)
    sub = row % 8
    nt = (((1,), (1,)), ((), ()))

    def roll8(x, j):
        return pltpu.roll(x.reshape(c // 8, 8, BR), j % 8, 1).reshape(c, BR)

    def earlier(x, j):
        return roll8(x, -j if rev else j)

    def later(x, j):
        return roll8(x, j if rev else -j)

    def has_earlier(j):
        return (sub + j <= 7) if rev else (sub >= j)

    def has_later(j):
        return (sub >= j) if rev else (sub + j <= 7)

    for ci in (reversed(range(chunks_per_tile)) if rev else range(chunks_per_tile)):
        sl = slice(ci * c, (ci + 1) * c)
        qd = qd_ref[0, sl, :].astype(F32)
        fl = fl_ref[0, sl, :].astype(F32)
        v = vd_ref[0, sl, :].astype(F32)
        q = qd * _sigmoid(qd)
        if first_layer:
            f = _sigmoid(fl)
            k = _sigmoid(-fl)
        else:
            f = lb + (1.0 - lb) * _sigmoid(fl)
            k = (1.0 - lb) * _sigmoid(-fl)

        o = jnp.dot((q * k).astype(BF16), bd16, preferred_element_type=F32) * v
        g = f
        for d in range(1, 8):
            if d > 1:
                g = g * earlier(f, d - 1)
            p = jnp.where(has_earlier(d), q * g * earlier(k, d), 0.0)
            o = o + jnp.dot(p.astype(BF16), bd16, preferred_element_type=F32) * earlier(v, d)

        qf = f
        for j in (1, 2, 4):
            qf = qf * jnp.where(has_earlier(j), earlier(qf, j), 1.0)
        kf = jnp.where(has_later(1), later(f, 1), 1.0)
        for j in (1, 2, 4):
            kf = kf * jnp.where(has_later(j), later(kf, j), 1.0)

        scores = None
        half = 8
        level = 0
        while half < c:
            late = ((row // half) % 2) == (0 if rev else 1)
            qb = jnp.where(late, q * qf, 0.0).astype(BF16)
            kb = jnp.where(late, 0.0, k * kf).astype(BF16)
            x = lax.dot_general(qb, _head_stack(kb), nt, preferred_element_type=F32)
            if 2 * half < c:
                x = x * mask_ref[level]
            scores = x if scores is None else scores + x
            blocks = c // (2 * half)
            qf3 = qf.reshape(blocks, 2 * half, BR)
            end_early = half if rev else half - 1
            end_late = 0 if rev else 2 * half - 1
            tot_early = jnp.broadcast_to(qf3[:, end_early:end_early + 1, :], qf3.shape).reshape(c, BR)
            tot_late = jnp.broadcast_to(qf3[:, end_late:end_late + 1, :], qf3.shape).reshape(c, BR)
            qf = qf * jnp.where(late, tot_early, 1.0)
            kf = kf * jnp.where(late, 1.0, tot_late)
            half *= 2
            level += 1

        o = o + jnp.dot(scores.astype(BF16), _head_stack(v.astype(BF16)), preferred_element_type=F32)

        st = st_ref[...]
        o = o + lax.dot_general((q * qf).astype(BF16), st.astype(BF16), nt, preferred_element_type=F32)
        o_ref[0, sl, :] = o
        upd = jnp.dot(v.T.astype(BF16), (k * kf).astype(BF16), preferred_element_type=F32)
        last = 0 if rev else c - 1
        st_ref[...] = st * qf[last:last + 1, :] + upd * bd


def _hgrn_scan(qd, fl, vd, cols, lb, first_layer, rev, chunks_per_tile=8):
    b, l, _ = qd.shape
    tile = HG_CHUNK * chunks_per_tile
    n_tiles = l // tile
    masks = _hgrn_level_masks()
    bd = _same_head_matrix()

    def tile_index(i):
        return n_tiles - 1 - i if rev else i

    def seq(col):
        return pl.BlockSpec((1, tile, BR), lambda bi, i: (bi, tile_index(i), col))

    return pl.pallas_call(
        functools.partial(_hgrn_kernel, first_layer=first_layer, chunks_per_tile=chunks_per_tile, rev=rev),
        grid=(b, n_tiles),
        in_specs=[seq(cols[0]), seq(cols[1]), seq(cols[2]),
                  pl.BlockSpec((1, BR), lambda bi, i: (0, 0)),
                  pl.BlockSpec(masks.shape, lambda bi, i: (0, 0, 0)),
                  pl.BlockSpec((BR, BR), lambda bi, i: (0, 0))],
        out_specs=pl.BlockSpec((1, tile, BR), lambda bi, i: (bi, tile_index(i), 0)),
        out_shape=jax.ShapeDtypeStruct((b, l, BR), F32),
        scratch_shapes=[pltpu.VMEM((BR, BR), F32)],
        compiler_params=_cparams("parallel", "arbitrary"),
        name="hgrn_bwd" if rev else "hgrn_fwd",
    )(qd, fl, vd, lb.reshape(1, BR), masks, bd)


def _hgrn2_bidirectional(pd, lb_f, lb_b, first_layer):
    o_f = _hgrn_scan(pd, pd, pd, (0, 1, 3), lb_f, first_layer, rev=False)
    o_b = _hgrn_scan(pd, pd, pd, (0, 2, 3), lb_b, first_layer, rev=True)
    return o_f, o_b


def _route(x, w_hi, w_lo, bias, tri, run):
    x_hi, x_lo = _split_bf16(x)
    logits = _dot_split(x_hi, x_lo, w_hi, w_lo) + bias
    lane = lax.broadcasted_iota(jnp.int32, logits.shape, 1)
    big = jnp.int32(LANES)
    is_group = (lane >= N_EXPERTS) & (lane < N_EXPERTS + N_GROUPS)
    lg = jnp.where(is_group, logits, -jnp.inf)
    mg = jnp.max(lg, axis=-1, keepdims=True)
    g_sel = jnp.min(jnp.where(lg == mg, lane, big), axis=-1, keepdims=True) - N_EXPERTS
    p_g = 1.0 / jnp.sum(jnp.exp(lg - mg), axis=-1, keepdims=True)
    in_group = (lane < N_EXPERTS) & ((lane // EXPERTS_PER_GROUP) == g_sel)
    le = jnp.where(in_group, logits, -jnp.inf)
    v1 = jnp.max(le, axis=-1, keepdims=True)
    i1 = jnp.min(jnp.where(le == v1, lane, big), axis=-1, keepdims=True)
    le2 = jnp.where(lane == i1, -jnp.inf, le)
    v2 = jnp.max(le2, axis=-1, keepdims=True)
    i2 = jnp.min(jnp.where(le2 == v2, lane, big), axis=-1, keepdims=True)
    e2 = jnp.exp(v2 - v1)
    w1 = p_g / (1.0 + e2)
    w2 = p_g * e2 / (1.0 + e2)
    hot1 = lane == i1
    hot2 = lane == i2
    hot = (hot1 | hot2).astype(BF16)
    before = jnp.dot(tri, hot, preferred_element_type=F32) + run
    rank1 = jnp.sum(jnp.where(hot1, before, 0.0), axis=-1, keepdims=True).astype(jnp.int32)
    rank2 = jnp.sum(jnp.where(hot2, before, 0.0), axis=-1, keepdims=True).astype(jnp.int32)
    run = run + jnp.sum(hot.astype(F32), axis=0, keepdims=True)
    e = jnp.where(lane == 0, i1, jnp.where(lane == 1, i2, jnp.where(lane == 2, rank1,
                                                                     jnp.where(lane == 3, rank2, 0))))
    p = jnp.where(lane == 0, w1, jnp.where(lane == 1, w2, 0.0))
    return e, p, run


def _merge_kernel(x_ref, ya_ref, yb_ref, yc_ref, of_ref, ob_ref, gd_ref, ng_ref, bd_ref, wg_ref, bg_ref, wbr_ref,
                  wo_ref, g_ref, b_ref, rhi_ref, rlo_ref, rb_ref, tri_ref, o_ref, o16_ref, e_ref, p_ref, cnt_ref,
                  run_ref):
    @pl.when(pl.program_id(0) == 0)
    def _():
        run_ref[...] = jnp.zeros_like(run_ref)

    od = of_ref[...] + ob_ref[...]
    ms = jnp.dot(od * od, bd_ref[...], precision=HIGHEST, preferred_element_type=F32) * (1.0 / HEAD_DIM)
    gd = gd_ref[...].astype(F32)
    y_d = od * lax.rsqrt(ms + RMS_EPS) * ng_ref[...] * (gd * _sigmoid(gd))

    x = x_ref[...]
    xb = x.astype(BF16)
    merged = None
    for i, y in enumerate((ya_ref[...], yb_ref[...], yc_ref[...], y_d)):
        gate = _sigmoid(jnp.dot(xb, wg_ref[i], preferred_element_type=F32) + bg_ref[i])
        term = gate * jnp.dot(y.astype(BF16), wbr_ref[i], preferred_element_type=F32)
        merged = term if merged is None else merged + term
    mix = jnp.dot(merged.astype(BF16), wo_ref[...], preferred_element_type=F32)
    out = _ln_rows(DEEPNORM_ALPHA * x + mix, g_ref[...], b_ref[...])
    o_ref[...] = out
    o16_ref[...] = _pack_bf16_pairs(out)
    e, p, run = _route(out, rhi_ref[...], rlo_ref[...], rb_ref[...], tri_ref[...], run_ref[...])
    e_ref[...] = e.T[:e_ref.shape[0], :]
    p_ref[...] = p
    run_ref[...] = run
    cnt_ref[...] = run.astype(jnp.int32)


def _merge(x, y_a, y_b, y_c, o_f, o_b, pd, norm_g, w_gate, b_gate, w_br, w_out, g, b, wg, bg, we, be, tm=512):
    t, d = x.shape
    row = pl.BlockSpec((tm, d), lambda i: (i, 0))
    yrow = pl.BlockSpec((tm, BR), lambda i: (i, 0))
    vec = pl.BlockSpec((1, d), lambda i: (0, 0))
    lanes = pl.BlockSpec((tm, LANES), lambda i: (i, 0))
    lvec = pl.BlockSpec((1, LANES), lambda i: (0, 0))

    def whole(a):
        return pl.BlockSpec(a.shape, lambda i: (0,) * a.ndim)

    bgate = b_gate.reshape(b_gate.shape[0], 1, d)
    rw = jnp.zeros((d, LANES), F32).at[:, :N_EXPERTS].set(we).at[:, N_EXPERTS:N_EXPERTS + N_GROUPS].set(wg)
    rb = jnp.zeros((1, LANES), F32).at[0, :N_EXPERTS].set(be).at[0, N_EXPERTS:N_EXPERTS + N_GROUPS].set(bg)
    r_hi, r_lo = _split_bf16(rw)
    tri = np.tril(np.ones((tm, tm), np.float32), -1).astype(BF16)
    ng = jnp.tile(norm_g, N_HEADS).reshape(1, BR)
    bd = _same_head_matrix()
    return pl.pallas_call(
        _merge_kernel, grid=(t // tm,),
        in_specs=[row, yrow, yrow, yrow, yrow, yrow, pl.BlockSpec((tm, BR), lambda i: (i, 4)), whole(ng), whole(bd),
                  whole(w_gate), whole(bgate), whole(w_br), whole(w_out), vec, vec,
                  whole(r_hi), whole(r_lo), lvec, whole(tri)],
        out_specs=[row, pl.BlockSpec((tm, d // 2), lambda i: (i, 0)), pl.BlockSpec((8, tm), lambda i: (0, i)),
                   lanes, lvec],
        out_shape=[jax.ShapeDtypeStruct((t, d), F32), jax.ShapeDtypeStruct((t, d // 2), jnp.uint32),
                   jax.ShapeDtypeStruct((8, t), jnp.int32), jax.ShapeDtypeStruct((t, LANES), F32),
                   jax.ShapeDtypeStruct((1, LANES), jnp.int32)],
        scratch_shapes=[pltpu.VMEM((1, LANES), F32)],
        compiler_params=_cparams("arbitrary"), name="merge_ln1_route",
    )(x, y_a, y_b, y_c, o_f, o_b, pd, ng, bd, w_gate, bgate, w_br, w_out, g.reshape(1, d), b.reshape(1, d),
      r_hi, r_lo, rb, tri)


def _expert_work_items(counts, n_blocks):
    end = jnp.cumsum(counts)
    start = end - counts
    first_blk = start // MOE_ROWS
    n_e = jnp.where(counts > 0, (end - 1) // MOE_ROWS - first_blk + 1, 0)
    item_end = jnp.cumsum(n_e)
    item_start = item_end - n_e
    w = jnp.arange(n_blocks + N_EXPERTS, dtype=jnp.int32)
    live = w < item_end[-1]
    wc = jnp.minimum(w, item_end[-1] - 1)
    e_w = jnp.sum((item_end[None, :] <= wc[:, None]).astype(jnp.int32), axis=1)
    blk = first_blk[e_w] + wc - item_start[e_w]
    lo = jnp.where(live, jnp.maximum(start[e_w], blk * MOE_ROWS) - blk * MOE_ROWS, 0)
    hi = jnp.where(live, jnp.minimum(end[e_w], (blk + 1) * MOE_ROWS) - blk * MOE_ROWS, 0)
    first = jnp.concatenate([jnp.ones((1,), jnp.int32), (blk[1:] != blk[:-1]).astype(jnp.int32)])
    return jnp.stack([e_w, blk, lo, hi, first]).astype(jnp.int32)


def _expert_kernel(items_ref, x_ref, w13_ref, w2_ref, o_ref):
    w = pl.program_id(0)
    lo = items_ref[2, w]
    hi = items_ref[3, w]

    @pl.when(hi > lo)
    def _():
        x = _unpack_bf16_pairs(x_ref[...]).astype(BF16)
        h = jnp.dot(x, w13_ref[0], preferred_element_type=F32)
        hg = h[:, :D_FF_E]
        act = hg * _sigmoid(hg) * h[:, D_FF_E:]
        y = _pack_bf16_pairs(jnp.dot(act.astype(BF16), w2_ref[0], preferred_element_type=F32))
        row = lax.broadcasted_iota(jnp.int32, (MOE_ROWS, 1), 0)
        mine = (row >= lo) & (row < hi)

        @pl.when(items_ref[4, w] == 1)
        def _():
            o_ref[...] = jnp.where(mine, y, jnp.zeros_like(y))

        @pl.when(items_ref[4, w] == 0)
        def _():
            o_ref[...] = jnp.where(mine, y, o_ref[...])


def _expert_blocks(xs, items, w13, w2):
    n_rows, half = xs.shape
    d = 2 * half
    grid_spec = pltpu.PrefetchScalarGridSpec(
        num_scalar_prefetch=1, grid=(items.shape[1],),
        in_specs=[pl.BlockSpec((MOE_ROWS, half), lambda w, it: (it[1, w], 0)),
                  pl.BlockSpec((1, d, 2 * D_FF_E), lambda w, it: (it[0, w], 0, 0)),
                  pl.BlockSpec((1, D_FF_E, d), lambda w, it: (it[0, w], 0, 0))],
        out_specs=pl.BlockSpec((MOE_ROWS, half), lambda w, it: (it[1, w], 0)))
    return pl.pallas_call(
        _expert_kernel, grid_spec=grid_spec, out_shape=jax.ShapeDtypeStruct((n_rows, half), jnp.uint32),
        compiler_params=_cparams("arbitrary"), name="moe_experts",
    )(items, xs, w13, w2)


def _combine_kernel(x_ref, y0_ref, y1_ref, p_ref, g_ref, b_ref, o_ref):
    o_ref[...] = _ln_rows(_moe_residual(x_ref, y0_ref, y1_ref, p_ref), g_ref[...], b_ref[...])


def _combine(x, y01, p, g, b, tm=1024):
    t, d = x.shape
    row = pl.BlockSpec((tm, d), lambda i: (i, 0))
    vec = pl.BlockSpec((1, d), lambda i: (0, 0))
    return pl.pallas_call(
        _combine_kernel, grid=(t // tm,),
        in_specs=[row] + _pair_specs(t, d, tm) + [pl.BlockSpec((tm, LANES), lambda i: (i, 0)), vec, vec],
        out_specs=row, out_shape=jax.ShapeDtypeStruct((t, d), F32),
        compiler_params=_cparams("parallel"), name="moe_combine_ln2",
    )(x, y01, y01, p, g.reshape(1, d), b.reshape(1, d))


def _dest_kernel(start_ref, r_ref, o_ref):
    r = r_ref[...]
    e = r[0:TOP_K, :]
    base = jnp.zeros_like(e)
    for x in range(N_EXPERTS):
        base = jnp.where(e == x, start_ref[x], base)
    pad = jnp.zeros((r.shape[0] - TOP_K, r.shape[1]), jnp.int32)
    o_ref[...] = jnp.concatenate([base + r[TOP_K:2 * TOP_K, :], pad], axis=0)


def _slot_rows(routing, start, tl=8192):
    rows, t = routing.shape
    tl = min(tl, t)
    grid_spec = pltpu.PrefetchScalarGridSpec(
        num_scalar_prefetch=1, grid=(t // tl,),
        in_specs=[pl.BlockSpec((rows, tl), lambda i, st: (0, i))],
        out_specs=pl.BlockSpec((rows, tl), lambda i, st: (0, i)))
    return pl.pallas_call(
        _dest_kernel, grid_spec=grid_spec, out_shape=jax.ShapeDtypeStruct((rows, t), jnp.int32),
        compiler_params=_cparams("parallel"), name="moe_slot_rows",
    )(start, routing)


def _sc_workers():
    info = plsc.get_sparse_core_info()
    return info.num_cores, info.num_cores * info.num_subcores


def _sc_move_rows(src, dest, scatter):
    t = dest.shape[1]
    width = src.shape[1]
    n_cores, n_workers = _sc_workers()
    steps = t // (n_workers * SC_WINDOW)
    assert steps * n_workers * SC_WINDOW == t and steps % 2 == 0, (t, n_workers)
    mesh = plsc.VectorSubcoreMesh(core_axis_name="core", subcore_axis_name="subcore")

    @functools.partial(
        pl.kernel, mesh=mesh, out_type=jax.ShapeDtypeStruct((TOP_K * t, width), src.dtype),
        scratch_types=[pltpu.VMEM((TOP_K, steps, SC_WINDOW), jnp.int32),
                       pltpu.VMEM((2, SC_WINDOW, width), src.dtype),
                       pltpu.SemaphoreType.DMA, pltpu.SemaphoreType.DMA])
    def move(src_hbm, idx_hbm, out_hbm, idx_v, rows_v, sem0, sem1):
        wid = lax.axis_index("subcore") * n_cores + lax.axis_index("core")
        for k in range(TOP_K):
            pltpu.sync_copy(idx_hbm.at[k, wid], idx_v.at[k])
        sems = (sem0, sem1)

        def window(j):
            return pl.ds((wid * steps + j) * SC_WINDOW, SC_WINDOW)

        def run(fetch, flush):
            fetch(0, 0).start()

            @pl.loop(0, steps, step=2)
            def _(j):
                fetch(j + 1, 1).start()
                fetch(j, 0).wait()
                flush(j, 0)

                @pl.when(j + 2 < steps)
                def _():
                    fetch(j + 2, 0).start()

                fetch(j + 1, 1).wait()
                flush(j + 1, 1)

        if scatter:
            def fetch(j, slot):
                return pltpu.make_async_copy(src_hbm.at[window(j)], rows_v.at[slot], sems[slot])

            def flush(j, slot):
                for k in range(TOP_K):
                    pltpu.sync_copy(rows_v.at[slot], out_hbm.at[idx_v.at[k, j]])

            run(fetch, flush)
        else:
            for k in range(TOP_K):
                def fetch(j, slot, k=k):
                    return pltpu.make_async_copy(src_hbm.at[idx_v.at[k, j]], rows_v.at[slot], sems[slot])

                def flush(j, slot, k=k):
                    pltpu.sync_copy(rows_v.at[slot], out_hbm.at[pl.ds(k * t + (wid * steps + j) * SC_WINDOW,
                                                                      SC_WINDOW)])

                run(fetch, flush)

    return move(src, dest.reshape(dest.shape[0], n_workers, steps, SC_WINDOW))


def _expert_outputs(x16, routing, cnt, w13, w2):
    t = x16.shape[0]
    counts = cnt[0, :N_EXPERTS]
    start = jnp.cumsum(counts) - counts
    dest = _slot_rows(routing, start)
    xs = _sc_move_rows(x16, dest, scatter=True)
    ys = _expert_blocks(xs, _expert_work_items(counts, t * TOP_K // MOE_ROWS), w13, w2)
    return _sc_move_rows(ys, dest, scatter=False)


def _trunk(x, prm):
    b, l, d = x.shape
    t = b * l
    x = x.reshape(t, d)
    moe, g, bias = None, prm['ln_in_g'], prm['ln_in_b']
    tc, ts = _fft_twiddles(l)
    for i in range(DEPTH):
        x, pa, pb, pc, pd = _input_projection(x, moe, g, bias, prm['w_in'][i])
        y_a = _neighbourhood_attention(pa.reshape(b, l, -1), prm['attn_bias'][i])
        y_b = _short_conv(pb.reshape(b, l, -1), prm['conv_w'][i])
        y_c = _fourier_mix(pc.reshape(b, l, -1), tc, ts)
        o_f, o_b = _hgrn2_bidirectional(pd.reshape(b, l, -1), prm['lb'][0, i], prm['lb'][1, i], i == 0)
        x, x16, routing, p_lanes, cnt = _merge(
            x, y_a.reshape(t, BR), y_b.reshape(t, BR), y_c.reshape(t, BR), o_f.reshape(t, BR), o_b.reshape(t, BR),
            pd, prm['hgrn_norm_g'][i], prm['w_gate'][i], prm['b_gate'][i], prm['w_br'][i], prm['w_out'][i],
            prm['ln1_g'][i], prm['ln1_b'][i], prm['router_g_w'][i], prm['router_g_b'][i], prm['router_e_w'][i],
            prm['router_e_b'][i])
        y01 = _expert_outputs(x16, routing, cnt, prm['w13'][i], prm['w2'][i])
        moe, g, bias = (y01, p_lanes), prm['ln2_g'][i], prm['ln2_b'][i]
    return _combine(x, *moe, g, bias).reshape(b, l, d)


def kernel(x_prompt, x_sample, ln_in_g, ln_in_b, w_in, na_rpb, conv_w, hgrn_lb, hgrn_norm_g, w_gate, b_gate, w_br,
           w_out, ln1_g, ln1_b, router_g_w, router_g_b, router_e_w, router_e_b, w13, w2, ln2_g, ln2_b):
    lb = jnp.cumsum(jax.nn.softmax(hgrn_lb.astype(F32), axis=1), axis=1)
    lb = lb - lb[:, :1]
    prm = dict(
        ln_in_g=ln_in_g, ln_in_b=ln_in_b,
        w_in=jnp.stack([_fold_channel_dft(w_in[i]) for i in range(DEPTH)]).astype(BF16),
        attn_bias=jnp.stack([_attn_bias_table(na_rpb[i]) for i in range(DEPTH)]),
        conv_w=conv_w, lb=lb, hgrn_norm_g=hgrn_norm_g,
        w_gate=w_gate.astype(BF16), b_gate=b_gate, w_br=w_br.astype(BF16), w_out=w_out.astype(BF16),
        ln1_g=ln1_g, ln1_b=ln1_b, router_g_w=router_g_w, router_g_b=router_g_b,
        router_e_w=router_e_w, router_e_b=router_e_b, w13=w13.astype(BF16), w2=w2.astype(BF16),
        ln2_g=ln2_g, ln2_b=ln2_b)
    return _trunk(x_prompt, prm), _trunk(x_sample, prm)
```

```python
import functools

import numpy as np
import jax
import jax.numpy as jnp
from jax import lax
from jax.experimental import pallas as pl
from jax.experimental.pallas import tpu as pltpu
from jax.experimental.pallas import tpu_sc as plsc

F32 = jnp.float32
BF16 = jnp.bfloat16
HIGHEST = lax.Precision.HIGHEST

D_MODEL = 1024
DEPTH = 4
GRID_W = 64
BR = 256
N_HEADS = 4
HEAD_DIM = 64
NA_ROWS = 8
NA_COLS = 16
FN_GROUP_DIM = 64
N_GROUPS = 4
EXPERTS_PER_GROUP = 8
N_EXPERTS = N_GROUPS * EXPERTS_PER_GROUP
TOP_K = 2
D_FF_E = 512
DEEPNORM_ALPHA = (2 * DEPTH) ** 0.25
LN_EPS = 1e-5
RMS_EPS = 1e-6
NEG_INF = -1e30

LANES = 128
VMEM_LIMIT_BYTES = 56 * 1024 * 1024
MOE_ROWS = 512
HG_CHUNK = 128
FFT_L1 = 64
SC_WINDOW = 64

COL_QA, COL_KA, COL_VA, COL_XB, COL_GB, COL_GC, COL_XF, COL_QD, COL_FF, COL_FB, COL_VD, COL_GD = range(12)
PROJ_GROUPS = ((0, 3), (3, 6), (6, 8), (8, 13))


def _cparams(*sem):
    return pltpu.CompilerParams(dimension_semantics=sem, vmem_limit_bytes=VMEM_LIMIT_BYTES)


def _sigmoid(x):
    return 1.0 / (1.0 + jnp.exp(-x))


def _split_bf16(x):
    hi = x.astype(BF16)
    return hi, (x - hi.astype(F32)).astype(BF16)


def _dot_split(a_hi, a_lo, b_hi, b_lo):
    dot = functools.partial(jnp.dot, preferred_element_type=F32)
    return dot(a_hi, b_hi) + (dot(a_lo, b_hi) + dot(a_hi, b_lo))


def _pack_bf16_pairs(x):
    m = x.shape[1] // 2
    hi = lax.bitcast_convert_type(x[:, :m].astype(BF16).astype(F32), jnp.uint32)
    lo = lax.bitcast_convert_type(x[:, m:].astype(BF16).astype(F32), jnp.uint32)
    return hi | (lo >> 16)


def _unpack_bf16_pairs(u):
    hi = lax.bitcast_convert_type(u & jnp.uint32(0xFFFF0000), F32)
    lo = lax.bitcast_convert_type(u << 16, F32)
    return jnp.concatenate([hi, lo], axis=1)


def _ln_rows(x, g, b):
    mu = jnp.mean(x, axis=-1, keepdims=True)
    xc = x - mu
    var = jnp.mean(xc * xc, axis=-1, keepdims=True)
    return xc * lax.rsqrt(var + LN_EPS) * g + b


def _head_of_lane(shape):
    return lax.broadcasted_iota(jnp.int32, shape, len(shape) - 1) // HEAD_DIM


def _head_stack(x):
    head = _head_of_lane(x.shape)
    return jnp.concatenate([jnp.where(head == h, x, jnp.zeros_like(x)) for h in range(N_HEADS)], axis=0)


def _same_head_matrix():
    h = np.arange(BR) // HEAD_DIM
    return (h[:, None] == h[None, :]).astype(np.float32)


def _moe_residual(x_ref, y0_ref, y1_ref, p_ref):
    p = p_ref[...]
    ffn = p[:, 0:1] * _unpack_bf16_pairs(y0_ref[...]) + p[:, 1:2] * _unpack_bf16_pairs(y1_ref[...])
    return DEEPNORM_ALPHA * x_ref[...] + ffn


def _proj_kernel(*refs, after_moe):
    if after_moe:
        x_ref, y0_ref, y1_ref, p_ref, g_ref, b_ref, w_ref, xo_ref, *out_refs = refs
        pre = _moe_residual(x_ref, y0_ref, y1_ref, p_ref)
    else:
        x_ref, g_ref, b_ref, w_ref, xo_ref, *out_refs = refs
        pre = x_ref[...]
    x = _ln_rows(pre, g_ref[...], b_ref[...])
    xo_ref[...] = x
    xb = x.astype(BF16)
    for ref, (lo, hi) in zip(out_refs, PROJ_GROUPS):
        ref[...] = jnp.dot(xb, w_ref[:, lo * BR:hi * BR], preferred_element_type=F32).astype(ref.dtype)


def _fold_channel_dft(w_in):
    cg, sg = _dft_cos_sin(FN_GROUP_DIM)
    eye = np.eye(BR // FN_GROUP_DIM, dtype=np.float32)
    w_xf = w_in[:, COL_XF * BR:(COL_XF + 1) * BR]
    w_re = jnp.dot(w_xf, np.kron(eye, cg), precision=HIGHEST)
    w_im = -jnp.dot(w_xf, np.kron(eye, sg), precision=HIGHEST)
    return jnp.concatenate([w_in[:, :COL_XF * BR], w_re, w_im, w_in[:, (COL_XF + 1) * BR:]], axis=1)


def _pair_specs(t, d, tm):
    return [pl.BlockSpec((tm, d // 2), lambda i: (i, 0)), pl.BlockSpec((tm, d // 2), lambda i: (i + t // tm, 0))]


def _input_projection(x, moe, g, b, w_in, tm=512):
    t, d = x.shape
    n = w_in.shape[1]
    widths = (d,) + tuple((hi - lo) * BR for lo, hi in PROJ_GROUPS)
    dtypes = (F32, BF16, BF16, F32, BF16)
    row = pl.BlockSpec((tm, d), lambda i: (i, 0))
    vec = pl.BlockSpec((1, d), lambda i: (0, 0))
    acts, act_specs = [x], [row]
    if moe is not None:
        y01, p = moe
        acts += [y01, y01, p]
        act_specs += _pair_specs(t, d, tm) + [pl.BlockSpec((tm, LANES), lambda i: (i, 0))]
    return pl.pallas_call(
        functools.partial(_proj_kernel, after_moe=moe is not None), grid=(t // tm,),
        in_specs=act_specs + [vec, vec, pl.BlockSpec((d, n), lambda i: (0, 0))],
        out_specs=[pl.BlockSpec((tm, w), lambda i: (i, 0)) for w in widths],
        out_shape=[jax.ShapeDtypeStruct((t, w), dt) for w, dt in zip(widths, dtypes)],
        compiler_params=_cparams("parallel"), name="ln_in_proj",
    )(*acts, g.reshape(1, d), b.reshape(1, d), w_in)


def _attn_bias_table(rpb):
    c = np.arange(GRID_W)
    win_c0 = np.clip(c - NA_COLS // 2, 0, GRID_W - NA_COLS)
    valid = (c[None, :] >= win_c0[:, None]) & (c[None, :] < win_c0[:, None] + NA_COLS)
    dc = np.clip(c[None, :] - c[:, None], -(NA_COLS - 1), NA_COLS - 1) + NA_COLS - 1
    dr = np.arange(NA_ROWS)[None, :] - np.arange(NA_ROWS)[:, None] + NA_ROWS - 1
    row_hot = (dr[:, :, None] == np.arange(2 * NA_ROWS - 1)).astype(np.float32)
    col_hot = (dc[:, :, None] == np.arange(2 * NA_COLS - 1)).astype(np.float32)
    t = jnp.einsum('oia,hab,qkb->ohqik', row_hot, rpb.astype(F32), col_hot, precision=HIGHEST)
    t = jnp.where(valid[None, None, :, None, :], t, NEG_INF)
    return t.reshape(NA_ROWS, N_HEADS * GRID_W, NA_ROWS * GRID_W)


def _attn_kernel(q_ref, k_ref, v_ref, bias_ref, o_ref, *, n_rows, rows_per_tile):
    i = pl.program_id(1)
    head = _head_of_lane((GRID_W, BR))
    scale = HEAD_DIM ** -0.5
    span = NA_ROWS * GRID_W
    for j in range(rows_per_tile):
        r = i * rows_per_tile + j
        r0 = jnp.clip(r - NA_ROWS // 2, 0, n_rows - NA_ROWS)
        q = q_ref[0, j * GRID_W:(j + 1) * GRID_W, :]
        qs = _head_stack(q)
        start = pl.multiple_of(r0 * GRID_W, GRID_W)
        ks = k_ref[0, pl.ds(start, span), :]
        vs = v_ref[0, pl.ds(start, span), :]
        s = lax.dot_general(qs, ks, (((1,), (1,)), ((), ())), preferred_element_type=F32)
        s = s * scale + bias_ref[r - r0]
        m = jnp.max(s, axis=-1, keepdims=True)
        e = jnp.exp(s - m)
        p = e / jnp.sum(e, axis=-1, keepdims=True)
        o4 = jnp.dot(p.astype(BF16), vs, preferred_element_type=F32)
        o = jnp.zeros((GRID_W, BR), F32)
        for h in range(N_HEADS):
            o = o + jnp.where(head == h, o4[h * GRID_W:(h + 1) * GRID_W, :], 0.0)
        o_ref[0, j * GRID_W:(j + 1) * GRID_W, :] = o.astype(o_ref.dtype)


def _neighbourhood_attention(pa, bias, rows_per_tile=8):
    b, l, _ = pa.shape
    n_rows = l // GRID_W
    tq = rows_per_tile * GRID_W
    kern = functools.partial(_attn_kernel, n_rows=n_rows, rows_per_tile=rows_per_tile)
    return pl.pallas_call(
        kern, grid=(b, l // tq),
        in_specs=[pl.BlockSpec((1, tq, BR), lambda bi, i: (bi, i, 0)),
                  pl.BlockSpec((1, l, BR), lambda bi, i: (bi, 0, 1)),
                  pl.BlockSpec((1, l, BR), lambda bi, i: (bi, 0, 2)),
                  pl.BlockSpec(bias.shape, lambda bi, i: (0, 0, 0))],
        out_specs=pl.BlockSpec((1, tq, BR), lambda bi, i: (bi, i, 0)),
        out_shape=jax.ShapeDtypeStruct((b, l, BR), BF16),
        compiler_params=_cparams("parallel", "arbitrary"), name="nbr_attn",
    )(pa, pa, pa, bias)


def _conv_kernel(xb_ref, gb_ref, gc_ref, xbp_ref, gcp_ref, xbn_ref, gcn_ref, w_ref, o_ref, *, n_tiles):
    i = pl.program_id(1)
    tl = xb_ref.shape[1]
    last = xbp_ref.shape[1] - 1

    def gated(gc, xb):
        return gc.astype(F32) * xb.astype(F32)

    u = gated(gc_ref[0], xb_ref[0])
    u_before = jnp.where(i > 0, gated(gcp_ref[0, last:, :], xbp_ref[0, last:, :]), 0.0)
    u_after = jnp.where(i < n_tiles - 1, gated(gcn_ref[0, 0:1, :], xbn_ref[0, 0:1, :]), 0.0)
    row = lax.broadcasted_iota(jnp.int32, (tl, 1), 0)
    u_prev = jnp.where(row == 0, u_before, pltpu.roll(u, 1, 0))
    u_next = jnp.where(row == tl - 1, u_after, pltpu.roll(u, tl - 1, 0))
    conv = w_ref[0:1, :] * u_prev + w_ref[1:2, :] * u + w_ref[2:3, :] * u_next
    o_ref[0] = (gb_ref[0].astype(F32) * conv).astype(o_ref.dtype)


def _short_conv(pb, conv_w, tl=1024):
    b, l, _ = pb.shape
    n_tiles = l // tl
    halo = 16
    per_tile = tl // halo
    last_halo = l // halo - 1

    def main(col):
        return pl.BlockSpec((1, tl, BR), lambda bi, i: (bi, i, col))

    def before(col):
        return pl.BlockSpec((1, halo, BR), lambda bi, i: (bi, jnp.maximum(i * per_tile - 1, 0), col))

    def after(col):
        return pl.BlockSpec((1, halo, BR), lambda bi, i: (bi, jnp.minimum((i + 1) * per_tile, last_halo), col))

    return pl.pallas_call(
        functools.partial(_conv_kernel, n_tiles=n_tiles), grid=(b, n_tiles),
        in_specs=[main(0), main(1), main(2), before(0), before(2), after(0), after(2),
                  pl.BlockSpec(conv_w.shape, lambda bi, i: (0, 0))],
        out_specs=pl.BlockSpec((1, tl, BR), lambda bi, i: (bi, i, 0)),
        out_shape=jax.ShapeDtypeStruct((b, l, BR), BF16),
        compiler_params=_cparams("parallel", "parallel"), name="short_conv",
    )(pb, pb, pb, pb, pb, pb, pb, conv_w)


def _dft_cos_sin(n):
    ang = 2.0 * np.pi * ((np.arange(n)[:, None] * np.arange(n)[None, :]) % n) / n
    return np.cos(ang).astype(np.float32), np.sin(ang).astype(np.float32)


def _fft_major_kernel(g_ref, m_ref, tc_ref, ts_ref, br_ref, bi_ref):
    for s in range(8):
        g = g_ref[0, :, s, :]
        swapped = jnp.concatenate([g[:, BR:], -g[:, :BR]], axis=1)
        a = jnp.dot(m_ref[...], jnp.concatenate([g, swapped], axis=0), precision=HIGHEST,
                    preferred_element_type=F32)
        ar, ai = a[:, :BR], a[:, BR:]
        tc = tc_ref[:, s, :]
        ts = ts_ref[:, s, :]
        br_ref[0, :, s, :] = ar * tc + ai * ts
        bi_ref[0, :, s, :] = ai * tc - ar * ts


def _fft_minor_kernel(br_ref, bi_ref, m_ref, o_ref, *, scale):
    l2 = br_ref.shape[2]
    group = m_ref.shape[1] // l2
    for j0 in range(0, 8, group):
        rhs = jnp.concatenate([br_ref[0, j0:j0 + group].reshape(group * l2, BR),
                               bi_ref[0, j0:j0 + group].reshape(group * l2, BR)], axis=0)
        y = _dot_split(m_ref[0], m_ref[1], *_split_bf16(rhs)) * scale
        for j in range(group):
            o_ref[0, :, j0 + j, :] = y[j * l2:(j + 1) * l2].astype(o_ref.dtype)


def _fft_twiddles(l):
    l1 = FFT_L1
    l2 = l // l1
    idx = (jnp.arange(l1, dtype=jnp.int32)[:, None] * jnp.arange(l2, dtype=jnp.int32)[None, :]) % l
    ang = idx.astype(F32) * (2.0 * np.pi / l)
    tc = jnp.broadcast_to(jnp.cos(ang)[:, :, None], (l1, l2, BR))
    ts = jnp.broadcast_to(jnp.sin(ang)[:, :, None], (l1, l2, BR))
    return tc, ts


def _fourier_mix(pc, tc, ts):
    b, l, _ = pc.shape
    l1 = FFT_L1
    l2 = l // l1

    c1, s1 = _dft_cos_sin(l1)
    c2, s2 = _dft_cos_sin(l2)
    group = min(max(1, 128 // l2), 8)
    eye = np.eye(group, dtype=np.float32)
    minor = jnp.asarray(np.concatenate([np.kron(eye, c2), np.kron(eye, s2)], axis=1))
    minor_hi = minor.astype(BF16)
    minor_parts = jnp.stack([minor_hi, (minor - minor_hi.astype(F32)).astype(BF16)])

    def strip(width):
        return pl.BlockSpec((1, l1, 8, width), lambda bi, i: (bi, 0, i, 0))

    tspec = pl.BlockSpec((l1, 8, BR), lambda bi, i: (0, i, 0))
    mspec = pl.BlockSpec((l1, 2 * l1), lambda bi, i: (0, 0))
    br, bi_ = pl.pallas_call(
        _fft_major_kernel, grid=(b, l2 // 8),
        in_specs=[strip(2 * BR), mspec, tspec, tspec], out_specs=[strip(BR), strip(BR)],
        out_shape=[jax.ShapeDtypeStruct((b, l1, l2, BR), F32)] * 2,
        compiler_params=_cparams("parallel", "parallel"), name="fft_major",
    )(pc.reshape(b, l1, l2, 2 * BR), np.concatenate([c1, s1], axis=1), tc, ts)

    bspec = pl.BlockSpec((1, 8, l2, BR), lambda bi, i: (bi, i, 0, 0))
    m2spec = pl.BlockSpec(minor_parts.shape, lambda bi, i: (0, 0, 0))
    scale = 1.0 / float(np.sqrt(l * FN_GROUP_DIM))
    y = pl.pallas_call(
        functools.partial(_fft_minor_kernel, scale=scale), grid=(b, l1 // 8),
        in_specs=[bspec, bspec, m2spec],
        out_specs=pl.BlockSpec((1, l2, 8, BR), lambda bi, i: (bi, 0, i, 0)),
        out_shape=jax.ShapeDtypeStruct((b, l2, l1, BR), F32),
        compiler_params=_cparams("parallel", "parallel"), name="fft_minor",
    )(br, bi_, minor_parts)
    return y.reshape(b, l, BR)


def _hgrn_level_masks():
    t = np.arange(HG_CHUNK)
    out = []
    half = 8
    while 2 * half < HG_CHUNK:
        same = (t[:, None] // (2 * half)) == (t[None, :] // (2 * half))
        out.append(np.tile(same, (1, N_HEADS)).astype(np.float32))
        half *= 2
    return np.stack(out)


def _hgrn_kernel(qd_ref, fl_ref, vd_ref, lb_ref, mask_ref, bd_ref, o_ref, st_ref, *, first_layer, chunks_per_tile,
                 rev):
    c = HG_CHUNK

    @pl.when(pl.program_id(1) == 0)
    def _():
        st_ref[...] = jnp.zeros_like(st_ref)

    bd = bd_ref[...]
    bd16 = bd.astype(BF16)
    lb = lb_ref[...]
    row = lax.broadcasted_iota(jnp.int32, (c, 1), 0)
    sub = row % 8
    nt = (((1,), (1,)), ((), ()))

    def roll8(x, j):
        return pltpu.roll(x.reshape(c // 8, 8, BR), j % 8, 1).reshape(c, BR)

    def earlier(x, j):
        return roll8(x, -j if rev else j)

    def later(x, j):
        return roll8(x, j if rev else -j)

    def has_earlier(j):
        return (sub + j <= 7) if rev else (sub >= j)

    def has_later(j):
        return (sub >= j) if rev else (sub + j <= 7)

    for ci in (reversed(range(chunks_per_tile)) if rev else range(chunks_per_tile)):
        sl = slice(ci * c, (ci + 1) * c)
        qd = qd_ref[0, sl, :].astype(F32)
        fl = fl_ref[0, sl, :].astype(F32)
        v = vd_ref[0, sl, :].astype(F32)
        q = qd * _sigmoid(qd)
        if first_layer:
            f = _sigmoid(fl)
            k = _sigmoid(-fl)
        else:
            f = lb + (1.0 - lb) * _sigmoid(fl)
            k = (1.0 - lb) * _sigmoid(-fl)

        o = jnp.dot((q * k).astype(BF16), bd16, preferred_element_type=F32) * v
        g = f
        for d in range(1, 8):
            if d > 1:
                g = g * earlier(f, d - 1)
            p = jnp.where(has_earlier(d), q * g * earlier(k, d), 0.0)
            o = o + jnp.dot(p.astype(BF16), bd16, preferred_element_type=F32) * earlier(v, d)

        qf = f
        for j in (1, 2, 4):
            qf = qf * jnp.where(has_earlier(j), earlier(qf, j), 1.0)
        kf = jnp.where(has_later(1), later(f, 1), 1.0)
        for j in (1, 2, 4):
            kf = kf * jnp.where(has_later(j), later(kf, j), 1.0)

        scores = None
        half = 8
        level = 0
        while half < c:
            late = ((row // half) % 2) == (0 if rev else 1)
            qb = jnp.where(late, q * qf, 0.0).astype(BF16)
            kb = jnp.where(late, 0.0, k * kf).astype(BF16)
            x = lax.dot_general(qb, _head_stack(kb), nt, preferred_element_type=F32)
            if 2 * half < c:
                x = x * mask_ref[level]
            scores = x if scores is None else scores + x
            blocks = c // (2 * half)
            qf3 = qf.reshape(blocks, 2 * half, BR)
            end_early = half if rev else half - 1
            end_late = 0 if rev else 2 * half - 1
            tot_early = jnp.broadcast_to(qf3[:, end_early:end_early + 1, :], qf3.shape).reshape(c, BR)
            tot_late = jnp.broadcast_to(qf3[:, end_late:end_late + 1, :], qf3.shape).reshape(c, BR)
            qf = qf * jnp.where(late, tot_early, 1.0)
            kf = kf * jnp.where(late, 1.0, tot_late)
            half *= 2
            level += 1

        o = o + jnp.dot(scores.astype(BF16), _head_stack(v.astype(BF16)), preferred_element_type=F32)

        st = st_ref[...]
        o = o + lax.dot_general((q * qf).astype(BF16), st.astype(BF16), nt, preferred_element_type=F32)
        o_ref[0, sl, :] = o
        upd = jnp.dot(v.T.astype(BF16), (k * kf).astype(BF16), preferred_element_type=F32)
        last = 0 if rev else c - 1
        st_ref[...] = st * qf[last:last + 1, :] + upd * bd


def _hgrn_scan(qd, fl, vd, cols, lb, first_layer, rev, chunks_per_tile=8):
    b, l, _ = qd.shape
    tile = HG_CHUNK * chunks_per_tile
    n_tiles = l // tile
    masks = _hgrn_level_masks()
    bd = _same_head_matrix()

    def tile_index(i):
        return n_tiles - 1 - i if rev else i

    def seq(col):
        return pl.BlockSpec((1, tile, BR), lambda bi, i: (bi, tile_index(i), col))

    return pl.pallas_call(
        functools.partial(_hgrn_kernel, first_layer=first_layer, chunks_per_tile=chunks_per_tile, rev=rev),
        grid=(b, n_tiles),
        in_specs=[seq(cols[0]), seq(cols[1]), seq(cols[2]),
                  pl.BlockSpec((1, BR), lambda bi, i: (0, 0)),
                  pl.BlockSpec(masks.shape, lambda bi, i: (0, 0, 0)),
                  pl.BlockSpec((BR, BR), lambda bi, i: (0, 0))],
        out_specs=pl.BlockSpec((1, tile, BR), lambda bi, i: (bi, tile_index(i), 0)),
        out_shape=jax.ShapeDtypeStruct((b, l, BR), F32),
        scratch_shapes=[pltpu.VMEM((BR, BR), F32)],
        compiler_params=_cparams("parallel", "arbitrary"),
        name="hgrn_bwd" if rev else "hgrn_fwd",
    )(qd, fl, vd, lb.reshape(1, BR), masks, bd)


def _hgrn2_bidirectional(pd, lb_f, lb_b, first_layer):
    o_f = _hgrn_scan(pd, pd, pd, (0, 1, 3), lb_f, first_layer, rev=False)
    o_b = _hgrn_scan(pd, pd, pd, (0, 2, 3), lb_b, first_layer, rev=True)
    return o_f, o_b


def _route(x, w_hi, w_lo, bias, tri, run):
    x_hi, x_lo = _split_bf16(x)
    logits = _dot_split(x_hi, x_lo, w_hi, w_lo) + bias
    lane = lax.broadcasted_iota(jnp.int32, logits.shape, 1)
    big = jnp.int32(LANES)
    is_group = (lane >= N_EXPERTS) & (lane < N_EXPERTS + N_GROUPS)
    lg = jnp.where(is_group, logits, -jnp.inf)
    mg = jnp.max(lg, axis=-1, keepdims=True)
    g_sel = jnp.min(jnp.where(lg == mg, lane, big), axis=-1, keepdims=True) - N_EXPERTS
    p_g = 1.0 / jnp.sum(jnp.exp(lg - mg), axis=-1, keepdims=True)
    in_group = (lane < N_EXPERTS) & ((lane // EXPERTS_PER_GROUP) == g_sel)
    le = jnp.where(in_group, logits, -jnp.inf)
    v1 = jnp.max(le, axis=-1, keepdims=True)
    i1 = jnp.min(jnp.where(le == v1, lane, big), axis=-1, keepdims=True)
    le2 = jnp.where(lane == i1, -jnp.inf, le)
    v2 = jnp.max(le2, axis=-1, keepdims=True)
    i2 = jnp.min(jnp.where(le2 == v2, lane, big), axis=-1, keepdims=True)
    e2 = jnp.exp(v2 - v1)
    w1 = p_g / (1.0 + e2)
    w2 = p_g * e2 / (1.0 + e2)
    hot1 = lane == i1
    hot2 = lane == i2
    hot = (hot1 | hot2).astype(BF16)
    before = jnp.dot(tri, hot, preferred_element_type=F32) + run
    rank1 = jnp.sum(jnp.where(hot1, before, 0.0), axis=-1, keepdims=True).astype(jnp.int32)
    rank2 = jnp.sum(jnp.where(hot2, before, 0.0), axis=-1, keepdims=True).astype(jnp.int32)
    run = run + jnp.sum(hot.astype(F32), axis=0, keepdims=True)
    e = jnp.where(lane == 0, i1, jnp.where(lane == 1, i2, jnp.where(lane == 2, rank1,
                                                                     jnp.where(lane == 3, rank2, 0))))
    p = jnp.where(lane == 0, w1, jnp.where(lane == 1, w2, 0.0))
    return e, p, run


def _merge_kernel(x_ref, ya_ref, yb_ref, yc_ref, of_ref, ob_ref, gd_ref, ng_ref, bd_ref, wg_ref, bg_ref, wbr_ref,
                  wo_ref, g_ref, b_ref, rhi_ref, rlo_ref, rb_ref, tri_ref, o_ref, o16_ref, e_ref, p_ref, cnt_ref,
                  run_ref):
    @pl.when(pl.program_id(0) == 0)
    def _():
        run_ref[...] = jnp.zeros_like(run_ref)

    od = of_ref[...] + ob_ref[...]
    ms = jnp.dot(od * od, bd_ref[...], precision=HIGHEST, preferred_element_type=F32) * (1.0 / HEAD_DIM)
    gd = gd_ref[...].astype(F32)
    y_d = od * lax.rsqrt(ms + RMS_EPS) * ng_ref[...] * (gd * _sigmoid(gd))

    x = x_ref[...]
    xb = x.astype(BF16)
    merged = None
    for i, y in enumerate((ya_ref[...], yb_ref[...], yc_ref[...], y_d)):
        gate = _sigmoid(jnp.dot(xb, wg_ref[i], preferred_element_type=F32) + bg_ref[i])
        term = gate * jnp.dot(y.astype(BF16), wbr_ref[i], preferred_element_type=F32)
        merged = term if merged is None else merged + term
    mix = jnp.dot(merged.astype(BF16), wo_ref[...], preferred_element_type=F32)
    out = _ln_rows(DEEPNORM_ALPHA * x + mix, g_ref[...], b_ref[...])
    o_ref[...] = out
    o16_ref[...] = _pack_bf16_pairs(out)
    e, p, run = _route(out, rhi_ref[...], rlo_ref[...], rb_ref[...], tri_ref[...], run_ref[...])
    e_ref[...] = e.T[:e_ref.shape[0], :]
    p_ref[...] = p
    run_ref[...] = run
    cnt_ref[...] = run.astype(jnp.int32)


def _merge(x, y_a, y_b, y_c, o_f, o_b, pd, norm_g, w_gate, b_gate, w_br, w_out, g, b, wg, bg, we, be, tm=512):
    t, d = x.shape
    row = pl.BlockSpec((tm, d), lambda i: (i, 0))
    yrow = pl.BlockSpec((tm, BR), lambda i: (i, 0))
    vec = pl.BlockSpec((1, d), lambda i: (0, 0))
    lanes = pl.BlockSpec((tm, LANES), lambda i: (i, 0))
    lvec = pl.BlockSpec((1, LANES), lambda i: (0, 0))

    def whole(a):
        return pl.BlockSpec(a.shape, lambda i: (0,) * a.ndim)

    bgate = b_gate.reshape(b_gate.shape[0], 1, d)
    rw = jnp.zeros((d, LANES), F32).at[:, :N_EXPERTS].set(we).at[:, N_EXPERTS:N_EXPERTS + N_GROUPS].set(wg)
    rb = jnp.zeros((1, LANES), F32).at[0, :N_EXPERTS].set(be).at[0, N_EXPERTS:N_EXPERTS + N_GROUPS].set(bg)
    r_hi, r_lo = _split_bf16(rw)
    tri = np.tril(np.ones((tm, tm), np.float32), -1).astype(BF16)
    ng = jnp.tile(norm_g, N_HEADS).reshape(1, BR)
    bd = _same_head_matrix()
    return pl.pallas_call(
        _merge_kernel, grid=(t // tm,),
        in_specs=[row, yrow, yrow, yrow, yrow, yrow, pl.BlockSpec((tm, BR), lambda i: (i, 4)), whole(ng), whole(bd),
                  whole(w_gate), whole(bgate), whole(w_br), whole(w_out), vec, vec,
                  whole(r_hi), whole(r_lo), lvec, whole(tri)],
        out_specs=[row, pl.BlockSpec((tm, d // 2), lambda i: (i, 0)), pl.BlockSpec((8, tm), lambda i: (0, i)),
                   lanes, lvec],
        out_shape=[jax.ShapeDtypeStruct((t, d), F32), jax.ShapeDtypeStruct((t, d // 2), jnp.uint32),
                   jax.ShapeDtypeStruct((8, t), jnp.int32), jax.ShapeDtypeStruct((t, LANES), F32),
                   jax.ShapeDtypeStruct((1, LANES), jnp.int32)],
        scratch_shapes=[pltpu.VMEM((1, LANES), F32)],
        compiler_params=_cparams("arbitrary"), name="merge_ln1_route",
    )(x, y_a, y_b, y_c, o_f, o_b, pd, ng, bd, w_gate, bgate, w_br, w_out, g.reshape(1, d), b.reshape(1, d),
      r_hi, r_lo, rb, tri)


def _work_items_kernel(cnt_ref, start_ref, items_ref):
    n_items = items_ref.shape[1]

    def put(w, e, blk, lo, hi, first):
        for row, val in enumerate((e, blk, lo, hi, first)):
            items_ref[row, w] = val

    def expert(e, carry):
        start, w, _, _ = carry
        start_ref[e] = start
        end = start + cnt_ref[e]
        first_blk = start // MOE_ROWS
        n_blk = jnp.where(end > start, (end - 1) // MOE_ROWS - first_blk + 1, 0)

        def block(i, w):
            blk = first_blk + i
            row0 = blk * MOE_ROWS
            lo = jnp.maximum(start, row0) - row0
            first = jnp.where((i > 0) | (lo == 0), 1, 0)
            put(w, e, blk, lo, jnp.minimum(end, row0 + MOE_ROWS) - row0, first)
            return w + 1

        w_end = lax.fori_loop(0, n_blk, block, w)
        owner = jnp.where(n_blk > 0, e, carry[2])
        last_blk = jnp.where(n_blk > 0, first_blk + n_blk - 1, carry[3])
        return end, w_end, owner, last_blk

    zero = jnp.int32(0)
    _, w_live, owner, last_blk = lax.fori_loop(0, N_EXPERTS, expert, (zero, zero, zero, zero))

    def surplus(w, _):
        put(w, owner, last_blk, zero, zero, zero)
        return 0

    lax.fori_loop(w_live, n_items, surplus, 0)


def _expert_work_items(counts, n_blocks):
    smem = pl.BlockSpec(memory_space=pltpu.SMEM)
    return pl.pallas_call(
        _work_items_kernel, in_specs=[smem], out_specs=[smem, smem],
        out_shape=[jax.ShapeDtypeStruct((N_EXPERTS,), jnp.int32),
                   jax.ShapeDtypeStruct((5, n_blocks + N_EXPERTS), jnp.int32)],
        name="moe_work_items",
    )(counts)


def _expert_kernel(items_ref, x_ref, w13_ref, w2_ref, o_ref):
    w = pl.program_id(0)
    lo = items_ref[2, w]
    hi = items_ref[3, w]

    @pl.when(hi > lo)
    def _():
        x = _unpack_bf16_pairs(x_ref[...]).astype(BF16)
        h = jnp.dot(x, w13_ref[0], preferred_element_type=F32)
        hg = h[:, :D_FF_E]
        act = hg * _sigmoid(hg) * h[:, D_FF_E:]
        y = _pack_bf16_pairs(jnp.dot(act.astype(BF16), w2_ref[0], preferred_element_type=F32))
        row = lax.broadcasted_iota(jnp.int32, (MOE_ROWS, 1), 0)
        mine = (row >= lo) & (row < hi)

        @pl.when(items_ref[4, w] == 1)
        def _():
            o_ref[...] = jnp.where(mine, y, jnp.zeros_like(y))

        @pl.when(items_ref[4, w] == 0)
        def _():
            o_ref[...] = jnp.where(mine, y, o_ref[...])


def _expert_blocks(xs, items, w13, w2):
    n_rows, half = xs.shape
    d = 2 * half
    grid_spec = pltpu.PrefetchScalarGridSpec(
        num_scalar_prefetch=1, grid=(items.shape[1],),
        in_specs=[pl.BlockSpec((MOE_ROWS, half), lambda w, it: (it[1, w], 0)),
                  pl.BlockSpec((1, d, 2 * D_FF_E), lambda w, it: (it[0, w], 0, 0)),
                  pl.BlockSpec((1, D_FF_E, d), lambda w, it: (it[0, w], 0, 0))],
        out_specs=pl.BlockSpec((MOE_ROWS, half), lambda w, it: (it[1, w], 0)))
    return pl.pallas_call(
        _expert_kernel, grid_spec=grid_spec, out_shape=jax.ShapeDtypeStruct((n_rows, half), jnp.uint32),
        compiler_params=_cparams("arbitrary"), name="moe_experts",
    )(items, xs, w13, w2)


def _combine_kernel(x_ref, y0_ref, y1_ref, p_ref, g_ref, b_ref, o_ref):
    o_ref[...] = _ln_rows(_moe_residual(x_ref, y0_ref, y1_ref, p_ref), g_ref[...], b_ref[...])


def _combine(x, y01, p, g, b, tm=1024):
    t, d = x.shape
    row = pl.BlockSpec((tm, d), lambda i: (i, 0))
    vec = pl.BlockSpec((1, d), lambda i: (0, 0))
    return pl.pallas_call(
        _combine_kernel, grid=(t // tm,),
        in_specs=[row] + _pair_specs(t, d, tm) + [pl.BlockSpec((tm, LANES), lambda i: (i, 0)), vec, vec],
        out_specs=row, out_shape=jax.ShapeDtypeStruct((t, d), F32),
        compiler_params=_cparams("parallel"), name="moe_combine_ln2",
    )(x, y01, y01, p, g.reshape(1, d), b.reshape(1, d))


def _dest_kernel(start_ref, r_ref, o_ref):
    r = r_ref[...]
    e = r[0:TOP_K, :]
    base = jnp.zeros_like(e)
    for x in range(N_EXPERTS):
        base = jnp.where(e == x, start_ref[x], base)
    pad = jnp.zeros((r.shape[0] - TOP_K, r.shape[1]), jnp.int32)
    o_ref[...] = jnp.concatenate([base + r[TOP_K:2 * TOP_K, :], pad], axis=0)


def _slot_rows(routing, start, tl=8192):
    rows, t = routing.shape
    tl = min(tl, t)
    grid_spec = pltpu.PrefetchScalarGridSpec(
        num_scalar_prefetch=1, grid=(t // tl,),
        in_specs=[pl.BlockSpec((rows, tl), lambda i, st: (0, i))],
        out_specs=pl.BlockSpec((rows, tl), lambda i, st: (0, i)))
    return pl.pallas_call(
        _dest_kernel, grid_spec=grid_spec, out_shape=jax.ShapeDtypeStruct((rows, t), jnp.int32),
        compiler_params=_cparams("parallel"), name="moe_slot_rows",
    )(start, routing)


def _sc_workers():
    info = plsc.get_sparse_core_info()
    return info.num_cores, info.num_cores * info.num_subcores


def _sc_move_rows(src, dest, scatter):
    t = dest.shape[1]
    width = src.shape[1]
    n_cores, n_workers = _sc_workers()
    steps = t // (n_workers * SC_WINDOW)
    assert steps * n_workers * SC_WINDOW == t and steps % 2 == 0, (t, n_workers)
    mesh = plsc.VectorSubcoreMesh(core_axis_name="core", subcore_axis_name="subcore")

    @functools.partial(
        pl.kernel, mesh=mesh, out_type=jax.ShapeDtypeStruct((TOP_K * t, width), src.dtype),
        scratch_types=[pltpu.VMEM((TOP_K, steps, SC_WINDOW), jnp.int32),
                       pltpu.VMEM((2, SC_WINDOW, width), src.dtype),
                       pltpu.SemaphoreType.DMA, pltpu.SemaphoreType.DMA])
    def move(src_hbm, idx_hbm, out_hbm, idx_v, rows_v, sem0, sem1):
        wid = lax.axis_index("subcore") * n_cores + lax.axis_index("core")
        for k in range(TOP_K):
            pltpu.sync_copy(idx_hbm.at[k, wid], idx_v.at[k])
        sems = (sem0, sem1)

        def window(j):
            return pl.ds((wid * steps + j) * SC_WINDOW, SC_WINDOW)

        def run(fetch, flush):
            fetch(0, 0).start()

            @pl.loop(0, steps, step=2)
            def _(j):
                fetch(j + 1, 1).start()
                fetch(j, 0).wait()
                flush(j, 0)

                @pl.when(j + 2 < steps)
                def _():
                    fetch(j + 2, 0).start()

                fetch(j + 1, 1).wait()
                flush(j + 1, 1)

        if scatter:
            def fetch(j, slot):
                return pltpu.make_async_copy(src_hbm.at[window(j)], rows_v.at[slot], sems[slot])

            def flush(j, slot):
                for k in range(TOP_K):
                    pltpu.sync_copy(rows_v.at[slot], out_hbm.at[idx_v.at[k, j]])

            run(fetch, flush)
        else:
            for k in range(TOP_K):
                def fetch(j, slot, k=k):
                    return pltpu.make_async_copy(src_hbm.at[idx_v.at[k, j]], rows_v.at[slot], sems[slot])

                def flush(j, slot, k=k):
                    pltpu.sync_copy(rows_v.at[slot], out_hbm.at[pl.ds(k * t + (wid * steps + j) * SC_WINDOW,
                                                                      SC_WINDOW)])

                run(fetch, flush)

    return move(src, dest.reshape(dest.shape[0], n_workers, steps, SC_WINDOW))


def _expert_outputs(x16, routing, cnt, w13, w2):
    t = x16.shape[0]
    start, items = _expert_work_items(cnt[0, :N_EXPERTS], t * TOP_K // MOE_ROWS)
    dest = _slot_rows(routing, start)
    xs = _sc_move_rows(x16, dest, scatter=True)
    ys = _expert_blocks(xs, items, w13, w2)
    return _sc_move_rows(ys, dest, scatter=False)


def _trunk(x, prm):
    b, l, d = x.shape
    t = b * l
    x = x.reshape(t, d)
    moe, g, bias = None, prm['ln_in_g'], prm['ln_in_b']
    tc, ts = _fft_twiddles(l)
    for i in range(DEPTH):
        x, pa, pb, pc, pd = _input_projection(x, moe, g, bias, prm['w_in'][i])
        y_a = _neighbourhood_attention(pa.reshape(b, l, -1), prm['attn_bias'][i])
        y_b = _short_conv(pb.reshape(b, l, -1), prm['conv_w'][i])
        y_c = _fourier_mix(pc.reshape(b, l, -1), tc, ts)
        o_f, o_b = _hgrn2_bidirectional(pd.reshape(b, l, -1), prm['lb'][0, i], prm['lb'][1, i], i == 0)
        x, x16, routing, p_lanes, cnt = _merge(
            x, y_a.reshape(t, BR), y_b.reshape(t, BR), y_c.reshape(t, BR), o_f.reshape(t, BR), o_b.reshape(t, BR),
            pd, prm['hgrn_norm_g'][i], prm['w_gate'][i], prm['b_gate'][i], prm['w_br'][i], prm['w_out'][i],
            prm['ln1_g'][i], prm['ln1_b'][i], prm['router_g_w'][i], prm['router_g_b'][i], prm['router_e_w'][i],
            prm['router_e_b'][i])
        y01 = _expert_outputs(x16, routing, cnt, prm['w13'][i], prm['w2'][i])
        moe, g, bias = (y01, p_lanes), prm['ln2_g'][i], prm['ln2_b'][i]
    return _combine(x, *moe, g, bias).reshape(b, l, d)


def kernel(x_prompt, x_sample, ln_in_g, ln_in_b, w_in, na_rpb, conv_w, hgrn_lb, hgrn_norm_g, w_gate, b_gate, w_br,
           w_out, ln1_g, ln1_b, router_g_w, router_g_b, router_e_w, router_e_b, w13, w2, ln2_g, ln2_b):
    lb = jnp.cumsum(jax.nn.softmax(hgrn_lb.astype(F32), axis=1), axis=1)
    lb = lb - lb[:, :1]
    prm = dict(
        ln_in_g=ln_in_g, ln_in_b=ln_in_b,
        w_in=jnp.stack([_fold_channel_dft(w_in[i]) for i in range(DEPTH)]).astype(BF16),
        attn_bias=jnp.stack([_attn_bias_table(na_rpb[i]) for i in range(DEPTH)]),
        conv_w=conv_w, lb=lb, hgrn_norm_g=hgrn_norm_g,
        w_gate=w_gate.astype(BF16), b_gate=b_gate, w_br=w_br.astype(BF16), w_out=w_out.astype(BF16),
        ln1_g=ln1_g, ln1_b=ln1_b, router_g_w=router_g_w, router_g_b=router_g_b,
        router_e_w=router_e_w, router_e_b=router_e_b, w13=w13.astype(BF16), w2=w2.astype(BF16),
        ln2_g=ln2_g, ln2_b=ln2_b)
    return _trunk(x_prompt, prm), _trunk(x_sample, prm)
```

```python
import functools

import numpy as np
import jax
import jax.numpy as jnp
from jax import lax
from jax.experimental import pallas as pl
from jax.experimental.pallas import tpu as pltpu
from jax.experimental.pallas import tpu_sc as plsc

F32 = jnp.float32
BF16 = jnp.bfloat16
HIGHEST = lax.Precision.HIGHEST

D_MODEL = 1024
DEPTH = 4
GRID_W = 64
BR = 256
N_HEADS = 4
HEAD_DIM = 64
NA_ROWS = 8
NA_COLS = 16
FN_GROUP_DIM = 64
N_GROUPS = 4
EXPERTS_PER_GROUP = 8
N_EXPERTS = N_GROUPS * EXPERTS_PER_GROUP
TOP_K = 2
D_FF_E = 512
DEEPNORM_ALPHA = (2 * DEPTH) ** 0.25
LN_EPS = 1e-5
RMS_EPS = 1e-6
NEG_INF = -1e30

LANES = 128
VMEM_LIMIT_BYTES = 56 * 1024 * 1024
MOE_ROWS = 512
HG_CHUNK = 128
FFT_L1 = 64
SC_WINDOW = 64

COL_QA, COL_KA, COL_VA, COL_XB, COL_GB, COL_GC, COL_XF, COL_QD, COL_FF, COL_FB, COL_VD, COL_GD = range(12)
PROJ_GROUPS = ((0, 3), (3, 6), (6, 8), (8, 13))


def _cparams(*sem):
    return pltpu.CompilerParams(dimension_semantics=sem, vmem_limit_bytes=VMEM_LIMIT_BYTES)


def _sigmoid(x):
    return 1.0 / (1.0 + jnp.exp(-x))


def _split_bf16(x):
    hi = x.astype(BF16)
    return hi, (x - hi.astype(F32)).astype(BF16)


def _dot_split(a_hi, a_lo, b_hi, b_lo):
    dot = functools.partial(jnp.dot, preferred_element_type=F32)
    return dot(a_hi, b_hi) + (dot(a_lo, b_hi) + dot(a_hi, b_lo))


def _pack_bf16_pairs(x):
    m = x.shape[1] // 2
    hi = lax.bitcast_convert_type(x[:, :m].astype(BF16).astype(F32), jnp.uint32)
    lo = lax.bitcast_convert_type(x[:, m:].astype(BF16).astype(F32), jnp.uint32)
    return hi | (lo >> 16)


def _unpack_bf16_pairs(u):
    hi = lax.bitcast_convert_type(u & jnp.uint32(0xFFFF0000), F32)
    lo = lax.bitcast_convert_type(u << 16, F32)
    return jnp.concatenate([hi, lo], axis=1)


def _ln_rows(x, g, b):
    mu = jnp.mean(x, axis=-1, keepdims=True)
    xc = x - mu
    var = jnp.mean(xc * xc, axis=-1, keepdims=True)
    return xc * lax.rsqrt(var + LN_EPS) * g + b


def _head_of_lane(shape):
    return lax.broadcasted_iota(jnp.int32, shape, len(shape) - 1) // HEAD_DIM


def _head_stack(x):
    head = _head_of_lane(x.shape)
    return jnp.concatenate([jnp.where(head == h, x, jnp.zeros_like(x)) for h in range(N_HEADS)], axis=0)


def _same_head_matrix():
    h = np.arange(BR) // HEAD_DIM
    return (h[:, None] == h[None, :]).astype(np.float32)


def _moe_residual(x_ref, y0_ref, y1_ref, p_ref):
    p = p_ref[...]
    ffn = p[:, 0:1] * _unpack_bf16_pairs(y0_ref[...]) + p[:, 1:2] * _unpack_bf16_pairs(y1_ref[...])
    return DEEPNORM_ALPHA * x_ref[...] + ffn


def _proj_kernel(*refs, after_moe):
    if after_moe:
        x_ref, y0_ref, y1_ref, p_ref, g_ref, b_ref, w_ref, xo_ref, *out_refs = refs
        pre = _moe_residual(x_ref, y0_ref, y1_ref, p_ref)
    else:
        x_ref, g_ref, b_ref, w_ref, xo_ref, *out_refs = refs
        pre = x_ref[...]
    x = _ln_rows(pre, g_ref[...], b_ref[...])
    xo_ref[...] = x
    xb = x.astype(BF16)
    for ref, (lo, hi) in zip(out_refs, PROJ_GROUPS):
        ref[...] = jnp.dot(xb, w_ref[:, lo * BR:hi * BR], preferred_element_type=F32).astype(ref.dtype)


def _fold_channel_dft(w_in):
    cg, sg = _dft_cos_sin(FN_GROUP_DIM)
    eye = np.eye(BR // FN_GROUP_DIM, dtype=np.float32)
    w_xf = w_in[:, COL_XF * BR:(COL_XF + 1) * BR]
    w_re = jnp.dot(w_xf, np.kron(eye, cg), precision=HIGHEST)
    w_im = -jnp.dot(w_xf, np.kron(eye, sg), precision=HIGHEST)
    return jnp.concatenate([w_in[:, :COL_XF * BR], w_re, w_im, w_in[:, (COL_XF + 1) * BR:]], axis=1)


def _pair_specs(t, d, tm):
    return [pl.BlockSpec((tm, d // 2), lambda i: (i, 0)), pl.BlockSpec((tm, d // 2), lambda i: (i + t // tm, 0))]


def _input_projection(x, moe, g, b, w_in, tm=512):
    t, d = x.shape
    n = w_in.shape[1]
    widths = (d,) + tuple((hi - lo) * BR for lo, hi in PROJ_GROUPS)
    dtypes = (F32, BF16, BF16, F32, BF16)
    row = pl.BlockSpec((tm, d), lambda i: (i, 0))
    vec = pl.BlockSpec((1, d), lambda i: (0, 0))
    acts, act_specs = [x], [row]
    if moe is not None:
        y01, p = moe
        acts += [y01, y01, p]
        act_specs += _pair_specs(t, d, tm) + [pl.BlockSpec((tm, LANES), lambda i: (i, 0))]
    return pl.pallas_call(
        functools.partial(_proj_kernel, after_moe=moe is not None), grid=(t // tm,),
        in_specs=act_specs + [vec, vec, pl.BlockSpec((d, n), lambda i: (0, 0))],
        out_specs=[pl.BlockSpec((tm, w), lambda i: (i, 0)) for w in widths],
        out_shape=[jax.ShapeDtypeStruct((t, w), dt) for w, dt in zip(widths, dtypes)],
        compiler_params=_cparams("parallel"), name="ln_in_proj",
    )(*acts, g.reshape(1, d), b.reshape(1, d), w_in)


def _attn_bias_table(rpb):
    c = np.arange(GRID_W)
    win_c0 = np.clip(c - NA_COLS // 2, 0, GRID_W - NA_COLS)
    valid = (c[None, :] >= win_c0[:, None]) & (c[None, :] < win_c0[:, None] + NA_COLS)
    dc = np.clip(c[None, :] - c[:, None], -(NA_COLS - 1), NA_COLS - 1) + NA_COLS - 1
    dr = np.arange(NA_ROWS)[None, :] - np.arange(NA_ROWS)[:, None] + NA_ROWS - 1
    row_hot = (dr[:, :, None] == np.arange(2 * NA_ROWS - 1)).astype(np.float32)
    col_hot = (dc[:, :, None] == np.arange(2 * NA_COLS - 1)).astype(np.float32)
    t = jnp.einsum('oia,hab,qkb->ohqik', row_hot, rpb.astype(F32), col_hot, precision=HIGHEST)
    t = jnp.where(valid[None, None, :, None, :], t, NEG_INF)
    return t.reshape(NA_ROWS, N_HEADS * GRID_W, NA_ROWS * GRID_W)


def _attn_kernel(q_ref, k_ref, v_ref, bias_ref, o_ref, *, n_rows, rows_per_tile):
    i = pl.program_id(1)
    head = _head_of_lane((GRID_W, BR))
    scale = HEAD_DIM ** -0.5
    span = NA_ROWS * GRID_W
    for j in range(rows_per_tile):
        r = i * rows_per_tile + j
        r0 = jnp.clip(r - NA_ROWS // 2, 0, n_rows - NA_ROWS)
        q = q_ref[0, j * GRID_W:(j + 1) * GRID_W, :]
        qs = _head_stack(q)
        start = pl.multiple_of(r0 * GRID_W, GRID_W)
        ks = k_ref[0, pl.ds(start, span), :]
        vs = v_ref[0, pl.ds(start, span), :]
        s = lax.dot_general(qs, ks, (((1,), (1,)), ((), ())), preferred_element_type=F32)
        s = s * scale + bias_ref[r - r0]
        m = jnp.max(s, axis=-1, keepdims=True)
        e = jnp.exp(s - m)
        p = e / jnp.sum(e, axis=-1, keepdims=True)
        o4 = jnp.dot(p.astype(BF16), vs, preferred_element_type=F32)
        o = jnp.zeros((GRID_W, BR), F32)
        for h in range(N_HEADS):
            o = o + jnp.where(head == h, o4[h * GRID_W:(h + 1) * GRID_W, :], 0.0)
        o_ref[0, j * GRID_W:(j + 1) * GRID_W, :] = o.astype(o_ref.dtype)


def _neighbourhood_attention(pa, bias, rows_per_tile=8):
    b, l, _ = pa.shape
    n_rows = l // GRID_W
    tq = rows_per_tile * GRID_W
    kern = functools.partial(_attn_kernel, n_rows=n_rows, rows_per_tile=rows_per_tile)
    return pl.pallas_call(
        kern, grid=(b, l // tq),
        in_specs=[pl.BlockSpec((1, tq, BR), lambda bi, i: (bi, i, 0)),
                  pl.BlockSpec((1, l, BR), lambda bi, i: (bi, 0, 1)),
                  pl.BlockSpec((1, l, BR), lambda bi, i: (bi, 0, 2)),
                  pl.BlockSpec(bias.shape, lambda bi, i: (0, 0, 0))],
        out_specs=pl.BlockSpec((1, tq, BR), lambda bi, i: (bi, i, 0)),
        out_shape=jax.ShapeDtypeStruct((b, l, BR), BF16),
        compiler_params=_cparams("parallel", "arbitrary"), name="nbr_attn",
    )(pa, pa, pa, bias)


def _conv_kernel(xb_ref, gb_ref, gc_ref, xbp_ref, gcp_ref, xbn_ref, gcn_ref, w_ref, o_ref, *, n_tiles):
    i = pl.program_id(1)
    tl = xb_ref.shape[1]
    last = xbp_ref.shape[1] - 1

    def gated(gc, xb):
        return gc.astype(F32) * xb.astype(F32)

    u = gated(gc_ref[0], xb_ref[0])
    u_before = jnp.where(i > 0, gated(gcp_ref[0, last:, :], xbp_ref[0, last:, :]), 0.0)
    u_after = jnp.where(i < n_tiles - 1, gated(gcn_ref[0, 0:1, :], xbn_ref[0, 0:1, :]), 0.0)
    row = lax.broadcasted_iota(jnp.int32, (tl, 1), 0)
    u_prev = jnp.where(row == 0, u_before, pltpu.roll(u, 1, 0))
    u_next = jnp.where(row == tl - 1, u_after, pltpu.roll(u, tl - 1, 0))
    conv = w_ref[0:1, :] * u_prev + w_ref[1:2, :] * u + w_ref[2:3, :] * u_next
    o_ref[0] = (gb_ref[0].astype(F32) * conv).astype(o_ref.dtype)


def _short_conv(pb, conv_w, tl=1024):
    b, l, _ = pb.shape
    n_tiles = l // tl
    halo = 16
    per_tile = tl // halo
    last_halo = l // halo - 1

    def main(col):
        return pl.BlockSpec((1, tl, BR), lambda bi, i: (bi, i, col))

    def before(col):
        return pl.BlockSpec((1, halo, BR), lambda bi, i: (bi, jnp.maximum(i * per_tile - 1, 0), col))

    def after(col):
        return pl.BlockSpec((1, halo, BR), lambda bi, i: (bi, jnp.minimum((i + 1) * per_tile, last_halo), col))

    return pl.pallas_call(
        functools.partial(_conv_kernel, n_tiles=n_tiles), grid=(b, n_tiles),
        in_specs=[main(0), main(1), main(2), before(0), before(2), after(0), after(2),
                  pl.BlockSpec(conv_w.shape, lambda bi, i: (0, 0))],
        out_specs=pl.BlockSpec((1, tl, BR), lambda bi, i: (bi, i, 0)),
        out_shape=jax.ShapeDtypeStruct((b, l, BR), BF16),
        compiler_params=_cparams("parallel", "parallel"), name="short_conv",
    )(pb, pb, pb, pb, pb, pb, pb, conv_w)


def _dft_cos_sin(n):
    ang = 2.0 * np.pi * ((np.arange(n)[:, None] * np.arange(n)[None, :]) % n) / n
    return np.cos(ang).astype(np.float32), np.sin(ang).astype(np.float32)


def _fft_major_kernel(g_ref, m_ref, tc_ref, ts_ref, br_ref, bi_ref):
    for s in range(8):
        g = g_ref[0, :, s, :]
        swapped = jnp.concatenate([g[:, BR:], -g[:, :BR]], axis=1)
        a = jnp.dot(m_ref[...], jnp.concatenate([g, swapped], axis=0), precision=HIGHEST,
                    preferred_element_type=F32)
        ar, ai = a[:, :BR], a[:, BR:]
        tc = tc_ref[:, s, :]
        ts = ts_ref[:, s, :]
        br_ref[0, :, s, :] = ar * tc + ai * ts
        bi_ref[0, :, s, :] = ai * tc - ar * ts


def _fft_minor_kernel(br_ref, bi_ref, m_ref, o_ref, *, scale):
    l2 = br_ref.shape[2]
    group = m_ref.shape[1] // l2
    for j0 in range(0, 8, group):
        rhs = jnp.concatenate([br_ref[0, j0:j0 + group].reshape(group * l2, BR),
                               bi_ref[0, j0:j0 + group].reshape(group * l2, BR)], axis=0)
        y = _dot_split(m_ref[0], m_ref[1], *_split_bf16(rhs)) * scale
        for j in range(group):
            o_ref[0, :, j0 + j, :] = y[j * l2:(j + 1) * l2].astype(o_ref.dtype)


def _fft_twiddles(l):
    l1 = FFT_L1
    l2 = l // l1
    idx = (jnp.arange(l1, dtype=jnp.int32)[:, None] * jnp.arange(l2, dtype=jnp.int32)[None, :]) % l
    ang = idx.astype(F32) * (2.0 * np.pi / l)
    tc = jnp.broadcast_to(jnp.cos(ang)[:, :, None], (l1, l2, BR))
    ts = jnp.broadcast_to(jnp.sin(ang)[:, :, None], (l1, l2, BR))
    return tc, ts


def _fourier_mix(pc, tc, ts):
    b, l, _ = pc.shape
    l1 = FFT_L1
    l2 = l // l1

    c1, s1 = _dft_cos_sin(l1)
    c2, s2 = _dft_cos_sin(l2)
    group = min(max(1, 128 // l2), 8)
    eye = np.eye(group, dtype=np.float32)
    minor = jnp.asarray(np.concatenate([np.kron(eye, c2), np.kron(eye, s2)], axis=1))
    minor_hi = minor.astype(BF16)
    minor_parts = jnp.stack([minor_hi, (minor - minor_hi.astype(F32)).astype(BF16)])

    def strip(width):
        return pl.BlockSpec((1, l1, 8, width), lambda bi, i: (bi, 0, i, 0))

    tspec = pl.BlockSpec((l1, 8, BR), lambda bi, i: (0, i, 0))
    mspec = pl.BlockSpec((l1, 2 * l1), lambda bi, i: (0, 0))
    br, bi_ = pl.pallas_call(
        _fft_major_kernel, grid=(b, l2 // 8),
        in_specs=[strip(2 * BR), mspec, tspec, tspec], out_specs=[strip(BR), strip(BR)],
        out_shape=[jax.ShapeDtypeStruct((b, l1, l2, BR), F32)] * 2,
        compiler_params=_cparams("parallel", "parallel"), name="fft_major",
    )(pc.reshape(b, l1, l2, 2 * BR), np.concatenate([c1, s1], axis=1), tc, ts)

    bspec = pl.BlockSpec((1, 8, l2, BR), lambda bi, i: (bi, i, 0, 0))
    m2spec = pl.BlockSpec(minor_parts.shape, lambda bi, i: (0, 0, 0))
    scale = 1.0 / float(np.sqrt(l * FN_GROUP_DIM))
    y = pl.pallas_call(
        functools.partial(_fft_minor_kernel, scale=scale), grid=(b, l1 // 8),
        in_specs=[bspec, bspec, m2spec],
        out_specs=pl.BlockSpec((1, l2, 8, BR), lambda bi, i: (bi, 0, i, 0)),
        out_shape=jax.ShapeDtypeStruct((b, l2, l1, BR), F32),
        compiler_params=_cparams("parallel", "parallel"), name="fft_minor",
    )(br, bi_, minor_parts)
    return y.reshape(b, l, BR)


def _hgrn_level_masks():
    t = np.arange(HG_CHUNK)
    out = []
    half = 8
    while 2 * half < HG_CHUNK:
        same = (t[:, None] // (2 * half)) == (t[None, :] // (2 * half))
        out.append(np.tile(same, (1, N_HEADS)).astype(np.float32))
        half *= 2
    return np.stack(out)


def _hgrn_kernel(qd_ref, fl_ref, vd_ref, lb_ref, mask_ref, bd_ref, o_ref, st_ref, *, first_layer, chunks_per_tile,
                 rev):
    c = HG_CHUNK

    @pl.when(pl.program_id(1) == 0)
    def _():
        st_ref[...] = jnp.zeros_like(st_ref)

    bd = bd_ref[...]
    bd16 = bd.astype(BF16)
    lb = lb_ref[...]
    row = lax.broadcasted_iota(jnp.int32, (c, 1), 0)
    sub = row % 8
    nt = (((1,), (1,)), ((), ()))

    def roll8(x, j):
        return pltpu.roll(x.reshape(c // 8, 8, BR), j % 8, 1).reshape(c, BR)

    def earlier(x, j):
        return roll8(x, -j if rev else j)

    def later(x, j):
        return roll8(x, j if rev else -j)

    def has_earlier(j):
        return (sub + j <= 7) if rev else (sub >= j)

    def has_later(j):
        return (sub >= j) if rev else (sub + j <= 7)

    for ci in (reversed(range(chunks_per_tile)) if rev else range(chunks_per_tile)):
        sl = slice(ci * c, (ci + 1) * c)
        qd = qd_ref[0, sl, :].astype(F32)
        fl = fl_ref[0, sl, :].astype(F32)
        v = vd_ref[0, sl, :].astype(F32)
        q = qd * _sigmoid(qd)
        if first_layer:
            f = _sigmoid(fl)
            k = _sigmoid(-fl)
        else:
            f = lb + (1.0 - lb) * _sigmoid(fl)
            k = (1.0 - lb) * _sigmoid(-fl)

        o = jnp.dot((q * k).astype(BF16), bd16, preferred_element_type=F32) * v
        g = q * f
        for d in range(1, 8):
            if d > 1:
                g = g * earlier(f, d - 1)
            p = jnp.where(has_earlier(d), g * earlier(k, d), 0.0)
            o = o + jnp.dot(p.astype(BF16), bd16, preferred_element_type=F32) * earlier(v, d)

        qf = f
        for j in (1, 2, 4):
            qf = qf * jnp.where(has_earlier(j), earlier(qf, j), 1.0)
        kf = jnp.where(has_later(1), later(f, 1), 1.0)
        for j in (1, 2, 4):
            kf = kf * jnp.where(has_later(j), later(kf, j), 1.0)

        scores = None
        half = 8
        level = 0
        while half < c:
            late = ((row // half) % 2) == (0 if rev else 1)
            qb = jnp.where(late, q * qf, 0.0).astype(BF16)
            kb = jnp.where(late, 0.0, k * kf).astype(BF16)
            x = lax.dot_general(qb, _head_stack(kb), nt, preferred_element_type=F32)
            if 2 * half < c:
                x = x * mask_ref[level]
            scores = x if scores is None else scores + x
            blocks = c // (2 * half)
            qf3 = qf.reshape(blocks, 2 * half, BR)
            end_early = half if rev else half - 1
            end_late = 0 if rev else 2 * half - 1
            tot_early = jnp.broadcast_to(qf3[:, end_early:end_early + 1, :], qf3.shape).reshape(c, BR)
            tot_late = jnp.broadcast_to(qf3[:, end_late:end_late + 1, :], qf3.shape).reshape(c, BR)
            qf = qf * jnp.where(late, tot_early, 1.0)
            kf = kf * jnp.where(late, 1.0, tot_late)
            half *= 2
            level += 1

        o = o + jnp.dot(scores.astype(BF16), _head_stack(v.astype(BF16)), preferred_element_type=F32)

        st = st_ref[...]
        o = o + lax.dot_general((q * qf).astype(BF16), st.astype(BF16), nt, preferred_element_type=F32)
        o_ref[0, sl, :] = o
        upd = jnp.dot(v.T.astype(BF16), (k * kf).astype(BF16), preferred_element_type=F32)
        last = 0 if rev else c - 1
        st_ref[...] = st * qf[last:last + 1, :] + upd * bd


def _hgrn_scan(qd, fl, vd, cols, lb, first_layer, rev, chunks_per_tile=8):
    b, l, _ = qd.shape
    tile = HG_CHUNK * chunks_per_tile
    n_tiles = l // tile
    masks = _hgrn_level_masks()
    bd = _same_head_matrix()

    def tile_index(i):
        return n_tiles - 1 - i if rev else i

    def seq(col):
        return pl.BlockSpec((1, tile, BR), lambda bi, i: (bi, tile_index(i), col))

    return pl.pallas_call(
        functools.partial(_hgrn_kernel, first_layer=first_layer, chunks_per_tile=chunks_per_tile, rev=rev),
        grid=(b, n_tiles),
        in_specs=[seq(cols[0]), seq(cols[1]), seq(cols[2]),
                  pl.BlockSpec((1, BR), lambda bi, i: (0, 0)),
                  pl.BlockSpec(masks.shape, lambda bi, i: (0, 0, 0)),
                  pl.BlockSpec((BR, BR), lambda bi, i: (0, 0))],
        out_specs=pl.BlockSpec((1, tile, BR), lambda bi, i: (bi, tile_index(i), 0)),
        out_shape=jax.ShapeDtypeStruct((b, l, BR), F32),
        scratch_shapes=[pltpu.VMEM((BR, BR), F32)],
        compiler_params=_cparams("parallel", "arbitrary"),
        name="hgrn_bwd" if rev else "hgrn_fwd",
    )(qd, fl, vd, lb.reshape(1, BR), masks, bd)


def _hgrn2_bidirectional(pd, lb_f, lb_b, first_layer):
    o_f = _hgrn_scan(pd, pd, pd, (0, 1, 3), lb_f, first_layer, rev=False)
    o_b = _hgrn_scan(pd, pd, pd, (0, 2, 3), lb_b, first_layer, rev=True)
    return o_f, o_b


def _route(x, w_parts, bias, tri, run):
    x_hi, x_lo = _split_bf16(x)
    both = jnp.dot(x_hi, w_parts, preferred_element_type=F32)
    cross = jnp.dot(x_lo, w_parts[:, :LANES], preferred_element_type=F32)
    logits = both[:, :LANES] + (cross + both[:, LANES:]) + bias
    lane = lax.broadcasted_iota(jnp.int32, logits.shape, 1)
    big = jnp.int32(LANES)
    is_group = (lane >= N_EXPERTS) & (lane < N_EXPERTS + N_GROUPS)
    lg = jnp.where(is_group, logits, -jnp.inf)
    mg = jnp.max(lg, axis=-1, keepdims=True)
    g_sel = jnp.min(jnp.where(lg == mg, lane, big), axis=-1, keepdims=True) - N_EXPERTS
    p_g = 1.0 / jnp.sum(jnp.exp(lg - mg), axis=-1, keepdims=True)
    in_group = (lane < N_EXPERTS) & ((lane // EXPERTS_PER_GROUP) == g_sel)
    le = jnp.where(in_group, logits, -jnp.inf)
    v1 = jnp.max(le, axis=-1, keepdims=True)
    i1 = jnp.min(jnp.where(le == v1, lane, big), axis=-1, keepdims=True)
    le2 = jnp.where(lane == i1, -jnp.inf, le)
    v2 = jnp.max(le2, axis=-1, keepdims=True)
    i2 = jnp.min(jnp.where(le2 == v2, lane, big), axis=-1, keepdims=True)
    e2 = jnp.exp(v2 - v1)
    w1 = p_g / (1.0 + e2)
    w2 = p_g * e2 / (1.0 + e2)
    hot1 = lane == i1
    hot2 = lane == i2
    hot = (hot1 | hot2).astype(BF16)
    before = jnp.dot(tri, hot, preferred_element_type=F32) + run
    rank1 = jnp.sum(jnp.where(hot1, before, 0.0), axis=-1, keepdims=True).astype(jnp.int32)
    rank2 = jnp.sum(jnp.where(hot2, before, 0.0), axis=-1, keepdims=True).astype(jnp.int32)
    run = run + jnp.sum(hot.astype(F32), axis=0, keepdims=True)
    e = jnp.where(lane == 0, i1, jnp.where(lane == 1, i2, jnp.where(lane == 2, rank1,
                                                                     jnp.where(lane == 3, rank2, 0))))
    p = jnp.where(lane == 0, w1, jnp.where(lane == 1, w2, 0.0))
    return e, p, run


def _merge_kernel(x_ref, ya_ref, yb_ref, yc_ref, of_ref, ob_ref, gd_ref, ng_ref, bd_ref, wg_ref, bg_ref, wbr_ref,
                  wo_ref, g_ref, b_ref, rw_ref, rb_ref, tri_ref, o_ref, o16_ref, e_ref, p_ref, cnt_ref, run_ref):
    @pl.when(pl.program_id(0) == 0)
    def _():
        run_ref[...] = jnp.zeros_like(run_ref)

    od = of_ref[...] + ob_ref[...]
    ms = jnp.dot(od * od, bd_ref[...], precision=HIGHEST, preferred_element_type=F32) * (1.0 / HEAD_DIM)
    gd = gd_ref[...].astype(F32)
    y_d = od * lax.rsqrt(ms + RMS_EPS) * ng_ref[...] * (gd * _sigmoid(gd))

    x = x_ref[...]
    xb = x.astype(BF16)
    merged = None
    for i, y in enumerate((ya_ref[...], yb_ref[...], yc_ref[...], y_d)):
        gate = _sigmoid(jnp.dot(xb, wg_ref[i], preferred_element_type=F32) + bg_ref[i])
        term = gate * jnp.dot(y.astype(BF16), wbr_ref[i], preferred_element_type=F32)
        merged = term if merged is None else merged + term
    mix = jnp.dot(merged.astype(BF16), wo_ref[...], preferred_element_type=F32)
    out = _ln_rows(DEEPNORM_ALPHA * x + mix, g_ref[...], b_ref[...])
    o_ref[...] = out
    o16_ref[...] = _pack_bf16_pairs(out)
    e, p, run = _route(out, rw_ref[...], rb_ref[...], tri_ref[...], run_ref[...])
    e_ref[...] = e.T[:e_ref.shape[0], :]
    p_ref[...] = p
    run_ref[...] = run
    cnt_ref[...] = run.astype(jnp.int32)


def _merge(x, y_a, y_b, y_c, o_f, o_b, pd, norm_g, w_gate, b_gate, w_br, w_out, g, b, wg, bg, we, be, tm=512):
    t, d = x.shape
    row = pl.BlockSpec((tm, d), lambda i: (i, 0))
    yrow = pl.BlockSpec((tm, BR), lambda i: (i, 0))
    vec = pl.BlockSpec((1, d), lambda i: (0, 0))
    lanes = pl.BlockSpec((tm, LANES), lambda i: (i, 0))
    lvec = pl.BlockSpec((1, LANES), lambda i: (0, 0))

    def whole(a):
        return pl.BlockSpec(a.shape, lambda i: (0,) * a.ndim)

    bgate = b_gate.reshape(b_gate.shape[0], 1, d)
    rw = jnp.zeros((d, LANES), F32).at[:, :N_EXPERTS].set(we).at[:, N_EXPERTS:N_EXPERTS + N_GROUPS].set(wg)
    rb = jnp.zeros((1, LANES), F32).at[0, :N_EXPERTS].set(be).at[0, N_EXPERTS:N_EXPERTS + N_GROUPS].set(bg)
    r_parts = jnp.concatenate(_split_bf16(rw), axis=1)
    tri = np.tril(np.ones((tm, tm), np.float32), -1).astype(BF16)
    ng = jnp.tile(norm_g, N_HEADS).reshape(1, BR)
    bd = _same_head_matrix()
    return pl.pallas_call(
        _merge_kernel, grid=(t // tm,),
        in_specs=[row, yrow, yrow, yrow, yrow, yrow, pl.BlockSpec((tm, BR), lambda i: (i, 4)), whole(ng), whole(bd),
                  whole(w_gate), whole(bgate), whole(w_br), whole(w_out), vec, vec,
                  whole(r_parts), lvec, whole(tri)],
        out_specs=[row, pl.BlockSpec((tm, d // 2), lambda i: (i, 0)), pl.BlockSpec((8, tm), lambda i: (0, i)),
                   lanes, lvec],
        out_shape=[jax.ShapeDtypeStruct((t, d), F32), jax.ShapeDtypeStruct((t, d // 2), jnp.uint32),
                   jax.ShapeDtypeStruct((8, t), jnp.int32), jax.ShapeDtypeStruct((t, LANES), F32),
                   jax.ShapeDtypeStruct((1, LANES), jnp.int32)],
        scratch_shapes=[pltpu.VMEM((1, LANES), F32)],
        compiler_params=_cparams("arbitrary"), name="merge_ln1_route",
    )(x, y_a, y_b, y_c, o_f, o_b, pd, ng, bd, w_gate, bgate, w_br, w_out, g.reshape(1, d), b.reshape(1, d),
      r_parts, rb, tri)


def _work_items_kernel(cnt_ref, start_ref, items_ref):
    n_items = items_ref.shape[1]

    def put(w, e, blk, lo, hi, first):
        for row, val in enumerate((e, blk, lo, hi, first)):
            items_ref[row, w] = val

    def expert(e, carry):
        start, w, _, _ = carry
        start_ref[e] = start
        end = start + cnt_ref[e]
        first_blk = start // MOE_ROWS
        n_blk = jnp.where(end > start, (end - 1) // MOE_ROWS - first_blk + 1, 0)

        def block(i, w):
            blk = first_blk + i
            row0 = blk * MOE_ROWS
            lo = jnp.maximum(start, row0) - row0
            first = jnp.where((i > 0) | (lo == 0), 1, 0)
            put(w, e, blk, lo, jnp.minimum(end, row0 + MOE_ROWS) - row0, first)
            return w + 1

        w_end = lax.fori_loop(0, n_blk, block, w)
        owner = jnp.where(n_blk > 0, e, carry[2])
        last_blk = jnp.where(n_blk > 0, first_blk + n_blk - 1, carry[3])
        return end, w_end, owner, last_blk

    zero = jnp.int32(0)
    _, w_live, owner, last_blk = lax.fori_loop(0, N_EXPERTS, expert, (zero, zero, zero, zero))

    def surplus(w, _):
        put(w, owner, last_blk, zero, zero, zero)
        return 0

    lax.fori_loop(w_live, n_items, surplus, 0)


def _expert_work_items(counts, n_blocks):
    smem = pl.BlockSpec(memory_space=pltpu.SMEM)
    return pl.pallas_call(
        _work_items_kernel, in_specs=[smem], out_specs=[smem, smem],
        out_shape=[jax.ShapeDtypeStruct((N_EXPERTS,), jnp.int32),
                   jax.ShapeDtypeStruct((5, n_blocks + N_EXPERTS), jnp.int32)],
        name="moe_work_items",
    )(counts)


def _expert_kernel(items_ref, x_ref, w13_ref, w2_ref, o_ref):
    w = pl.program_id(0)
    lo = items_ref[2, w]
    hi = items_ref[3, w]

    @pl.when(hi > lo)
    def _():
        x = _unpack_bf16_pairs(x_ref[...]).astype(BF16)
        h = jnp.dot(x, w13_ref[0], preferred_element_type=F32)
        hg = h[:, :D_FF_E]
        act = hg * _sigmoid(hg) * h[:, D_FF_E:]
        y = _pack_bf16_pairs(jnp.dot(act.astype(BF16), w2_ref[0], preferred_element_type=F32))
        row = lax.broadcasted_iota(jnp.int32, (MOE_ROWS, 1), 0)
        mine = (row >= lo) & (row < hi)

        @pl.when(items_ref[4, w] == 1)
        def _():
            o_ref[...] = jnp.where(mine, y, jnp.zeros_like(y))

        @pl.when(items_ref[4, w] == 0)
        def _():
            o_ref[...] = jnp.where(mine, y, o_ref[...])


def _expert_blocks(xs, items, w13, w2):
    n_rows, half = xs.shape
    d = 2 * half
    grid_spec = pltpu.PrefetchScalarGridSpec(
        num_scalar_prefetch=1, grid=(items.shape[1],),
        in_specs=[pl.BlockSpec((MOE_ROWS, half), lambda w, it: (it[1, w], 0)),
                  pl.BlockSpec((1, d, 2 * D_FF_E), lambda w, it: (it[0, w], 0, 0)),
                  pl.BlockSpec((1, D_FF_E, d), lambda w, it: (it[0, w], 0, 0))],
        out_specs=pl.BlockSpec((MOE_ROWS, half), lambda w, it: (it[1, w], 0)))
    return pl.pallas_call(
        _expert_kernel, grid_spec=grid_spec, out_shape=jax.ShapeDtypeStruct((n_rows, half), jnp.uint32),
        compiler_params=_cparams("arbitrary"), name="moe_experts",
    )(items, xs, w13, w2)


def _combine_kernel(x_ref, y0_ref, y1_ref, p_ref, g_ref, b_ref, o_ref):
    o_ref[...] = _ln_rows(_moe_residual(x_ref, y0_ref, y1_ref, p_ref), g_ref[...], b_ref[...])


def _combine(x, y01, p, g, b, tm=1024):
    t, d = x.shape
    row = pl.BlockSpec((tm, d), lambda i: (i, 0))
    vec = pl.BlockSpec((1, d), lambda i: (0, 0))
    return pl.pallas_call(
        _combine_kernel, grid=(t // tm,),
        in_specs=[row] + _pair_specs(t, d, tm) + [pl.BlockSpec((tm, LANES), lambda i: (i, 0)), vec, vec],
        out_specs=row, out_shape=jax.ShapeDtypeStruct((t, d), F32),
        compiler_params=_cparams("parallel"), name="moe_combine_ln2",
    )(x, y01, y01, p, g.reshape(1, d), b.reshape(1, d))


def _dest_kernel(start_ref, r_ref, o_ref):
    r = r_ref[...]
    e = r[0:TOP_K, :]
    base = jnp.zeros_like(e)
    for x in range(N_EXPERTS):
        base = jnp.where(e == x, start_ref[x], base)
    pad = jnp.zeros((r.shape[0] - TOP_K, r.shape[1]), jnp.int32)
    o_ref[...] = jnp.concatenate([base + r[TOP_K:2 * TOP_K, :], pad], axis=0)


def _slot_rows(routing, start, tl=8192):
    rows, t = routing.shape
    tl = min(tl, t)
    grid_spec = pltpu.PrefetchScalarGridSpec(
        num_scalar_prefetch=1, grid=(t // tl,),
        in_specs=[pl.BlockSpec((rows, tl), lambda i, st: (0, i))],
        out_specs=pl.BlockSpec((rows, tl), lambda i, st: (0, i)))
    return pl.pallas_call(
        _dest_kernel, grid_spec=grid_spec, out_shape=jax.ShapeDtypeStruct((rows, t), jnp.int32),
        compiler_params=_cparams("parallel"), name="moe_slot_rows",
    )(start, routing)


def _sc_workers():
    info = plsc.get_sparse_core_info()
    return info.num_cores, info.num_cores * info.num_subcores


def _sc_move_rows(src, dest, scatter):
    t = dest.shape[1]
    width = src.shape[1]
    n_cores, n_workers = _sc_workers()
    steps = t // (n_workers * SC_WINDOW)
    assert steps * n_workers * SC_WINDOW == t and steps % 2 == 0, (t, n_workers)
    mesh = plsc.VectorSubcoreMesh(core_axis_name="core", subcore_axis_name="subcore")

    @functools.partial(
        pl.kernel, mesh=mesh, out_type=jax.ShapeDtypeStruct((TOP_K * t, width), src.dtype),
        scratch_types=[pltpu.VMEM((TOP_K, steps, SC_WINDOW), jnp.int32),
                       pltpu.VMEM((2, SC_WINDOW, width), src.dtype),
                       pltpu.SemaphoreType.DMA, pltpu.SemaphoreType.DMA])
    def move(src_hbm, idx_hbm, out_hbm, idx_v, rows_v, sem0, sem1):
        wid = lax.axis_index("subcore") * n_cores + lax.axis_index("core")
        for k in range(TOP_K):
            pltpu.sync_copy(idx_hbm.at[k, wid], idx_v.at[k])
        sems = (sem0, sem1)

        def window(j):
            return pl.ds((wid * steps + j) * SC_WINDOW, SC_WINDOW)

        def run(fetch, flush):
            fetch(0, 0).start()

            @pl.loop(0, steps, step=2)
            def _(j):
                fetch(j + 1, 1).start()
                fetch(j, 0).wait()
                flush(j, 0)

                @pl.when(j + 2 < steps)
                def _():
                    fetch(j + 2, 0).start()

                fetch(j + 1, 1).wait()
                flush(j + 1, 1)

        if scatter:
            def fetch(j, slot):
                return pltpu.make_async_copy(src_hbm.at[window(j)], rows_v.at[slot], sems[slot])

            def flush(j, slot):
                for k in range(TOP_K):
                    pltpu.sync_copy(rows_v.at[slot], out_hbm.at[idx_v.at[k, j]])

            run(fetch, flush)
        else:
            for k in range(TOP_K):
                def fetch(j, slot, k=k):
                    return pltpu.make_async_copy(src_hbm.at[idx_v.at[k, j]], rows_v.at[slot], sems[slot])

                def flush(j, slot, k=k):
                    pltpu.sync_copy(rows_v.at[slot], out_hbm.at[pl.ds(k * t + (wid * steps + j) * SC_WINDOW,
                                                                      SC_WINDOW)])

                run(fetch, flush)

    return move(src, dest.reshape(dest.shape[0], n_workers, steps, SC_WINDOW))


def _expert_outputs(x16, routing, cnt, w13, w2):
    t = x16.shape[0]
    start, items = _expert_work_items(cnt[0, :N_EXPERTS], t * TOP_K // MOE_ROWS)
    dest = _slot_rows(routing, start)
    xs = _sc_move_rows(x16, dest, scatter=True)
    ys = _expert_blocks(xs, items, w13, w2)
    return _sc_move_rows(ys, dest, scatter=False)


def _trunk(x, prm):
    b, l, d = x.shape
    t = b * l
    x = x.reshape(t, d)
    moe, g, bias = None, prm['ln_in_g'], prm['ln_in_b']
    tc, ts = _fft_twiddles(l)
    for i in range(DEPTH):
        x, pa, pb, pc, pd = _input_projection(x, moe, g, bias, prm['w_in'][i])
        y_a = _neighbourhood_attention(pa.reshape(b, l, -1), prm['attn_bias'][i])
        y_b = _short_conv(pb.reshape(b, l, -1), prm['conv_w'][i])
        y_c = _fourier_mix(pc.reshape(b, l, -1), tc, ts)
        o_f, o_b = _hgrn2_bidirectional(pd.reshape(b, l, -1), prm['lb'][0, i], prm['lb'][1, i], i == 0)
        x, x16, routing, p_lanes, cnt = _merge(
            x, y_a.reshape(t, BR), y_b.reshape(t, BR), y_c.reshape(t, BR), o_f.reshape(t, BR), o_b.reshape(t, BR),
            pd, prm['hgrn_norm_g'][i], prm['w_gate'][i], prm['b_gate'][i], prm['w_br'][i], prm['w_out'][i],
            prm['ln1_g'][i], prm['ln1_b'][i], prm['router_g_w'][i], prm['router_g_b'][i], prm['router_e_w'][i],
            prm['router_e_b'][i])
        y01 = _expert_outputs(x16, routing, cnt, prm['w13'][i], prm['w2'][i])
        moe, g, bias = (y01, p_lanes), prm['ln2_g'][i], prm['ln2_b'][i]
    return _combine(x, *moe, g, bias).reshape(b, l, d)


def kernel(x_prompt, x_sample, ln_in_g, ln_in_b, w_in, na_rpb, conv_w, hgrn_lb, hgrn_norm_g, w_gate, b_gate, w_br,
           w_out, ln1_g, ln1_b, router_g_w, router_g_b, router_e_w, router_e_b, w13, w2, ln2_g, ln2_b):
    lb = jnp.cumsum(jax.nn.softmax(hgrn_lb.astype(F32), axis=1), axis=1)
    lb = lb - lb[:, :1]
    prm = dict(
        ln_in_g=ln_in_g, ln_in_b=ln_in_b,
        w_in=jnp.stack([_fold_channel_dft(w_in[i]) for i in range(DEPTH)]).astype(BF16),
        attn_bias=jnp.stack([_attn_bias_table(na_rpb[i]) for i in range(DEPTH)]),
        conv_w=conv_w, lb=lb, hgrn_norm_g=hgrn_norm_g,
        w_gate=w_gate.astype(BF16), b_gate=b_gate, w_br=w_br.astype(BF16), w_out=w_out.astype(BF16),
        ln1_g=ln1_g, ln1_b=ln1_b, router_g_w=router_g_w, router_g_b=router_g_b,
        router_e_w=router_e_w, router_e_b=router_e_b, w13=w13.astype(BF16), w2=w2.astype(BF16),
        ln2_g=ln2_g, ln2_b=ln2_b)
    return _trunk(x_prompt, prm), _trunk(x_sample, prm)
```

```python
import functools

import numpy as np
import jax
import jax.numpy as jnp
from jax import lax
from jax.experimental import pallas as pl
from jax.experimental.pallas import tpu as pltpu
from jax.experimental.pallas import tpu_sc as plsc

F32 = jnp.float32
BF16 = jnp.bfloat16
HIGHEST = lax.Precision.HIGHEST

D_MODEL = 1024
DEPTH = 4
GRID_W = 64
BR = 256
N_HEADS = 4
HEAD_DIM = 64
NA_ROWS = 8
NA_COLS = 16
FN_GROUP_DIM = 64
N_GROUPS = 4
EXPERTS_PER_GROUP = 8
N_EXPERTS = N_GROUPS * EXPERTS_PER_GROUP
TOP_K = 2
D_FF_E = 512
DEEPNORM_ALPHA = (2 * DEPTH) ** 0.25
LN_EPS = 1e-5
RMS_EPS = 1e-6
NEG_INF = -1e30

LANES = 128
VMEM_LIMIT_BYTES = 56 * 1024 * 1024
MOE_ROWS = 512
HG_CHUNK = 128
FFT_L1 = 64
SC_WINDOW = 64

COL_QA, COL_KA, COL_VA, COL_XB, COL_GB, COL_GC, COL_XF, COL_QD, COL_FF, COL_FB, COL_VD, COL_GD = range(12)
PROJ_GROUPS = ((0, 3), (3, 6), (6, 8), (8, 13))


def _cparams(*sem):
    return pltpu.CompilerParams(dimension_semantics=sem, vmem_limit_bytes=VMEM_LIMIT_BYTES)


def _sigmoid(x):
    return 1.0 / (1.0 + jnp.exp(-x))


def _split_bf16(x):
    hi = x.astype(BF16)
    return hi, (x - hi.astype(F32)).astype(BF16)


def _dot_split(a_hi, a_lo, b_hi, b_lo):
    dot = functools.partial(jnp.dot, preferred_element_type=F32)
    return dot(a_hi, b_hi) + (dot(a_lo, b_hi) + dot(a_hi, b_lo))


def _pack_bf16_pairs(x):
    m = x.shape[1] // 2
    hi = lax.bitcast_convert_type(x[:, :m].astype(BF16).astype(F32), jnp.uint32)
    lo = lax.bitcast_convert_type(x[:, m:].astype(BF16).astype(F32), jnp.uint32)
    return hi | (lo >> 16)


def _unpack_bf16_pairs(u):
    hi = lax.bitcast_convert_type(u & jnp.uint32(0xFFFF0000), F32)
    lo = lax.bitcast_convert_type(u << 16, F32)
    return jnp.concatenate([hi, lo], axis=1)


def _ln_rows(x, g, b):
    mu = jnp.mean(x, axis=-1, keepdims=True)
    xc = x - mu
    var = jnp.mean(xc * xc, axis=-1, keepdims=True)
    return xc * lax.rsqrt(var + LN_EPS) * g + b


def _head_of_lane(shape):
    return lax.broadcasted_iota(jnp.int32, shape, len(shape) - 1) // HEAD_DIM


def _head_stack(x):
    head = _head_of_lane(x.shape)
    return jnp.concatenate([jnp.where(head == h, x, jnp.zeros_like(x)) for h in range(N_HEADS)], axis=0)


def _same_head_matrix():
    h = np.arange(BR) // HEAD_DIM
    return (h[:, None] == h[None, :]).astype(np.float32)


def _moe_residual(x_ref, y0_ref, y1_ref, p_ref):
    p = p_ref[...]
    ffn = p[:, 0:1] * _unpack_bf16_pairs(y0_ref[...]) + p[:, 1:2] * _unpack_bf16_pairs(y1_ref[...])
    return DEEPNORM_ALPHA * x_ref[...] + ffn


def _proj_kernel(*refs, after_moe):
    if after_moe:
        x_ref, y0_ref, y1_ref, p_ref, g_ref, b_ref, w_ref, xo_ref, *out_refs = refs
        pre = _moe_residual(x_ref, y0_ref, y1_ref, p_ref)
    else:
        x_ref, g_ref, b_ref, w_ref, xo_ref, *out_refs = refs
        pre = x_ref[...]
    x = _ln_rows(pre, g_ref[...], b_ref[...])
    xo_ref[...] = x
    xb = x.astype(BF16)
    for ref, (lo, hi) in zip(out_refs, PROJ_GROUPS):
        ref[...] = jnp.dot(xb, w_ref[:, lo * BR:hi * BR], preferred_element_type=F32).astype(ref.dtype)


def _fold_channel_dft(w_in):
    cg, sg = _dft_cos_sin(FN_GROUP_DIM)
    eye = np.eye(BR // FN_GROUP_DIM, dtype=np.float32)
    w_xf = w_in[:, COL_XF * BR:(COL_XF + 1) * BR]
    w_re = jnp.dot(w_xf, np.kron(eye, cg), precision=HIGHEST)
    w_im = -jnp.dot(w_xf, np.kron(eye, sg), precision=HIGHEST)
    return jnp.concatenate([w_in[:, :COL_XF * BR], w_re, w_im, w_in[:, (COL_XF + 1) * BR:]], axis=1)


def _pair_specs(t, d, tm):
    return [pl.BlockSpec((tm, d // 2), lambda i: (i, 0)), pl.BlockSpec((tm, d // 2), lambda i: (i + t // tm, 0))]


def _input_projection(x, moe, g, b, w_in, tm=512):
    t, d = x.shape
    n = w_in.shape[1]
    widths = (d,) + tuple((hi - lo) * BR for lo, hi in PROJ_GROUPS)
    dtypes = (F32, BF16, BF16, F32, BF16)
    row = pl.BlockSpec((tm, d), lambda i: (i, 0))
    vec = pl.BlockSpec((1, d), lambda i: (0, 0))
    acts, act_specs = [x], [row]
    if moe is not None:
        y01, p = moe
        acts += [y01, y01, p]
        act_specs += _pair_specs(t, d, tm) + [pl.BlockSpec((tm, LANES), lambda i: (i, 0))]
    return pl.pallas_call(
        functools.partial(_proj_kernel, after_moe=moe is not None), grid=(t // tm,),
        in_specs=act_specs + [vec, vec, pl.BlockSpec((d, n), lambda i: (0, 0))],
        out_specs=[pl.BlockSpec((tm, w), lambda i: (i, 0)) for w in widths],
        out_shape=[jax.ShapeDtypeStruct((t, w), dt) for w, dt in zip(widths, dtypes)],
        compiler_params=_cparams("parallel"), name="ln_in_proj",
    )(*acts, g.reshape(1, d), b.reshape(1, d), w_in)


def _attn_bias_table(rpb):
    c = np.arange(GRID_W)
    win_c0 = np.clip(c - NA_COLS // 2, 0, GRID_W - NA_COLS)
    valid = (c[None, :] >= win_c0[:, None]) & (c[None, :] < win_c0[:, None] + NA_COLS)
    dc = np.clip(c[None, :] - c[:, None], -(NA_COLS - 1), NA_COLS - 1) + NA_COLS - 1
    dr = np.arange(NA_ROWS)[None, :] - np.arange(NA_ROWS)[:, None] + NA_ROWS - 1
    row_hot = (dr[:, :, None] == np.arange(2 * NA_ROWS - 1)).astype(np.float32)
    col_hot = (dc[:, :, None] == np.arange(2 * NA_COLS - 1)).astype(np.float32)
    t = jnp.einsum('oia,hab,qkb->ohqik', row_hot, rpb.astype(F32), col_hot, precision=HIGHEST)
    t = jnp.where(valid[None, None, :, None, :], t, NEG_INF)
    return t.reshape(NA_ROWS, N_HEADS * GRID_W, NA_ROWS * GRID_W)


def _attn_kernel(q_ref, k_ref, v_ref, bias_ref, o_ref, *, n_rows, rows_per_tile):
    i = pl.program_id(1)
    head = _head_of_lane((GRID_W, BR))
    scale = HEAD_DIM ** -0.5
    span = NA_ROWS * GRID_W
    for j in range(rows_per_tile):
        r = i * rows_per_tile + j
        r0 = jnp.clip(r - NA_ROWS // 2, 0, n_rows - NA_ROWS)
        q = q_ref[0, j * GRID_W:(j + 1) * GRID_W, :] * scale
        qs = _head_stack(q)
        start = pl.multiple_of(r0 * GRID_W, GRID_W)
        ks = k_ref[0, pl.ds(start, span), :]
        vs = v_ref[0, pl.ds(start, span), :]
        s = lax.dot_general(qs, ks, (((1,), (1,)), ((), ())), preferred_element_type=F32)
        s = s + bias_ref[r - r0]
        m = jnp.max(s, axis=-1, keepdims=True)
        e = jnp.exp(s - m)
        inv = 1.0 / jnp.sum(e, axis=-1, keepdims=True)
        o4 = jnp.dot(e.astype(BF16), vs, preferred_element_type=F32) * inv
        o = jnp.zeros((GRID_W, BR), F32)
        for h in range(N_HEADS):
            o = o + jnp.where(head == h, o4[h * GRID_W:(h + 1) * GRID_W, :], 0.0)
        o_ref[0, j * GRID_W:(j + 1) * GRID_W, :] = o.astype(o_ref.dtype)


def _neighbourhood_attention(pa, bias, rows_per_tile=8):
    b, l, _ = pa.shape
    n_rows = l // GRID_W
    tq = rows_per_tile * GRID_W
    assert HEAD_DIM in (16, 64, 256), "the kernel scales bf16 queries by HEAD_DIM ** -0.5, exact for powers of 4"
    kern = functools.partial(_attn_kernel, n_rows=n_rows, rows_per_tile=rows_per_tile)
    return pl.pallas_call(
        kern, grid=(b, l // tq),
        in_specs=[pl.BlockSpec((1, tq, BR), lambda bi, i: (bi, i, 0)),
                  pl.BlockSpec((1, l, BR), lambda bi, i: (bi, 0, 1)),
                  pl.BlockSpec((1, l, BR), lambda bi, i: (bi, 0, 2)),
                  pl.BlockSpec(bias.shape, lambda bi, i: (0, 0, 0))],
        out_specs=pl.BlockSpec((1, tq, BR), lambda bi, i: (bi, i, 0)),
        out_shape=jax.ShapeDtypeStruct((b, l, BR), BF16),
        compiler_params=_cparams("parallel", "arbitrary"), name="nbr_attn",
    )(pa, pa, pa, bias)


def _conv_kernel(xb_ref, gb_ref, gc_ref, xbp_ref, gcp_ref, xbn_ref, gcn_ref, w_ref, o_ref, *, n_tiles):
    i = pl.program_id(1)
    tl = xb_ref.shape[1]
    last = xbp_ref.shape[1] - 1

    def gated(gc, xb):
        return gc.astype(F32) * xb.astype(F32)

    u = gated(gc_ref[0], xb_ref[0])
    u_before = jnp.where(i > 0, gated(gcp_ref[0, last:, :], xbp_ref[0, last:, :]), 0.0)
    u_after = jnp.where(i < n_tiles - 1, gated(gcn_ref[0, 0:1, :], xbn_ref[0, 0:1, :]), 0.0)
    row = lax.broadcasted_iota(jnp.int32, (tl, 1), 0)
    u_prev = jnp.where(row == 0, u_before, pltpu.roll(u, 1, 0))
    u_next = jnp.where(row == tl - 1, u_after, pltpu.roll(u, tl - 1, 0))
    conv = w_ref[0:1, :] * u_prev + w_ref[1:2, :] * u + w_ref[2:3, :] * u_next
    o_ref[0] = (gb_ref[0].astype(F32) * conv).astype(o_ref.dtype)


def _short_conv(pb, conv_w, tl=1024):
    b, l, _ = pb.shape
    n_tiles = l // tl
    halo = 16
    per_tile = tl // halo
    last_halo = l // halo - 1

    def main(col):
        return pl.BlockSpec((1, tl, BR), lambda bi, i: (bi, i, col))

    def before(col):
        return pl.BlockSpec((1, halo, BR), lambda bi, i: (bi, jnp.maximum(i * per_tile - 1, 0), col))

    def after(col):
        return pl.BlockSpec((1, halo, BR), lambda bi, i: (bi, jnp.minimum((i + 1) * per_tile, last_halo), col))

    return pl.pallas_call(
        functools.partial(_conv_kernel, n_tiles=n_tiles), grid=(b, n_tiles),
        in_specs=[main(0), main(1), main(2), before(0), before(2), after(0), after(2),
                  pl.BlockSpec(conv_w.shape, lambda bi, i: (0, 0))],
        out_specs=pl.BlockSpec((1, tl, BR), lambda bi, i: (bi, i, 0)),
        out_shape=jax.ShapeDtypeStruct((b, l, BR), BF16),
        compiler_params=_cparams("parallel", "parallel"), name="short_conv",
    )(pb, pb, pb, pb, pb, pb, pb, conv_w)


def _dft_cos_sin(n):
    ang = 2.0 * np.pi * ((np.arange(n)[:, None] * np.arange(n)[None, :]) % n) / n
    return np.cos(ang).astype(np.float32), np.sin(ang).astype(np.float32)


def _fft_major_kernel(g_ref, m_ref, tc_ref, ts_ref, br_ref, bi_ref):
    for s in range(8):
        g = g_ref[0, :, s, :]
        swapped = jnp.concatenate([g[:, BR:], -g[:, :BR]], axis=1)
        a = jnp.dot(m_ref[...], jnp.concatenate([g, swapped], axis=0), precision=HIGHEST,
                    preferred_element_type=F32)
        ar, ai = a[:, :BR], a[:, BR:]
        tc = tc_ref[:, s, :]
        ts = ts_ref[:, s, :]
        br_ref[0, :, s, :] = ar * tc + ai * ts
        bi_ref[0, :, s, :] = ai * tc - ar * ts


def _fft_minor_kernel(br_ref, bi_ref, m_ref, o_ref, *, scale):
    l2 = br_ref.shape[2]
    group = m_ref.shape[1] // l2
    for j0 in range(0, 8, group):
        rhs = jnp.concatenate([br_ref[0, j0:j0 + group].reshape(group * l2, BR),
                               bi_ref[0, j0:j0 + group].reshape(group * l2, BR)], axis=0)
        y = _dot_split(m_ref[0], m_ref[1], *_split_bf16(rhs)) * scale
        for j in range(group):
            o_ref[0, :, j0 + j, :] = y[j * l2:(j + 1) * l2].astype(o_ref.dtype)


def _fft_twiddles(l):
    l1 = FFT_L1
    l2 = l // l1
    idx = (jnp.arange(l1, dtype=jnp.int32)[:, None] * jnp.arange(l2, dtype=jnp.int32)[None, :]) % l
    ang = idx.astype(F32) * (2.0 * np.pi / l)
    tc = jnp.broadcast_to(jnp.cos(ang)[:, :, None], (l1, l2, BR))
    ts = jnp.broadcast_to(jnp.sin(ang)[:, :, None], (l1, l2, BR))
    return tc, ts


def _fourier_mix(pc, tc, ts):
    b, l, _ = pc.shape
    l1 = FFT_L1
    l2 = l // l1

    c1, s1 = _dft_cos_sin(l1)
    c2, s2 = _dft_cos_sin(l2)
    group = min(max(1, 128 // l2), 8)
    eye = np.eye(group, dtype=np.float32)
    minor = jnp.asarray(np.concatenate([np.kron(eye, c2), np.kron(eye, s2)], axis=1))
    minor_hi = minor.astype(BF16)
    minor_parts = jnp.stack([minor_hi, (minor - minor_hi.astype(F32)).astype(BF16)])

    def strip(width):
        return pl.BlockSpec((1, l1, 8, width), lambda bi, i: (bi, 0, i, 0))

    tspec = pl.BlockSpec((l1, 8, BR), lambda bi, i: (0, i, 0))
    mspec = pl.BlockSpec((l1, 2 * l1), lambda bi, i: (0, 0))
    br, bi_ = pl.pallas_call(
        _fft_major_kernel, grid=(b, l2 // 8),
        in_specs=[strip(2 * BR), mspec, tspec, tspec], out_specs=[strip(BR), strip(BR)],
        out_shape=[jax.ShapeDtypeStruct((b, l1, l2, BR), F32)] * 2,
        compiler_params=_cparams("parallel", "parallel"), name="fft_major",
    )(pc.reshape(b, l1, l2, 2 * BR), np.concatenate([c1, s1], axis=1), tc, ts)

    bspec = pl.BlockSpec((1, 8, l2, BR), lambda bi, i: (bi, i, 0, 0))
    m2spec = pl.BlockSpec(minor_parts.shape, lambda bi, i: (0, 0, 0))
    scale = 1.0 / float(np.sqrt(l * FN_GROUP_DIM))
    y = pl.pallas_call(
        functools.partial(_fft_minor_kernel, scale=scale), grid=(b, l1 // 8),
        in_specs=[bspec, bspec, m2spec],
        out_specs=pl.BlockSpec((1, l2, 8, BR), lambda bi, i: (bi, 0, i, 0)),
        out_shape=jax.ShapeDtypeStruct((b, l2, l1, BR), F32),
        compiler_params=_cparams("parallel", "parallel"), name="fft_minor",
    )(br, bi_, minor_parts)
    return y.reshape(b, l, BR)


def _hgrn_level_masks():
    t = np.arange(HG_CHUNK)
    out = []
    half = 8
    while 2 * half < HG_CHUNK:
        same = (t[:, None] // (2 * half)) == (t[None, :] // (2 * half))
        out.append(np.tile(same, (1, N_HEADS)).astype(np.float32))
        half *= 2
    return np.stack(out)


def _hgrn_kernel(qd_ref, fl_ref, vd_ref, lb_ref, mask_ref, bd_ref, o_ref, st_ref, *, first_layer, chunks_per_tile,
                 rev):
    c = HG_CHUNK

    @pl.when(pl.program_id(1) == 0)
    def _():
        st_ref[...] = jnp.zeros_like(st_ref)

    bd = bd_ref[...]
    bd16 = bd.astype(BF16)
    lb = lb_ref[...]
    row = lax.broadcasted_iota(jnp.int32, (c, 1), 0)
    sub = row % 8
    nt = (((1,), (1,)), ((), ()))

    def roll8(x, j):
        return pltpu.roll(x.reshape(c // 8, 8, BR), j % 8, 1).reshape(c, BR)

    def earlier(x, j):
        return roll8(x, -j if rev else j)

    def later(x, j):
        return roll8(x, j if rev else -j)

    def has_earlier(j):
        return (sub + j <= 7) if rev else (sub >= j)

    def has_later(j):
        return (sub >= j) if rev else (sub + j <= 7)

    for ci in (reversed(range(chunks_per_tile)) if rev else range(chunks_per_tile)):
        sl = slice(ci * c, (ci + 1) * c)
        qd = qd_ref[0, sl, :].astype(F32)
        fl = fl_ref[0, sl, :].astype(F32)
        v = vd_ref[0, sl, :].astype(F32)
        q = qd * _sigmoid(qd)
        if first_layer:
            f = _sigmoid(fl)
            k = _sigmoid(-fl)
        else:
            f = lb + (1.0 - lb) * _sigmoid(fl)
            k = (1.0 - lb) * _sigmoid(-fl)

        o = jnp.dot((q * k).astype(BF16), bd16, preferred_element_type=F32) * v
        g = q * f
        for d in range(1, 8):
            if d > 1:
                g = g * earlier(f, d - 1)
            p = jnp.where(has_earlier(d), g * earlier(k, d), 0.0)
            o = o + jnp.dot(p.astype(BF16), bd16, preferred_element_type=F32) * earlier(v, d)

        qf = f
        for j in (1, 2, 4):
            qf = qf * jnp.where(has_earlier(j), earlier(qf, j), 1.0)
        kf = jnp.where(has_later(1), later(f, 1), 1.0)
        for j in (1, 2, 4):
            kf = kf * jnp.where(has_later(j), later(kf, j), 1.0)

        scores = None
        half = 8
        level = 0
        while half < c:
            late = ((row // half) % 2) == (0 if rev else 1)
            qb = jnp.where(late, q * qf, 0.0).astype(BF16)
            kb = jnp.where(late, 0.0, k * kf).astype(BF16)
            x = lax.dot_general(qb, _head_stack(kb), nt, preferred_element_type=F32)
            if 2 * half < c:
                x = x * mask_ref[level]
            scores = x if scores is None else scores + x
            blocks = c // (2 * half)
            qf3 = qf.reshape(blocks, 2 * half, BR)
            end_early = half if rev else half - 1
            end_late = 0 if rev else 2 * half - 1
            tot_early = jnp.broadcast_to(qf3[:, end_early:end_early + 1, :], qf3.shape).reshape(c, BR)
            tot_late = jnp.broadcast_to(qf3[:, end_late:end_late + 1, :], qf3.shape).reshape(c, BR)
            qf = qf * jnp.where(late, tot_early, 1.0)
            kf = kf * jnp.where(late, 1.0, tot_late)
            half *= 2
            level += 1

        o = o + jnp.dot(scores.astype(BF16), _head_stack(v.astype(BF16)), preferred_element_type=F32)

        st = st_ref[...]
        o = o + lax.dot_general((q * qf).astype(BF16), st.astype(BF16), nt, preferred_element_type=F32)
        o_ref[0, sl, :] = o
        upd = jnp.dot(v.T.astype(BF16), (k * kf).astype(BF16), preferred_element_type=F32)
        last = 0 if rev else c - 1
        st_ref[...] = st * qf[last:last + 1, :] + upd * bd


def _hgrn_scan(qd, fl, vd, cols, lb, first_layer, rev, chunks_per_tile=8):
    b, l, _ = qd.shape
    tile = HG_CHUNK * chunks_per_tile
    n_tiles = l // tile
    masks = _hgrn_level_masks()
    bd = _same_head_matrix()

    def tile_index(i):
        return n_tiles - 1 - i if rev else i

    def seq(col):
        return pl.BlockSpec((1, tile, BR), lambda bi, i: (bi, tile_index(i), col))

    return pl.pallas_call(
        functools.partial(_hgrn_kernel, first_layer=first_layer, chunks_per_tile=chunks_per_tile, rev=rev),
        grid=(b, n_tiles),
        in_specs=[seq(cols[0]), seq(cols[1]), seq(cols[2]),
                  pl.BlockSpec((1, BR), lambda bi, i: (0, 0)),
                  pl.BlockSpec(masks.shape, lambda bi, i: (0, 0, 0)),
                  pl.BlockSpec((BR, BR), lambda bi, i: (0, 0))],
        out_specs=pl.BlockSpec((1, tile, BR), lambda bi, i: (bi, tile_index(i), 0)),
        out_shape=jax.ShapeDtypeStruct((b, l, BR), F32),
        scratch_shapes=[pltpu.VMEM((BR, BR), F32)],
        compiler_params=_cparams("parallel", "arbitrary"),
        name="hgrn_bwd" if rev else "hgrn_fwd",
    )(qd, fl, vd, lb.reshape(1, BR), masks, bd)


def _hgrn2_bidirectional(pd, lb_f, lb_b, first_layer):
    o_f = _hgrn_scan(pd, pd, pd, (0, 1, 3), lb_f, first_layer, rev=False)
    o_b = _hgrn_scan(pd, pd, pd, (0, 2, 3), lb_b, first_layer, rev=True)
    return o_f, o_b


def _route(x, w_parts, bias, tri, run):
    x_hi, x_lo = _split_bf16(x)
    both = jnp.dot(x_hi, w_parts, preferred_element_type=F32)
    cross = jnp.dot(x_lo, w_parts[:, :LANES], preferred_element_type=F32)
    logits = both[:, :LANES] + (cross + both[:, LANES:]) + bias
    lane = lax.broadcasted_iota(jnp.int32, logits.shape, 1)
    big = jnp.int32(LANES)
    is_group = (lane >= N_EXPERTS) & (lane < N_EXPERTS + N_GROUPS)
    lg = jnp.where(is_group, logits, -jnp.inf)
    mg = jnp.max(lg, axis=-1, keepdims=True)
    g_sel = jnp.min(jnp.where(lg == mg, lane, big), axis=-1, keepdims=True) - N_EXPERTS
    p_g = 1.0 / jnp.sum(jnp.exp(lg - mg), axis=-1, keepdims=True)
    in_group = (lane < N_EXPERTS) & ((lane // EXPERTS_PER_GROUP) == g_sel)
    le = jnp.where(in_group, logits, -jnp.inf)
    v1 = jnp.max(le, axis=-1, keepdims=True)
    i1 = jnp.min(jnp.where(le == v1, lane, big), axis=-1, keepdims=True)
    le2 = jnp.where(lane == i1, -jnp.inf, le)
    v2 = jnp.max(le2, axis=-1, keepdims=True)
    i2 = jnp.min(jnp.where(le2 == v2, lane, big), axis=-1, keepdims=True)
    e2 = jnp.exp(v2 - v1)
    w1 = p_g / (1.0 + e2)
    w2 = p_g * e2 / (1.0 + e2)
    hot1 = lane == i1
    hot2 = lane == i2
    hot = (hot1 | hot2).astype(BF16)
    before = jnp.dot(tri, hot, preferred_element_type=F32) + run
    rank1 = jnp.sum(jnp.where(hot1, before, 0.0), axis=-1, keepdims=True).astype(jnp.int32)
    rank2 = jnp.sum(jnp.where(hot2, before, 0.0), axis=-1, keepdims=True).astype(jnp.int32)
    run = run + jnp.sum(hot.astype(F32), axis=0, keepdims=True)
    e = jnp.where(lane == 0, i1, jnp.where(lane == 1, i2, jnp.where(lane == 2, rank1,
                                                                     jnp.where(lane == 3, rank2, 0))))
    p = jnp.where(lane == 0, w1, jnp.where(lane == 1, w2, 0.0))
    return e, p, run


def _merge_kernel(x_ref, ya_ref, yb_ref, yc_ref, of_ref, ob_ref, gd_ref, ng_ref, bd_ref, wg_ref, bg_ref, wbr_ref,
                  wo_ref, g_ref, b_ref, rw_ref, rb_ref, tri_ref, o_ref, o16_ref, e_ref, p_ref, cnt_ref, run_ref):
    @pl.when(pl.program_id(0) == 0)
    def _():
        run_ref[...] = jnp.zeros_like(run_ref)

    od = of_ref[...] + ob_ref[...]
    ms = jnp.dot(od * od, bd_ref[...], precision=HIGHEST, preferred_element_type=F32) * (1.0 / HEAD_DIM)
    gd = gd_ref[...].astype(F32)
    y_d = od * lax.rsqrt(ms + RMS_EPS) * ng_ref[...] * (gd * _sigmoid(gd))

    x = x_ref[...]
    xb = x.astype(BF16)
    merged = None
    for i, y in enumerate((ya_ref[...], yb_ref[...], yc_ref[...], y_d)):
        gate = _sigmoid(jnp.dot(xb, wg_ref[i], preferred_element_type=F32) + bg_ref[i])
        term = gate * jnp.dot(y.astype(BF16), wbr_ref[i], preferred_element_type=F32)
        merged = term if merged is None else merged + term
    mix = jnp.dot(merged.astype(BF16), wo_ref[...], preferred_element_type=F32)
    out = _ln_rows(DEEPNORM_ALPHA * x + mix, g_ref[...], b_ref[...])
    o_ref[...] = out
    o16_ref[...] = _pack_bf16_pairs(out)
    e, p, run = _route(out, rw_ref[...], rb_ref[...], tri_ref[...], run_ref[...])
    e_ref[...] = e.T[:e_ref.shape[0], :]
    p_ref[...] = p
    run_ref[...] = run
    cnt_ref[...] = run.astype(jnp.int32)


def _merge(x, y_a, y_b, y_c, o_f, o_b, pd, norm_g, w_gate, b_gate, w_br, w_out, g, b, wg, bg, we, be, tm=512):
    t, d = x.shape
    row = pl.BlockSpec((tm, d), lambda i: (i, 0))
    yrow = pl.BlockSpec((tm, BR), lambda i: (i, 0))
    vec = pl.BlockSpec((1, d), lambda i: (0, 0))
    lanes = pl.BlockSpec((tm, LANES), lambda i: (i, 0))
    lvec = pl.BlockSpec((1, LANES), lambda i: (0, 0))

    def whole(a):
        return pl.BlockSpec(a.shape, lambda i: (0,) * a.ndim)

    bgate = b_gate.reshape(b_gate.shape[0], 1, d)
    rw = jnp.zeros((d, LANES), F32).at[:, :N_EXPERTS].set(we).at[:, N_EXPERTS:N_EXPERTS + N_GROUPS].set(wg)
    rb = jnp.zeros((1, LANES), F32).at[0, :N_EXPERTS].set(be).at[0, N_EXPERTS:N_EXPERTS + N_GROUPS].set(bg)
    r_parts = jnp.concatenate(_split_bf16(rw), axis=1)
    tri = np.tril(np.ones((tm, tm), np.float32), -1).astype(BF16)
    ng = jnp.tile(norm_g, N_HEADS).reshape(1, BR)
    bd = _same_head_matrix()
    return pl.pallas_call(
        _merge_kernel, grid=(t // tm,),
        in_specs=[row, yrow, yrow, yrow, yrow, yrow, pl.BlockSpec((tm, BR), lambda i: (i, 4)), whole(ng), whole(bd),
                  whole(w_gate), whole(bgate), whole(w_br), whole(w_out), vec, vec,
                  whole(r_parts), lvec, whole(tri)],
        out_specs=[row, pl.BlockSpec((tm, d // 2), lambda i: (i, 0)), pl.BlockSpec((8, tm), lambda i: (0, i)),
                   lanes, lvec],
        out_shape=[jax.ShapeDtypeStruct((t, d), F32), jax.ShapeDtypeStruct((t, d // 2), jnp.uint32),
                   jax.ShapeDtypeStruct((8, t), jnp.int32), jax.ShapeDtypeStruct((t, LANES), F32),
                   jax.ShapeDtypeStruct((1, LANES), jnp.int32)],
        scratch_shapes=[pltpu.VMEM((1, LANES), F32)],
        compiler_params=_cparams("arbitrary"), name="merge_ln1_route",
    )(x, y_a, y_b, y_c, o_f, o_b, pd, ng, bd, w_gate, bgate, w_br, w_out, g.reshape(1, d), b.reshape(1, d),
      r_parts, rb, tri)


def _work_items_kernel(cnt_ref, start_ref, items_ref):
    n_items = items_ref.shape[1]

    def put(w, e, blk, lo, hi, first):
        for row, val in enumerate((e, blk, lo, hi, first)):
            items_ref[row, w] = val

    def expert(e, carry):
        start, w, _, _ = carry
        start_ref[e] = start
        end = start + cnt_ref[e]
        first_blk = start // MOE_ROWS
        n_blk = jnp.where(end > start, (end - 1) // MOE_ROWS - first_blk + 1, 0)

        def block(i, w):
            blk = first_blk + i
            row0 = blk * MOE_ROWS
            lo = jnp.maximum(start, row0) - row0
            first = jnp.where((i > 0) | (lo == 0), 1, 0)
            put(w, e, blk, lo, jnp.minimum(end, row0 + MOE_ROWS) - row0, first)
            return w + 1

        w_end = lax.fori_loop(0, n_blk, block, w)
        owner = jnp.where(n_blk > 0, e, carry[2])
        last_blk = jnp.where(n_blk > 0, first_blk + n_blk - 1, carry[3])
        return end, w_end, owner, last_blk

    zero = jnp.int32(0)
    _, w_live, owner, last_blk = lax.fori_loop(0, N_EXPERTS, expert, (zero, zero, zero, zero))

    def surplus(w, _):
        put(w, owner, last_blk, zero, zero, zero)
        return 0

    lax.fori_loop(w_live, n_items, surplus, 0)


def _expert_work_items(counts, n_blocks):
    smem = pl.BlockSpec(memory_space=pltpu.SMEM)
    return pl.pallas_call(
        _work_items_kernel, in_specs=[smem], out_specs=[smem, smem],
        out_shape=[jax.ShapeDtypeStruct((N_EXPERTS,), jnp.int32),
                   jax.ShapeDtypeStruct((5, n_blocks + N_EXPERTS), jnp.int32)],
        name="moe_work_items",
    )(counts)


def _expert_kernel(items_ref, x_ref, w13_ref, w2_ref, o_ref):
    w = pl.program_id(0)
    lo = items_ref[2, w]
    hi = items_ref[3, w]

    @pl.when(hi > lo)
    def _():
        x = _unpack_bf16_pairs(x_ref[...]).astype(BF16)
        h = jnp.dot(x, w13_ref[0], preferred_element_type=F32)
        hg = h[:, :D_FF_E]
        act = hg * _sigmoid(hg) * h[:, D_FF_E:]
        y = _pack_bf16_pairs(jnp.dot(act.astype(BF16), w2_ref[0], preferred_element_type=F32))
        row = lax.broadcasted_iota(jnp.int32, (MOE_ROWS, 1), 0)
        mine = (row >= lo) & (row < hi)

        @pl.when(items_ref[4, w] == 1)
        def _():
            o_ref[...] = jnp.where(mine, y, jnp.zeros_like(y))

        @pl.when(items_ref[4, w] == 0)
        def _():
            o_ref[...] = jnp.where(mine, y, o_ref[...])


def _expert_blocks(xs, items, w13, w2):
    n_rows, half = xs.shape
    d = 2 * half
    grid_spec = pltpu.PrefetchScalarGridSpec(
        num_scalar_prefetch=1, grid=(items.shape[1],),
        in_specs=[pl.BlockSpec((MOE_ROWS, half), lambda w, it: (it[1, w], 0)),
                  pl.BlockSpec((1, d, 2 * D_FF_E), lambda w, it: (it[0, w], 0, 0)),
                  pl.BlockSpec((1, D_FF_E, d), lambda w, it: (it[0, w], 0, 0))],
        out_specs=pl.BlockSpec((MOE_ROWS, half), lambda w, it: (it[1, w], 0)))
    return pl.pallas_call(
        _expert_kernel, grid_spec=grid_spec, out_shape=jax.ShapeDtypeStruct((n_rows, half), jnp.uint32),
        compiler_params=_cparams("arbitrary"), name="moe_experts",
    )(items, xs, w13, w2)


def _combine_kernel(x_ref, y0_ref, y1_ref, p_ref, g_ref, b_ref, o_ref):
    o_ref[...] = _ln_rows(_moe_residual(x_ref, y0_ref, y1_ref, p_ref), g_ref[...], b_ref[...])


def _combine(x, y01, p, g, b, tm=1024):
    t, d = x.shape
    row = pl.BlockSpec((tm, d), lambda i: (i, 0))
    vec = pl.BlockSpec((1, d), lambda i: (0, 0))
    return pl.pallas_call(
        _combine_kernel, grid=(t // tm,),
        in_specs=[row] + _pair_specs(t, d, tm) + [pl.BlockSpec((tm, LANES), lambda i: (i, 0)), vec, vec],
        out_specs=row, out_shape=jax.ShapeDtypeStruct((t, d), F32),
        compiler_params=_cparams("parallel"), name="moe_combine_ln2",
    )(x, y01, y01, p, g.reshape(1, d), b.reshape(1, d))


def _dest_kernel(start_ref, r_ref, o_ref):
    r = r_ref[...]
    e = r[0:TOP_K, :]
    base = jnp.zeros_like(e)
    for x in range(N_EXPERTS):
        base = jnp.where(e == x, start_ref[x], base)
    pad = jnp.zeros((r.shape[0] - TOP_K, r.shape[1]), jnp.int32)
    o_ref[...] = jnp.concatenate([base + r[TOP_K:2 * TOP_K, :], pad], axis=0)


def _slot_rows(routing, start, tl=8192):
    rows, t = routing.shape
    tl = min(tl, t)
    grid_spec = pltpu.PrefetchScalarGridSpec(
        num_scalar_prefetch=1, grid=(t // tl,),
        in_specs=[pl.BlockSpec((rows, tl), lambda i, st: (0, i))],
        out_specs=pl.BlockSpec((rows, tl), lambda i, st: (0, i)))
    return pl.pallas_call(
        _dest_kernel, grid_spec=grid_spec, out_shape=jax.ShapeDtypeStruct((rows, t), jnp.int32),
        compiler_params=_cparams("parallel"), name="moe_slot_rows",
    )(start, routing)


def _sc_workers():
    info = plsc.get_sparse_core_info()
    return info.num_cores, info.num_cores * info.num_subcores


def _sc_move_rows(src, dest, scatter):
    t = dest.shape[1]
    width = src.shape[1]
    n_cores, n_workers = _sc_workers()
    steps = t // (n_workers * SC_WINDOW)
    assert steps * n_workers * SC_WINDOW == t and steps % 2 == 0, (t, n_workers)
    mesh = plsc.VectorSubcoreMesh(core_axis_name="core", subcore_axis_name="subcore")

    @functools.partial(
        pl.kernel, mesh=mesh, out_type=jax.ShapeDtypeStruct((TOP_K * t, width), src.dtype),
        scratch_types=[pltpu.VMEM((TOP_K, steps, SC_WINDOW), jnp.int32),
                       pltpu.VMEM((2, SC_WINDOW, width), src.dtype),
                       pltpu.SemaphoreType.DMA, pltpu.SemaphoreType.DMA])
    def move(src_hbm, idx_hbm, out_hbm, idx_v, rows_v, sem0, sem1):
        wid = lax.axis_index("subcore") * n_cores + lax.axis_index("core")
        for k in range(TOP_K):
            pltpu.sync_copy(idx_hbm.at[k, wid], idx_v.at[k])
        sems = (sem0, sem1)

        def window(j):
            return pl.ds((wid * steps + j) * SC_WINDOW, SC_WINDOW)

        def run(fetch, flush):
            fetch(0, 0).start()

            @pl.loop(0, steps, step=2)
            def _(j):
                fetch(j + 1, 1).start()
                fetch(j, 0).wait()
                flush(j, 0)

                @pl.when(j + 2 < steps)
                def _():
                    fetch(j + 2, 0).start()

                fetch(j + 1, 1).wait()
                flush(j + 1, 1)

        if scatter:
            def fetch(j, slot):
                return pltpu.make_async_copy(src_hbm.at[window(j)], rows_v.at[slot], sems[slot])

            def flush(j, slot):
                for k in range(TOP_K):
                    pltpu.sync_copy(rows_v.at[slot], out_hbm.at[idx_v.at[k, j]])

            run(fetch, flush)
        else:
            for k in range(TOP_K):
                def fetch(j, slot, k=k):
                    return pltpu.make_async_copy(src_hbm.at[idx_v.at[k, j]], rows_v.at[slot], sems[slot])

                def flush(j, slot, k=k):
                    pltpu.sync_copy(rows_v.at[slot], out_hbm.at[pl.ds(k * t + (wid * steps + j) * SC_WINDOW,
                                                                      SC_WINDOW)])

                run(fetch, flush)

    return move(src, dest.reshape(dest.shape[0], n_workers, steps, SC_WINDOW))


def _expert_outputs(x16, routing, cnt, w13, w2):
    t = x16.shape[0]
    start, items = _expert_work_items(cnt[0, :N_EXPERTS], t * TOP_K // MOE_ROWS)
    dest = _slot_rows(routing, start)
    xs = _sc_move_rows(x16, dest, scatter=True)
    ys = _expert_blocks(xs, items, w13, w2)
    return _sc_move_rows(ys, dest, scatter=False)


def _trunk(x, prm):
    b, l, d = x.shape
    t = b * l
    x = x.reshape(t, d)
    moe, g, bias = None, prm['ln_in_g'], prm['ln_in_b']
    tc, ts = _fft_twiddles(l)
    for i in range(DEPTH):
        x, pa, pb, pc, pd = _input_projection(x, moe, g, bias, prm['w_in'][i])
        y_a = _neighbourhood_attention(pa.reshape(b, l, -1), prm['attn_bias'][i])
        y_b = _short_conv(pb.reshape(b, l, -1), prm['conv_w'][i])
        y_c = _fourier_mix(pc.reshape(b, l, -1), tc, ts)
        o_f, o_b = _hgrn2_bidirectional(pd.reshape(b, l, -1), prm['lb'][0, i], prm['lb'][1, i], i == 0)
        x, x16, routing, p_lanes, cnt = _merge(
            x, y_a.reshape(t, BR), y_b.reshape(t, BR), y_c.reshape(t, BR), o_f.reshape(t, BR), o_b.reshape(t, BR),
            pd, prm['hgrn_norm_g'][i], prm['w_gate'][i], prm['b_gate'][i], prm['w_br'][i], prm['w_out'][i],
            prm['ln1_g'][i], prm['ln1_b'][i], prm['router_g_w'][i], prm['router_g_b'][i], prm['router_e_w'][i],
            prm['router_e_b'][i])
        y01 = _expert_outputs(x16, routing, cnt, prm['w13'][i], prm['w2'][i])
        moe, g, bias = (y01, p_lanes), prm['ln2_g'][i], prm['ln2_b'][i]
    return _combine(x, *moe, g, bias).reshape(b, l, d)


def kernel(x_prompt, x_sample, ln_in_g, ln_in_b, w_in, na_rpb, conv_w, hgrn_lb, hgrn_norm_g, w_gate, b_gate, w_br,
           w_out, ln1_g, ln1_b, router_g_w, router_g_b, router_e_w, router_e_b, w13, w2, ln2_g, ln2_b):
    lb = jnp.cumsum(jax.nn.softmax(hgrn_lb.astype(F32), axis=1), axis=1)
    lb = lb - lb[:, :1]
    prm = dict(
        ln_in_g=ln_in_g, ln_in_b=ln_in_b,
        w_in=jnp.stack([_fold_channel_dft(w_in[i]) for i in range(DEPTH)]).astype(BF16),
        attn_bias=jnp.stack([_attn_bias_table(na_rpb[i]) for i in range(DEPTH)]),
        conv_w=conv_w, lb=lb, hgrn_norm_g=hgrn_norm_g,
        w_gate=w_gate.astype(BF16), b_gate=b_gate, w_br=w_br.astype(BF16), w_out=w_out.astype(BF16),
        ln1_g=ln1_g, ln1_b=ln1_b, router_g_w=router_g_w, router_g_b=router_g_b,
        router_e_w=router_e_w, router_e_b=router_e_b, w13=w13.astype(BF16), w2=w2.astype(BF16),
        ln2_g=ln2_g, ln2_b=ln2_b)
    return _trunk(x_prompt, prm), _trunk(x_sample, prm)
```

```python
import functools

import numpy as np
import jax
import jax.numpy as jnp
from jax import lax
from jax.experimental import pallas as pl
from jax.experimental.pallas import tpu as pltpu
from jax.experimental.pallas import tpu_sc as plsc

F32 = jnp.float32
BF16 = jnp.bfloat16
HIGHEST = lax.Precision.HIGHEST

D_MODEL = 1024
DEPTH = 4
GRID_W = 64
BR = 256
N_HEADS = 4
HEAD_DIM = 64
NA_ROWS = 8
NA_COLS = 16
FN_GROUP_DIM = 64
N_GROUPS = 4
EXPERTS_PER_GROUP = 8
N_EXPERTS = N_GROUPS * EXPERTS_PER_GROUP
TOP_K = 2
D_FF_E = 512
DEEPNORM_ALPHA = (2 * DEPTH) ** 0.25
LN_EPS = 1e-5
RMS_EPS = 1e-6
NEG_INF = -1e30

LANES = 128
VMEM_LIMIT_BYTES = 56 * 1024 * 1024
MOE_ROWS = 512
HG_CHUNK = 128
FFT_L1 = 64
SC_WINDOW = 64

COL_QA, COL_KA, COL_VA, COL_XB, COL_GB, COL_GC, COL_XF, COL_QD, COL_FF, COL_FB, COL_VD, COL_GD = range(12)
PROJ_GROUPS = ((0, 3), (3, 6), (6, 8), (8, 13))


def _cparams(*sem):
    return pltpu.CompilerParams(dimension_semantics=sem, vmem_limit_bytes=VMEM_LIMIT_BYTES)


def _sigmoid(x):
    return 0.5 * jnp.tanh(0.5 * x) + 0.5


def _split_bf16(x):
    hi = x.astype(BF16)
    return hi, (x - hi.astype(F32)).astype(BF16)


def _dot_split(a_hi, a_lo, b_hi, b_lo):
    dot = functools.partial(jnp.dot, preferred_element_type=F32)
    return dot(a_hi, b_hi) + (dot(a_lo, b_hi) + dot(a_hi, b_lo))


def _pack_bf16_pairs(x):
    m = x.shape[1] // 2
    hi = lax.bitcast_convert_type(x[:, :m].astype(BF16).astype(F32), jnp.uint32)
    lo = lax.bitcast_convert_type(x[:, m:].astype(BF16).astype(F32), jnp.uint32)
    return hi | (lo >> 16)


def _unpack_bf16_pairs(u):
    hi = lax.bitcast_convert_type(u & jnp.uint32(0xFFFF0000), F32)
    lo = lax.bitcast_convert_type(u << 16, F32)
    return jnp.concatenate([hi, lo], axis=1)


def _ln_rows(x, g, b):
    mu = jnp.mean(x, axis=-1, keepdims=True)
    xc = x - mu
    var = jnp.mean(xc * xc, axis=-1, keepdims=True)
    return xc * lax.rsqrt(var + LN_EPS) * g + b


def _head_of_lane(shape):
    return lax.broadcasted_iota(jnp.int32, shape, len(shape) - 1) // HEAD_DIM


def _head_stack(x):
    head = _head_of_lane(x.shape)
    return jnp.concatenate([jnp.where(head == h, x, jnp.zeros_like(x)) for h in range(N_HEADS)], axis=0)


def _same_head_matrix():
    h = np.arange(BR) // HEAD_DIM
    return (h[:, None] == h[None, :]).astype(np.float32)


def _moe_residual(x_ref, y0_ref, y1_ref, p_ref):
    p = p_ref[...]
    ffn = p[:, 0:1] * _unpack_bf16_pairs(y0_ref[...]) + p[:, 1:2] * _unpack_bf16_pairs(y1_ref[...])
    return DEEPNORM_ALPHA * x_ref[...] + ffn


def _proj_kernel(*refs, after_moe):
    if after_moe:
        x_ref, y0_ref, y1_ref, p_ref, g_ref, b_ref, w_ref, xo_ref, *out_refs = refs
        pre = _moe_residual(x_ref, y0_ref, y1_ref, p_ref)
    else:
        x_ref, g_ref, b_ref, w_ref, xo_ref, *out_refs = refs
        pre = x_ref[...]
    x = _ln_rows(pre, g_ref[...], b_ref[...])
    xo_ref[...] = x
    xb = x.astype(BF16)
    for ref, (lo, hi) in zip(out_refs, PROJ_GROUPS):
        ref[...] = jnp.dot(xb, w_ref[:, lo * BR:hi * BR], preferred_element_type=F32).astype(ref.dtype)


def _fold_channel_dft(w_in):
    cg, sg = _dft_cos_sin(FN_GROUP_DIM)
    eye = np.eye(BR // FN_GROUP_DIM, dtype=np.float32)
    w_xf = w_in[:, COL_XF * BR:(COL_XF + 1) * BR]
    w_re = jnp.dot(w_xf, np.kron(eye, cg), precision=HIGHEST)
    w_im = -jnp.dot(w_xf, np.kron(eye, sg), precision=HIGHEST)
    return jnp.concatenate([w_in[:, :COL_XF * BR], w_re, w_im, w_in[:, (COL_XF + 1) * BR:]], axis=1)


def _pair_specs(t, d, tm):
    return [pl.BlockSpec((tm, d // 2), lambda i: (i, 0)), pl.BlockSpec((tm, d // 2), lambda i: (i + t // tm, 0))]


def _input_projection(x, moe, g, b, w_in, tm=512):
    t, d = x.shape
    n = w_in.shape[1]
    widths = (d,) + tuple((hi - lo) * BR for lo, hi in PROJ_GROUPS)
    dtypes = (F32, BF16, BF16, F32, BF16)
    row = pl.BlockSpec((tm, d), lambda i: (i, 0))
    vec = pl.BlockSpec((1, d), lambda i: (0, 0))
    acts, act_specs = [x], [row]
    if moe is not None:
        y01, p = moe
        acts += [y01, y01, p]
        act_specs += _pair_specs(t, d, tm) + [pl.BlockSpec((tm, LANES), lambda i: (i, 0))]
    return pl.pallas_call(
        functools.partial(_proj_kernel, after_moe=moe is not None), grid=(t // tm,),
        in_specs=act_specs + [vec, vec, pl.BlockSpec((d, n), lambda i: (0, 0))],
        out_specs=[pl.BlockSpec((tm, w), lambda i: (i, 0)) for w in widths],
        out_shape=[jax.ShapeDtypeStruct((t, w), dt) for w, dt in zip(widths, dtypes)],
        compiler_params=_cparams("parallel"), name="ln_in_proj",
    )(*acts, g.reshape(1, d), b.reshape(1, d), w_in)


def _attn_bias_table(rpb):
    c = np.arange(GRID_W)
    win_c0 = np.clip(c - NA_COLS // 2, 0, GRID_W - NA_COLS)
    valid = (c[None, :] >= win_c0[:, None]) & (c[None, :] < win_c0[:, None] + NA_COLS)
    dc = np.clip(c[None, :] - c[:, None], -(NA_COLS - 1), NA_COLS - 1) + NA_COLS - 1
    dr = np.arange(NA_ROWS)[None, :] - np.arange(NA_ROWS)[:, None] + NA_ROWS - 1
    row_hot = (dr[:, :, None] == np.arange(2 * NA_ROWS - 1)).astype(np.float32)
    col_hot = (dc[:, :, None] == np.arange(2 * NA_COLS - 1)).astype(np.float32)
    t = jnp.einsum('oia,hab,qkb->ohqik', row_hot, rpb.astype(F32), col_hot, precision=HIGHEST)
    t = jnp.where(valid[None, None, :, None, :], t, NEG_INF)
    return t.reshape(NA_ROWS, N_HEADS * GRID_W, NA_ROWS * GRID_W)


def _attn_kernel(q_ref, k_ref, v_ref, bias_ref, o_ref, *, n_rows, rows_per_tile):
    i = pl.program_id(1)
    head = _head_of_lane((GRID_W, BR))
    scale = HEAD_DIM ** -0.5
    span = NA_ROWS * GRID_W
    for j in range(rows_per_tile):
        r = i * rows_per_tile + j
        r0 = jnp.clip(r - NA_ROWS // 2, 0, n_rows - NA_ROWS)
        q = q_ref[0, j * GRID_W:(j + 1) * GRID_W, :] * scale
        qs = _head_stack(q)
        start = pl.multiple_of(r0 * GRID_W, GRID_W)
        ks = k_ref[0, pl.ds(start, span), :]
        vs = v_ref[0, pl.ds(start, span), :]
        s = lax.dot_general(qs, ks, (((1,), (1,)), ((), ())), preferred_element_type=F32)
        s = s + bias_ref[r - r0]
        m = jnp.max(s, axis=-1, keepdims=True)
        e = jnp.exp(s - m)
        inv = 1.0 / jnp.sum(e, axis=-1, keepdims=True)
        o4 = jnp.dot(e.astype(BF16), vs, preferred_element_type=F32) * inv
        o = jnp.zeros((GRID_W, BR), F32)
        for h in range(N_HEADS):
            o = o + jnp.where(head == h, o4[h * GRID_W:(h + 1) * GRID_W, :], 0.0)
        o_ref[0, j * GRID_W:(j + 1) * GRID_W, :] = o.astype(o_ref.dtype)


def _neighbourhood_attention(pa, bias, rows_per_tile=8):
    b, l, _ = pa.shape
    n_rows = l // GRID_W
    tq = rows_per_tile * GRID_W
    assert HEAD_DIM in (16, 64, 256), "the kernel scales bf16 queries by HEAD_DIM ** -0.5, exact for powers of 4"
    kern = functools.partial(_attn_kernel, n_rows=n_rows, rows_per_tile=rows_per_tile)
    return pl.pallas_call(
        kern, grid=(b, l // tq),
        in_specs=[pl.BlockSpec((1, tq, BR), lambda bi, i: (bi, i, 0)),
                  pl.BlockSpec((1, l, BR), lambda bi, i: (bi, 0, 1)),
                  pl.BlockSpec((1, l, BR), lambda bi, i: (bi, 0, 2)),
                  pl.BlockSpec(bias.shape, lambda bi, i: (0, 0, 0))],
        out_specs=pl.BlockSpec((1, tq, BR), lambda bi, i: (bi, i, 0)),
        out_shape=jax.ShapeDtypeStruct((b, l, BR), BF16),
        compiler_params=_cparams("parallel", "arbitrary"), name="nbr_attn",
    )(pa, pa, pa, bias)


def _conv_kernel(xb_ref, gb_ref, gc_ref, xbp_ref, gcp_ref, xbn_ref, gcn_ref, w_ref, o_ref, *, n_tiles):
    i = pl.program_id(1)
    tl = xb_ref.shape[1]
    last = xbp_ref.shape[1] - 1

    def gated(gc, xb):
        return gc.astype(F32) * xb.astype(F32)

    u = gated(gc_ref[0], xb_ref[0])
    u_before = jnp.where(i > 0, gated(gcp_ref[0, last:, :], xbp_ref[0, last:, :]), 0.0)
    u_after = jnp.where(i < n_tiles - 1, gated(gcn_ref[0, 0:1, :], xbn_ref[0, 0:1, :]), 0.0)
    row = lax.broadcasted_iota(jnp.int32, (tl, 1), 0)
    u_prev = jnp.where(row == 0, u_before, pltpu.roll(u, 1, 0))
    u_next = jnp.where(row == tl - 1, u_after, pltpu.roll(u, tl - 1, 0))
    conv = w_ref[0:1, :] * u_prev + w_ref[1:2, :] * u + w_ref[2:3, :] * u_next
    o_ref[0] = (gb_ref[0].astype(F32) * conv).astype(o_ref.dtype)


def _short_conv(pb, conv_w, tl=1024):
    b, l, _ = pb.shape
    n_tiles = l // tl
    halo = 16
    per_tile = tl // halo
    last_halo = l // halo - 1

    def main(col):
        return pl.BlockSpec((1, tl, BR), lambda bi, i: (bi, i, col))

    def before(col):
        return pl.BlockSpec((1, halo, BR), lambda bi, i: (bi, jnp.maximum(i * per_tile - 1, 0), col))

    def after(col):
        return pl.BlockSpec((1, halo, BR), lambda bi, i: (bi, jnp.minimum((i + 1) * per_tile, last_halo), col))

    return pl.pallas_call(
        functools.partial(_conv_kernel, n_tiles=n_tiles), grid=(b, n_tiles),
        in_specs=[main(0), main(1), main(2), before(0), before(2), after(0), after(2),
                  pl.BlockSpec(conv_w.shape, lambda bi, i: (0, 0))],
        out_specs=pl.BlockSpec((1, tl, BR), lambda bi, i: (bi, i, 0)),
        out_shape=jax.ShapeDtypeStruct((b, l, BR), BF16),
        compiler_params=_cparams("parallel", "parallel"), name="short_conv",
    )(pb, pb, pb, pb, pb, pb, pb, conv_w)


def _dft_cos_sin(n):
    ang = 2.0 * np.pi * ((np.arange(n)[:, None] * np.arange(n)[None, :]) % n) / n
    return np.cos(ang).astype(np.float32), np.sin(ang).astype(np.float32)


def _fft_major_kernel(g_ref, m_ref, tc_ref, ts_ref, br_ref, bi_ref):
    for s in range(8):
        g = g_ref[0, :, s, :]
        swapped = jnp.concatenate([g[:, BR:], -g[:, :BR]], axis=1)
        a = jnp.dot(m_ref[...], jnp.concatenate([g, swapped], axis=0), precision=HIGHEST,
                    preferred_element_type=F32)
        ar, ai = a[:, :BR], a[:, BR:]
        tc = tc_ref[:, s, :]
        ts = ts_ref[:, s, :]
        br_ref[0, :, s, :] = ar * tc + ai * ts
        bi_ref[0, :, s, :] = ai * tc - ar * ts


def _fft_minor_kernel(br_ref, bi_ref, m_ref, o_ref, *, scale):
    l2 = br_ref.shape[2]
    group = m_ref.shape[1] // l2
    for j0 in range(0, 8, group):
        rhs = jnp.concatenate([br_ref[0, j0:j0 + group].reshape(group * l2, BR),
                               bi_ref[0, j0:j0 + group].reshape(group * l2, BR)], axis=0)
        y = _dot_split(m_ref[0], m_ref[1], *_split_bf16(rhs)) * scale
        for j in range(group):
            o_ref[0, :, j0 + j, :] = y[j * l2:(j + 1) * l2].astype(o_ref.dtype)


def _fft_twiddles(l):
    l1 = FFT_L1
    l2 = l // l1
    idx = (jnp.arange(l1, dtype=jnp.int32)[:, None] * jnp.arange(l2, dtype=jnp.int32)[None, :]) % l
    ang = idx.astype(F32) * (2.0 * np.pi / l)
    tc = jnp.broadcast_to(jnp.cos(ang)[:, :, None], (l1, l2, BR))
    ts = jnp.broadcast_to(jnp.sin(ang)[:, :, None], (l1, l2, BR))
    return tc, ts


def _fourier_mix(pc, tc, ts):
    b, l, _ = pc.shape
    l1 = FFT_L1
    l2 = l // l1

    c1, s1 = _dft_cos_sin(l1)
    c2, s2 = _dft_cos_sin(l2)
    group = min(max(1, 128 // l2), 8)
    eye = np.eye(group, dtype=np.float32)
    minor = jnp.asarray(np.concatenate([np.kron(eye, c2), np.kron(eye, s2)], axis=1))
    minor_hi = minor.astype(BF16)
    minor_parts = jnp.stack([minor_hi, (minor - minor_hi.astype(F32)).astype(BF16)])

    def strip(width):
        return pl.BlockSpec((1, l1, 8, width), lambda bi, i: (bi, 0, i, 0))

    tspec = pl.BlockSpec((l1, 8, BR), lambda bi, i: (0, i, 0))
    mspec = pl.BlockSpec((l1, 2 * l1), lambda bi, i: (0, 0))
    br, bi_ = pl.pallas_call(
        _fft_major_kernel, grid=(b, l2 // 8),
        in_specs=[strip(2 * BR), mspec, tspec, tspec], out_specs=[strip(BR), strip(BR)],
        out_shape=[jax.ShapeDtypeStruct((b, l1, l2, BR), F32)] * 2,
        compiler_params=_cparams("parallel", "parallel"), name="fft_major",
    )(pc.reshape(b, l1, l2, 2 * BR), np.concatenate([c1, s1], axis=1), tc, ts)

    bspec = pl.BlockSpec((1, 8, l2, BR), lambda bi, i: (bi, i, 0, 0))
    m2spec = pl.BlockSpec(minor_parts.shape, lambda bi, i: (0, 0, 0))
    scale = 1.0 / float(np.sqrt(l * FN_GROUP_DIM))
    y = pl.pallas_call(
        functools.partial(_fft_minor_kernel, scale=scale), grid=(b, l1 // 8),
        in_specs=[bspec, bspec, m2spec],
        out_specs=pl.BlockSpec((1, l2, 8, BR), lambda bi, i: (bi, 0, i, 0)),
        out_shape=jax.ShapeDtypeStruct((b, l2, l1, BR), F32),
        compiler_params=_cparams("parallel", "parallel"), name="fft_minor",
    )(br, bi_, minor_parts)
    return y.reshape(b, l, BR)


def _hgrn_level_masks():
    t = np.arange(HG_CHUNK)
    out = []
    half = 8
    while 2 * half < HG_CHUNK:
        same = (t[:, None] // (2 * half)) == (t[None, :] // (2 * half))
        out.append(np.tile(same, (1, N_HEADS)).astype(np.float32))
        half *= 2
    return np.stack(out)


def _hgrn_kernel(qd_ref, fl_ref, vd_ref, lb_ref, mask_ref, bd_ref, o_ref, st_ref, *, first_layer, chunks_per_tile,
                 rev):
    c = HG_CHUNK

    @pl.when(pl.program_id(1) == 0)
    def _():
        st_ref[...] = jnp.zeros_like(st_ref)

    bd = bd_ref[...]
    bd16 = bd.astype(BF16)
    lb = lb_ref[...]
    row = lax.broadcasted_iota(jnp.int32, (c, 1), 0)
    sub = row % 8
    nt = (((1,), (1,)), ((), ()))

    def roll8(x, j):
        return pltpu.roll(x.reshape(c // 8, 8, BR), j % 8, 1).reshape(c, BR)

    def earlier(x, j):
        return roll8(x, -j if rev else j)

    def later(x, j):
        return roll8(x, j if rev else -j)

    def has_earlier(j):
        return (sub + j <= 7) if rev else (sub >= j)

    def has_later(j):
        return (sub >= j) if rev else (sub + j <= 7)

    for ci in (reversed(range(chunks_per_tile)) if rev else range(chunks_per_tile)):
        sl = slice(ci * c, (ci + 1) * c)
        qd = qd_ref[0, sl, :].astype(F32)
        fl = fl_ref[0, sl, :].astype(F32)
        v = vd_ref[0, sl, :].astype(F32)
        q = qd * _sigmoid(qd)
        if first_layer:
            f = _sigmoid(fl)
            k = _sigmoid(-fl)
        else:
            f = lb + (1.0 - lb) * _sigmoid(fl)
            k = (1.0 - lb) * _sigmoid(-fl)

        o = jnp.dot((q * k).astype(BF16), bd16, preferred_element_type=F32) * v
        g = q * f
        for d in range(1, 8):
            if d > 1:
                g = g * earlier(f, d - 1)
            p = jnp.where(has_earlier(d), g * earlier(k, d), 0.0)
            o = o + jnp.dot(p.astype(BF16), bd16, preferred_element_type=F32) * earlier(v, d)

        qf = f
        for j in (1, 2, 4):
            qf = qf * jnp.where(has_earlier(j), earlier(qf, j), 1.0)
        kf = jnp.where(has_later(1), later(f, 1), 1.0)
        for j in (1, 2, 4):
            kf = kf * jnp.where(has_later(j), later(kf, j), 1.0)

        scores = None
        half = 8
        level = 0
        while half < c:
            late = ((row // half) % 2) == (0 if rev else 1)
            qb = jnp.where(late, q * qf, 0.0).astype(BF16)
            kb = jnp.where(late, 0.0, k * kf).astype(BF16)
            x = lax.dot_general(qb, _head_stack(kb), nt, preferred_element_type=F32)
            if 2 * half < c:
                x = x * mask_ref[level]
            scores = x if scores is None else scores + x
            blocks = c // (2 * half)
            qf3 = qf.reshape(blocks, 2 * half, BR)
            end_early = half if rev else half - 1
            end_late = 0 if rev else 2 * half - 1
            tot_early = jnp.broadcast_to(qf3[:, end_early:end_early + 1, :], qf3.shape).reshape(c, BR)
            tot_late = jnp.broadcast_to(qf3[:, end_late:end_late + 1, :], qf3.shape).reshape(c, BR)
            qf = qf * jnp.where(late, tot_early, 1.0)
            kf = kf * jnp.where(late, 1.0, tot_late)
            half *= 2
            level += 1

        o = o + jnp.dot(scores.astype(BF16), _head_stack(v.astype(BF16)), preferred_element_type=F32)

        st = st_ref[...]
        o = o + lax.dot_general((q * qf).astype(BF16), st.astype(BF16), nt, preferred_element_type=F32)
        o_ref[0, sl, :] = o
        upd = jnp.dot(v.T.astype(BF16), (k * kf).astype(BF16), preferred_element_type=F32)
        last = 0 if rev else c - 1
        st_ref[...] = st * qf[last:last + 1, :] + upd * bd


def _hgrn_scan(qd, fl, vd, cols, lb, first_layer, rev, chunks_per_tile=8):
    b, l, _ = qd.shape
    tile = HG_CHUNK * chunks_per_tile
    n_tiles = l // tile
    masks = _hgrn_level_masks()
    bd = _same_head_matrix()

    def tile_index(i):
        return n_tiles - 1 - i if rev else i

    def seq(col):
        return pl.BlockSpec((1, tile, BR), lambda bi, i: (bi, tile_index(i), col))

    return pl.pallas_call(
        functools.partial(_hgrn_kernel, first_layer=first_layer, chunks_per_tile=chunks_per_tile, rev=rev),
        grid=(b, n_tiles),
        in_specs=[seq(cols[0]), seq(cols[1]), seq(cols[2]),
                  pl.BlockSpec((1, BR), lambda bi, i: (0, 0)),
                  pl.BlockSpec(masks.shape, lambda bi, i: (0, 0, 0)),
                  pl.BlockSpec((BR, BR), lambda bi, i: (0, 0))],
        out_specs=pl.BlockSpec((1, tile, BR), lambda bi, i: (bi, tile_index(i), 0)),
        out_shape=jax.ShapeDtypeStruct((b, l, BR), F32),
        scratch_shapes=[pltpu.VMEM((BR, BR), F32)],
        compiler_params=_cparams("parallel", "arbitrary"),
        name="hgrn_bwd" if rev else "hgrn_fwd",
    )(qd, fl, vd, lb.reshape(1, BR), masks, bd)


def _hgrn2_bidirectional(pd, lb_f, lb_b, first_layer):
    o_f = _hgrn_scan(pd, pd, pd, (0, 1, 3), lb_f, first_layer, rev=False)
    o_b = _hgrn_scan(pd, pd, pd, (0, 2, 3), lb_b, first_layer, rev=True)
    return o_f, o_b


def _route(x, w_parts, bias, tri, run):
    x_hi, x_lo = _split_bf16(x)
    both = jnp.dot(x_hi, w_parts, preferred_element_type=F32)
    cross = jnp.dot(x_lo, w_parts[:, :LANES], preferred_element_type=F32)
    logits = both[:, :LANES] + (cross + both[:, LANES:]) + bias
    lane = lax.broadcasted_iota(jnp.int32, logits.shape, 1)
    big = jnp.int32(LANES)
    is_group = (lane >= N_EXPERTS) & (lane < N_EXPERTS + N_GROUPS)
    lg = jnp.where(is_group, logits, -jnp.inf)
    mg = jnp.max(lg, axis=-1, keepdims=True)
    g_sel = jnp.min(jnp.where(lg == mg, lane, big), axis=-1, keepdims=True) - N_EXPERTS
    p_g = 1.0 / jnp.sum(jnp.exp(lg - mg), axis=-1, keepdims=True)
    in_group = (lane < N_EXPERTS) & ((lane // EXPERTS_PER_GROUP) == g_sel)
    le = jnp.where(in_group, logits, -jnp.inf)
    v1 = jnp.max(le, axis=-1, keepdims=True)
    i1 = jnp.min(jnp.where(le == v1, lane, big), axis=-1, keepdims=True)
    le2 = jnp.where(lane == i1, -jnp.inf, le)
    v2 = jnp.max(le2, axis=-1, keepdims=True)
    i2 = jnp.min(jnp.where(le2 == v2, lane, big), axis=-1, keepdims=True)
    e2 = jnp.exp(v2 - v1)
    w1 = p_g / (1.0 + e2)
    w2 = p_g * e2 / (1.0 + e2)
    hot1 = lane == i1
    hot2 = lane == i2
    hot = (hot1 | hot2).astype(BF16)
    before = jnp.dot(tri, hot, preferred_element_type=F32) + run
    rank1 = jnp.sum(jnp.where(hot1, before, 0.0), axis=-1, keepdims=True).astype(jnp.int32)
    rank2 = jnp.sum(jnp.where(hot2, before, 0.0), axis=-1, keepdims=True).astype(jnp.int32)
    run = run + jnp.sum(hot.astype(F32), axis=0, keepdims=True)
    e = jnp.where(lane == 0, i1, jnp.where(lane == 1, i2, jnp.where(lane == 2, rank1,
                                                                     jnp.where(lane == 3, rank2, 0))))
    p = jnp.where(lane == 0, w1, jnp.where(lane == 1, w2, 0.0))
    return e, p, run


def _merge_kernel(x_ref, ya_ref, yb_ref, yc_ref, of_ref, ob_ref, gd_ref, ng_ref, bd_ref, wg_ref, bg_ref, wbr_ref,
                  wo_ref, g_ref, b_ref, rw_ref, rb_ref, tri_ref, o_ref, o16_ref, e_ref, p_ref, cnt_ref, run_ref):
    @pl.when(pl.program_id(0) == 0)
    def _():
        run_ref[...] = jnp.zeros_like(run_ref)

    od = of_ref[...] + ob_ref[...]
    ms = jnp.dot(od * od, bd_ref[...], precision=HIGHEST, preferred_element_type=F32) * (1.0 / HEAD_DIM)
    gd = gd_ref[...].astype(F32)
    y_d = od * lax.rsqrt(ms + RMS_EPS) * ng_ref[...] * (gd * _sigmoid(gd))

    x = x_ref[...]
    xb = x.astype(BF16)
    merged = None
    for i, y in enumerate((ya_ref[...], yb_ref[...], yc_ref[...], y_d)):
        gate = _sigmoid(jnp.dot(xb, wg_ref[i], preferred_element_type=F32) + bg_ref[i])
        term = gate * jnp.dot(y.astype(BF16), wbr_ref[i], preferred_element_type=F32)
        merged = term if merged is None else merged + term
    mix = jnp.dot(merged.astype(BF16), wo_ref[...], preferred_element_type=F32)
    out = _ln_rows(DEEPNORM_ALPHA * x + mix, g_ref[...], b_ref[...])
    o_ref[...] = out
    o16_ref[...] = _pack_bf16_pairs(out)
    e, p, run = _route(out, rw_ref[...], rb_ref[...], tri_ref[...], run_ref[...])
    e_ref[...] = e.T[:e_ref.shape[0], :]
    p_ref[...] = p
    run_ref[...] = run
    cnt_ref[...] = run.astype(jnp.int32)


def _merge(x, y_a, y_b, y_c, o_f, o_b, pd, norm_g, w_gate, b_gate, w_br, w_out, g, b, wg, bg, we, be, tm=512):
    t, d = x.shape
    row = pl.BlockSpec((tm, d), lambda i: (i, 0))
    yrow = pl.BlockSpec((tm, BR), lambda i: (i, 0))
    vec = pl.BlockSpec((1, d), lambda i: (0, 0))
    lanes = pl.BlockSpec((tm, LANES), lambda i: (i, 0))
    lvec = pl.BlockSpec((1, LANES), lambda i: (0, 0))

    def whole(a):
        return pl.BlockSpec(a.shape, lambda i: (0,) * a.ndim)

    bgate = b_gate.reshape(b_gate.shape[0], 1, d)
    rw = jnp.zeros((d, LANES), F32).at[:, :N_EXPERTS].set(we).at[:, N_EXPERTS:N_EXPERTS + N_GROUPS].set(wg)
    rb = jnp.zeros((1, LANES), F32).at[0, :N_EXPERTS].set(be).at[0, N_EXPERTS:N_EXPERTS + N_GROUPS].set(bg)
    r_parts = jnp.concatenate(_split_bf16(rw), axis=1)
    tri = np.tril(np.ones((tm, tm), np.float32), -1).astype(BF16)
    ng = jnp.tile(norm_g, N_HEADS).reshape(1, BR)
    bd = _same_head_matrix()
    return pl.pallas_call(
        _merge_kernel, grid=(t // tm,),
        in_specs=[row, yrow, yrow, yrow, yrow, yrow, pl.BlockSpec((tm, BR), lambda i: (i, 4)), whole(ng), whole(bd),
                  whole(w_gate), whole(bgate), whole(w_br), whole(w_out), vec, vec,
                  whole(r_parts), lvec, whole(tri)],
        out_specs=[row, pl.BlockSpec((tm, d // 2), lambda i: (i, 0)), pl.BlockSpec((8, tm), lambda i: (0, i)),
                   lanes, lvec],
        out_shape=[jax.ShapeDtypeStruct((t, d), F32), jax.ShapeDtypeStruct((t, d // 2), jnp.uint32),
                   jax.ShapeDtypeStruct((8, t), jnp.int32), jax.ShapeDtypeStruct((t, LANES), F32),
                   jax.ShapeDtypeStruct((1, LANES), jnp.int32)],
        scratch_shapes=[pltpu.VMEM((1, LANES), F32)],
        compiler_params=_cparams("arbitrary"), name="merge_ln1_route",
    )(x, y_a, y_b, y_c, o_f, o_b, pd, ng, bd, w_gate, bgate, w_br, w_out, g.reshape(1, d), b.reshape(1, d),
      r_parts, rb, tri)


def _work_items_kernel(cnt_ref, start_ref, items_ref):
    n_items = items_ref.shape[1]

    def put(w, e, blk, lo, hi, first):
        for row, val in enumerate((e, blk, lo, hi, first)):
            items_ref[row, w] = val

    def expert(e, carry):
        start, w, _, _ = carry
        start_ref[e] = start
        end = start + cnt_ref[e]
        first_blk = start // MOE_ROWS
        n_blk = jnp.where(end > start, (end - 1) // MOE_ROWS - first_blk + 1, 0)

        def block(i, w):
            blk = first_blk + i
            row0 = blk * MOE_ROWS
            lo = jnp.maximum(start, row0) - row0
            first = jnp.where((i > 0) | (lo == 0), 1, 0)
            put(w, e, blk, lo, jnp.minimum(end, row0 + MOE_ROWS) - row0, first)
            return w + 1

        w_end = lax.fori_loop(0, n_blk, block, w)
        owner = jnp.where(n_blk > 0, e, carry[2])
        last_blk = jnp.where(n_blk > 0, first_blk + n_blk - 1, carry[3])
        return end, w_end, owner, last_blk

    zero = jnp.int32(0)
    _, w_live, owner, last_blk = lax.fori_loop(0, N_EXPERTS, expert, (zero, zero, zero, zero))

    def surplus(w, _):
        put(w, owner, last_blk, zero, zero, zero)
        return 0

    lax.fori_loop(w_live, n_items, surplus, 0)


def _expert_work_items(counts, n_blocks):
    smem = pl.BlockSpec(memory_space=pltpu.SMEM)
    return pl.pallas_call(
        _work_items_kernel, in_specs=[smem], out_specs=[smem, smem],
        out_shape=[jax.ShapeDtypeStruct((N_EXPERTS,), jnp.int32),
                   jax.ShapeDtypeStruct((5, n_blocks + N_EXPERTS), jnp.int32)],
        name="moe_work_items",
    )(counts)


def _expert_kernel(items_ref, x_ref, w13_ref, w2_ref, o_ref):
    w = pl.program_id(0)
    lo = items_ref[2, w]
    hi = items_ref[3, w]

    @pl.when(hi > lo)
    def _():
        x = _unpack_bf16_pairs(x_ref[...]).astype(BF16)
        h = jnp.dot(x, w13_ref[0], preferred_element_type=F32)
        hg = h[:, :D_FF_E]
        act = hg * _sigmoid(hg) * h[:, D_FF_E:]
        y = _pack_bf16_pairs(jnp.dot(act.astype(BF16), w2_ref[0], preferred_element_type=F32))
        row = lax.broadcasted_iota(jnp.int32, (MOE_ROWS, 1), 0)
        mine = (row >= lo) & (row < hi)

        @pl.when(items_ref[4, w] == 1)
        def _():
            o_ref[...] = jnp.where(mine, y, jnp.zeros_like(y))

        @pl.when(items_ref[4, w] == 0)
        def _():
            o_ref[...] = jnp.where(mine, y, o_ref[...])


def _expert_blocks(xs, items, w13, w2):
    n_rows, half = xs.shape
    d = 2 * half
    grid_spec = pltpu.PrefetchScalarGridSpec(
        num_scalar_prefetch=1, grid=(items.shape[1],),
        in_specs=[pl.BlockSpec((MOE_ROWS, half), lambda w, it: (it[1, w], 0)),
                  pl.BlockSpec((1, d, 2 * D_FF_E), lambda w, it: (it[0, w], 0, 0)),
                  pl.BlockSpec((1, D_FF_E, d), lambda w, it: (it[0, w], 0, 0))],
        out_specs=pl.BlockSpec((MOE_ROWS, half), lambda w, it: (it[1, w], 0)))
    return pl.pallas_call(
        _expert_kernel, grid_spec=grid_spec, out_shape=jax.ShapeDtypeStruct((n_rows, half), jnp.uint32),
        compiler_params=_cparams("arbitrary"), name="moe_experts",
    )(items, xs, w13, w2)


def _combine_kernel(x_ref, y0_ref, y1_ref, p_ref, g_ref, b_ref, o_ref):
    o_ref[...] = _ln_rows(_moe_residual(x_ref, y0_ref, y1_ref, p_ref), g_ref[...], b_ref[...])


def _combine(x, y01, p, g, b, tm=1024):
    t, d = x.shape
    row = pl.BlockSpec((tm, d), lambda i: (i, 0))
    vec = pl.BlockSpec((1, d), lambda i: (0, 0))
    return pl.pallas_call(
        _combine_kernel, grid=(t // tm,),
        in_specs=[row] + _pair_specs(t, d, tm) + [pl.BlockSpec((tm, LANES), lambda i: (i, 0)), vec, vec],
        out_specs=row, out_shape=jax.ShapeDtypeStruct((t, d), F32),
        compiler_params=_cparams("parallel"), name="moe_combine_ln2",
    )(x, y01, y01, p, g.reshape(1, d), b.reshape(1, d))


def _dest_kernel(start_ref, r_ref, o_ref):
    r = r_ref[...]
    e = r[0:TOP_K, :]
    base = jnp.zeros_like(e)
    for x in range(N_EXPERTS):
        base = jnp.where(e == x, start_ref[x], base)
    pad = jnp.zeros((r.shape[0] - TOP_K, r.shape[1]), jnp.int32)
    o_ref[...] = jnp.concatenate([base + r[TOP_K:2 * TOP_K, :], pad], axis=0)


def _slot_rows(routing, start, tl=8192):
    rows, t = routing.shape
    tl = min(tl, t)
    grid_spec = pltpu.PrefetchScalarGridSpec(
        num_scalar_prefetch=1, grid=(t // tl,),
        in_specs=[pl.BlockSpec((rows, tl), lambda i, st: (0, i))],
        out_specs=pl.BlockSpec((rows, tl), lambda i, st: (0, i)))
    return pl.pallas_call(
        _dest_kernel, grid_spec=grid_spec, out_shape=jax.ShapeDtypeStruct((rows, t), jnp.int32),
        compiler_params=_cparams("parallel"), name="moe_slot_rows",
    )(start, routing)


def _sc_workers():
    info = plsc.get_sparse_core_info()
    return info.num_cores, info.num_cores * info.num_subcores


def _sc_move_rows(src, dest, scatter):
    t = dest.shape[1]
    width = src.shape[1]
    n_cores, n_workers = _sc_workers()
    steps = t // (n_workers * SC_WINDOW)
    assert steps * n_workers * SC_WINDOW == t and steps % 2 == 0, (t, n_workers)
    mesh = plsc.VectorSubcoreMesh(core_axis_name="core", subcore_axis_name="subcore")

    @functools.partial(
        pl.kernel, mesh=mesh, out_type=jax.ShapeDtypeStruct((TOP_K * t, width), src.dtype),
        scratch_types=[pltpu.VMEM((TOP_K, steps, SC_WINDOW), jnp.int32),
                       pltpu.VMEM((2, SC_WINDOW, width), src.dtype),
                       pltpu.SemaphoreType.DMA, pltpu.SemaphoreType.DMA])
    def move(src_hbm, idx_hbm, out_hbm, idx_v, rows_v, sem0, sem1):
        wid = lax.axis_index("subcore") * n_cores + lax.axis_index("core")
        for k in range(TOP_K):
            pltpu.sync_copy(idx_hbm.at[k, wid], idx_v.at[k])
        sems = (sem0, sem1)

        def window(j):
            return pl.ds((wid * steps + j) * SC_WINDOW, SC_WINDOW)

        def run(fetch, flush):
            fetch(0, 0).start()

            @pl.loop(0, steps, step=2)
            def _(j):
                fetch(j + 1, 1).start()
                fetch(j, 0).wait()
                flush(j, 0)

                @pl.when(j + 2 < steps)
                def _():
                    fetch(j + 2, 0).start()

                fetch(j + 1, 1).wait()
                flush(j + 1, 1)

        if scatter:
            def fetch(j, slot):
                return pltpu.make_async_copy(src_hbm.at[window(j)], rows_v.at[slot], sems[slot])

            def flush(j, slot):
                for k in range(TOP_K):
                    pltpu.sync_copy(rows_v.at[slot], out_hbm.at[idx_v.at[k, j]])

            run(fetch, flush)
        else:
            for k in range(TOP_K):
                def fetch(j, slot, k=k):
                    return pltpu.make_async_copy(src_hbm.at[idx_v.at[k, j]], rows_v.at[slot], sems[slot])

                def flush(j, slot, k=k):
                    pltpu.sync_copy(rows_v.at[slot], out_hbm.at[pl.ds(k * t + (wid * steps + j) * SC_WINDOW,
                                                                      SC_WINDOW)])

                run(fetch, flush)

    return move(src, dest.reshape(dest.shape[0], n_workers, steps, SC_WINDOW))


def _expert_outputs(x16, routing, cnt, w13, w2):
    t = x16.shape[0]
    start, items = _expert_work_items(cnt[0, :N_EXPERTS], t * TOP_K // MOE_ROWS)
    dest = _slot_rows(routing, start)
    xs = _sc_move_rows(x16, dest, scatter=True)
    ys = _expert_blocks(xs, items, w13, w2)
    return _sc_move_rows(ys, dest, scatter=False)


def _trunk(x, prm):
    b, l, d = x.shape
    t = b * l
    x = x.reshape(t, d)
    moe, g, bias = None, prm['ln_in_g'], prm['ln_in_b']
    tc, ts = _fft_twiddles(l)
    for i in range(DEPTH):
        x, pa, pb, pc, pd = _input_projection(x, moe, g, bias, prm['w_in'][i])
        y_a = _neighbourhood_attention(pa.reshape(b, l, -1), prm['attn_bias'][i])
        y_b = _short_conv(pb.reshape(b, l, -1), prm['conv_w'][i])
        y_c = _fourier_mix(pc.reshape(b, l, -1), tc, ts)
        o_f, o_b = _hgrn2_bidirectional(pd.reshape(b, l, -1), prm['lb'][0, i], prm['lb'][1, i], i == 0)
        x, x16, routing, p_lanes, cnt = _merge(
            x, y_a.reshape(t, BR), y_b.reshape(t, BR), y_c.reshape(t, BR), o_f.reshape(t, BR), o_b.reshape(t, BR),
            pd, prm['hgrn_norm_g'][i], prm['w_gate'][i], prm['b_gate'][i], prm['w_br'][i], prm['w_out'][i],
            prm['ln1_g'][i], prm['ln1_b'][i], prm['router_g_w'][i], prm['router_g_b'][i], prm['router_e_w'][i],
            prm['router_e_b'][i])
        y01 = _expert_outputs(x16, routing, cnt, prm['w13'][i], prm['w2'][i])
        moe, g, bias = (y01, p_lanes), prm['ln2_g'][i], prm['ln2_b'][i]
    return _combine(x, *moe, g, bias).reshape(b, l, d)


def kernel(x_prompt, x_sample, ln_in_g, ln_in_b, w_in, na_rpb, conv_w, hgrn_lb, hgrn_norm_g, w_gate, b_gate, w_br,
           w_out, ln1_g, ln1_b, router_g_w, router_g_b, router_e_w, router_e_b, w13, w2, ln2_g, ln2_b):
    lb = jnp.cumsum(jax.nn.softmax(hgrn_lb.astype(F32), axis=1), axis=1)
    lb = lb - lb[:, :1]
    prm = dict(
        ln_in_g=ln_in_g, ln_in_b=ln_in_b,
        w_in=jnp.stack([_fold_channel_dft(w_in[i]) for i in range(DEPTH)]).astype(BF16),
        attn_bias=jnp.stack([_attn_bias_table(na_rpb[i]) for i in range(DEPTH)]),
        conv_w=conv_w, lb=lb, hgrn_norm_g=hgrn_norm_g,
        w_gate=w_gate.astype(BF16), b_gate=b_gate, w_br=w_br.astype(BF16), w_out=w_out.astype(BF16),
        ln1_g=ln1_g, ln1_b=ln1_b, router_g_w=router_g_w, router_g_b=router_g_b,
        router_e_w=router_e_w, router_e_b=router_e_b, w13=w13.astype(BF16), w2=w2.astype(BF16),
        ln2_g=ln2_g, ln2_b=ln2_b)
    return _trunk(x_prompt, prm), _trunk(x_sample, prm)
```

```python
import functools

import numpy as np
import jax
import jax.numpy as jnp
from jax import lax
from jax.experimental import pallas as pl
from jax.experimental.pallas import tpu as pltpu
from jax.experimental.pallas import tpu_sc as plsc

F32 = jnp.float32
BF16 = jnp.bfloat16
HIGHEST = lax.Precision.HIGHEST

D_MODEL = 1024
DEPTH = 4
GRID_W = 64
BR = 256
N_HEADS = 4
HEAD_DIM = 64
NA_ROWS = 8
NA_COLS = 16
FN_GROUP_DIM = 64
N_GROUPS = 4
EXPERTS_PER_GROUP = 8
N_EXPERTS = N_GROUPS * EXPERTS_PER_GROUP
TOP_K = 2
D_FF_E = 512
DEEPNORM_ALPHA = (2 * DEPTH) ** 0.25
LN_EPS = 1e-5
RMS_EPS = 1e-6
NEG_INF = -1e30

LANES = 128
VMEM_LIMIT_BYTES = 56 * 1024 * 1024
MOE_ROWS = 512
HG_CHUNK = 128
FFT_L1 = 64
SC_WINDOW = 64

COL_QA, COL_KA, COL_VA, COL_XB, COL_GB, COL_GC, COL_XF, COL_QD, COL_FF, COL_FB, COL_VD, COL_GD = range(12)
PROJ_GROUPS = ((0, 3), (3, 6), (6, 8), (8, 13))


def _cparams(*sem):
    return pltpu.CompilerParams(dimension_semantics=sem, vmem_limit_bytes=VMEM_LIMIT_BYTES)


def _sigmoid(x):
    return 1.0 / (1.0 + jnp.exp(-x))


def _split_bf16(x):
    hi = x.astype(BF16)
    return hi, (x - hi.astype(F32)).astype(BF16)


def _dot_split(a_hi, a_lo, b_hi, b_lo):
    dot = functools.partial(jnp.dot, preferred_element_type=F32)
    return dot(a_hi, b_hi) + (dot(a_lo, b_hi) + dot(a_hi, b_lo))


def _pack_bf16_pairs(x):
    m = x.shape[1] // 2
    hi = lax.bitcast_convert_type(x[:, :m].astype(BF16).astype(F32), jnp.uint32)
    lo = lax.bitcast_convert_type(x[:, m:].astype(BF16).astype(F32), jnp.uint32)
    return hi | (lo >> 16)


def _unpack_bf16_pairs(u):
    hi = lax.bitcast_convert_type(u & jnp.uint32(0xFFFF0000), F32)
    lo = lax.bitcast_convert_type(u << 16, F32)
    return jnp.concatenate([hi, lo], axis=1)


def _ln_rows(x, g, b):
    mu = jnp.mean(x, axis=-1, keepdims=True)
    xc = x - mu
    var = jnp.mean(xc * xc, axis=-1, keepdims=True)
    return xc * lax.rsqrt(var + LN_EPS) * g + b


def _head_of_lane(shape):
    return lax.broadcasted_iota(jnp.int32, shape, len(shape) - 1) // HEAD_DIM


def _head_stack(x):
    head = _head_of_lane(x.shape)
    return jnp.concatenate([jnp.where(head == h, x, jnp.zeros_like(x)) for h in range(N_HEADS)], axis=0)


def _same_head_matrix():
    h = np.arange(BR) // HEAD_DIM
    return (h[:, None] == h[None, :]).astype(np.float32)


def _moe_residual(x_ref, y0_ref, y1_ref, p_ref):
    p = p_ref[...]
    ffn = p[:, 0:1] * _unpack_bf16_pairs(y0_ref[...]) + p[:, 1:2] * _unpack_bf16_pairs(y1_ref[...])
    return DEEPNORM_ALPHA * x_ref[...] + ffn


def _proj_kernel(*refs, after_moe):
    if after_moe:
        x_ref, y0_ref, y1_ref, p_ref, g_ref, b_ref, w_ref, xo_ref, *out_refs = refs
        pre = _moe_residual(x_ref, y0_ref, y1_ref, p_ref)
    else:
        x_ref, g_ref, b_ref, w_ref, xo_ref, *out_refs = refs
        pre = x_ref[...]
    x = _ln_rows(pre, g_ref[...], b_ref[...])
    xo_ref[...] = x
    xb = x.astype(BF16)
    for ref, (lo, hi) in zip(out_refs, PROJ_GROUPS):
        ref[...] = jnp.dot(xb, w_ref[:, lo * BR:hi * BR], preferred_element_type=F32).astype(ref.dtype)


def _fold_channel_dft(w_in):
    cg, sg = _dft_cos_sin(FN_GROUP_DIM)
    eye = np.eye(BR // FN_GROUP_DIM, dtype=np.float32)
    w_xf = w_in[:, COL_XF * BR:(COL_XF + 1) * BR]
    w_re = jnp.dot(w_xf, np.kron(eye, cg), precision=HIGHEST)
    w_im = -jnp.dot(w_xf, np.kron(eye, sg), precision=HIGHEST)
    return jnp.concatenate([w_in[:, :COL_XF * BR], w_re, w_im, w_in[:, (COL_XF + 1) * BR:]], axis=1)


def _pair_specs(t, d, tm):
    return [pl.BlockSpec((tm, d // 2), lambda i: (i, 0)), pl.BlockSpec((tm, d // 2), lambda i: (i + t // tm, 0))]


def _input_projection(x, moe, g, b, w_in, tm=512):
    t, d = x.shape
    n = w_in.shape[1]
    widths = (d,) + tuple((hi - lo) * BR for lo, hi in PROJ_GROUPS)
    dtypes = (F32, BF16, BF16, F32, BF16)
    row = pl.BlockSpec((tm, d), lambda i: (i, 0))
    vec = pl.BlockSpec((1, d), lambda i: (0, 0))
    acts, act_specs = [x], [row]
    if moe is not None:
        y01, p = moe
        acts += [y01, y01, p]
        act_specs += _pair_specs(t, d, tm) + [pl.BlockSpec((tm, LANES), lambda i: (i, 0))]
    return pl.pallas_call(
        functools.partial(_proj_kernel, after_moe=moe is not None), grid=(t // tm,),
        in_specs=act_specs + [vec, vec, pl.BlockSpec((d, n), lambda i: (0, 0))],
        out_specs=[pl.BlockSpec((tm, w), lambda i: (i, 0)) for w in widths],
        out_shape=[jax.ShapeDtypeStruct((t, w), dt) for w, dt in zip(widths, dtypes)],
        compiler_params=_cparams("parallel"), name="ln_in_proj",
    )(*acts, g.reshape(1, d), b.reshape(1, d), w_in)


def _attn_bias_table(rpb):
    c = np.arange(GRID_W)
    win_c0 = np.clip(c - NA_COLS // 2, 0, GRID_W - NA_COLS)
    valid = (c[None, :] >= win_c0[:, None]) & (c[None, :] < win_c0[:, None] + NA_COLS)
    dc = np.clip(c[None, :] - c[:, None], -(NA_COLS - 1), NA_COLS - 1) + NA_COLS - 1
    dr = np.arange(NA_ROWS)[None, :] - np.arange(NA_ROWS)[:, None] + NA_ROWS - 1
    row_hot = (dr[:, :, None] == np.arange(2 * NA_ROWS - 1)).astype(np.float32)
    col_hot = (dc[:, :, None] == np.arange(2 * NA_COLS - 1)).astype(np.float32)
    t = jnp.einsum('oia,hab,qkb->ohqik', row_hot, rpb.astype(F32), col_hot, precision=HIGHEST)
    t = jnp.where(valid[None, None, :, None, :], t, NEG_INF)
    return t.reshape(NA_ROWS, N_HEADS * GRID_W, NA_ROWS * GRID_W)


def _attn_kernel(q_ref, k_ref, v_ref, bias_ref, o_ref, *, n_rows, rows_per_tile):
    i = pl.program_id(1)
    head = _head_of_lane((GRID_W, BR))
    scale = HEAD_DIM ** -0.5
    span = NA_ROWS * GRID_W
    for j in range(rows_per_tile):
        r = i * rows_per_tile + j
        r0 = jnp.clip(r - NA_ROWS // 2, 0, n_rows - NA_ROWS)
        q = q_ref[0, j * GRID_W:(j + 1) * GRID_W, :] * scale
        qs = _head_stack(q)
        start = pl.multiple_of(r0 * GRID_W, GRID_W)
        ks = k_ref[0, pl.ds(start, span), :]
        vs = v_ref[0, pl.ds(start, span), :]
        s = lax.dot_general(qs, ks, (((1,), (1,)), ((), ())), preferred_element_type=F32)
        s = s + bias_ref[r - r0]
        m = jnp.max(s, axis=-1, keepdims=True)
        e = jnp.exp(s - m)
        inv = 1.0 / jnp.sum(e, axis=-1, keepdims=True)
        o4 = jnp.dot(e.astype(BF16), vs, preferred_element_type=F32) * inv
        o = jnp.zeros((GRID_W, BR), F32)
        for h in range(N_HEADS):
            o = o + jnp.where(head == h, o4[h * GRID_W:(h + 1) * GRID_W, :], 0.0)
        o_ref[0, j * GRID_W:(j + 1) * GRID_W, :] = o.astype(o_ref.dtype)


def _neighbourhood_attention(pa, bias, rows_per_tile=8):
    b, l, _ = pa.shape
    n_rows = l // GRID_W
    tq = rows_per_tile * GRID_W
    assert HEAD_DIM in (16, 64, 256), "the kernel scales bf16 queries by HEAD_DIM ** -0.5, exact for powers of 4"
    kern = functools.partial(_attn_kernel, n_rows=n_rows, rows_per_tile=rows_per_tile)
    return pl.pallas_call(
        kern, grid=(b, l // tq),
        in_specs=[pl.BlockSpec((1, tq, BR), lambda bi, i: (bi, i, 0)),
                  pl.BlockSpec((1, l, BR), lambda bi, i: (bi, 0, 1)),
                  pl.BlockSpec((1, l, BR), lambda bi, i: (bi, 0, 2)),
                  pl.BlockSpec(bias.shape, lambda bi, i: (0, 0, 0))],
        out_specs=pl.BlockSpec((1, tq, BR), lambda bi, i: (bi, i, 0)),
        out_shape=jax.ShapeDtypeStruct((b, l, BR), BF16),
        compiler_params=_cparams("parallel", "arbitrary"), name="nbr_attn",
    )(pa, pa, pa, bias)


def _conv_kernel(xb_ref, gb_ref, gc_ref, xbp_ref, gcp_ref, xbn_ref, gcn_ref, w_ref, o_ref, *, n_tiles):
    i = pl.program_id(1)
    tl = xb_ref.shape[1]
    last = xbp_ref.shape[1] - 1

    def gated(gc, xb):
        return gc.astype(F32) * xb.astype(F32)

    u = gated(gc_ref[0], xb_ref[0])
    u_before = jnp.where(i > 0, gated(gcp_ref[0, last:, :], xbp_ref[0, last:, :]), 0.0)
    u_after = jnp.where(i < n_tiles - 1, gated(gcn_ref[0, 0:1, :], xbn_ref[0, 0:1, :]), 0.0)
    row = lax.broadcasted_iota(jnp.int32, (tl, 1), 0)
    u_prev = jnp.where(row == 0, u_before, pltpu.roll(u, 1, 0))
    u_next = jnp.where(row == tl - 1, u_after, pltpu.roll(u, tl - 1, 0))
    conv = w_ref[0:1, :] * u_prev + w_ref[1:2, :] * u + w_ref[2:3, :] * u_next
    o_ref[0] = (gb_ref[0].astype(F32) * conv).astype(o_ref.dtype)


def _short_conv(pb, conv_w, tl=1024):
    b, l, _ = pb.shape
    n_tiles = l // tl
    halo = 16
    per_tile = tl // halo
    last_halo = l // halo - 1

    def main(col):
        return pl.BlockSpec((1, tl, BR), lambda bi, i: (bi, i, col))

    def before(col):
        return pl.BlockSpec((1, halo, BR), lambda bi, i: (bi, jnp.maximum(i * per_tile - 1, 0), col))

    def after(col):
        return pl.BlockSpec((1, halo, BR), lambda bi, i: (bi, jnp.minimum((i + 1) * per_tile, last_halo), col))

    return pl.pallas_call(
        functools.partial(_conv_kernel, n_tiles=n_tiles), grid=(b, n_tiles),
        in_specs=[main(0), main(1), main(2), before(0), before(2), after(0), after(2),
                  pl.BlockSpec(conv_w.shape, lambda bi, i: (0, 0))],
        out_specs=pl.BlockSpec((1, tl, BR), lambda bi, i: (bi, i, 0)),
        out_shape=jax.ShapeDtypeStruct((b, l, BR), BF16),
        compiler_params=_cparams("parallel", "parallel"), name="short_conv",
    )(pb, pb, pb, pb, pb, pb, pb, conv_w)


def _dft_cos_sin(n):
    ang = 2.0 * np.pi * ((np.arange(n)[:, None] * np.arange(n)[None, :]) % n) / n
    return np.cos(ang).astype(np.float32), np.sin(ang).astype(np.float32)


def _fft_major_kernel(g_ref, m_ref, tc_ref, ts_ref, br_ref, bi_ref):
    for s in range(8):
        g = g_ref[0, :, s, :]
        swapped = jnp.concatenate([g[:, BR:], -g[:, :BR]], axis=1)
        rhs = jnp.concatenate([g, swapped], axis=0)
        a = _dot_split(m_ref[0], m_ref[1], *_split_bf16(rhs))
        ar, ai = a[:, :BR], a[:, BR:]
        tc = tc_ref[:, s, :]
        ts = ts_ref[:, s, :]
        br_ref[0, :, s, :] = ar * tc + ai * ts
        bi_ref[0, :, s, :] = ai * tc - ar * ts


def _fft_minor_kernel(br_ref, bi_ref, m_ref, o_ref, *, scale):
    l2 = br_ref.shape[2]
    group = m_ref.shape[1] // l2
    for j0 in range(0, 8, group):
        rhs = jnp.concatenate([br_ref[0, j0:j0 + group].reshape(group * l2, BR),
                               bi_ref[0, j0:j0 + group].reshape(group * l2, BR)], axis=0)
        y = _dot_split(m_ref[0], m_ref[1], *_split_bf16(rhs)) * scale
        for j in range(group):
            o_ref[0, :, j0 + j, :] = y[j * l2:(j + 1) * l2].astype(o_ref.dtype)


def _fft_twiddles(l):
    l1 = FFT_L1
    l2 = l // l1
    idx = (jnp.arange(l1, dtype=jnp.int32)[:, None] * jnp.arange(l2, dtype=jnp.int32)[None, :]) % l
    ang = idx.astype(F32) * (2.0 * np.pi / l)
    tc = jnp.broadcast_to(jnp.cos(ang)[:, :, None], (l1, l2, BR))
    ts = jnp.broadcast_to(jnp.sin(ang)[:, :, None], (l1, l2, BR))
    return tc, ts


def _fourier_mix(pc, tc, ts):
    b, l, _ = pc.shape
    l1 = FFT_L1
    l2 = l // l1

    c1, s1 = _dft_cos_sin(l1)
    c2, s2 = _dft_cos_sin(l2)
    group = min(max(1, 128 // l2), 8)
    eye = np.eye(group, dtype=np.float32)
    minor = jnp.asarray(np.concatenate([np.kron(eye, c2), np.kron(eye, s2)], axis=1))
    minor_hi = minor.astype(BF16)
    minor_parts = jnp.stack([minor_hi, (minor - minor_hi.astype(F32)).astype(BF16)])

    def strip(width):
        return pl.BlockSpec((1, l1, 8, width), lambda bi, i: (bi, 0, i, 0))

    tspec = pl.BlockSpec((l1, 8, BR), lambda bi, i: (0, i, 0))
    major = jnp.asarray(np.concatenate([c1, s1], axis=1))
    major_hi = major.astype(BF16)
    major_parts = jnp.stack([major_hi, (major - major_hi.astype(F32)).astype(BF16)])
    mspec = pl.BlockSpec(major_parts.shape, lambda bi, i: (0, 0, 0))
    br, bi_ = pl.pallas_call(
        _fft_major_kernel, grid=(b, l2 // 8),
        in_specs=[strip(2 * BR), mspec, tspec, tspec], out_specs=[strip(BR), strip(BR)],
        out_shape=[jax.ShapeDtypeStruct((b, l1, l2, BR), F32)] * 2,
        compiler_params=_cparams("parallel", "parallel"), name="fft_major",
    )(pc.reshape(b, l1, l2, 2 * BR), major_parts, tc, ts)

    bspec = pl.BlockSpec((1, 8, l2, BR), lambda bi, i: (bi, i, 0, 0))
    m2spec = pl.BlockSpec(minor_parts.shape, lambda bi, i: (0, 0, 0))
    scale = 1.0 / float(np.sqrt(l * FN_GROUP_DIM))
    y = pl.pallas_call(
        functools.partial(_fft_minor_kernel, scale=scale), grid=(b, l1 // 8),
        in_specs=[bspec, bspec, m2spec],
        out_specs=pl.BlockSpec((1, l2, 8, BR), lambda bi, i: (bi, 0, i, 0)),
        out_shape=jax.ShapeDtypeStruct((b, l2, l1, BR), F32),
        compiler_params=_cparams("parallel", "parallel"), name="fft_minor",
    )(br, bi_, minor_parts)
    return y.reshape(b, l, BR)


def _hgrn_level_masks():
    t = np.arange(HG_CHUNK)
    out = []
    half = 8
    while 2 * half < HG_CHUNK:
        same = (t[:, None] // (2 * half)) == (t[None, :] // (2 * half))
        out.append(np.tile(same, (1, N_HEADS)).astype(np.float32))
        half *= 2
    return np.stack(out)


def _hgrn_kernel(qd_ref, fl_ref, vd_ref, lb_ref, mask_ref, bd_ref, o_ref, st_ref, *, first_layer, chunks_per_tile,
                 rev):
    c = HG_CHUNK

    @pl.when(pl.program_id(1) == 0)
    def _():
        st_ref[...] = jnp.zeros_like(st_ref)

    bd = bd_ref[...]
    bd16 = bd.astype(BF16)
    lb = lb_ref[...]
    row = lax.broadcasted_iota(jnp.int32, (c, 1), 0)
    sub = row % 8
    nt = (((1,), (1,)), ((), ()))

    def roll8(x, j):
        return pltpu.roll(x.reshape(c // 8, 8, BR), j % 8, 1).reshape(c, BR)

    def earlier(x, j):
        return roll8(x, -j if rev else j)

    def later(x, j):
        return roll8(x, j if rev else -j)

    def has_earlier(j):
        return (sub + j <= 7) if rev else (sub >= j)

    def has_later(j):
        return (sub >= j) if rev else (sub + j <= 7)

    for ci in (reversed(range(chunks_per_tile)) if rev else range(chunks_per_tile)):
        sl = slice(ci * c, (ci + 1) * c)
        qd = qd_ref[0, sl, :].astype(F32)
        fl = fl_ref[0, sl, :].astype(F32)
        v = vd_ref[0, sl, :].astype(F32)
        q = qd * _sigmoid(qd)
        if first_layer:
            f = _sigmoid(fl)
            k = _sigmoid(-fl)
        else:
            f = lb + (1.0 - lb) * _sigmoid(fl)
            k = (1.0 - lb) * _sigmoid(-fl)

        o = jnp.dot((q * k).astype(BF16), bd16, preferred_element_type=F32) * v
        g = q * f
        for d in range(1, 8):
            if d > 1:
                g = g * earlier(f, d - 1)
            p = jnp.where(has_earlier(d), g * earlier(k, d), 0.0)
            o = o + jnp.dot(p.astype(BF16), bd16, preferred_element_type=F32) * earlier(v, d)

        qf = f
        for j in (1, 2, 4):
            qf = qf * jnp.where(has_earlier(j), earlier(qf, j), 1.0)
        kf = jnp.where(has_later(1), later(f, 1), 1.0)
        for j in (1, 2, 4):
            kf = kf * jnp.where(has_later(j), later(kf, j), 1.0)

        scores = None
        half = 8
        level = 0
        while half < c:
            late = ((row // half) % 2) == (0 if rev else 1)
            qb = jnp.where(late, q * qf, 0.0).astype(BF16)
            kb = jnp.where(late, 0.0, k * kf).astype(BF16)
            x = lax.dot_general(qb, _head_stack(kb), nt, preferred_element_type=F32)
            if 2 * half < c:
                x = x * mask_ref[level]
            scores = x if scores is None else scores + x
            blocks = c // (2 * half)
            qf3 = qf.reshape(blocks, 2 * half, BR)
            end_early = half if rev else half - 1
            end_late = 0 if rev else 2 * half - 1
            tot_early = jnp.broadcast_to(qf3[:, end_early:end_early + 1, :], qf3.shape).reshape(c, BR)
            tot_late = jnp.broadcast_to(qf3[:, end_late:end_late + 1, :], qf3.shape).reshape(c, BR)
            qf = qf * jnp.where(late, tot_early, 1.0)
            kf = kf * jnp.where(late, 1.0, tot_late)
            half *= 2
            level += 1

        o = o + jnp.dot(scores.astype(BF16), _head_stack(v.astype(BF16)), preferred_element_type=F32)

        st = st_ref[...]
        o = o + lax.dot_general((q * qf).astype(BF16), st.astype(BF16), nt, preferred_element_type=F32)
        o_ref[0, sl, :] = o
        upd = jnp.dot(v.T.astype(BF16), (k * kf).astype(BF16), preferred_element_type=F32)
        last = 0 if rev else c - 1
        st_ref[...] = st * qf[last:last + 1, :] + upd * bd


def _hgrn_scan(qd, fl, vd, cols, lb, first_layer, rev, chunks_per_tile=8):
    b, l, _ = qd.shape
    tile = HG_CHUNK * chunks_per_tile
    n_tiles = l // tile
    masks = _hgrn_level_masks()
    bd = _same_head_matrix()

    def tile_index(i):
        return n_tiles - 1 - i if rev else i

    def seq(col):
        return pl.BlockSpec((1, tile, BR), lambda bi, i: (bi, tile_index(i), col))

    return pl.pallas_call(
        functools.partial(_hgrn_kernel, first_layer=first_layer, chunks_per_tile=chunks_per_tile, rev=rev),
        grid=(b, n_tiles),
        in_specs=[seq(cols[0]), seq(cols[1]), seq(cols[2]),
                  pl.BlockSpec((1, BR), lambda bi, i: (0, 0)),
                  pl.BlockSpec(masks.shape, lambda bi, i: (0, 0, 0)),
                  pl.BlockSpec((BR, BR), lambda bi, i: (0, 0))],
        out_specs=pl.BlockSpec((1, tile, BR), lambda bi, i: (bi, tile_index(i), 0)),
        out_shape=jax.ShapeDtypeStruct((b, l, BR), F32),
        scratch_shapes=[pltpu.VMEM((BR, BR), F32)],
        compiler_params=_cparams("parallel", "arbitrary"),
        name="hgrn_bwd" if rev else "hgrn_fwd",
    )(qd, fl, vd, lb.reshape(1, BR), masks, bd)


def _hgrn2_bidirectional(pd, lb_f, lb_b, first_layer):
    o_f = _hgrn_scan(pd, pd, pd, (0, 1, 3), lb_f, first_layer, rev=False)
    o_b = _hgrn_scan(pd, pd, pd, (0, 2, 3), lb_b, first_layer, rev=True)
    return o_f, o_b


def _route(x, w_parts, bias, tri, run):
    x_hi, x_lo = _split_bf16(x)
    both = jnp.dot(x_hi, w_parts, preferred_element_type=F32)
    cross = jnp.dot(x_lo, w_parts[:, :LANES], preferred_element_type=F32)
    logits = both[:, :LANES] + (cross + both[:, LANES:]) + bias
    lane = lax.broadcasted_iota(jnp.int32, logits.shape, 1)
    big = jnp.int32(LANES)
    is_group = (lane >= N_EXPERTS) & (lane < N_EXPERTS + N_GROUPS)
    lg = jnp.where(is_group, logits, -jnp.inf)
    mg = jnp.max(lg, axis=-1, keepdims=True)
    g_sel = jnp.min(jnp.where(lg == mg, lane, big), axis=-1, keepdims=True) - N_EXPERTS
    p_g = 1.0 / jnp.sum(jnp.exp(lg - mg), axis=-1, keepdims=True)
    in_group = (lane < N_EXPERTS) & ((lane // EXPERTS_PER_GROUP) == g_sel)
    le = jnp.where(in_group, logits, -jnp.inf)
    v1 = jnp.max(le, axis=-1, keepdims=True)
    i1 = jnp.min(jnp.where(le == v1, lane, big), axis=-1, keepdims=True)
    le2 = jnp.where(lane == i1, -jnp.inf, le)
    v2 = jnp.max(le2, axis=-1, keepdims=True)
    i2 = jnp.min(jnp.where(le2 == v2, lane, big), axis=-1, keepdims=True)
    e2 = jnp.exp(v2 - v1)
    w1 = p_g / (1.0 + e2)
    w2 = p_g * e2 / (1.0 + e2)
    hot1 = lane == i1
    hot2 = lane == i2
    hot = (hot1 | hot2).astype(BF16)
    before = jnp.dot(tri, hot, preferred_element_type=F32) + run
    rank1 = jnp.sum(jnp.where(hot1, before, 0.0), axis=-1, keepdims=True).astype(jnp.int32)
    rank2 = jnp.sum(jnp.where(hot2, before, 0.0), axis=-1, keepdims=True).astype(jnp.int32)
    run = run + jnp.sum(hot.astype(F32), axis=0, keepdims=True)
    e = jnp.where(lane == 0, i1, jnp.where(lane == 1, i2, jnp.where(lane == 2, rank1,
                                                                     jnp.where(lane == 3, rank2, 0))))
    p = jnp.where(lane == 0, w1, jnp.where(lane == 1, w2, 0.0))
    return e, p, run


def _merge_kernel(x_ref, ya_ref, yb_ref, yc_ref, of_ref, ob_ref, gd_ref, ng_ref, bd_ref, wg_ref, bg_ref, wbr_ref,
                  wo_ref, g_ref, b_ref, rw_ref, rb_ref, tri_ref, o_ref, o16_ref, e_ref, p_ref, cnt_ref, run_ref):
    @pl.when(pl.program_id(0) == 0)
    def _():
        run_ref[...] = jnp.zeros_like(run_ref)

    od = of_ref[...] + ob_ref[...]
    ms = jnp.dot(od * od, bd_ref[...], precision=HIGHEST, preferred_element_type=F32) * (1.0 / HEAD_DIM)
    gd = gd_ref[...].astype(F32)
    y_d = od * lax.rsqrt(ms + RMS_EPS) * ng_ref[...] * (gd * _sigmoid(gd))

    x = x_ref[...]
    xb = x.astype(BF16)
    merged = None
    for i, y in enumerate((ya_ref[...], yb_ref[...], yc_ref[...], y_d)):
        gate = _sigmoid(jnp.dot(xb, wg_ref[i], preferred_element_type=F32) + bg_ref[i])
        term = gate * jnp.dot(y.astype(BF16), wbr_ref[i], preferred_element_type=F32)
        merged = term if merged is None else merged + term
    mix = jnp.dot(merged.astype(BF16), wo_ref[...], preferred_element_type=F32)
    out = _ln_rows(DEEPNORM_ALPHA * x + mix, g_ref[...], b_ref[...])
    o_ref[...] = out
    o16_ref[...] = _pack_bf16_pairs(out)
    e, p, run = _route(out, rw_ref[...], rb_ref[...], tri_ref[...], run_ref[...])
    e_ref[...] = e.T[:e_ref.shape[0], :]
    p_ref[...] = p
    run_ref[...] = run
    cnt_ref[...] = run.astype(jnp.int32)


def _merge(x, y_a, y_b, y_c, o_f, o_b, pd, norm_g, w_gate, b_gate, w_br, w_out, g, b, wg, bg, we, be, tm=512):
    t, d = x.shape
    row = pl.BlockSpec((tm, d), lambda i: (i, 0))
    yrow = pl.BlockSpec((tm, BR), lambda i: (i, 0))
    vec = pl.BlockSpec((1, d), lambda i: (0, 0))
    lanes = pl.BlockSpec((tm, LANES), lambda i: (i, 0))
    lvec = pl.BlockSpec((1, LANES), lambda i: (0, 0))

    def whole(a):
        return pl.BlockSpec(a.shape, lambda i: (0,) * a.ndim)

    bgate = b_gate.reshape(b_gate.shape[0], 1, d)
    rw = jnp.zeros((d, LANES), F32).at[:, :N_EXPERTS].set(we).at[:, N_EXPERTS:N_EXPERTS + N_GROUPS].set(wg)
    rb = jnp.zeros((1, LANES), F32).at[0, :N_EXPERTS].set(be).at[0, N_EXPERTS:N_EXPERTS + N_GROUPS].set(bg)
    r_parts = jnp.concatenate(_split_bf16(rw), axis=1)
    tri = np.tril(np.ones((tm, tm), np.float32), -1).astype(BF16)
    ng = jnp.tile(norm_g, N_HEADS).reshape(1, BR)
    bd = _same_head_matrix()
    return pl.pallas_call(
        _merge_kernel, grid=(t // tm,),
        in_specs=[row, yrow, yrow, yrow, yrow, yrow, pl.BlockSpec((tm, BR), lambda i: (i, 4)), whole(ng), whole(bd),
                  whole(w_gate), whole(bgate), whole(w_br), whole(w_out), vec, vec,
                  whole(r_parts), lvec, whole(tri)],
        out_specs=[row, pl.BlockSpec((tm, d // 2), lambda i: (i, 0)), pl.BlockSpec((8, tm), lambda i: (0, i)),
                   lanes, lvec],
        out_shape=[jax.ShapeDtypeStruct((t, d), F32), jax.ShapeDtypeStruct((t, d // 2), jnp.uint32),
                   jax.ShapeDtypeStruct((8, t), jnp.int32), jax.ShapeDtypeStruct((t, LANES), F32),
                   jax.ShapeDtypeStruct((1, LANES), jnp.int32)],
        scratch_shapes=[pltpu.VMEM((1, LANES), F32)],
        compiler_params=_cparams("arbitrary"), name="merge_ln1_route",
    )(x, y_a, y_b, y_c, o_f, o_b, pd, ng, bd, w_gate, bgate, w_br, w_out, g.reshape(1, d), b.reshape(1, d),
      r_parts, rb, tri)


def _work_items_kernel(cnt_ref, start_ref, items_ref):
    n_items = items_ref.shape[1]

    def put(w, e, blk, lo, hi, first):
        for row, val in enumerate((e, blk, lo, hi, first)):
            items_ref[row, w] = val

    def expert(e, carry):
        start, w, _, _ = carry
        start_ref[e] = start
        end = start + cnt_ref[e]
        first_blk = start // MOE_ROWS
        n_blk = jnp.where(end > start, (end - 1) // MOE_ROWS - first_blk + 1, 0)

        def block(i, w):
            blk = first_blk + i
            row0 = blk * MOE_ROWS
            lo = jnp.maximum(start, row0) - row0
            first = jnp.where((i > 0) | (lo == 0), 1, 0)
            put(w, e, blk, lo, jnp.minimum(end, row0 + MOE_ROWS) - row0, first)
            return w + 1

        w_end = lax.fori_loop(0, n_blk, block, w)
        owner = jnp.where(n_blk > 0, e, carry[2])
        last_blk = jnp.where(n_blk > 0, first_blk + n_blk - 1, carry[3])
        return end, w_end, owner, last_blk

    zero = jnp.int32(0)
    _, w_live, owner, last_blk = lax.fori_loop(0, N_EXPERTS, expert, (zero, zero, zero, zero))

    def surplus(w, _):
        put(w, owner, last_blk, zero, zero, zero)
        return 0

    lax.fori_loop(w_live, n_items, surplus, 0)


def _expert_work_items(counts, n_blocks):
    smem = pl.BlockSpec(memory_space=pltpu.SMEM)
    return pl.pallas_call(
        _work_items_kernel, in_specs=[smem], out_specs=[smem, smem],
        out_shape=[jax.ShapeDtypeStruct((N_EXPERTS,), jnp.int32),
                   jax.ShapeDtypeStruct((5, n_blocks + N_EXPERTS), jnp.int32)],
        name="moe_work_items",
    )(counts)


def _expert_kernel(items_ref, x_ref, w13_ref, w2_ref, o_ref):
    w = pl.program_id(0)
    lo = items_ref[2, w]
    hi = items_ref[3, w]

    @pl.when(hi > lo)
    def _():
        x = _unpack_bf16_pairs(x_ref[...]).astype(BF16)
        h = jnp.dot(x, w13_ref[0], preferred_element_type=F32)
        hg = h[:, :D_FF_E]
        act = hg * _sigmoid(hg) * h[:, D_FF_E:]
        y = _pack_bf16_pairs(jnp.dot(act.astype(BF16), w2_ref[0], preferred_element_type=F32))
        row = lax.broadcasted_iota(jnp.int32, (MOE_ROWS, 1), 0)
        mine = (row >= lo) & (row < hi)

        @pl.when(items_ref[4, w] == 1)
        def _():
            o_ref[...] = jnp.where(mine, y, jnp.zeros_like(y))

        @pl.when(items_ref[4, w] == 0)
        def _():
            o_ref[...] = jnp.where(mine, y, o_ref[...])


def _expert_blocks(xs, items, w13, w2):
    n_rows, half = xs.shape
    d = 2 * half
    grid_spec = pltpu.PrefetchScalarGridSpec(
        num_scalar_prefetch=1, grid=(items.shape[1],),
        in_specs=[pl.BlockSpec((MOE_ROWS, half), lambda w, it: (it[1, w], 0)),
                  pl.BlockSpec((1, d, 2 * D_FF_E), lambda w, it: (it[0, w], 0, 0)),
                  pl.BlockSpec((1, D_FF_E, d), lambda w, it: (it[0, w], 0, 0))],
        out_specs=pl.BlockSpec((MOE_ROWS, half), lambda w, it: (it[1, w], 0)))
    return pl.pallas_call(
        _expert_kernel, grid_spec=grid_spec, out_shape=jax.ShapeDtypeStruct((n_rows, half), jnp.uint32),
        compiler_params=_cparams("arbitrary"), name="moe_experts",
    )(items, xs, w13, w2)


def _combine_kernel(x_ref, y0_ref, y1_ref, p_ref, g_ref, b_ref, o_ref):
    o_ref[...] = _ln_rows(_moe_residual(x_ref, y0_ref, y1_ref, p_ref), g_ref[...], b_ref[...])


def _combine(x, y01, p, g, b, tm=1024):
    t, d = x.shape
    row = pl.BlockSpec((tm, d), lambda i: (i, 0))
    vec = pl.BlockSpec((1, d), lambda i: (0, 0))
    return pl.pallas_call(
        _combine_kernel, grid=(t // tm,),
        in_specs=[row] + _pair_specs(t, d, tm) + [pl.BlockSpec((tm, LANES), lambda i: (i, 0)), vec, vec],
        out_specs=row, out_shape=jax.ShapeDtypeStruct((t, d), F32),
        compiler_params=_cparams("parallel"), name="moe_combine_ln2",
    )(x, y01, y01, p, g.reshape(1, d), b.reshape(1, d))


def _dest_kernel(start_ref, r_ref, o_ref):
    r = r_ref[...]
    e = r[0:TOP_K, :]
    base = jnp.zeros_like(e)
    for x in range(N_EXPERTS):
        base = jnp.where(e == x, start_ref[x], base)
    pad = jnp.zeros((r.shape[0] - TOP_K, r.shape[1]), jnp.int32)
    o_ref[...] = jnp.concatenate([base + r[TOP_K:2 * TOP_K, :], pad], axis=0)


def _slot_rows(routing, start, tl=8192):
    rows, t = routing.shape
    tl = min(tl, t)
    grid_spec = pltpu.PrefetchScalarGridSpec(
        num_scalar_prefetch=1, grid=(t // tl,),
        in_specs=[pl.BlockSpec((rows, tl), lambda i, st: (0, i))],
        out_specs=pl.BlockSpec((rows, tl), lambda i, st: (0, i)))
    return pl.pallas_call(
        _dest_kernel, grid_spec=grid_spec, out_shape=jax.ShapeDtypeStruct((rows, t), jnp.int32),
        compiler_params=_cparams("parallel"), name="moe_slot_rows",
    )(start, routing)


def _sc_workers():
    info = plsc.get_sparse_core_info()
    return info.num_cores, info.num_cores * info.num_subcores


def _sc_move_rows(src, dest, scatter):
    t = dest.shape[1]
    width = src.shape[1]
    n_cores, n_workers = _sc_workers()
    steps = t // (n_workers * SC_WINDOW)
    assert steps * n_workers * SC_WINDOW == t and steps % 2 == 0, (t, n_workers)
    mesh = plsc.VectorSubcoreMesh(core_axis_name="core", subcore_axis_name="subcore")

    @functools.partial(
        pl.kernel, mesh=mesh, out_type=jax.ShapeDtypeStruct((TOP_K * t, width), src.dtype),
        scratch_types=[pltpu.VMEM((TOP_K, steps, SC_WINDOW), jnp.int32),
                       pltpu.VMEM((2, SC_WINDOW, width), src.dtype),
                       pltpu.SemaphoreType.DMA, pltpu.SemaphoreType.DMA])
    def move(src_hbm, idx_hbm, out_hbm, idx_v, rows_v, sem0, sem1):
        wid = lax.axis_index("subcore") * n_cores + lax.axis_index("core")
        for k in range(TOP_K):
            pltpu.sync_copy(idx_hbm.at[k, wid], idx_v.at[k])
        sems = (sem0, sem1)

        def window(j):
            return pl.ds((wid * steps + j) * SC_WINDOW, SC_WINDOW)

        def run(fetch, flush):
            fetch(0, 0).start()

            @pl.loop(0, steps, step=2)
            def _(j):
                fetch(j + 1, 1).start()
                fetch(j, 0).wait()
                flush(j, 0)

                @pl.when(j + 2 < steps)
                def _():
                    fetch(j + 2, 0).start()

                fetch(j + 1, 1).wait()
                flush(j + 1, 1)

        if scatter:
            def fetch(j, slot):
                return pltpu.make_async_copy(src_hbm.at[window(j)], rows_v.at[slot], sems[slot])

            def flush(j, slot):
                for k in range(TOP_K):
                    pltpu.sync_copy(rows_v.at[slot], out_hbm.at[idx_v.at[k, j]])

            run(fetch, flush)
        else:
            for k in range(TOP_K):
                def fetch(j, slot, k=k):
                    return pltpu.make_async_copy(src_hbm.at[idx_v.at[k, j]], rows_v.at[slot], sems[slot])

                def flush(j, slot, k=k):
                    pltpu.sync_copy(rows_v.at[slot], out_hbm.at[pl.ds(k * t + (wid * steps + j) * SC_WINDOW,
                                                                      SC_WINDOW)])

                run(fetch, flush)

    return move(src, dest.reshape(dest.shape[0], n_workers, steps, SC_WINDOW))


def _expert_outputs(x16, routing, cnt, w13, w2):
    t = x16.shape[0]
    start, items = _expert_work_items(cnt[0, :N_EXPERTS], t * TOP_K // MOE_ROWS)
    dest = _slot_rows(routing, start)
    xs = _sc_move_rows(x16, dest, scatter=True)
    ys = _expert_blocks(xs, items, w13, w2)
    return _sc_move_rows(ys, dest, scatter=False)


def _trunk(x, prm):
    b, l, d = x.shape
    t = b * l
    x = x.reshape(t, d)
    moe, g, bias = None, prm['ln_in_g'], prm['ln_in_b']
    tc, ts = _fft_twiddles(l)
    for i in range(DEPTH):
        x, pa, pb, pc, pd = _input_projection(x, moe, g, bias, prm['w_in'][i])
        y_a = _neighbourhood_attention(pa.reshape(b, l, -1), prm['attn_bias'][i])
        y_b = _short_conv(pb.reshape(b, l, -1), prm['conv_w'][i])
        y_c = _fourier_mix(pc.reshape(b, l, -1), tc, ts)
        o_f, o_b = _hgrn2_bidirectional(pd.reshape(b, l, -1), prm['lb'][0, i], prm['lb'][1, i], i == 0)
        x, x16, routing, p_lanes, cnt = _merge(
            x, y_a.reshape(t, BR), y_b.reshape(t, BR), y_c.reshape(t, BR), o_f.reshape(t, BR), o_b.reshape(t, BR),
            pd, prm['hgrn_norm_g'][i], prm['w_gate'][i], prm['b_gate'][i], prm['w_br'][i], prm['w_out'][i],
            prm['ln1_g'][i], prm['ln1_b'][i], prm['router_g_w'][i], prm['router_g_b'][i], prm['router_e_w'][i],
            prm['router_e_b'][i])
        y01 = _expert_outputs(x16, routing, cnt, prm['w13'][i], prm['w2'][i])
        moe, g, bias = (y01, p_lanes), prm['ln2_g'][i], prm['ln2_b'][i]
    return _combine(x, *moe, g, bias).reshape(b, l, d)


def kernel(x_prompt, x_sample, ln_in_g, ln_in_b, w_in, na_rpb, conv_w, hgrn_lb, hgrn_norm_g, w_gate, b_gate, w_br,
           w_out, ln1_g, ln1_b, router_g_w, router_g_b, router_e_w, router_e_b, w13, w2, ln2_g, ln2_b):
    lb = jnp.cumsum(jax.nn.softmax(hgrn_lb.astype(F32), axis=1), axis=1)
    lb = lb - lb[:, :1]
    prm = dict(
        ln_in_g=ln_in_g, ln_in_b=ln_in_b,
        w_in=jnp.stack([_fold_channel_dft(w_in[i]) for i in range(DEPTH)]).astype(BF16),
        attn_bias=jnp.stack([_attn_bias_table(na_rpb[i]) for i in range(DEPTH)]),
        conv_w=conv_w, lb=lb, hgrn_norm_g=hgrn_norm_g,
        w_gate=w_gate.astype(BF16), b_gate=b_gate, w_br=w_br.astype(BF16), w_out=w_out.astype(BF16),
        ln1_g=ln1_g, ln1_b=ln1_b, router_g_w=router_g_w, router_g_b=router_g_b,
        router_e_w=router_e_w, router_e_b=router_e_b, w13=w13.astype(BF16), w2=w2.astype(BF16),
        ln2_g=ln2_g, ln2_b=ln2_b)
    return _trunk(x_prompt, prm), _trunk(x_sample, prm)
```

```python
import functools

import numpy as np
import jax
import jax.numpy as jnp
from jax import lax
from jax.experimental import pallas as pl
from jax.experimental.pallas import tpu as pltpu
from jax.experimental.pallas import tpu_sc as plsc

F32 = jnp.float32
BF16 = jnp.bfloat16
HIGHEST = lax.Precision.HIGHEST

D_MODEL = 1024
DEPTH = 4
GRID_W = 64
BR = 256
N_HEADS = 4
HEAD_DIM = 64
NA_ROWS = 8
NA_COLS = 16
FN_GROUP_DIM = 64
N_GROUPS = 4
EXPERTS_PER_GROUP = 8
N_EXPERTS = N_GROUPS * EXPERTS_PER_GROUP
TOP_K = 2
D_FF_E = 512
DEEPNORM_ALPHA = (2 * DEPTH) ** 0.25
LN_EPS = 1e-5
RMS_EPS = 1e-6
NEG_INF = -1e30

LANES = 128
VMEM_LIMIT_BYTES = 56 * 1024 * 1024
MOE_ROWS = 512
HG_CHUNK = 128
FFT_L1 = 64
SC_WINDOW = 64

COL_QA, COL_KA, COL_VA, COL_XB, COL_GB, COL_GC, COL_XF, COL_QD, COL_FF, COL_FB, COL_VD, COL_GD = range(12)
PROJ_GROUPS = ((0, 3), (3, 6), (6, 8), (8, 13))


def _cparams(*sem):
    return pltpu.CompilerParams(dimension_semantics=sem, vmem_limit_bytes=VMEM_LIMIT_BYTES)


def _sigmoid(x):
    return 1.0 / (1.0 + jnp.exp(-x))


def _split_bf16(x):
    hi = x.astype(BF16)
    return hi, (x - hi.astype(F32)).astype(BF16)


def _dot_split(a_hi, a_lo, b_hi, b_lo):
    dot = functools.partial(jnp.dot, preferred_element_type=F32)
    return dot(a_hi, b_hi) + (dot(a_lo, b_hi) + dot(a_hi, b_lo))


def _pack_bf16_pairs(x):
    m = x.shape[1] // 2
    hi = lax.bitcast_convert_type(x[:, :m].astype(BF16).astype(F32), jnp.uint32)
    lo = lax.bitcast_convert_type(x[:, m:].astype(BF16).astype(F32), jnp.uint32)
    return hi | (lo >> 16)


def _unpack_bf16_pairs(u):
    hi = lax.bitcast_convert_type(u & jnp.uint32(0xFFFF0000), F32)
    lo = lax.bitcast_convert_type(u << 16, F32)
    return jnp.concatenate([hi, lo], axis=1)


def _ln_rows(x, g, b):
    mu = jnp.mean(x, axis=-1, keepdims=True)
    xc = x - mu
    var = jnp.mean(xc * xc, axis=-1, keepdims=True)
    return xc * lax.rsqrt(var + LN_EPS) * g + b


def _head_of_lane(shape):
    return lax.broadcasted_iota(jnp.int32, shape, len(shape) - 1) // HEAD_DIM


def _head_stack(x):
    head = _head_of_lane(x.shape)
    return jnp.concatenate([jnp.where(head == h, x, jnp.zeros_like(x)) for h in range(N_HEADS)], axis=0)


def _same_head_matrix():
    h = np.arange(BR) // HEAD_DIM
    return (h[:, None] == h[None, :]).astype(np.float32)


def _moe_residual(x_ref, y0_ref, y1_ref, p_ref):
    p = p_ref[...]
    ffn = p[:, 0:1] * _unpack_bf16_pairs(y0_ref[...]) + p[:, 1:2] * _unpack_bf16_pairs(y1_ref[...])
    return DEEPNORM_ALPHA * x_ref[...] + ffn


def _proj_kernel(*refs, after_moe):
    if after_moe:
        x_ref, y0_ref, y1_ref, p_ref, g_ref, b_ref, w_ref, xo_ref, *out_refs = refs
        pre = _moe_residual(x_ref, y0_ref, y1_ref, p_ref)
    else:
        x_ref, g_ref, b_ref, w_ref, xo_ref, *out_refs = refs
        pre = x_ref[...]
    x = _ln_rows(pre, g_ref[...], b_ref[...])
    xo_ref[...] = x
    xb = x.astype(BF16)
    for ref, (lo, hi) in zip(out_refs, PROJ_GROUPS):
        ref[...] = jnp.dot(xb, w_ref[:, lo * BR:hi * BR], preferred_element_type=F32).astype(ref.dtype)


def _fold_channel_dft(w_in):
    cg, sg = _dft_cos_sin(FN_GROUP_DIM)
    eye = np.eye(BR // FN_GROUP_DIM, dtype=np.float32)
    w_xf = w_in[:, COL_XF * BR:(COL_XF + 1) * BR]
    w_re = jnp.dot(w_xf, np.kron(eye, cg), precision=HIGHEST)
    w_im = -jnp.dot(w_xf, np.kron(eye, sg), precision=HIGHEST)
    return jnp.concatenate([w_in[:, :COL_XF * BR], w_re, w_im, w_in[:, (COL_XF + 1) * BR:]], axis=1)


def _pair_specs(t, d, tm):
    return [pl.BlockSpec((tm, d // 2), lambda i: (i, 0)), pl.BlockSpec((tm, d // 2), lambda i: (i + t // tm, 0))]


def _input_projection(x, moe, g, b, w_in, tm=512):
    t, d = x.shape
    n = w_in.shape[1]
    widths = (d,) + tuple((hi - lo) * BR for lo, hi in PROJ_GROUPS)
    dtypes = (F32, BF16, BF16, F32, BF16)
    row = pl.BlockSpec((tm, d), lambda i: (i, 0))
    vec = pl.BlockSpec((1, d), lambda i: (0, 0))
    acts, act_specs = [x], [row]
    if moe is not None:
        y01, p = moe
        acts += [y01, y01, p]
        act_specs += _pair_specs(t, d, tm) + [pl.BlockSpec((tm, LANES), lambda i: (i, 0))]
    return pl.pallas_call(
        functools.partial(_proj_kernel, after_moe=moe is not None), grid=(t // tm,),
        in_specs=act_specs + [vec, vec, pl.BlockSpec((d, n), lambda i: (0, 0))],
        out_specs=[pl.BlockSpec((tm, w), lambda i: (i, 0)) for w in widths],
        out_shape=[jax.ShapeDtypeStruct((t, w), dt) for w, dt in zip(widths, dtypes)],
        compiler_params=_cparams("parallel"), name="ln_in_proj",
    )(*acts, g.reshape(1, d), b.reshape(1, d), w_in)


def _attn_bias_table(rpb):
    c = np.arange(GRID_W)
    win_c0 = np.clip(c - NA_COLS // 2, 0, GRID_W - NA_COLS)
    valid = (c[None, :] >= win_c0[:, None]) & (c[None, :] < win_c0[:, None] + NA_COLS)
    dc = np.clip(c[None, :] - c[:, None], -(NA_COLS - 1), NA_COLS - 1) + NA_COLS - 1
    dr = np.arange(NA_ROWS)[None, :] - np.arange(NA_ROWS)[:, None] + NA_ROWS - 1
    row_hot = (dr[:, :, None] == np.arange(2 * NA_ROWS - 1)).astype(np.float32)
    col_hot = (dc[:, :, None] == np.arange(2 * NA_COLS - 1)).astype(np.float32)
    t = jnp.einsum('oia,hab,qkb->ohqik', row_hot, rpb.astype(F32), col_hot, precision=HIGHEST)
    t = jnp.where(valid[None, None, :, None, :], t, NEG_INF)
    return t.reshape(NA_ROWS, N_HEADS * GRID_W, NA_ROWS * GRID_W)


def _attn_kernel(q_ref, k_ref, v_ref, bias_ref, o_ref, *, n_rows, rows_per_tile):
    i = pl.program_id(1)
    head = _head_of_lane((GRID_W, BR))
    scale = HEAD_DIM ** -0.5
    span = NA_ROWS * GRID_W
    for j in range(rows_per_tile):
        r = i * rows_per_tile + j
        r0 = jnp.clip(r - NA_ROWS // 2, 0, n_rows - NA_ROWS)
        q = q_ref[0, j * GRID_W:(j + 1) * GRID_W, :] * scale
        qs = _head_stack(q)
        start = pl.multiple_of(r0 * GRID_W, GRID_W)
        ks = k_ref[0, pl.ds(start, span), :]
        vs = v_ref[0, pl.ds(start, span), :]
        s = lax.dot_general(qs, ks, (((1,), (1,)), ((), ())), preferred_element_type=F32)
        s = s + bias_ref[r - r0]
        m = jnp.max(s, axis=-1, keepdims=True)
        e = jnp.exp(s - m)
        inv = 1.0 / jnp.sum(e, axis=-1, keepdims=True)
        o4 = jnp.dot(e.astype(BF16), vs, preferred_element_type=F32) * inv
        o = jnp.zeros((GRID_W, BR), F32)
        for h in range(N_HEADS):
            o = o + jnp.where(head == h, o4[h * GRID_W:(h + 1) * GRID_W, :], 0.0)
        o_ref[0, j * GRID_W:(j + 1) * GRID_W, :] = o.astype(o_ref.dtype)


def _neighbourhood_attention(pa, bias, rows_per_tile=8):
    b, l, _ = pa.shape
    n_rows = l // GRID_W
    tq = rows_per_tile * GRID_W
    assert HEAD_DIM in (16, 64, 256), "the kernel scales bf16 queries by HEAD_DIM ** -0.5, exact for powers of 4"
    kern = functools.partial(_attn_kernel, n_rows=n_rows, rows_per_tile=rows_per_tile)
    return pl.pallas_call(
        kern, grid=(b, l // tq),
        in_specs=[pl.BlockSpec((1, tq, BR), lambda bi, i: (bi, i, 0)),
                  pl.BlockSpec((1, l, BR), lambda bi, i: (bi, 0, 1)),
                  pl.BlockSpec((1, l, BR), lambda bi, i: (bi, 0, 2)),
                  pl.BlockSpec(bias.shape, lambda bi, i: (0, 0, 0))],
        out_specs=pl.BlockSpec((1, tq, BR), lambda bi, i: (bi, i, 0)),
        out_shape=jax.ShapeDtypeStruct((b, l, BR), BF16),
        compiler_params=_cparams("parallel", "arbitrary"), name="nbr_attn",
    )(pa, pa, pa, bias)


def _conv_kernel(xb_ref, gb_ref, gc_ref, xbp_ref, gcp_ref, xbn_ref, gcn_ref, w_ref, o_ref, *, n_tiles):
    i = pl.program_id(1)
    tl = xb_ref.shape[1]
    last = xbp_ref.shape[1] - 1

    def gated(gc, xb):
        return gc.astype(F32) * xb.astype(F32)

    u = gated(gc_ref[0], xb_ref[0])
    u_before = jnp.where(i > 0, gated(gcp_ref[0, last:, :], xbp_ref[0, last:, :]), 0.0)
    u_after = jnp.where(i < n_tiles - 1, gated(gcn_ref[0, 0:1, :], xbn_ref[0, 0:1, :]), 0.0)
    row = lax.broadcasted_iota(jnp.int32, (tl, 1), 0)
    u_prev = jnp.where(row == 0, u_before, pltpu.roll(u, 1, 0))
    u_next = jnp.where(row == tl - 1, u_after, pltpu.roll(u, tl - 1, 0))
    conv = w_ref[0:1, :] * u_prev + w_ref[1:2, :] * u + w_ref[2:3, :] * u_next
    o_ref[0] = (gb_ref[0].astype(F32) * conv).astype(o_ref.dtype)


def _short_conv(pb, conv_w, tl=1024):
    b, l, _ = pb.shape
    n_tiles = l // tl
    halo = 16
    per_tile = tl // halo
    last_halo = l // halo - 1

    def main(col):
        return pl.BlockSpec((1, tl, BR), lambda bi, i: (bi, i, col))

    def before(col):
        return pl.BlockSpec((1, halo, BR), lambda bi, i: (bi, jnp.maximum(i * per_tile - 1, 0), col))

    def after(col):
        return pl.BlockSpec((1, halo, BR), lambda bi, i: (bi, jnp.minimum((i + 1) * per_tile, last_halo), col))

    return pl.pallas_call(
        functools.partial(_conv_kernel, n_tiles=n_tiles), grid=(b, n_tiles),
        in_specs=[main(0), main(1), main(2), before(0), before(2), after(0), after(2),
                  pl.BlockSpec(conv_w.shape, lambda bi, i: (0, 0))],
        out_specs=pl.BlockSpec((1, tl, BR), lambda bi, i: (bi, i, 0)),
        out_shape=jax.ShapeDtypeStruct((b, l, BR), BF16),
        compiler_params=_cparams("parallel", "parallel"), name="short_conv",
    )(pb, pb, pb, pb, pb, pb, pb, conv_w)


def _dft_cos_sin(n):
    ang = 2.0 * np.pi * ((np.arange(n)[:, None] * np.arange(n)[None, :]) % n) / n
    return np.cos(ang).astype(np.float32), np.sin(ang).astype(np.float32)


def _fft_major_kernel(g_ref, m_ref, tc_ref, ts_ref, br_ref, bi_ref):
    for s in range(8):
        g = g_ref[0, :, s, :]
        swapped = jnp.concatenate([g[:, BR:], -g[:, :BR]], axis=1)
        rhs = jnp.concatenate([g, swapped], axis=0)
        a = _dot_split(m_ref[0], m_ref[1], *_split_bf16(rhs))
        ar, ai = a[:, :BR], a[:, BR:]
        tc = tc_ref[:, s, :]
        ts = ts_ref[:, s, :]
        br_ref[0, :, s, :] = ar * tc + ai * ts
        bi_ref[0, :, s, :] = ai * tc - ar * ts


def _fft_minor_kernel(br_ref, bi_ref, m_ref, o_ref, *, scale):
    l2 = br_ref.shape[2]
    group = m_ref.shape[1] // l2
    for j0 in range(0, 8, group):
        rhs = jnp.concatenate([br_ref[0, j0:j0 + group].reshape(group * l2, BR),
                               bi_ref[0, j0:j0 + group].reshape(group * l2, BR)], axis=0)
        y = _dot_split(m_ref[0], m_ref[1], *_split_bf16(rhs)) * scale
        for j in range(group):
            o_ref[0, :, j0 + j, :] = y[j * l2:(j + 1) * l2].astype(o_ref.dtype)


def _fft_twiddles(l):
    l1 = FFT_L1
    l2 = l // l1
    idx = (jnp.arange(l1, dtype=jnp.int32)[:, None] * jnp.arange(l2, dtype=jnp.int32)[None, :]) % l
    ang = idx.astype(F32) * (2.0 * np.pi / l)
    tc = jnp.broadcast_to(jnp.cos(ang)[:, :, None], (l1, l2, BR))
    ts = jnp.broadcast_to(jnp.sin(ang)[:, :, None], (l1, l2, BR))
    return tc, ts


def _fourier_mix(pc, tc, ts):
    b, l, _ = pc.shape
    l1 = FFT_L1
    l2 = l // l1

    c1, s1 = _dft_cos_sin(l1)
    c2, s2 = _dft_cos_sin(l2)
    group = min(max(1, 128 // l2), 8)
    eye = np.eye(group, dtype=np.float32)
    minor = jnp.asarray(np.concatenate([np.kron(eye, c2), np.kron(eye, s2)], axis=1))
    minor_hi = minor.astype(BF16)
    minor_parts = jnp.stack([minor_hi, (minor - minor_hi.astype(F32)).astype(BF16)])

    def strip(width):
        return pl.BlockSpec((1, l1, 8, width), lambda bi, i: (bi, 0, i, 0))

    tspec = pl.BlockSpec((l1, 8, BR), lambda bi, i: (0, i, 0))
    major = jnp.asarray(np.concatenate([c1, s1], axis=1))
    major_hi = major.astype(BF16)
    major_parts = jnp.stack([major_hi, (major - major_hi.astype(F32)).astype(BF16)])
    mspec = pl.BlockSpec(major_parts.shape, lambda bi, i: (0, 0, 0))
    br, bi_ = pl.pallas_call(
        _fft_major_kernel, grid=(b, l2 // 8),
        in_specs=[strip(2 * BR), mspec, tspec, tspec], out_specs=[strip(BR), strip(BR)],
        out_shape=[jax.ShapeDtypeStruct((b, l1, l2, BR), F32)] * 2,
        compiler_params=_cparams("parallel", "parallel"), name="fft_major",
    )(pc.reshape(b, l1, l2, 2 * BR), major_parts, tc, ts)

    bspec = pl.BlockSpec((1, 8, l2, BR), lambda bi, i: (bi, i, 0, 0))
    m2spec = pl.BlockSpec(minor_parts.shape, lambda bi, i: (0, 0, 0))
    scale = 1.0 / float(np.sqrt(l * FN_GROUP_DIM))
    y = pl.pallas_call(
        functools.partial(_fft_minor_kernel, scale=scale), grid=(b, l1 // 8),
        in_specs=[bspec, bspec, m2spec],
        out_specs=pl.BlockSpec((1, l2, 8, BR), lambda bi, i: (bi, 0, i, 0)),
        out_shape=jax.ShapeDtypeStruct((b, l2, l1, BR), F32),
        compiler_params=_cparams("parallel", "parallel"), name="fft_minor",
    )(br, bi_, minor_parts)
    return y.reshape(b, l, BR)


def _hgrn_level_masks():
    t = np.arange(HG_CHUNK)
    out = []
    half = 8
    while 2 * half < HG_CHUNK:
        same = (t[:, None] // (2 * half)) == (t[None, :] // (2 * half))
        out.append(np.tile(same, (1, N_HEADS)).astype(np.float32))
        half *= 2
    return np.stack(out)


def _hgrn_kernel(qd_ref, fl_ref, vd_ref, lb_ref, mask_ref, bd_ref, o_ref, st_ref, *, first_layer, chunks_per_tile,
                 rev):
    c = HG_CHUNK

    @pl.when(pl.program_id(1) == 0)
    def _():
        st_ref[...] = jnp.zeros_like(st_ref)

    bd = bd_ref[...]
    bd16 = bd.astype(BF16)
    lb = lb_ref[...]
    row = lax.broadcasted_iota(jnp.int32, (c, 1), 0)
    sub = row % 8
    nt = (((1,), (1,)), ((), ()))

    def roll8(x, j):
        return pltpu.roll(x.reshape(c // 8, 8, BR), j % 8, 1).reshape(c, BR)

    def earlier(x, j):
        return roll8(x, -j if rev else j)

    def later(x, j):
        return roll8(x, j if rev else -j)

    def has_earlier(j):
        return (sub + j <= 7) if rev else (sub >= j)

    def has_later(j):
        return (sub >= j) if rev else (sub + j <= 7)

    for ci in (reversed(range(chunks_per_tile)) if rev else range(chunks_per_tile)):
        sl = slice(ci * c, (ci + 1) * c)
        qd = qd_ref[0, sl, :].astype(F32)
        fl = fl_ref[0, sl, :].astype(F32)
        v = vd_ref[0, sl, :].astype(F32)
        q = qd * _sigmoid(qd)
        if first_layer:
            f = _sigmoid(fl)
            k = _sigmoid(-fl)
        else:
            f = lb + (1.0 - lb) * _sigmoid(fl)
            k = (1.0 - lb) * _sigmoid(-fl)

        o = jnp.dot((q * k).astype(BF16), bd16, preferred_element_type=F32) * v
        g = q * f
        for d in range(1, 8):
            if d > 1:
                g = g * earlier(f, d - 1)
            p = jnp.where(has_earlier(d), g * earlier(k, d), 0.0)
            o = o + jnp.dot(p.astype(BF16), bd16, preferred_element_type=F32) * earlier(v, d)

        qf = f
        for j in (1, 2, 4):
            qf = qf * jnp.where(has_earlier(j), earlier(qf, j), 1.0)
        kf = jnp.where(has_later(1), later(f, 1), 1.0)
        for j in (1, 2, 4):
            kf = kf * jnp.where(has_later(j), later(kf, j), 1.0)

        scores = None
        half = 8
        level = 0
        while half < c:
            late = ((row // half) % 2) == (0 if rev else 1)
            qb = jnp.where(late, q * qf, 0.0).astype(BF16)
            kb = jnp.where(late, 0.0, k * kf).astype(BF16)
            x = lax.dot_general(qb, _head_stack(kb), nt, preferred_element_type=F32)
            if 2 * half < c:
                x = x * mask_ref[level]
            scores = x if scores is None else scores + x
            blocks = c // (2 * half)
            qf3 = qf.reshape(blocks, 2 * half, BR)
            end_early = half if rev else half - 1
            end_late = 0 if rev else 2 * half - 1
            tot_early = jnp.broadcast_to(qf3[:, end_early:end_early + 1, :], qf3.shape).reshape(c, BR)
            tot_late = jnp.broadcast_to(qf3[:, end_late:end_late + 1, :], qf3.shape).reshape(c, BR)
            qf = qf * jnp.where(late, tot_early, 1.0)
            kf = kf * jnp.where(late, 1.0, tot_late)
            half *= 2
            level += 1

        o = o + jnp.dot(scores.astype(BF16), _head_stack(v.astype(BF16)), preferred_element_type=F32)

        st = st_ref[...]
        o = o + lax.dot_general((q * qf).astype(BF16), st.astype(BF16), nt, preferred_element_type=F32)
        o_ref[0, sl, :] = o
        upd = jnp.dot(v.T.astype(BF16), (k * kf).astype(BF16), preferred_element_type=F32)
        last = 0 if rev else c - 1
        st_ref[...] = st * qf[last:last + 1, :] + upd * bd


def _hgrn_scan(qd, fl, vd, cols, lb, first_layer, rev, chunks_per_tile=8):
    b, l, _ = qd.shape
    tile = HG_CHUNK * chunks_per_tile
    n_tiles = l // tile
    masks = _hgrn_level_masks()
    bd = _same_head_matrix()

    def tile_index(i):
        return n_tiles - 1 - i if rev else i

    def seq(col):
        return pl.BlockSpec((1, tile, BR), lambda bi, i: (bi, tile_index(i), col))

    return pl.pallas_call(
        functools.partial(_hgrn_kernel, first_layer=first_layer, chunks_per_tile=chunks_per_tile, rev=rev),
        grid=(b, n_tiles),
        in_specs=[seq(cols[0]), seq(cols[1]), seq(cols[2]),
                  pl.BlockSpec((1, BR), lambda bi, i: (0, 0)),
                  pl.BlockSpec(masks.shape, lambda bi, i: (0, 0, 0)),
                  pl.BlockSpec((BR, BR), lambda bi, i: (0, 0))],
        out_specs=pl.BlockSpec((1, tile, BR), lambda bi, i: (bi, tile_index(i), 0)),
        out_shape=jax.ShapeDtypeStruct((b, l, BR), F32),
        scratch_shapes=[pltpu.VMEM((BR, BR), F32)],
        compiler_params=_cparams("parallel", "arbitrary"),
        name="hgrn_bwd" if rev else "hgrn_fwd",
    )(qd, fl, vd, lb.reshape(1, BR), masks, bd)


def _hgrn2_bidirectional(pd, lb_f, lb_b, first_layer):
    o_f = _hgrn_scan(pd, pd, pd, (0, 1, 3), lb_f, first_layer, rev=False)
    o_b = _hgrn_scan(pd, pd, pd, (0, 2, 3), lb_b, first_layer, rev=True)
    return o_f, o_b


def _route(x, w_parts, bias, tri, run):
    x_hi, x_lo = _split_bf16(x)
    both = jnp.dot(x_hi, w_parts, preferred_element_type=F32)
    cross = jnp.dot(x_lo, w_parts[:, :LANES], preferred_element_type=F32)
    logits = both[:, :LANES] + (cross + both[:, LANES:]) + bias
    lane = lax.broadcasted_iota(jnp.int32, logits.shape, 1)
    big = jnp.int32(LANES)
    is_group = (lane >= N_EXPERTS) & (lane < N_EXPERTS + N_GROUPS)
    lg = jnp.where(is_group, logits, -jnp.inf)
    mg = jnp.max(lg, axis=-1, keepdims=True)
    g_sel = jnp.min(jnp.where(lg == mg, lane, big), axis=-1, keepdims=True) - N_EXPERTS
    p_g = 1.0 / jnp.sum(jnp.exp(lg - mg), axis=-1, keepdims=True)
    in_group = (lane < N_EXPERTS) & ((lane // EXPERTS_PER_GROUP) == g_sel)
    le = jnp.where(in_group, logits, -jnp.inf)
    v1 = jnp.max(le, axis=-1, keepdims=True)
    i1 = jnp.min(jnp.where(le == v1, lane, big), axis=-1, keepdims=True)
    le2 = jnp.where(lane == i1, -jnp.inf, le)
    v2 = jnp.max(le2, axis=-1, keepdims=True)
    i2 = jnp.min(jnp.where(le2 == v2, lane, big), axis=-1, keepdims=True)
    e2 = jnp.exp(v2 - v1)
    w1 = p_g / (1.0 + e2)
    w2 = p_g * e2 / (1.0 + e2)
    hot1 = lane == i1
    hot2 = lane == i2
    hot = (hot1 | hot2).astype(BF16)
    before = jnp.dot(tri, hot, preferred_element_type=F32) + run
    rank1 = jnp.sum(jnp.where(hot1, before, 0.0), axis=-1, keepdims=True).astype(jnp.int32)
    rank2 = jnp.sum(jnp.where(hot2, before, 0.0), axis=-1, keepdims=True).astype(jnp.int32)
    run = run + jnp.sum(hot.astype(F32), axis=0, keepdims=True)
    e = jnp.where(lane == 0, i1, jnp.where(lane == 1, i2, jnp.where(lane == 2, rank1,
                                                                     jnp.where(lane == 3, rank2, 0))))
    p = jnp.where(lane == 0, w1, jnp.where(lane == 1, w2, 0.0))
    return e, p, run


def _merge_kernel(x_ref, ya_ref, yb_ref, yc_ref, of_ref, ob_ref, gd_ref, ng_ref, bd_ref, wg_ref, bg_ref, wbr_ref,
                  wo_ref, g_ref, b_ref, rw_ref, rb_ref, tri_ref, o_ref, o16_ref, e_ref, p_ref, cnt_ref, run_ref):
    @pl.when(pl.program_id(0) == 0)
    def _():
        run_ref[...] = jnp.zeros_like(run_ref)

    od = of_ref[...] + ob_ref[...]
    sq_hi, sq_lo = _split_bf16(od * od)
    same_head = bd_ref[...].astype(BF16)
    ms = (jnp.dot(sq_hi, same_head, preferred_element_type=F32)
          + jnp.dot(sq_lo, same_head, preferred_element_type=F32)) * (1.0 / HEAD_DIM)
    gd = gd_ref[...].astype(F32)
    y_d = od * lax.rsqrt(ms + RMS_EPS) * ng_ref[...] * (gd * _sigmoid(gd))

    x = x_ref[...]
    xb = x.astype(BF16)
    merged = None
    for i, y in enumerate((ya_ref[...], yb_ref[...], yc_ref[...], y_d)):
        gate = _sigmoid(jnp.dot(xb, wg_ref[i], preferred_element_type=F32) + bg_ref[i])
        term = gate * jnp.dot(y.astype(BF16), wbr_ref[i], preferred_element_type=F32)
        merged = term if merged is None else merged + term
    mix = jnp.dot(merged.astype(BF16), wo_ref[...], preferred_element_type=F32)
    out = _ln_rows(DEEPNORM_ALPHA * x + mix, g_ref[...], b_ref[...])
    o_ref[...] = out
    o16_ref[...] = _pack_bf16_pairs(out)
    e, p, run = _route(out, rw_ref[...], rb_ref[...], tri_ref[...], run_ref[...])
    e_ref[...] = e.T[:e_ref.shape[0], :]
    p_ref[...] = p
    run_ref[...] = run
    cnt_ref[...] = run.astype(jnp.int32)


def _merge(x, y_a, y_b, y_c, o_f, o_b, pd, norm_g, w_gate, b_gate, w_br, w_out, g, b, wg, bg, we, be, tm=512):
    t, d = x.shape
    row = pl.BlockSpec((tm, d), lambda i: (i, 0))
    yrow = pl.BlockSpec((tm, BR), lambda i: (i, 0))
    vec = pl.BlockSpec((1, d), lambda i: (0, 0))
    lanes = pl.BlockSpec((tm, LANES), lambda i: (i, 0))
    lvec = pl.BlockSpec((1, LANES), lambda i: (0, 0))

    def whole(a):
        return pl.BlockSpec(a.shape, lambda i: (0,) * a.ndim)

    bgate = b_gate.reshape(b_gate.shape[0], 1, d)
    rw = jnp.zeros((d, LANES), F32).at[:, :N_EXPERTS].set(we).at[:, N_EXPERTS:N_EXPERTS + N_GROUPS].set(wg)
    rb = jnp.zeros((1, LANES), F32).at[0, :N_EXPERTS].set(be).at[0, N_EXPERTS:N_EXPERTS + N_GROUPS].set(bg)
    r_parts = jnp.concatenate(_split_bf16(rw), axis=1)
    tri = np.tril(np.ones((tm, tm), np.float32), -1).astype(BF16)
    ng = jnp.tile(norm_g, N_HEADS).reshape(1, BR)
    bd = _same_head_matrix()
    return pl.pallas_call(
        _merge_kernel, grid=(t // tm,),
        in_specs=[row, yrow, yrow, yrow, yrow, yrow, pl.BlockSpec((tm, BR), lambda i: (i, 4)), whole(ng), whole(bd),
                  whole(w_gate), whole(bgate), whole(w_br), whole(w_out), vec, vec,
                  whole(r_parts), lvec, whole(tri)],
        out_specs=[row, pl.BlockSpec((tm, d // 2), lambda i: (i, 0)), pl.BlockSpec((8, tm), lambda i: (0, i)),
                   lanes, lvec],
        out_shape=[jax.ShapeDtypeStruct((t, d), F32), jax.ShapeDtypeStruct((t, d // 2), jnp.uint32),
                   jax.ShapeDtypeStruct((8, t), jnp.int32), jax.ShapeDtypeStruct((t, LANES), F32),
                   jax.ShapeDtypeStruct((1, LANES), jnp.int32)],
        scratch_shapes=[pltpu.VMEM((1, LANES), F32)],
        compiler_params=_cparams("arbitrary"), name="merge_ln1_route",
    )(x, y_a, y_b, y_c, o_f, o_b, pd, ng, bd, w_gate, bgate, w_br, w_out, g.reshape(1, d), b.reshape(1, d),
      r_parts, rb, tri)


def _work_items_kernel(cnt_ref, start_ref, items_ref):
    n_items = items_ref.shape[1]

    def put(w, e, blk, lo, hi, first):
        for row, val in enumerate((e, blk, lo, hi, first)):
            items_ref[row, w] = val

    def expert(e, carry):
        start, w, _, _ = carry
        start_ref[e] = start
        end = start + cnt_ref[e]
        first_blk = start // MOE_ROWS
        n_blk = jnp.where(end > start, (end - 1) // MOE_ROWS - first_blk + 1, 0)

        def block(i, w):
            blk = first_blk + i
            row0 = blk * MOE_ROWS
            lo = jnp.maximum(start, row0) - row0
            first = jnp.where((i > 0) | (lo == 0), 1, 0)
            put(w, e, blk, lo, jnp.minimum(end, row0 + MOE_ROWS) - row0, first)
            return w + 1

        w_end = lax.fori_loop(0, n_blk, block, w)
        owner = jnp.where(n_blk > 0, e, carry[2])
        last_blk = jnp.where(n_blk > 0, first_blk + n_blk - 1, carry[3])
        return end, w_end, owner, last_blk

    zero = jnp.int32(0)
    _, w_live, owner, last_blk = lax.fori_loop(0, N_EXPERTS, expert, (zero, zero, zero, zero))

    def surplus(w, _):
        put(w, owner, last_blk, zero, zero, zero)
        return 0

    lax.fori_loop(w_live, n_items, surplus, 0)


def _expert_work_items(counts, n_blocks):
    smem = pl.BlockSpec(memory_space=pltpu.SMEM)
    return pl.pallas_call(
        _work_items_kernel, in_specs=[smem], out_specs=[smem, smem],
        out_shape=[jax.ShapeDtypeStruct((N_EXPERTS,), jnp.int32),
                   jax.ShapeDtypeStruct((5, n_blocks + N_EXPERTS), jnp.int32)],
        name="moe_work_items",
    )(counts)


def _expert_kernel(items_ref, x_ref, w13_ref, w2_ref, o_ref):
    w = pl.program_id(0)
    lo = items_ref[2, w]
    hi = items_ref[3, w]

    @pl.when(hi > lo)
    def _():
        x = _unpack_bf16_pairs(x_ref[...]).astype(BF16)
        h = jnp.dot(x, w13_ref[0], preferred_element_type=F32)
        hg = h[:, :D_FF_E]
        act = hg * _sigmoid(hg) * h[:, D_FF_E:]
        y = _pack_bf16_pairs(jnp.dot(act.astype(BF16), w2_ref[0], preferred_element_type=F32))
        row = lax.broadcasted_iota(jnp.int32, (MOE_ROWS, 1), 0)
        mine = (row >= lo) & (row < hi)

        @pl.when(items_ref[4, w] == 1)
        def _():
            o_ref[...] = jnp.where(mine, y, jnp.zeros_like(y))

        @pl.when(items_ref[4, w] == 0)
        def _():
            o_ref[...] = jnp.where(mine, y, o_ref[...])


def _expert_blocks(xs, items, w13, w2):
    n_rows, half = xs.shape
    d = 2 * half
    grid_spec = pltpu.PrefetchScalarGridSpec(
        num_scalar_prefetch=1, grid=(items.shape[1],),
        in_specs=[pl.BlockSpec((MOE_ROWS, half), lambda w, it: (it[1, w], 0)),
                  pl.BlockSpec((1, d, 2 * D_FF_E), lambda w, it: (it[0, w], 0, 0)),
                  pl.BlockSpec((1, D_FF_E, d), lambda w, it: (it[0, w], 0, 0))],
        out_specs=pl.BlockSpec((MOE_ROWS, half), lambda w, it: (it[1, w], 0)))
    return pl.pallas_call(
        _expert_kernel, grid_spec=grid_spec, out_shape=jax.ShapeDtypeStruct((n_rows, half), jnp.uint32),
        compiler_params=_cparams("arbitrary"), name="moe_experts",
    )(items, xs, w13, w2)


def _combine_kernel(x_ref, y0_ref, y1_ref, p_ref, g_ref, b_ref, o_ref):
    o_ref[...] = _ln_rows(_moe_residual(x_ref, y0_ref, y1_ref, p_ref), g_ref[...], b_ref[...])


def _combine(x, y01, p, g, b, tm=1024):
    t, d = x.shape
    row = pl.BlockSpec((tm, d), lambda i: (i, 0))
    vec = pl.BlockSpec((1, d), lambda i: (0, 0))
    return pl.pallas_call(
        _combine_kernel, grid=(t // tm,),
        in_specs=[row] + _pair_specs(t, d, tm) + [pl.BlockSpec((tm, LANES), lambda i: (i, 0)), vec, vec],
        out_specs=row, out_shape=jax.ShapeDtypeStruct((t, d), F32),
        compiler_params=_cparams("parallel"), name="moe_combine_ln2",
    )(x, y01, y01, p, g.reshape(1, d), b.reshape(1, d))


def _dest_kernel(start_ref, r_ref, o_ref):
    r = r_ref[...]
    e = r[0:TOP_K, :]
    base = jnp.zeros_like(e)
    for x in range(N_EXPERTS):
        base = jnp.where(e == x, start_ref[x], base)
    pad = jnp.zeros((r.shape[0] - TOP_K, r.shape[1]), jnp.int32)
    o_ref[...] = jnp.concatenate([base + r[TOP_K:2 * TOP_K, :], pad], axis=0)


def _slot_rows(routing, start, tl=8192):
    rows, t = routing.shape
    tl = min(tl, t)
    grid_spec = pltpu.PrefetchScalarGridSpec(
        num_scalar_prefetch=1, grid=(t // tl,),
        in_specs=[pl.BlockSpec((rows, tl), lambda i, st: (0, i))],
        out_specs=pl.BlockSpec((rows, tl), lambda i, st: (0, i)))
    return pl.pallas_call(
        _dest_kernel, grid_spec=grid_spec, out_shape=jax.ShapeDtypeStruct((rows, t), jnp.int32),
        compiler_params=_cparams("parallel"), name="moe_slot_rows",
    )(start, routing)


def _sc_workers():
    info = plsc.get_sparse_core_info()
    return info.num_cores, info.num_cores * info.num_subcores


def _sc_move_rows(src, dest, scatter):
    t = dest.shape[1]
    width = src.shape[1]
    n_cores, n_workers = _sc_workers()
    steps = t // (n_workers * SC_WINDOW)
    assert steps * n_workers * SC_WINDOW == t and steps % 2 == 0, (t, n_workers)
    mesh = plsc.VectorSubcoreMesh(core_axis_name="core", subcore_axis_name="subcore")

    @functools.partial(
        pl.kernel, mesh=mesh, out_type=jax.ShapeDtypeStruct((TOP_K * t, width), src.dtype),
        scratch_types=[pltpu.VMEM((TOP_K, steps, SC_WINDOW), jnp.int32),
                       pltpu.VMEM((2, SC_WINDOW, width), src.dtype),
                       pltpu.SemaphoreType.DMA, pltpu.SemaphoreType.DMA])
    def move(src_hbm, idx_hbm, out_hbm, idx_v, rows_v, sem0, sem1):
        wid = lax.axis_index("subcore") * n_cores + lax.axis_index("core")
        for k in range(TOP_K):
            pltpu.sync_copy(idx_hbm.at[k, wid], idx_v.at[k])
        sems = (sem0, sem1)

        def window(j):
            return pl.ds((wid * steps + j) * SC_WINDOW, SC_WINDOW)

        def run(fetch, flush):
            fetch(0, 0).start()

            @pl.loop(0, steps, step=2)
            def _(j):
                fetch(j + 1, 1).start()
                fetch(j, 0).wait()
                flush(j, 0)

                @pl.when(j + 2 < steps)
                def _():
                    fetch(j + 2, 0).start()

                fetch(j + 1, 1).wait()
                flush(j + 1, 1)

        if scatter:
            def fetch(j, slot):
                return pltpu.make_async_copy(src_hbm.at[window(j)], rows_v.at[slot], sems[slot])

            def flush(j, slot):
                for k in range(TOP_K):
                    pltpu.sync_copy(rows_v.at[slot], out_hbm.at[idx_v.at[k, j]])

            run(fetch, flush)
        else:
            for k in range(TOP_K):
                def fetch(j, slot, k=k):
                    return pltpu.make_async_copy(src_hbm.at[idx_v.at[k, j]], rows_v.at[slot], sems[slot])

                def flush(j, slot, k=k):
                    pltpu.sync_copy(rows_v.at[slot], out_hbm.at[pl.ds(k * t + (wid * steps + j) * SC_WINDOW,
                                                                      SC_WINDOW)])

                run(fetch, flush)

    return move(src, dest.reshape(dest.shape[0], n_workers, steps, SC_WINDOW))


def _expert_outputs(x16, routing, cnt, w13, w2):
    t = x16.shape[0]
    start, items = _expert_work_items(cnt[0, :N_EXPERTS], t * TOP_K // MOE_ROWS)
    dest = _slot_rows(routing, start)
    xs = _sc_move_rows(x16, dest, scatter=True)
    ys = _expert_blocks(xs, items, w13, w2)
    return _sc_move_rows(ys, dest, scatter=False)


def _trunk(x, prm):
    b, l, d = x.shape
    t = b * l
    x = x.reshape(t, d)
    moe, g, bias = None, prm['ln_in_g'], prm['ln_in_b']
    tc, ts = _fft_twiddles(l)
    for i in range(DEPTH):
        x, pa, pb, pc, pd = _input_projection(x, moe, g, bias, prm['w_in'][i])
        y_a = _neighbourhood_attention(pa.reshape(b, l, -1), prm['attn_bias'][i])
        y_b = _short_conv(pb.reshape(b, l, -1), prm['conv_w'][i])
        y_c = _fourier_mix(pc.reshape(b, l, -1), tc, ts)
        o_f, o_b = _hgrn2_bidirectional(pd.reshape(b, l, -1), prm['lb'][0, i], prm['lb'][1, i], i == 0)
        x, x16, routing, p_lanes, cnt = _merge(
            x, y_a.reshape(t, BR), y_b.reshape(t, BR), y_c.reshape(t, BR), o_f.reshape(t, BR), o_b.reshape(t, BR),
            pd, prm['hgrn_norm_g'][i], prm['w_gate'][i], prm['b_gate'][i], prm['w_br'][i], prm['w_out'][i],
            prm['ln1_g'][i], prm['ln1_b'][i], prm['router_g_w'][i], prm['router_g_b'][i], prm['router_e_w'][i],
            prm['router_e_b'][i])
        y01 = _expert_outputs(x16, routing, cnt, prm['w13'][i], prm['w2'][i])
        moe, g, bias = (y01, p_lanes), prm['ln2_g'][i], prm['ln2_b'][i]
    return _combine(x, *moe, g, bias).reshape(b, l, d)


def kernel(x_prompt, x_sample, ln_in_g, ln_in_b, w_in, na_rpb, conv_w, hgrn_lb, hgrn_norm_g, w_gate, b_gate, w_br,
           w_out, ln1_g, ln1_b, router_g_w, router_g_b, router_e_w, router_e_b, w13, w2, ln2_g, ln2_b):
    lb = jnp.cumsum(jax.nn.softmax(hgrn_lb.astype(F32), axis=1), axis=1)
    lb = lb - lb[:, :1]
    prm = dict(
        ln_in_g=ln_in_g, ln_in_b=ln_in_b,
        w_in=jnp.stack([_fold_channel_dft(w_in[i]) for i in range(DEPTH)]).astype(BF16),
        attn_bias=jnp.stack([_attn_bias_table(na_rpb[i]) for i in range(DEPTH)]),
        conv_w=conv_w, lb=lb, hgrn_norm_g=hgrn_norm_g,
        w_gate=w_gate.astype(BF16), b_gate=b_gate, w_br=w_br.astype(BF16), w_out=w_out.astype(BF16),
        ln1_g=ln1_g, ln1_b=ln1_b, router_g_w=router_g_w, router_g_b=router_g_b,
        router_e_w=router_e_w, router_e_b=router_e_b, w13=w13.astype(BF16), w2=w2.astype(BF16),
        ln2_g=ln2_g, ln2_b=ln2_b)
    return _trunk(x_prompt, prm), _trunk(x_sample, prm)
```
